```python
import math
import jax, jax.numpy as jnp
from jax import lax
import numpy as np

D_MODEL = 1024
BATCH = 8
SEQ = 4096
DEPTH = 2

CTX_LEN = 256
GRID_W = 64
F32 = jnp.float32
NORM_EPS = 1e-6

N_BRANCH = 4
BRANCH_W = D_MODEL // N_BRANCH

A_HD = 64
A_HEADS = BRANCH_W // A_HD
A_W = A_HEADS * A_HD
A_DECAY_R = 32
A_AAA_R = 32
A_GATE_R = 64
A_GN_EPS = 64e-5
A_COLS = (A_W, A_W, A_W, A_DECAY_R, A_DECAY_R, A_AAA_R, A_AAA_R, A_GATE_R)
A_IN = sum(A_COLS)

B_DK = 64
B_DV = 64
B_HEADS = BRANCH_W // B_DV
B_QK = B_HEADS * B_DK
B_VW = B_HEADS * B_DV
B_CONV = 7
B_CHUNK = 64
B_COLS = (B_QK, B_QK, B_VW, B_HEADS, B_HEADS, B_HEADS, B_HEADS, B_VW)
B_IN = sum(B_COLS)

C_DV = 64
C_HEADS = BRANCH_W // C_DV
C_DK = C_DV // 2
C_QK = C_HEADS * C_DK
C_VW = C_HEADS * C_DV
C_GATE_R = 16
C_GATE_NORM = 16.0
C_CHUNK = 64
C_COLS = (C_QK, C_QK, C_VW, C_GATE_R, C_GATE_R, C_VW)
C_IN = sum(C_COLS)

D_HD = 64
D_HEADS = BRANCH_W // D_HD
D_KV_HEADS = 2
WINDOW = 128
BLOCK = 128
ROPE_BASE = 10000.0
D_COLS = (D_HEADS * D_HD, D_KV_HEADS * D_HD, D_KV_HEADS * D_HD)
D_IN = sum(D_COLS)

MIXER_COLS = (A_IN, B_IN, C_IN, D_IN, N_BRANCH * D_MODEL)
IN_WIDTH = sum(MIXER_COLS)
FFN_HIDDEN = ((8 * D_MODEL + 3 * 256 - 1) // (3 * 256)) * 256

kernel_name = 'hybrid_flow_block'


def _split(t, sizes):
    cuts, acc = [], 0
    for s in sizes[:-1]:
        acc += s
        cuts.append(acc)
    return jnp.split(t, cuts, axis=-1)


def _rms(x, g, eps=NORM_EPS):
    xf = x.astype(F32)
    y = xf * lax.rsqrt(jnp.mean(xf * xf, axis=-1, keepdims=True) + eps)
    return y * g.astype(F32)


def _l2norm(x):
    return x * lax.rsqrt(jnp.sum(x * x, axis=-1, keepdims=True) + 1e-6)


def _centred_shift(x):
    xp = jnp.pad(x, ((0, 0), (1, 1), (0, 0)))
    return 0.5 * (xp[:, :-2] + xp[:, 2:])


def _centred_dwconv(x, w):
    k = w.shape[0]
    return lax.conv_general_dilated(x, w[:, None, :].astype(x.dtype), window_strides=(1,),
                                    padding=[(k // 2, k // 2)], dimension_numbers=('NWC', 'WIO', 'NWC'),
                                    feature_group_count=x.shape[-1])


def _to_chunks(t, c):
    b, n = t.shape[:2]
    return jnp.swapaxes(t.reshape(b, n // c, c, *t.shape[2:]), 2, 3)


def _from_chunks(t):
    t = jnp.swapaxes(t, 2, 3)
    return t.reshape(t.shape[0], -1, *t.shape[3:])


def _dir_fn(d):
    return (lambda t: jnp.flip(t, axis=1)) if d == 1 else (lambda t: t)


def _rwkv7_inputs(p, mu, w0, w2, a0, a2, g2, k_k, k_a):
    b, t, _ = p.shape
    xm = p + (_centred_shift(p) - p) * mu
    r, k, v, xwf, xwb, xaf, xab, xg = _split(xm, A_COLS)
    hd = lambda z: z.reshape(b, t, A_HEADS, A_HD)
    gate = jax.nn.sigmoid(xg) @ g2
    kk = _l2norm(hd(k * k_k))
    dirs = []
    for d, (xw, xa) in enumerate(((xwf, xaf), (xwb, xab))):
        w_raw = w0[d] + jnp.tanh(xw) @ w2[d]
        decay = jnp.exp(-jnp.exp(-jax.nn.softplus(-w_raw) - 0.5))
        a = jax.nn.sigmoid(a0[d] + xa @ a2[d])
        kd = k * (1.0 + (a - 1.0) * k_a)
        dirs.append((hd(decay), hd(kd), hd(a)))
    return hd(r), hd(v), kk, gate, dirs


def _rwkv7_scan(inputs, d, s0, reverse):
    r, v, kk, _, dirs = inputs
    decay, kd, a = dirs[d]

    def step(s, inp):
        r_t, w_t, k_t, v_t, kk_t, a_t = inp
        sa = jnp.einsum('bhvk,bhk->bhv', s, kk_t)
        s = (s * w_t[:, :, None, :] - sa[..., None] * (kk_t * a_t)[:, :, None, :]
             + v_t[..., None] * k_t[:, :, None, :])
        return s, jnp.einsum('bhvk,bhk->bhv', s, r_t)

    xs = tuple(jnp.moveaxis(z, 1, 0) for z in (r, decay, kd, v, kk, a))
    s, y = lax.scan(step, s0, xs, reverse=reverse)
    return s, jnp.moveaxis(y, 0, 1)


def _rwkv7_output(y, inputs, r_k, ln_g, ln_b):
    r, v, _, gate, dirs = inputs
    b, t = y.shape[:2]
    mean = jnp.mean(y, axis=-1, keepdims=True)
    var = jnp.mean(jnp.square(y - mean), axis=-1, keepdims=True)
    yn = ((y - mean) * lax.rsqrt(var + A_GN_EPS)).reshape(b, t, A_W) * ln_g + ln_b
    bonus = sum(jnp.sum(r * kd * r_k, axis=-1, keepdims=True) * v for (_, kd, _) in dirs)
    return (yn + bonus.reshape(b, t, A_W)) * gate


def _rwkv7_mixer(pa, ca, mu, w0, w2, a0, a2, g2, k_k, k_a, r_k, ln_g, ln_b, with_ctx):
    lat = _rwkv7_inputs(pa, mu, w0, w2, a0, a2, g2, k_k, k_a)
    ctx = _rwkv7_inputs(ca, mu, w0, w2, a0, a2, g2, k_k, k_a)
    y_lat, y_ctx = 0.0, 0.0
    for d in range(2):
        s0 = jnp.zeros((pa.shape[0], A_HEADS, A_HD, A_HD), F32)
        s_c, yc = _rwkv7_scan(ctx, d, s0, d == 1)
        _, yl = _rwkv7_scan(lat, d, s_c, d == 1)
        y_lat, y_ctx = y_lat + yl, y_ctx + yc
    out = _rwkv7_output(y_lat, lat, r_k, ln_g, ln_b)
    out_c = _rwkv7_output(y_ctx, ctx, r_k, ln_g, ln_b) if with_ctx else None
    return out, out_c


def _gdn_inputs(p, conv_w, a_log, dt_bias):
    b, t, _ = p.shape
    q, k, v, bf, bb, af, ab, g = _split(p, B_COLS)
    qkv = jax.nn.silu(_centred_dwconv(jnp.concatenate([q, k, v], axis=-1), conv_w))
    q, k, v = _split(qkv, (B_QK, B_QK, B_VW))
    q = _l2norm(q.reshape(b, t, B_HEADS, B_DK)) * (B_DK ** -0.5)
    k = _l2norm(k.reshape(b, t, B_HEADS, B_DK))
    v = v.reshape(b, t, B_HEADS, B_DV)
    dirs = []
    for d, (bx, ax) in enumerate(((bf, af), (bb, ab))):
        beta = jax.nn.sigmoid(bx)
        log_decay = -jnp.exp(a_log[d]) * jax.nn.softplus(ax + dt_bias[d])
        dirs.append((beta, log_decay))
    return q, k, v, g, dirs


def _gdn_chunked(q, k, v, beta, g, s0):
    c = B_CHUNK
    q, k, v = _to_chunks(q, c), _to_chunks(k, c), _to_chunks(v, c)
    beta, g = _to_chunks(beta, c), _to_chunks(g, c)
    gc = jnp.cumsum(g, axis=-1)
    causal = jnp.tril(jnp.ones((c, c), bool))
    strict = jnp.tril(jnp.ones((c, c), bool), -1)
    decay = jnp.exp(jnp.where(causal, gc[..., :, None] - gc[..., None, :], -jnp.inf))
    lmat = jnp.where(strict, beta[..., :, None] * jnp.einsum('bnhid,bnhjd->bnhij', k, k) * decay, 0.0)
    m = lmat + jnp.eye(c, dtype=F32)
    u = lax.linalg.triangular_solve(m, v * beta[..., None], left_side=True, lower=True, unit_diagonal=True)
    w = lax.linalg.triangular_solve(m, k * (beta * jnp.exp(gc))[..., None], left_side=True, lower=True,
                                    unit_diagonal=True)
    attn = jnp.einsum('bnhid,bnhjd->bnhij', q, k) * decay
    q_in = q * jnp.exp(gc)[..., None]
    k_st = k * jnp.exp(gc[..., -1:] - gc)[..., None]
    dec_last = jnp.exp(gc[..., -1])

    def step(s, inp):
        qi, ki, ui, wi, ai, di = inp
        v_new = ui - jnp.einsum('bhcd,bhde->bhce', wi, s)
        o = jnp.einsum('bhcd,bhde->bhce', qi, s) + jnp.einsum('bhij,bhje->bhie', ai, v_new)
        s = s * di[..., None, None] + jnp.einsum('bhcd,bhce->bhde', ki, v_new)
        return s, o

    xs = tuple(jnp.moveaxis(z, 1, 0) for z in (q_in, k_st, u, w, attn, dec_last))
    s, o = lax.scan(step, s0, xs)
    return s, _from_chunks(jnp.moveaxis(o, 0, 1))


def _gdn_mixer(pb, cb, conv_w, a_log, dt_bias, norm_g, with_ctx):
    lat = _gdn_inputs(pb, conv_w, a_log, dt_bias)
    ctx = _gdn_inputs(cb, conv_w, a_log, dt_bias)
    y_lat, y_ctx = 0.0, 0.0
    for d in range(2):
        f = _dir_fn(d)
        s0 = jnp.zeros((pb.shape[0], B_HEADS, B_DK, B_DV), F32)
        s_c, oc = _gdn_chunked(f(ctx[0]), f(ctx[1]), f(ctx[2]), f(ctx[4][d][0]), f(ctx[4][d][1]), s0)
        _, ol = _gdn_chunked(f(lat[0]), f(lat[1]), f(lat[2]), f(lat[4][d][0]), f(lat[4][d][1]), s_c)
        y_lat, y_ctx = y_lat + f(ol), y_ctx + f(oc)

    def finish(y, gate):
        b, t = y.shape[:2]
        return (_rms(y, norm_g) * jax.nn.silu(gate.reshape(b, t, B_HEADS, B_DV))).reshape(b, t, B_VW)

    return finish(y_lat, lat[3]), (finish(y_ctx, ctx[3]) if with_ctx else None)


def _gla_inputs(p, gw2, gb):
    b, t, _ = p.shape
    q, k, v, gf, gbk, g = _split(p, C_COLS)
    q = q.reshape(b, t, C_HEADS, C_DK) * (C_DK ** -0.5)
    k = k.reshape(b, t, C_HEADS, C_DK)
    v = v.reshape(b, t, C_HEADS, C_DV)
    logg = [(jax.nn.log_sigmoid(xg @ gw2[d] + gb[d]) / C_GATE_NORM).reshape(b, t, C_HEADS, C_DK)
            for d, xg in enumerate((gf, gbk))]
    return q, k, v, g, logg


def _gla_chunked(q, k, v, lg, s0):
    c = C_CHUNK
    q, k, v, lg = (_to_chunks(z, c) for z in (q, k, v, lg))
    bcum = jnp.cumsum(lg, axis=3)
    ref = bcum[:, :, :, c // 2:c // 2 + 1]
    causal = jnp.tril(jnp.ones((c, c), bool))
    a = jnp.einsum('bnhid,bnhjd->bnhij', q * jnp.exp(bcum - ref), k * jnp.exp(ref - bcum))
    o_intra = jnp.einsum('bnhij,bnhje->bnhie', jnp.where(causal, a, 0.0), v)
    b_last = bcum[:, :, :, -1:]
    q_in = q * jnp.exp(bcum)
    k_st = k * jnp.exp(b_last - bcum)
    dec_last = jnp.exp(b_last[:, :, :, 0])

    def step(s, inp):
        qi, ki, vi, di = inp
        o = jnp.einsum('bhcd,bhde->bhce', qi, s)
        s = s * di[..., None] + jnp.einsum('bhcd,bhce->bhde', ki, vi)
        return s, o

    xs = tuple(jnp.moveaxis(z, 1, 0) for z in (q_in, k_st, v, dec_last))
    s, o_inter = lax.scan(step, s0, xs)
    return s, _from_chunks(o_intra + jnp.moveaxis(o_inter, 0, 1))


def _gla_mixer(pc, cc, gw2, gb, norm_g, with_ctx):
    lat = _gla_inputs(pc, gw2, gb)
    ctx = _gla_inputs(cc, gw2, gb)
    y_lat, y_ctx = 0.0, 0.0
    for d in range(2):
        f = _dir_fn(d)
        s0 = jnp.zeros((pc.shape[0], C_HEADS, C_DK, C_DV), F32)
        s_c, oc = _gla_chunked(f(ctx[0]), f(ctx[1]), f(ctx[2]), f(ctx[4][d]), s0)
        _, ol = _gla_chunked(f(lat[0]), f(lat[1]), f(lat[2]), f(lat[4][d]), s_c)
        y_lat, y_ctx = y_lat + f(ol), y_ctx + f(oc)

    def finish(y, gate):
        b, t = y.shape[:2]
        return (_rms(y, norm_g) * jax.nn.silu(gate.reshape(b, t, C_HEADS, C_DV))).reshape(b, t, C_VW)

    return finish(y_lat, lat[3]), (finish(y_ctx, ctx[3]) if with_ctx else None)


def _axial_rope(x, rows, cols):
    half = x.shape[-1] // 2
    quarter = half // 2
    inv = ROPE_BASE ** (-jnp.arange(quarter, dtype=F32) / quarter)

    def rot(xa, pos):
        ang = pos[:, None] * inv[None, :]
        cos, sin = jnp.cos(ang)[None, :, None, :], jnp.sin(ang)[None, :, None, :]
        x1, x2 = xa[..., :quarter], xa[..., quarter:]
        return jnp.concatenate([x1 * cos - x2 * sin, x1 * sin + x2 * cos], axis=-1)

    return jnp.concatenate([rot(x[..., :half], rows), rot(x[..., half:], cols)], axis=-1)


def _window_attention(q, k, v, kc, vc, sink):
    b, s, h, hd = q.shape
    nb, grp, lc = s // BLOCK, h // D_KV_HEADS, kc.shape[1]
    scale = hd ** -0.5
    qb = q.reshape(b, nb, BLOCK, D_KV_HEADS, grp, hd)

    def band(t):
        tp = jnp.pad(t, ((0, 0), (BLOCK, BLOCK), (0, 0), (0, 0)))
        return jnp.concatenate([tp[:, o:o + s].reshape(b, nb, BLOCK, D_KV_HEADS, hd)
                                for o in (0, BLOCK, 2 * BLOCK)], axis=2)

    kb, vb = band(k), band(v)
    qpos = jnp.arange(s).reshape(nb, BLOCK)
    kpos = (jnp.arange(nb) * BLOCK - BLOCK)[:, None] + jnp.arange(3 * BLOCK)[None, :]
    valid = ((jnp.abs(qpos[:, :, None] - kpos[:, None, :]) <= WINDOW)
             & (kpos >= 0)[:, None, :] & (kpos < s)[:, None, :])
    s_loc = jnp.einsum('bnqhgd,bnkhd->bnhgqk', qb, kb) * scale
    s_loc = jnp.where(valid[None, :, None, None], s_loc, -jnp.inf)
    s_ctx = jnp.einsum('bnqhgd,bkhd->bnhgqk', qb, kc) * scale
    sink_col = jnp.broadcast_to(sink.reshape(D_KV_HEADS, grp)[:, :, None, None], s_ctx.shape[:-1] + (1,))
    prob = jax.nn.softmax(jnp.concatenate([s_ctx, s_loc, sink_col], axis=-1), axis=-1)
    o = (jnp.einsum('bnhgqk,bkhd->bnqhgd', prob[..., :lc], vc)
         + jnp.einsum('bnhgqk,bnkhd->bnqhgd', prob[..., lc:lc + 3 * BLOCK], vb))
    return o.reshape(b, s, h * hd)


def _context_attention(qc, kc, vc, sink):
    b, lc, h, hd = qc.shape
    grp = h // D_KV_HEADS
    qg = qc.reshape(b, lc, D_KV_HEADS, grp, hd)
    sc = jnp.einsum('bqhgd,bkhd->bhgqk', qg, kc) * (hd ** -0.5)
    sink_col = jnp.broadcast_to(sink.reshape(D_KV_HEADS, grp)[:, :, None, None], sc.shape[:-1] + (1,))
    prob = jax.nn.softmax(jnp.concatenate([sc, sink_col], axis=-1), axis=-1)
    return jnp.einsum('bhgqk,bkhd->bqhgd', prob[..., :lc], vc).reshape(b, lc, h * hd)


def _attn_mixer(pd, cd, sink, rows, cols, with_ctx):
    def heads(p):
        b, t, _ = p.shape
        q, k, v = _split(p, D_COLS)
        return (q.reshape(b, t, D_HEADS, D_HD), k.reshape(b, t, D_KV_HEADS, D_HD),
                v.reshape(b, t, D_KV_HEADS, D_HD))

    q, k, v = heads(pd)
    qc, kc, vc = heads(cd)
    q, k = _axial_rope(q, rows, cols), _axial_rope(k, rows, cols)
    sink = sink.astype(F32)
    y = _window_attention(q, k, v, kc, vc, sink)
    yc = _context_attention(qc, kc, vc, sink) if with_ctx else None
    return y, yc


def _merge(ys, gate_pre, gate_b, w_branch, w_out):
    acc = 0.0
    for i, y in enumerate(ys):
        g = jax.nn.sigmoid(gate_pre[..., i * D_MODEL:(i + 1) * D_MODEL] + gate_b[i])
        acc = acc + g * (y @ w_branch[i])
    return acc @ w_out


def _mixer_block(h, hc, w_in, gate_b, w_branch, w_out,
                 rwkv_mu, rwkv_w0, rwkv_w2, rwkv_a0, rwkv_a2, rwkv_g2, rwkv_kk, rwkv_ka, rwkv_rk,
                 rwkv_ln_g, rwkv_ln_b, gdn_conv, gdn_a_log, gdn_dt_bias, gdn_norm_g,
                 gla_gw2, gla_gb, gla_norm_g, attn_sink, rows, cols, with_ctx):
    p = (h @ w_in).astype(F32)
    pc = (hc @ w_in).astype(F32)
    pa, pb, pg, pd, gl = _split(p, MIXER_COLS)
    ca, cb, cg, cd, gcx = _split(pc, MIXER_COLS)
    ya, yac = _rwkv7_mixer(pa, ca, rwkv_mu, rwkv_w0, rwkv_w2, rwkv_a0, rwkv_a2, rwkv_g2, rwkv_kk, rwkv_ka,
                           rwkv_rk, rwkv_ln_g, rwkv_ln_b, with_ctx)
    yb, ybc = _gdn_mixer(pb, cb, gdn_conv, gdn_a_log, gdn_dt_bias, gdn_norm_g, with_ctx)
    yg, ygc = _gla_mixer(pg, cg, gla_gw2, gla_gb, gla_norm_g, with_ctx)
    yd, ydc = _attn_mixer(pd, cd, attn_sink, rows, cols, with_ctx)
    out = _merge((ya, yb, yg, yd), gl, gate_b, w_branch, w_out)
    out_c = _merge((yac, ybc, ygc, ydc), gcx, gate_b, w_branch, w_out) if with_ctx else None
    return out, out_c


def _swiglu(h, w1, w2):
    gt, up = jnp.split(h @ w1, 2, axis=-1)
    return (jax.nn.silu(gt) * up) @ w2


def setup_inputs(seed: int = 0) -> dict:
    key = jax.random.key(seed)
    ks = iter(jax.random.split(key, 48))
    nrm = lambda shape, s: jax.random.normal(next(ks), shape, F32) * s
    L = DEPTH
    dt = jnp.exp(jax.random.uniform(next(ks), (L, 2, B_HEADS), F32, math.log(1e-3), math.log(1e-1)))
    return {
        'x': nrm((BATCH, SEQ, D_MODEL), 1.0),
        'c': nrm((BATCH, D_MODEL), 1.0),
        'ctx': nrm((BATCH, CTX_LEN, D_MODEL), 1.0),
        'c_ctx': nrm((D_MODEL,), 1.0),
        'ada_w': nrm((L, D_MODEL, 6 * D_MODEL), 0.5 * D_MODEL ** -0.5),
        'ada_b': nrm((L, 6 * D_MODEL), 0.01),
        'norm_g': 1.0 + nrm((L, 4, D_MODEL), 0.02),
        'w_in': nrm((L, D_MODEL, IN_WIDTH), D_MODEL ** -0.5),
        'gate_b': nrm((L, N_BRANCH, D_MODEL), 0.01),
        'w_branch': nrm((L, N_BRANCH, BRANCH_W, D_MODEL), BRANCH_W ** -0.5),
        'w_out': nrm((L, D_MODEL, D_MODEL), D_MODEL ** -0.5),
        'rwkv_mu': jax.random.uniform(next(ks), (L, A_IN), F32),
        'rwkv_w0': nrm((L, 2, A_W), 0.5),
        'rwkv_w2': nrm((L, 2, A_DECAY_R, A_W), A_DECAY_R ** -0.5),
        'rwkv_a0': nrm((L, 2, A_W), 0.5),
        'rwkv_a2': nrm((L, 2, A_AAA_R, A_W), A_AAA_R ** -0.5),
        'rwkv_g2': nrm((L, A_GATE_R, A_W), A_GATE_R ** -0.5),
        'rwkv_kk': 0.85 + nrm((L, A_W), 0.05),
        'rwkv_ka': 1.0 + nrm((L, A_W), 0.05),
        'rwkv_rk': nrm((L, A_HEADS, A_HD), 0.1),
        'rwkv_ln_g': 1.0 + nrm((L, A_W), 0.02),
        'rwkv_ln_b': nrm((L, A_W), 0.01),
        'gdn_conv': nrm((L, B_CONV, 2 * B_QK + B_VW), B_CONV ** -0.5),
        'gdn_a_log': jnp.log(jax.random.uniform(next(ks), (L, 2, B_HEADS), F32, 1.0, 16.0)),
        'gdn_dt_bias': dt + jnp.log(-jnp.expm1(-dt)),
        'gdn_norm_g': 1.0 + nrm((L, B_DV), 0.02),
        'gla_gw2': nrm((L, 2, C_GATE_R, C_QK), C_GATE_R ** -0.5),
        'gla_gb': nrm((L, 2, C_QK), 0.1),
        'gla_norm_g': 1.0 + nrm((L, C_DV), 0.02),
        'attn_sink': nrm((L, D_HEADS), 0.5),
        'ffn_w1': nrm((L, D_MODEL, 2 * FFN_HIDDEN), D_MODEL ** -0.5),
        'ffn_w2': nrm((L, FFN_HIDDEN, D_MODEL), FFN_HIDDEN ** -0.5),
    }


def reference(x, c, ctx, c_ctx, ada_w, ada_b, norm_g, w_in, gate_b, w_branch, w_out,
              rwkv_mu, rwkv_w0, rwkv_w2, rwkv_a0, rwkv_a2, rwkv_g2, rwkv_kk, rwkv_ka, rwkv_rk,
              rwkv_ln_g, rwkv_ln_b, gdn_conv, gdn_a_log, gdn_dt_bias, gdn_norm_g,
              gla_gw2, gla_gb, gla_norm_g, attn_sink, ffn_w1, ffn_w2):
    n_rows = x.shape[1] // GRID_W
    rows = jnp.repeat(jnp.arange(n_rows, dtype=F32), GRID_W)
    cols = jnp.tile(jnp.arange(GRID_W, dtype=F32), n_rows)
    xc = ctx
    for l in range(DEPTH):
        last = l == DEPTH - 1
        mod = (jax.nn.silu(c) @ ada_w[l] + ada_b[l])[:, None, :]
        mod_c = (jax.nn.silu(c_ctx) @ ada_w[l] + ada_b[l])[None, None, :]
        sh1, sc1, g1, sh2, sc2, g2 = jnp.split(mod, 6, axis=-1)
        csh1, csc1, cg1, csh2, csc2, cg2 = jnp.split(mod_c, 6, axis=-1)
        ng = norm_g[l]
        h = (_rms(x, ng[0]) * (1.0 + sc1) + sh1).astype(x.dtype)
        hc = (_rms(xc, ng[0]) * (1.0 + csc1) + csh1).astype(xc.dtype)
        y, yc = _mixer_block(h, hc, w_in[l], gate_b[l], w_branch[l], w_out[l],
                             rwkv_mu[l], rwkv_w0[l], rwkv_w2[l], rwkv_a0[l], rwkv_a2[l], rwkv_g2[l],
                             rwkv_kk[l], rwkv_ka[l], rwkv_rk[l], rwkv_ln_g[l], rwkv_ln_b[l],
                             gdn_conv[l], gdn_a_log[l], gdn_dt_bias[l], gdn_norm_g[l],
                             gla_gw2[l], gla_gb[l], gla_norm_g[l], attn_sink[l], rows, cols, not last)
        x = x + (g1 * _rms(y, ng[1])).astype(x.dtype)
        h = (_rms(x, ng[2]) * (1.0 + sc2) + sh2).astype(x.dtype)
        x = x + (g2 * _rms(_swiglu(h, ffn_w1[l], ffn_w2[l]), ng[3])).astype(x.dtype)
        if not last:
            xc = xc + (cg1 * _rms(yc, ng[1])).astype(xc.dtype)
            hc = (_rms(xc, ng[2]) * (1.0 + csc2) + csh2).astype(xc.dtype)
            xc = xc + (cg2 * _rms(_swiglu(hc, ffn_w1[l], ffn_w2[l]), ng[3])).astype(xc.dtype)
    return x
```

```python
import functools
import math

import numpy as np
import jax
import jax.numpy as jnp
from jax import lax
from jax.experimental import pallas as pl
from jax.experimental.pallas import tpu as pltpu

F32 = jnp.float32
BF16 = jnp.bfloat16

D_MODEL = 1024
N_BRANCH = 4
BRANCH_W = 256
HEAD = 64
N_HEADS = BRANCH_W // HEAD
NORM_EPS = 1e-6
A_GN_EPS = 64e-5
A_LORA = (32, 32, 32, 32, 64)
A_IN = 3 * BRANCH_W + sum(A_LORA)
B_IN = 4 * BRANCH_W + 4 * N_HEADS
C_DK = 32
C_QK = N_HEADS * C_DK
C_GATE_R = 16
C_GATE_NORM = 16.0
C_IN = 2 * C_QK + 2 * BRANCH_W + 2 * C_GATE_R
D_KV_HEADS = 2
D_IN = BRANCH_W + 2 * D_KV_HEADS * HEAD
B_CONV = 7
WINDOW = 128
ROPE_BASE = 10000.0
GRID_W = 64
FFN_HIDDEN = 2816

CHUNK = 64
PREP_ROWS = 256
HALO = 8
ATT_BLOCK = 128
VMEM_LIMIT = 48 * 1024 * 1024

NN = (((1,), (0,)), ((), ()))
NT = (((1,), (1,)), ((), ()))
TN = (((0,), (0,)), ((), ()))


def _mm(a, b, dims=NN, mode="bf16"):
    if mode == "f32":
        return lax.dot_general(a, b, dims, precision=lax.Precision.HIGHEST, preferred_element_type=F32)
    if mode == "x3":
        ah = a.astype(BF16)
        al = (a - ah.astype(F32)).astype(BF16)
        bh = b.astype(BF16)
        bl = (b - bh.astype(F32)).astype(BF16)
        dot = functools.partial(lax.dot_general, dimension_numbers=dims, preferred_element_type=F32)
        return dot(ah, bh) + (dot(ah, bl) + dot(al, bh))
    return lax.dot_general(a.astype(BF16), b.astype(BF16), dims, preferred_element_type=F32)


def _sigmoid(x):
    return 1.0 / (1.0 + jnp.exp(-x))


def _silu(x):
    return x * _sigmoid(x)


def _softplus(x):
    return jnp.maximum(x, 0.0) + jnp.log1p(jnp.exp(-jnp.abs(x)))


def _cparams(sem):
    return pltpu.CompilerParams(dimension_semantics=sem, vmem_limit_bytes=VMEM_LIMIT)


def _mod_kernel(c_ref, w_ref, b_ref, o_ref):
    c = c_ref[...]
    o_ref[...] = _mm(_silu(c), w_ref[...], mode="f32") + b_ref[...]


def _modulation(c_rows, ada_w, ada_b):
    depth, d, n = ada_w.shape
    rows = c_rows.shape[0]
    tn = 1024
    return pl.pallas_call(
        _mod_kernel,
        grid=(depth, n // tn),
        in_specs=[
            pl.BlockSpec((rows, d), lambda l, j: (0, 0)),
            pl.BlockSpec((None, d, tn), lambda l, j: (l, 0, j)),
            pl.BlockSpec((None, 1, tn), lambda l, j: (l, 0, j)),
        ],
        out_specs=pl.BlockSpec((None, rows, tn), lambda l, j: (l, 0, j)),
        out_shape=jax.ShapeDtypeStruct((depth, rows, n), F32),
        compiler_params=_cparams(("parallel", "parallel")),
        name="adaln_mod",
    )(c_rows, ada_w, ada_b.reshape(depth, 1, n))


def _mod_spec(which, n_ctx_tiles, batch, grid_rank):
    def index(b, i, *rest):
        return (jnp.where(i < n_ctx_tiles, batch, b), 0, which)
    del grid_rank
    return pl.BlockSpec((None, 1, D_MODEL), index)


def _prenorm(x, gain, scale, shift):
    ms = jnp.mean(x * x, axis=-1, keepdims=True)
    y = x * lax.rsqrt(ms + NORM_EPS) * gain
    return y * (1.0 + scale) + shift


def _proj_kernel(x_ref, sc_ref, sh_ref, g_ref, w_ref, o_ref, h_ref):
    @pl.when(pl.program_id(2) == 0)
    def _():
        h_ref[...] = _prenorm(x_ref[...], g_ref[...], sc_ref[...], sh_ref[...]).astype(BF16)

    o_ref[...] = jnp.dot(h_ref[...], w_ref[...], preferred_element_type=F32)


def _project(xs, mod, gain, w, n_ctx_rows, sc_idx, sh_idx, tm=512, tn=None):
    batch, rows, d = xs.shape
    n = w.shape[1]
    tm = math.gcd(tm, n_ctx_rows)
    if tn is None:
        tn = next(t for t in (512, 384, 256, 128) if n % t == 0)
    nct = n_ctx_rows // tm
    return pl.pallas_call(
        _proj_kernel,
        grid=(batch, rows // tm, n // tn),
        in_specs=[
            pl.BlockSpec((None, tm, d), lambda b, i, j: (b, i, 0)),
            _mod_spec(sc_idx, nct, batch, 3),
            _mod_spec(sh_idx, nct, batch, 3),
            pl.BlockSpec((1, d), lambda b, i, j: (0, 0)),
            pl.BlockSpec((d, tn), lambda b, i, j: (0, j)),
        ],
        out_specs=pl.BlockSpec((None, tm, tn), lambda b, i, j: (b, i, j)),
        out_shape=jax.ShapeDtypeStruct((batch, rows, n), F32),
        scratch_shapes=[pltpu.VMEM((tm, d), BF16)],
        compiler_params=_cparams(("parallel", "parallel", "arbitrary")),
        name="prenorm_proj",
    )(xs, mod, mod, gain, w)


def _head_block_ones():
    idx = np.arange(BRANCH_W)
    return (idx[:, None] // HEAD == idx[None, :] // HEAD).astype(np.float32)


def _scan_masks(reverse):
    n = N_HEADS * CHUNK
    r = np.arange(n)[:, None]
    c = np.arange(n)[None, :]
    same = (r // CHUNK) == (c // CHUNK)
    ri, ci = r % CHUNK, c % CHUNK
    if reverse:
        strict, incl = ci > ri, ci >= ri
    else:
        strict, incl = ci < ri, ci <= ri
    mats = [same, same & strict, same & incl]
    for s in range(6):
        b = 1 << s
        blk = (ri // (2 * b)) == (ci // (2 * b))
        hi_r, hi_c = (ri // b) % 2 == 1, (ci // b) % 2 == 1
        lvl = (hi_c & ~hi_r) if reverse else (hi_r & ~hi_c)
        mats.append(same & blk & lvl)
    mats.append(np.broadcast_to(c % HEAD == 0, (n, n)))
    return np.stack([m.astype(np.float32) for m in mats])


def _tri(reverse):
    i = np.arange(CHUNK)
    m = (i[None, :] >= i[:, None]) if reverse else (i[None, :] <= i[:, None])
    return m.astype(np.float32)


M_SAME, M_STRICT, M_INCL, M_LVL0, M_SEL = 0, 1, 2, 3, 9


def _scan_kernel(*refs, reverse, scalar_decay, lowrank, slots, finish, mm_mode, inv_mode):
    it = iter(refs)
    sh_ref, dp_ref, mask_ref, tri_ref = next(it), next(it), next(it), next(it)
    if finish:
        ob_ref, fin_ref, p1_ref, p2_ref, avg_ref = next(it), next(it), next(it), next(it), next(it)
    o_ref, st_ref = next(it), next(it)

    @pl.when(pl.program_id(1) == 0)
    def _():
        st_ref[...] = jnp.zeros_like(st_ref)

    def get(name):
        src, idx = slots[name]
        ref = sh_ref if src == "sh" else dp_ref
        return ref[:, idx * BRANCH_W:(idx + 1) * BRANCH_W]

    same = mask_ref[M_SAME]

    def expand(x):
        return jnp.concatenate([x] * N_HEADS, axis=0) * same

    def masked(x, k):
        return jnp.where(mask_ref[k] > 0.5, x, 0.0)

    logw = get("w")
    r, k, v = get("r"), get("k"), get("v")
    cum = _mm(tri_ref[...], logw, mode="f32")
    cum_x = cum - logw
    last = 0 if reverse else CHUNK - 1
    total = cum[last:last + 1, :]
    st = st_ref[...]

    v_e = expand(v)
    r_abs = expand(r * jnp.exp(cum))
    k_end = expand(k * jnp.exp(total - cum))

    if scalar_decay:
        cum_e = expand(cum)
        sel = mask_ref[M_SEL]
        col_i = jnp.sum(cum_e * sel, axis=1, keepdims=True)
        col_x = jnp.sum(expand(cum_x) * sel, axis=1, keepdims=True)
        row_i = _mm(sel, cum_e, NT, mode="f32")
        d_ii = jnp.exp(jnp.where(mask_ref[M_INCL] > 0.5, col_i - row_i, -1e30))
        r_q, k_q = expand(r), expand(k)
        a_rk = _mm(r_q, k_q, NT, mm_mode) * d_ii
    else:
        ref_row = cum[CHUNK // 2:CHUNK // 2 + 1, :]
        p_inv = jnp.exp(ref_row - cum)
        r_q = expand(r * jnp.exp(cum - ref_row))
        k_q = expand(k * p_inv)
        a_rk = masked(_mm(r_q, k_q, NT, mm_mode), M_INCL)

    y_bd = _mm(r_abs, st, NT, mm_mode) + _mm(a_rk, v_e, NN, mm_mode)
    st_new = st * jnp.exp(total) + _mm(v_e, k_end, TN, mm_mode)

    if lowrank:
        a, b = get("a"), get("b")
        a_abs = expand(a * jnp.exp(cum_x))
        b_end = expand(b * jnp.exp(total - cum))
        if scalar_decay:
            d_xi = jnp.exp(jnp.where(mask_ref[M_STRICT] > 0.5, col_x - row_i, -1e30))
            a_q, b_q = expand(a), expand(b)
            a_ab = _mm(a_q, b_q, NT, mm_mode) * d_xi
            a_ak = _mm(a_q, k_q, NT, mm_mode) * d_xi
            a_rb = _mm(r_q, b_q, NT, mm_mode) * d_ii
        else:
            a_q = expand(a * jnp.exp(cum_x - ref_row))
            b_q = expand(b * p_inv)
            a_ab = masked(_mm(a_q, b_q, NT, mm_mode), M_STRICT)
            a_ak = masked(_mm(a_q, k_q, NT, mm_mode), M_STRICT)
            a_rb = masked(_mm(r_q, b_q, NT, mm_mode), M_INCL)
        row = lax.broadcasted_iota(jnp.int32, a_ab.shape, 0)
        col = lax.broadcasted_iota(jnp.int32, a_ab.shape, 1)
        inv = jnp.where(row == col, 1.0, 0.0) + a_ab * mask_ref[M_LVL0]
        for s in range(1, 6):
            inv = inv + _mm(_mm(inv, a_ab * mask_ref[M_LVL0 + s], NN, inv_mode), inv, NN, inv_mode)
        rhs = _mm(a_abs, st, NT, mm_mode) + _mm(a_ak, v_e, NN, mm_mode)
        z = _mm(inv, rhs, NN, inv_mode)
        y_bd = y_bd + _mm(a_rb, z, NN, mm_mode)
        st_new = st_new + _mm(z, b_end, TN, mm_mode)

    st_ref[...] = st_new
    y = y_bd[0:CHUNK]
    for h in range(1, N_HEADS):
        y = y + y_bd[h * CHUNK:(h + 1) * CHUNK]

    if not finish:
        o_ref[...] = y
        return

    y = y + ob_ref[...]
    avg = avg_ref[...]
    if finish == "groupnorm":
        gate, bonus = fin_ref[:, 0:BRANCH_W], fin_ref[:, BRANCH_W:2 * BRANCH_W]
        mean = _mm(y, avg, mode="f32")
        cen = y - mean
        var = _mm(cen * cen, avg, mode="f32")
        yn = cen * lax.rsqrt(var + A_GN_EPS) * p1_ref[...] + p2_ref[...]
        o_ref[...] = (yn + bonus) * gate
    else:
        gate = fin_ref[...]
        ms = _mm(y * y, avg, mode="f32")
        o_ref[...] = y * lax.rsqrt(ms + NORM_EPS) * p1_ref[...] * _silu(gate)


def _scan(sh, dp, n_ctx_rows, *, reverse, scalar_decay, lowrank, slots, finish=None, fin_args=None,
          mm_mode="bf16", inv_mode="bf16"):
    batch, rows, shw = sh.shape
    dpw = dp.shape[-1]
    nc, nctx = rows // CHUNK, n_ctx_rows // CHUNK

    if reverse:
        def chunk(n):
            return jnp.where(n < nctx, nctx - 1 - n, nc - 1 + nctx - n)
    else:
        def chunk(n):
            return n

    def row_spec(width):
        return pl.BlockSpec((None, CHUNK, width), lambda b, n: (b, chunk(n), 0))

    def const_spec(shape):
        zeros = (0,) * len(shape)
        return pl.BlockSpec(shape, lambda b, n: zeros)

    masks = jnp.asarray(_scan_masks(reverse))
    tri = jnp.asarray(_tri(reverse))
    in_specs = [row_spec(shw), row_spec(dpw), const_spec(masks.shape), const_spec(tri.shape)]
    args = [sh, dp, masks, tri]
    if finish:
        ob, fin, p1, p2 = fin_args
        avg = jnp.asarray(_head_block_ones() / HEAD)
        in_specs += [row_spec(BRANCH_W), row_spec(fin.shape[-1]), const_spec(p1.shape), const_spec(p2.shape),
                     const_spec(avg.shape)]
        args += [ob, fin, p1, p2, avg]
    kern = functools.partial(_scan_kernel, reverse=reverse, scalar_decay=scalar_decay, lowrank=lowrank,
                             slots=slots, finish=finish, mm_mode=mm_mode, inv_mode=inv_mode)
    return pl.pallas_call(
        kern,
        grid=(batch, nc),
        in_specs=in_specs,
        out_specs=row_spec(BRANCH_W),
        out_shape=jax.ShapeDtypeStruct((batch, rows, BRANCH_W), F32),
        scratch_shapes=[pltpu.VMEM((N_HEADS * HEAD, N_HEADS * HEAD), F32)],
        compiler_params=_cparams(("parallel", "arbitrary")),
        name="dplr_scan_" + ("bwd" if reverse else "fwd"),
    )(*args)


def _bidir_scan(sh, dps, n_ctx_rows, fin, p1, p2, *, finish, **kw):
    ob = _scan(sh, dps[1], n_ctx_rows, reverse=True, **kw)
    return _scan(sh, dps[0], n_ctx_rows, reverse=False, finish=finish, fin_args=(ob, fin, p1, p2), **kw)


def _halo_specs(width, rows):
    per = PREP_ROWS // HALO
    last = rows // HALO - 1
    own = pl.BlockSpec((None, PREP_ROWS, width), lambda b, i: (b, i, 0))
    prev = pl.BlockSpec((None, HALO, width), lambda b, i: (b, jnp.maximum(i * per - 1, 0), 0))
    nxt = pl.BlockSpec((None, HALO, width), lambda b, i: (b, jnp.minimum((i + 1) * per, last), 0))
    return [own, prev, nxt]


def _seq_edges(n_ctx_blocks):
    i = pl.program_id(1)
    first = (i == 0) | (i == n_ctx_blocks)
    lastb = (i == n_ctx_blocks - 1) | (i == pl.num_programs(1) - 1)
    return first, lastb


def _vec_spec(shape):
    zeros = (0,) * len(shape)
    return pl.BlockSpec(shape, lambda b, i: zeros)


def _rwkv_prep_kernel(p_ref, prev_ref, next_ref, mu_ref, w2_ref, a2_ref, g2_ref, w0_ref, a0_ref, kk_ref,
                      ka_ref, rk_ref, ones_ref, sh_ref, dp0_ref, dp1_ref, fin_ref, *, n_ctx_blocks, mode):
    first, lastb = _seq_edges(n_ctx_blocks)
    x = p_ref[...]
    prev_row = jnp.where(first, 0.0, prev_ref[HALO - 1:HALO, :])
    next_row = jnp.where(lastb, 0.0, next_ref[0:1, :])
    rid = lax.broadcasted_iota(jnp.int32, x.shape, 0)
    x_prev = jnp.where(rid == 0, prev_row, pltpu.roll(x, 1, axis=0))
    x_next = jnp.where(rid == PREP_ROWS - 1, next_row, pltpu.roll(x, PREP_ROWS - 1, axis=0))
    xm = x + (0.5 * (x_prev + x_next) - x) * mu_ref[...]
    w = BRANCH_W
    r, k, v, lo = xm[:, 0:w], xm[:, w:2 * w], xm[:, 2 * w:3 * w], xm[:, 3 * w:4 * w]
    ones = ones_ref[...]
    th, sg = jnp.tanh(lo), _sigmoid(lo)
    gate = _mm(sg, g2_ref[...], mode=mode)
    kx = k * kk_ref[...]
    kk = kx * lax.rsqrt(_mm(kx * kx, ones, mode="f32") + 1e-6)
    sh_ref[:, 0:w] = r
    sh_ref[:, w:2 * w] = v
    sh_ref[:, 2 * w:3 * w] = -kk
    bonus = jnp.zeros_like(v)
    for d, dp_ref in enumerate((dp0_ref, dp1_ref)):
        w_raw = w0_ref[d:d + 1, :] + _mm(th, w2_ref[d], mode=mode)
        logw = -jnp.exp(-_softplus(-w_raw) - 0.5)
        a = _sigmoid(a0_ref[d:d + 1, :] + _mm(lo, a2_ref[d], mode=mode))
        kd = k * (1.0 + (a - 1.0) * ka_ref[...])
        dp_ref[:, 0:w] = logw
        dp_ref[:, w:2 * w] = kd
        dp_ref[:, 2 * w:3 * w] = kk * a
        bonus = bonus + _mm(r * kd * rk_ref[...], ones, mode="f32") * v
    fin_ref[:, 0:w] = gate
    fin_ref[:, w:2 * w] = bonus


def _rwkv_prep(p, n_ctx_rows, mu, w2p, a2p, g2p, w0, a0, kk, ka, rk, mode="bf16"):
    batch, rows, width = p.shape
    w = BRANCH_W
    ones = jnp.asarray(_head_block_ones())
    consts = [mu, w2p, a2p, g2p, w0, a0, kk, ka, rk, ones]
    out = lambda n: jax.ShapeDtypeStruct((batch, rows, n * w), F32)
    ospec = lambda n: pl.BlockSpec((None, PREP_ROWS, n * w), lambda b, i: (b, i, 0))
    return pl.pallas_call(
        functools.partial(_rwkv_prep_kernel, n_ctx_blocks=n_ctx_rows // PREP_ROWS, mode=mode),
        grid=(batch, rows // PREP_ROWS),
        in_specs=_halo_specs(width, rows) + [_vec_spec(c.shape) for c in consts],
        out_specs=[ospec(3), ospec(3), ospec(3), ospec(2)],
        out_shape=[out(3), out(3), out(3), out(2)],
        compiler_params=_cparams(("parallel", "parallel")),
        name="rwkv_prep",
    )(p, p, p, *consts)


def _gdn_prep_kernel(p_ref, prev_ref, next_ref, conv_ref, alog_ref, dt_ref, eb_ref, ea_ref, ones_ref,
                     sh_ref, dp0_ref, dp1_ref, fin_ref, *, n_ctx_blocks):
    first, lastb = _seq_edges(n_ctx_blocks)
    w = BRANCH_W
    x = p_ref[:, 0:3 * w]
    top = jnp.where(first, 0.0, prev_ref[:, 0:3 * w])
    bot = jnp.where(lastb, 0.0, next_ref[:, 0:3 * w])
    xe = jnp.concatenate([top, x, bot], axis=0)
    ext = PREP_ROWS + 2 * HALO
    acc = jnp.zeros_like(x)
    for s in range(B_CONV):
        shift = (B_CONV // 2 - s) % ext
        rolled = xe if shift == 0 else pltpu.roll(xe, shift, axis=0)
        acc = acc + rolled[HALO:HALO + PREP_ROWS] * conv_ref[s:s + 1, :]
    qkv = _silu(acc)
    ones = ones_ref[...]

    def l2n(t):
        return t * lax.rsqrt(_mm(t * t, ones, mode="f32") + 1e-6)

    q = l2n(qkv[:, 0:w]) * (HEAD ** -0.5)
    k = l2n(qkv[:, w:2 * w])
    v = qkv[:, 2 * w:3 * w]
    sh_ref[:, 0:w] = q
    sh_ref[:, w:2 * w] = v
    sh_ref[:, 2 * w:3 * w] = k
    sr = p_ref[:, 4 * w:4 * w + 128]
    beta_all = _sigmoid(sr)
    g_all = -jnp.exp(alog_ref[...]) * _softplus(sr + dt_ref[...])
    for d, dp_ref in enumerate((dp0_ref, dp1_ref)):
        beta = _mm(beta_all, eb_ref[d], mode="f32")
        g = _mm(g_all, ea_ref[d], mode="f32")
        kb = k * beta
        dp_ref[:, 0:w] = g
        dp_ref[:, w:2 * w] = kb
        dp_ref[:, 2 * w:3 * w] = -jnp.exp(g) * kb
    fin_ref[...] = p_ref[:, 3 * w:4 * w]


def _gdn_prep(p, n_ctx_rows, conv_w, alog_vec, dt_vec, eb, ea):
    batch, rows, width = p.shape
    w = BRANCH_W
    ones = jnp.asarray(_head_block_ones())
    consts = [conv_w, alog_vec, dt_vec, eb, ea, ones]
    out = lambda n: jax.ShapeDtypeStruct((batch, rows, n * w), F32)
    ospec = lambda n: pl.BlockSpec((None, PREP_ROWS, n * w), lambda b, i: (b, i, 0))
    return pl.pallas_call(
        functools.partial(_gdn_prep_kernel, n_ctx_blocks=n_ctx_rows // PREP_ROWS),
        grid=(batch, rows // PREP_ROWS),
        in_specs=_halo_specs(width, rows) + [_vec_spec(c.shape) for c in consts],
        out_specs=[ospec(3), ospec(3), ospec(3), ospec(1)],
        out_shape=[out(3), out(3), out(3), out(1)],
        compiler_params=_cparams(("parallel", "parallel")),
        name="gdn_prep",
    )(p, p, p, *consts)


def _gla_prep_kernel(p_ref, gw_ref, gb_ref, lane_ref, sh_ref, dp0_ref, dp1_ref, fin_ref, *, mode):
    w = BRANCH_W
    sh_ref[:, 0:w] = p_ref[:, 0:w] * (C_DK ** -0.5)
    sh_ref[:, w:2 * w] = p_ref[:, 2 * w:3 * w]
    sh_ref[:, 2 * w:3 * w] = p_ref[:, w:2 * w]
    lo = p_ref[:, 4 * w:4 * w + 128]
    for d, dp_ref in enumerate((dp0_ref, dp1_ref)):
        z = _mm(lo, gw_ref[d], mode=mode) + gb_ref[d:d + 1, :]
        dp_ref[...] = (-_softplus(-z) / C_GATE_NORM) * lane_ref[...]
    fin_ref[...] = p_ref[:, 3 * w:4 * w]


def _gla_prep(p, gwp, gbp, lane_mask, mode="bf16"):
    batch, rows, width = p.shape
    w = BRANCH_W
    consts = [gwp, gbp, lane_mask]
    out = lambda n: jax.ShapeDtypeStruct((batch, rows, n * w), F32)
    ospec = lambda n: pl.BlockSpec((None, PREP_ROWS, n * w), lambda b, i: (b, i, 0))
    return pl.pallas_call(
        functools.partial(_gla_prep_kernel, mode=mode),
        grid=(batch, rows // PREP_ROWS),
        in_specs=[pl.BlockSpec((None, PREP_ROWS, width), lambda b, i: (b, i, 0))]
        + [_vec_spec(c.shape) for c in consts],
        out_specs=[ospec(3), ospec(1), ospec(1), ospec(1)],
        out_shape=[out(3), out(1), out(1), out(1)],
        compiler_params=_cparams(("parallel", "parallel")),
        name="gla_prep",
    )(p, *consts)


def _attn_kernel(sink_ref, q_ref, kc_ref, vc_ref, kp_ref, ko_ref, kn_ref, vp_ref, vo_ref, vn_ref,
                 co_ref, so_ref, cp_ref, sp_ref, cn_ref, sn_ref, o_ref, *, n_ctx_rows, n_lat_rows):
    t = pl.program_id(1)
    blk = ATT_BLOCK

    def rope(x, cos, sin):
        width = x.shape[-1]
        lane = lax.broadcasted_iota(jnp.int32, x.shape, 1)
        swapped = jnp.where(lane % 32 < 16, pltpu.roll(x, width - 16, axis=1), pltpu.roll(x, 16, axis=1))
        return x * cos[:, 0:width] + swapped * sin[:, 0:width]

    q = rope(q_ref[...], co_ref[...], so_ref[...])
    k_band = [rope(kp_ref[...], cp_ref[...], sp_ref[...]),
              rope(ko_ref[...], co_ref[...], so_ref[...]),
              rope(kn_ref[...], cn_ref[...], sn_ref[...])]
    v_band = [vp_ref[...], vo_ref[...], vn_ref[...]]

    qi = lax.broadcasted_iota(jnp.int32, (2 * blk, blk), 0) % blk
    ki = lax.broadcasted_iota(jnp.int32, (2 * blk, blk), 1)
    qpos = t * blk + qi - n_ctx_rows
    valid = []
    for j in range(3):
        kpos = (t - 1 + j) * blk + ki - n_ctx_rows
        valid.append((jnp.abs(qpos - kpos) <= WINDOW) & (kpos >= 0) & (kpos < n_lat_rows) & (qpos >= 0))
    lane = lax.broadcasted_iota(jnp.int32, (blk, 2 * HEAD), 1)
    row2 = lax.broadcasted_iota(jnp.int32, (2 * blk, 1), 0)
    scale = HEAD ** -0.5
    for g in range(D_KV_HEADS):
        cols = slice(g * 2 * HEAD, (g + 1) * 2 * HEAD)
        qg = q[:, cols]
        qs = jnp.concatenate([jnp.where(lane < HEAD, qg, 0.0), jnp.where(lane >= HEAD, qg, 0.0)], axis=0)
        s_ctx = _mm(qs, kc_ref[:, cols], NT) * scale
        s_band = [jnp.where(valid[j], _mm(qs, k_band[j][:, cols], NT) * scale, -1e30) for j in range(3)]
        sink = jnp.where(row2 < blk, sink_ref[2 * g], sink_ref[2 * g + 1])
        m = jnp.maximum(jnp.max(s_ctx, axis=-1, keepdims=True), sink)
        for s in s_band:
            m = jnp.maximum(m, jnp.max(s, axis=-1, keepdims=True))
        p_ctx = jnp.exp(s_ctx - m)
        den = jnp.sum(p_ctx, axis=-1, keepdims=True) + jnp.exp(sink - m)
        acc = _mm(p_ctx, vc_ref[:, cols], NN)
        for j in range(3):
            pj = jnp.exp(s_band[j] - m)
            den = den + jnp.sum(pj, axis=-1, keepdims=True)
            acc = acc + _mm(pj, v_band[j][:, cols], NN)
        og = acc / den
        o_ref[:, cols] = jnp.where(lane < HEAD, og[0:blk], og[blk:2 * blk])


def _attention(p, sink, cos, sin, n_ctx_rows):
    batch, rows, _ = p.shape
    blk = ATT_BLOCK
    nb = rows // blk
    w = BRANCH_W

    def band(col, off):
        def index(b, t):
            return (b, jnp.clip(t + off, 0, nb - 1), col)
        return pl.BlockSpec((None, blk, w), index)

    def tab(off):
        return pl.BlockSpec((blk, w), lambda b, t: (jnp.clip(t + off, 0, nb - 1), 0))

    ctx = lambda col: pl.BlockSpec((None, n_ctx_rows, w), lambda b, t: (b, 0, col))
    kern = functools.partial(_attn_kernel, n_ctx_rows=n_ctx_rows, n_lat_rows=rows - n_ctx_rows)
    return pl.pallas_call(
        kern,
        grid=(batch, nb),
        in_specs=[pl.BlockSpec(memory_space=pltpu.SMEM), band(0, 0), ctx(1), ctx(2),
                  band(1, -1), band(1, 0), band(1, 1), band(2, -1), band(2, 0), band(2, 1),
                  tab(0), tab(0), tab(-1), tab(-1), tab(1), tab(1)],
        out_specs=pl.BlockSpec((None, blk, w), lambda b, t: (b, t, 0)),
        out_shape=jax.ShapeDtypeStruct((batch, rows, w), F32),
        compiler_params=_cparams(("parallel", "parallel")),
        name="window_attn",
    )(sink, p, p, p, p, p, p, p, p, p, cos, sin, cos, sin, cos, sin)


def _merge_kernel(x_ref, sc_ref, sh_ref, gm_ref, g0_ref, g1_ref, ya_ref, yb_ref, yc_ref, yd_ref,
                  wg_ref, gb_ref, wb_ref, wo_ref, o_ref):
    x = x_ref[...]
    h = _prenorm(x, g0_ref[...], sc_ref[...], sh_ref[...]).astype(BF16)
    acc = jnp.zeros(x.shape, F32)
    for i, y_ref in enumerate((ya_ref, yb_ref, yc_ref, yd_ref)):
        pre = jnp.dot(h, wg_ref[:, i * D_MODEL:(i + 1) * D_MODEL], preferred_element_type=F32)
        gate = _sigmoid(pre + gb_ref[i:i + 1, :])
        acc = acc + gate * jnp.dot(y_ref[...].astype(BF16), wb_ref[i], preferred_element_type=F32)
    out = jnp.dot(acc.astype(BF16), wo_ref[...], preferred_element_type=F32)
    ms = jnp.mean(out * out, axis=-1, keepdims=True)
    o_ref[...] = x + gm_ref[...] * (out * lax.rsqrt(ms + NORM_EPS) * g1_ref[...])


def _merge(xs, mod, gain0, gain1, ys, wg, gate_b, wb, wo, n_ctx_rows, tm=256):
    batch, rows, d = xs.shape
    tm = math.gcd(tm, n_ctx_rows)
    nct = n_ctx_rows // tm
    tile = lambda width: pl.BlockSpec((None, tm, width), lambda b, i: (b, i, 0))
    consts = [wg, gate_b, wb, wo]
    return pl.pallas_call(
        _merge_kernel,
        grid=(batch, rows // tm),
        in_specs=[tile(d), _mod_spec(1, nct, batch, 2), _mod_spec(0, nct, batch, 2), _mod_spec(2, nct, batch, 2),
                  _vec_spec(gain0.shape), _vec_spec(gain1.shape)]
        + [tile(BRANCH_W)] * 4 + [_vec_spec(c.shape) for c in consts],
        out_specs=tile(d),
        out_shape=jax.ShapeDtypeStruct((batch, rows, d), F32),
        compiler_params=_cparams(("parallel", "parallel")),
        name="merge_out",
    )(xs, mod, mod, mod, gain0, gain1, *ys, *consts)


def _ffn_kernel(x_ref, sc_ref, sh_ref, gm_ref, g2_ref, g3_ref, w1g_ref, w1u_ref, w2_ref, o_ref, h_ref, acc_ref):
    j = pl.program_id(2)

    @pl.when(j == 0)
    def _():
        h_ref[...] = _prenorm(x_ref[...], g2_ref[...], sc_ref[...], sh_ref[...]).astype(BF16)
        acc_ref[...] = jnp.zeros_like(acc_ref)

    h = h_ref[...]
    gt = jnp.dot(h, w1g_ref[...], preferred_element_type=F32)
    up = jnp.dot(h, w1u_ref[...], preferred_element_type=F32)
    acc_ref[...] += jnp.dot((_silu(gt) * up).astype(BF16), w2_ref[...], preferred_element_type=F32)

    @pl.when(j == pl.num_programs(2) - 1)
    def _():
        out = acc_ref[...]
        ms = jnp.mean(out * out, axis=-1, keepdims=True)
        o_ref[...] = x_ref[...] + gm_ref[...] * (out * lax.rsqrt(ms + NORM_EPS) * g3_ref[...])


def _ffn(xs, mod, gain2, gain3, w1, w2, n_ctx_rows, tm=512, th=1408):
    batch, rows, d = xs.shape
    hidden = w2.shape[0]
    tm = math.gcd(tm, n_ctx_rows)
    nct = n_ctx_rows // tm
    nh = hidden // th
    tile = pl.BlockSpec((None, tm, d), lambda b, i, j: (b, i, 0))
    return pl.pallas_call(
        _ffn_kernel,
        grid=(batch, rows // tm, nh),
        in_specs=[tile, _mod_spec(4, nct, batch, 3), _mod_spec(3, nct, batch, 3), _mod_spec(5, nct, batch, 3),
                  pl.BlockSpec((1, d), lambda b, i, j: (0, 0)), pl.BlockSpec((1, d), lambda b, i, j: (0, 0)),
                  pl.BlockSpec((d, th), lambda b, i, j: (0, j)),
                  pl.BlockSpec((d, th), lambda b, i, j: (0, nh + j)),
                  pl.BlockSpec((th, d), lambda b, i, j: (j, 0))],
        out_specs=tile,
        out_shape=jax.ShapeDtypeStruct((batch, rows, d), F32),
        scratch_shapes=[pltpu.VMEM((tm, d), BF16), pltpu.VMEM((tm, d), F32)],
        compiler_params=_cparams(("parallel", "parallel", "arbitrary")),
        name="swiglu",
    )(xs, mod, mod, mod, gain2, gain3, w1, w1, w2)


def _pad_cols(w, width):
    return jnp.pad(w, ((0, 0), (0, width - w.shape[1])))


def _layer_weights(w_in, mu, w2, a2, g2, conv, a_log, dt_bias, gw2, gb):
    wts = {}
    off_b, off_c, off_d, off_g = A_IN, A_IN + B_IN, A_IN + B_IN + C_IN, A_IN + B_IN + C_IN + D_IN
    w = BRANCH_W
    wts["wa"] = _pad_cols(w_in[:, 0:A_IN], 4 * w).astype(BF16)
    wts["mu"] = _pad_cols(mu[None, :], 4 * w)
    lo = np.cumsum((0,) + A_LORA)
    place = lambda m, r0: jnp.zeros((w, w), F32).at[r0:r0 + m.shape[0], :].set(m)
    wts["w2p"] = jnp.stack([place(w2[0], lo[0]), place(w2[1], lo[1])])
    wts["a2p"] = jnp.stack([place(a2[0], lo[2]), place(a2[1], lo[3])])
    wts["g2p"] = place(g2, lo[4])

    wb = w_in[:, off_b:off_b + B_IN]
    wts["wb"] = jnp.concatenate([wb[:, 0:3 * w], wb[:, 3 * w + 16:], _pad_cols(wb[:, 3 * w:3 * w + 16], 128)],
                                axis=1).astype(BF16)
    nh = N_HEADS
    expand = np.zeros((4, 128, w), np.float32)
    for grp in range(4):
        for h in range(nh):
            expand[grp, grp * nh + h, h * HEAD:(h + 1) * HEAD] = 1.0
    wts["eb"] = jnp.asarray(expand[0:2])
    wts["ea"] = jnp.asarray(expand[2:4])
    vec = lambda t: jnp.zeros((1, 128), F32).at[0, 2 * nh:4 * nh].set(t.reshape(-1))
    wts["alog"] = vec(a_log)
    wts["dt"] = vec(dt_bias)
    wts["conv"] = conv

    wc = w_in[:, off_c:off_c + C_IN]
    pad_heads = lambda m: jnp.pad(m.reshape(m.shape[0], nh, C_DK),
                                  ((0, 0), (0, 0), (0, HEAD - C_DK))).reshape(m.shape[0], w)
    qc, kc, vc = wc[:, 0:C_QK], wc[:, C_QK:2 * C_QK], wc[:, 2 * C_QK:2 * C_QK + w]
    loc = wc[:, 2 * C_QK + w:2 * C_QK + w + 2 * C_GATE_R]
    gc = wc[:, 2 * C_QK + w + 2 * C_GATE_R:]
    wts["wc"] = jnp.concatenate([pad_heads(qc), pad_heads(kc), vc, gc, _pad_cols(loc, 128)], axis=1).astype(BF16)
    gwp = jnp.zeros((2, 128, w), F32)
    for d in range(2):
        gwp = gwp.at[d, d * C_GATE_R:(d + 1) * C_GATE_R, :].set(pad_heads(gw2[d]))
    wts["gwp"] = gwp
    wts["gbp"] = pad_heads(gb)
    wts["glane"] = jnp.asarray((np.arange(w) % HEAD < C_DK).astype(np.float32))[None, :]

    wd = w_in[:, off_d:off_d + D_IN]
    qd = wd[:, 0:w]
    dup = lambda m: jnp.concatenate([m[:, 0:HEAD], m[:, 0:HEAD], m[:, HEAD:], m[:, HEAD:]], axis=1)
    wts["wd"] = jnp.concatenate([qd, dup(wd[:, w:w + 2 * HEAD]), dup(wd[:, w + 2 * HEAD:])], axis=1).astype(BF16)
    wts["wg"] = w_in[:, off_g:].astype(BF16)
    return wts


def _rope_tables(n_ctx_rows, n_lat_rows):
    quarter = HEAD // 4
    inv = ROPE_BASE ** (-np.arange(quarter, dtype=np.float32) / quarter)
    pos = np.arange(n_lat_rows)
    rows = (pos // GRID_W).astype(np.float32)
    cols = (pos % GRID_W).astype(np.float32)
    inv = jnp.asarray(inv)
    ang_r = jnp.asarray(rows)[:, None] * inv[None, :]
    ang_c = jnp.asarray(cols)[:, None] * inv[None, :]
    cos = jnp.concatenate([jnp.cos(ang_r)] * 2 + [jnp.cos(ang_c)] * 2, axis=1)
    sin = jnp.concatenate([-jnp.sin(ang_r), jnp.sin(ang_r), -jnp.sin(ang_c), jnp.sin(ang_c)], axis=1)
    cos = jnp.concatenate([jnp.ones((n_ctx_rows, HEAD), F32), cos], axis=0)
    sin = jnp.concatenate([jnp.zeros((n_ctx_rows, HEAD), F32), sin], axis=0)
    return jnp.tile(cos, (1, N_HEADS)), jnp.tile(sin, (1, N_HEADS))


RWKV_SLOTS = {"r": ("sh", 0), "v": ("sh", 1), "a": ("sh", 2), "w": ("dp", 0), "k": ("dp", 1), "b": ("dp", 2)}
GDN_SLOTS = RWKV_SLOTS
GLA_SLOTS = {"r": ("sh", 0), "v": ("sh", 1), "k": ("sh", 2), "w": ("dp", 0)}


def kernel(x, c, ctx, c_ctx, ada_w, ada_b, norm_g, w_in, gate_b, w_branch, w_out, rwkv_mu, rwkv_w0, rwkv_w2, rwkv_a0, rwkv_a2, rwkv_g2, rwkv_kk, rwkv_ka, rwkv_rk, rwkv_ln_g, rwkv_ln_b, gdn_conv, gdn_a_log, gdn_dt_bias, gdn_norm_g, gla_gw2, gla_gb, gla_norm_g, attn_sink, ffn_w1, ffn_w2):
    batch, n_lat, d = x.shape
    n_ctx = ctx.shape[1]
    depth = ada_w.shape[0]
    assert n_ctx % PREP_ROWS == 0 and n_lat % PREP_ROWS == 0 and d == D_MODEL

    mod_rows = 8 * ((batch + 1 + 7) // 8)
    c_rows = jnp.concatenate([c, c_ctx[None, :], jnp.zeros((mod_rows - batch - 1, d), F32)], axis=0)
    mod_all = _modulation(c_rows, ada_w, ada_b)
    cos, sin = _rope_tables(n_ctx, n_lat)

    xs = jnp.concatenate([ctx, x], axis=1)
    row = lambda t: t.reshape(1, -1)
    for l in range(depth):
        mod = mod_all[l].reshape(mod_rows, 1, 6 * d)
        ng = norm_g[l]
        wts = _layer_weights(w_in[l], rwkv_mu[l], rwkv_w2[l], rwkv_a2[l], rwkv_g2[l], gdn_conv[l],
                             gdn_a_log[l], gdn_dt_bias[l], gla_gw2[l], gla_gb[l])
        proj = lambda w: _project(xs, mod, row(ng[0]), w, n_ctx, 1, 0)

        sh, dp0, dp1, fin = _rwkv_prep(proj(wts["wa"]), n_ctx, wts["mu"], wts["w2p"], wts["a2p"], wts["g2p"],
                                       rwkv_w0[l], rwkv_a0[l], row(rwkv_kk[l]), row(rwkv_ka[l]),
                                       row(rwkv_rk[l]))
        ya = _bidir_scan(sh, (dp0, dp1), n_ctx, fin, row(rwkv_ln_g[l]), row(rwkv_ln_b[l]), finish="groupnorm",
                         scalar_decay=False, lowrank=True, slots=RWKV_SLOTS)

        sh, dp0, dp1, fin = _gdn_prep(proj(wts["wb"]), n_ctx, wts["conv"], wts["alog"], wts["dt"], wts["eb"],
                                      wts["ea"])
        gnorm = row(jnp.tile(gdn_norm_g[l], N_HEADS))
        yb = _bidir_scan(sh, (dp0, dp1), n_ctx, fin, gnorm, gnorm, finish="rms",
                         scalar_decay=True, lowrank=True, slots=GDN_SLOTS)

        sh, dp0, dp1, fin = _gla_prep(proj(wts["wc"]), wts["gwp"], wts["gbp"], wts["glane"])
        cnorm = row(jnp.tile(gla_norm_g[l], N_HEADS))
        yc = _bidir_scan(sh, (dp0, dp1), n_ctx, fin, cnorm, cnorm, finish="rms",
                         scalar_decay=False, lowrank=False, slots=GLA_SLOTS)

        yd = _attention(proj(wts["wd"]), attn_sink[l], cos, sin, n_ctx)

        xs = _merge(xs, mod, row(ng[0]), row(ng[1]), (ya, yb, yc, yd), wts["wg"], gate_b[l],
                    w_branch[l].astype(BF16), w_out[l].astype(BF16), n_ctx)
        xs = _ffn(xs, mod, row(ng[2]), row(ng[3]), ffn_w1[l].astype(BF16), ffn_w2[l].astype(BF16), n_ctx)
    return xs[:, n_ctx:, :]
```

```python
import functools
import math

import numpy as np
import jax
import jax.numpy as jnp
from jax import lax
from jax.experimental import pallas as pl
from jax.experimental.pallas import tpu as pltpu

F32 = jnp.float32
BF16 = jnp.bfloat16

D_MODEL = 1024
N_BRANCH = 4
BRANCH_W = 256
HEAD = 64
N_HEADS = BRANCH_W // HEAD
NORM_EPS = 1e-6
A_GN_EPS = 64e-5
A_LORA = (32, 32, 32, 32, 64)
A_IN = 3 * BRANCH_W + sum(A_LORA)
B_IN = 4 * BRANCH_W + 4 * N_HEADS
C_DK = 32
C_QK = N_HEADS * C_DK
C_GATE_R = 16
C_GATE_NORM = 16.0
C_IN = 2 * C_QK + 2 * BRANCH_W + 2 * C_GATE_R
D_KV_HEADS = 2
D_IN = BRANCH_W + 2 * D_KV_HEADS * HEAD
B_CONV = 7
WINDOW = 128
ROPE_BASE = 10000.0
GRID_W = 64
FFN_HIDDEN = 2816

CHUNK = 64
PREP_ROWS = 256
HALO = 8
ATT_BLOCK = 128
VMEM_LIMIT = 48 * 1024 * 1024

NN = (((1,), (0,)), ((), ()))
NT = (((1,), (1,)), ((), ()))
TN = (((0,), (0,)), ((), ()))


def _mm(a, b, dims=NN, mode="bf16"):
    if mode == "f32":
        return lax.dot_general(a, b, dims, precision=lax.Precision.HIGHEST, preferred_element_type=F32)
    if mode == "x3":
        ah = a.astype(BF16)
        al = (a - ah.astype(F32)).astype(BF16)
        bh = b.astype(BF16)
        bl = (b - bh.astype(F32)).astype(BF16)
        dot = functools.partial(lax.dot_general, dimension_numbers=dims, preferred_element_type=F32)
        return dot(ah, bh) + (dot(ah, bl) + dot(al, bh))
    return lax.dot_general(a.astype(BF16), b.astype(BF16), dims, preferred_element_type=F32)


def _sigmoid(x):
    return 1.0 / (1.0 + jnp.exp(-x))


def _silu(x):
    return x * _sigmoid(x)


def _softplus(x):
    return jnp.maximum(x, 0.0) + jnp.log1p(jnp.exp(-jnp.abs(x)))


def _cparams(sem):
    return pltpu.CompilerParams(dimension_semantics=sem, vmem_limit_bytes=VMEM_LIMIT)


def _mod_kernel(c_ref, w_ref, b_ref, o_ref):
    c = c_ref[...]
    o_ref[...] = _mm(_silu(c), w_ref[...], mode="f32") + b_ref[...]


def _modulation(c_rows, ada_w, ada_b):
    depth, d, n = ada_w.shape
    rows = c_rows.shape[0]
    tn = 1024
    return pl.pallas_call(
        _mod_kernel,
        grid=(depth, n // tn),
        in_specs=[
            pl.BlockSpec((rows, d), lambda l, j: (0, 0)),
            pl.BlockSpec((None, d, tn), lambda l, j: (l, 0, j)),
            pl.BlockSpec((None, 1, tn), lambda l, j: (l, 0, j)),
        ],
        out_specs=pl.BlockSpec((None, rows, tn), lambda l, j: (l, 0, j)),
        out_shape=jax.ShapeDtypeStruct((depth, rows, n), F32),
        compiler_params=_cparams(("parallel", "parallel")),
        name="adaln_mod",
    )(c_rows, ada_w, ada_b.reshape(depth, 1, n))


def _row_tile(rows, target):
    return max(t for t in range(8, target + 1, 8) if rows % t == 0)


def _mod_specs(which, batch):
    lat = pl.BlockSpec((None, 1, D_MODEL), lambda b, i, *_: (b, 0, which))
    ctx = pl.BlockSpec((None, 1, D_MODEL), lambda b, i, *_: (batch, 0, which))
    return [lat, ctx]


def _ctx_rows(tm, n_ctx_rows):
    row = pl.program_id(1) * tm + lax.broadcasted_iota(jnp.int32, (tm, 1), 0)
    return row < n_ctx_rows


def _prenorm(x, gain, scale, shift):
    ms = jnp.mean(x * x, axis=-1, keepdims=True)
    y = x * lax.rsqrt(ms + NORM_EPS) * gain
    return y * (1.0 + scale) + shift


def _proj_kernel(x_ref, sc_ref, csc_ref, sh_ref, csh_ref, g_ref, w_ref, o_ref, h_ref, *, n_ctx_rows):
    @pl.when(pl.program_id(2) == 0)
    def _():
        is_ctx = _ctx_rows(x_ref.shape[0], n_ctx_rows)
        scale = jnp.where(is_ctx, csc_ref[...], sc_ref[...])
        shift = jnp.where(is_ctx, csh_ref[...], sh_ref[...])
        h_ref[...] = _prenorm(x_ref[...], g_ref[...], scale, shift).astype(BF16)

    o_ref[...] = jnp.dot(h_ref[...], w_ref[...], preferred_element_type=F32)


def _project(xs, mod, gain, w, n_ctx_rows, sc_idx, sh_idx):
    batch, rows, d = xs.shape
    n = w.shape[1]
    tm = _row_tile(rows, 1088)
    tn = n if n <= 1152 else next(t for t in (1024, 512, 384, 256, 128) if n % t == 0)
    return pl.pallas_call(
        functools.partial(_proj_kernel, n_ctx_rows=n_ctx_rows),
        grid=(batch, rows // tm, n // tn),
        in_specs=[pl.BlockSpec((None, tm, d), lambda b, i, j: (b, i, 0))]
        + _mod_specs(sc_idx, batch) + _mod_specs(sh_idx, batch)
        + [pl.BlockSpec((1, d), lambda b, i, j: (0, 0)),
           pl.BlockSpec((d, tn), lambda b, i, j: (0, j))],
        out_specs=pl.BlockSpec((None, tm, tn), lambda b, i, j: (b, i, j)),
        out_shape=jax.ShapeDtypeStruct((batch, rows, n), F32),
        scratch_shapes=[pltpu.VMEM((tm, d), BF16)],
        compiler_params=_cparams(("parallel", "parallel", "arbitrary")),
        name="prenorm_proj",
    )(xs, mod, mod, mod, mod, gain, w)


def _head_block_ones():
    idx = np.arange(BRANCH_W)
    return (idx[:, None] // HEAD == idx[None, :] // HEAD).astype(np.float32)


N_LEVELS = 6


def _level_mask(ri, ci, s, reverse):
    b = 1 << s
    blk = (ri // (2 * b)) == (ci // (2 * b))
    hi_r, hi_c = (ri // b) % 2 == 1, (ci // b) % 2 == 1
    return blk & ((hi_c & ~hi_r) if reverse else (hi_r & ~hi_c))


def _scan_masks(reverse):
    n = N_HEADS * CHUNK
    ri = np.arange(CHUNK)[:, None]
    ci = np.arange(n)[None, :] % CHUNK
    strict, incl = (ci > ri, ci >= ri) if reverse else (ci < ri, ci <= ri)
    compact = np.stack([strict, incl, ci == ri, _level_mask(ri, ci, 0, reverse)]).astype(np.float32)
    r = np.arange(n)[:, None]
    c = np.arange(n)[None, :]
    same = (r // CHUNK) == (c // CHUNK)
    mats = [same] + [same & _level_mask(r % CHUNK, c % CHUNK, s, reverse) for s in range(1, N_LEVELS)] + [r == c]
    return compact, np.stack(mats).astype(np.float32)


def _tri(reverse):
    i = np.arange(CHUNK)
    m = (i[None, :] >= i[:, None]) if reverse else (i[None, :] <= i[:, None])
    return m.astype(np.float32)


def _split3(x):
    hi = x.astype(BF16)
    r1 = x - hi.astype(F32)
    mid = r1.astype(BF16)
    lo = (r1 - mid.astype(F32)).astype(BF16)
    return hi, mid, lo


def _mm_exact(a, b, dims=NN, split="a"):
    dot = functools.partial(lax.dot_general, dimension_numbers=dims, preferred_element_type=F32)
    if split == "a":
        hi, mid, lo = _split3(a)
        other = b.astype(BF16)
        return dot(hi, other) + (dot(mid, other) + dot(lo, other))
    hi, mid, lo = _split3(b)
    other = a.astype(BF16)
    return dot(other, hi) + (dot(other, mid) + dot(other, lo))


C_STRICT, C_INCL, C_EYE, C_LVL0 = 0, 1, 2, 3
B_SAME, B_LVL1, B_EYE = 0, 1, N_LEVELS
SCAN_BATCH = 8


def _scan_chunk(sh, dp, st, tri, ones, cm_ref, bm_ref, *, slots, reverse, scalar_decay, lowrank, mm_mode):
    same = bm_ref[B_SAME]

    def get(name):
        src, idx = slots[name]
        return (sh if src == "sh" else dp)[:, idx * BRANCH_W:(idx + 1) * BRANCH_W]

    def expand(x):
        return (jnp.concatenate([x] * N_HEADS, axis=0) * same).astype(BF16)

    def keep(x, k):
        return jnp.where(cm_ref[k] > 0.5, x, 0.0)

    logw = get("w")
    r, k, v = get("r"), get("k"), get("v")
    cum = _mm_exact(tri, logw, split="b")
    cum_x = cum - logw
    last = 0 if reverse else CHUNK - 1
    total = cum[last:last + 1, :]
    to_end = jnp.exp(total - cum)

    if scalar_decay:
        diag = jnp.concatenate([cum] * N_HEADS, axis=0) * bm_ref[B_EYE]
        cum_row = _mm_exact(ones, diag, split="b")
        d_ii = jnp.exp(jnp.where(cm_ref[C_INCL] > 0.5, cum - cum_row, -1e30))
        r_q, k_q = r, k
    else:
        ref_row = cum[CHUNK // 2:CHUNK // 2 + 1, :]
        p_inv = jnp.exp(ref_row - cum)
        r_q, k_q = r * jnp.exp(cum - ref_row), k * p_inv

    k_e, v_e = expand(k_q), expand(v)
    r_abs = r * jnp.exp(cum)
    if not lowrank:
        s_k = _mm(r_q, k_e, NT, mm_mode)
        a_rk = s_k * d_ii if scalar_decay else keep(s_k, C_INCL)
        y = _mm(r_abs, st, NT, mm_mode) + _mm(a_rk, v_e, NN, mm_mode)
        upd = _mm(v, k * to_end, TN, mm_mode)
        return y, st * jnp.exp(total) + upd * same

    a, b = get("a"), get("b")
    if scalar_decay:
        d_xi = jnp.exp(jnp.where(cm_ref[C_STRICT] > 0.5, cum_x - cum_row, -1e30))
        a_q, b_q = a, b
    else:
        a_q, b_q = a * jnp.exp(cum_x - ref_row), b * p_inv
    lhs = jnp.concatenate([a_q, r_q], axis=0)
    s_b = _mm(lhs, expand(b_q), NT, mm_mode)
    s_k = _mm(lhs, k_e, NT, mm_mode)
    if scalar_decay:
        a_ab, a_rb = s_b[0:CHUNK] * d_xi, s_b[CHUNK:] * d_ii
        a_ak, a_rk = s_k[0:CHUNK] * d_xi, s_k[CHUNK:] * d_ii
    else:
        a_ab, a_rb = keep(s_b[0:CHUNK], C_STRICT), keep(s_b[CHUNK:], C_INCL)
        a_ak, a_rk = keep(s_k[0:CHUNK], C_STRICT), keep(s_k[CHUNK:], C_INCL)

    inv = cm_ref[C_EYE] + a_ab * cm_ref[C_LVL0]
    a_rep = jnp.concatenate([a_ab] * N_HEADS, axis=0)
    for s in range(1, N_LEVELS):
        c_s = (a_rep * bm_ref[B_LVL1 + s - 1]).astype(BF16)
        inv = inv + _mm(_mm(inv, c_s, NN, mm_mode), expand(inv), NN, mm_mode)

    a_abs = a * jnp.exp(cum_x)
    from_state = _mm(jnp.concatenate([a_abs, r_abs], axis=0), st, NT, mm_mode)
    from_v = _mm(jnp.concatenate([a_ak, a_rk], axis=0), v_e, NN, mm_mode)
    both = from_state + from_v
    z = _mm(inv, expand(both[0:CHUNK]), NN, mm_mode)
    y = both[CHUNK:] + _mm(a_rb, expand(z), NN, mm_mode)
    upd = _mm(jnp.concatenate([v, z], axis=0), jnp.concatenate([k * to_end, b * to_end], axis=0), TN, mm_mode)
    return y, st * jnp.exp(total) + upd * same


def _scan_kernel(*refs, reverse, scalar_decay, lowrank, slots, finish, mm_mode):
    it = iter(refs)
    sh_ref, dp_ref, cm_ref, bm_ref, tri_ref = next(it), next(it), next(it), next(it), next(it)
    if finish:
        ob_ref, fin_ref, p1_ref, p2_ref, avg_ref = next(it), next(it), next(it), next(it), next(it)
    o_ref, st_ref = next(it), next(it)

    @pl.when(pl.program_id(1) == 0)
    def _():
        st_ref[...] = jnp.zeros_like(st_ref)

    nb = sh_ref.shape[0]
    tri = jnp.broadcast_to(tri_ref[...], (nb, CHUNK, CHUNK))
    ones = jnp.ones((nb, CHUNK, N_HEADS * CHUNK), F32)
    chunk = functools.partial(_scan_chunk, cm_ref=cm_ref, bm_ref=bm_ref, slots=slots, reverse=reverse,
                              scalar_decay=scalar_decay, lowrank=lowrank, mm_mode=mm_mode)
    y, st_new = jax.vmap(chunk)(sh_ref[...], dp_ref[...], st_ref[...], tri, ones)
    st_ref[...] = st_new
    if not finish:
        o_ref[...] = y
        return

    avg = avg_ref[...]
    for i in range(nb):
        yi = y[i] + ob_ref[i]
        if finish == "groupnorm":
            gate, bonus = fin_ref[i, :, 0:BRANCH_W], fin_ref[i, :, BRANCH_W:2 * BRANCH_W]
            cen = yi - _mm_exact(yi, avg)
            var = _mm_exact(cen * cen, avg)
            yn = cen * lax.rsqrt(var + A_GN_EPS) * p1_ref[...] + p2_ref[...]
            o_ref[i] = (yn + bonus) * gate
        else:
            ms = _mm_exact(yi * yi, avg)
            o_ref[i] = yi * lax.rsqrt(ms + NORM_EPS) * p1_ref[...] * _silu(fin_ref[i])


def _scan(sh, dp, n_ctx_rows, *, reverse, scalar_decay, lowrank, slots, finish=None, fin_args=None,
          mm_mode="bf16"):
    batch, rows, shw = sh.shape
    dpw = dp.shape[-1]
    nc, nctx = rows // CHUNK, n_ctx_rows // CHUNK
    nb = math.gcd(batch, SCAN_BATCH)

    if reverse:
        def chunk(n):
            return jnp.where(n < nctx, nctx - 1 - n, nc - 1 + nctx - n)
    else:
        def chunk(n):
            return n

    def row_spec(width):
        return pl.BlockSpec((nb, CHUNK, width), lambda b, n: (b, chunk(n), 0))

    def const_spec(shape):
        zeros = (0,) * len(shape)
        return pl.BlockSpec(shape, lambda b, n: zeros)

    cmask, bmask = (jnp.asarray(m) for m in _scan_masks(reverse))
    tri = jnp.asarray(_tri(reverse))
    in_specs = [row_spec(shw), row_spec(dpw), const_spec(cmask.shape), const_spec(bmask.shape),
                const_spec(tri.shape)]
    args = [sh, dp, cmask, bmask, tri]
    if finish:
        ob, fin, p1, p2 = fin_args
        avg = jnp.asarray(_head_block_ones() / HEAD)
        in_specs += [row_spec(BRANCH_W), row_spec(fin.shape[-1]), const_spec(p1.shape), const_spec(p2.shape),
                     const_spec(avg.shape)]
        args += [ob, fin, p1, p2, avg]
    kern = functools.partial(_scan_kernel, reverse=reverse, scalar_decay=scalar_decay, lowrank=lowrank,
                             slots=slots, finish=finish, mm_mode=mm_mode)
    return pl.pallas_call(
        kern,
        grid=(batch // nb, nc),
        in_specs=in_specs,
        out_specs=row_spec(BRANCH_W),
        out_shape=jax.ShapeDtypeStruct((batch, rows, BRANCH_W), F32),
        scratch_shapes=[pltpu.VMEM((nb, N_HEADS * HEAD, N_HEADS * HEAD), F32)],
        compiler_params=_cparams(("parallel", "arbitrary")),
        name="dplr_scan_" + ("bwd" if reverse else "fwd"),
    )(*args)


def _bidir_scan(sh, dps, n_ctx_rows, fin, p1, p2, *, finish, **kw):
    ob = _scan(sh, dps[1], n_ctx_rows, reverse=True, **kw)
    return _scan(sh, dps[0], n_ctx_rows, reverse=False, finish=finish, fin_args=(ob, fin, p1, p2), **kw)


def _halo_specs(width, rows):
    per = PREP_ROWS // HALO
    last = rows // HALO - 1
    own = pl.BlockSpec((None, PREP_ROWS, width), lambda b, i: (b, i, 0))
    prev = pl.BlockSpec((None, HALO, width), lambda b, i: (b, jnp.maximum(i * per - 1, 0), 0))
    nxt = pl.BlockSpec((None, HALO, width), lambda b, i: (b, jnp.minimum((i + 1) * per, last), 0))
    return [own, prev, nxt]


def _seq_edges(n_ctx_blocks):
    i = pl.program_id(1)
    first = (i == 0) | (i == n_ctx_blocks)
    lastb = (i == n_ctx_blocks - 1) | (i == pl.num_programs(1) - 1)
    return first, lastb


def _vec_spec(shape):
    zeros = (0,) * len(shape)
    return pl.BlockSpec(shape, lambda b, i: zeros)


def _rwkv_prep_kernel(p_ref, prev_ref, next_ref, mu_ref, w2_ref, a2_ref, g2_ref, w0_ref, a0_ref, kk_ref,
                      ka_ref, rk_ref, ones_ref, sh_ref, dp0_ref, dp1_ref, fin_ref, *, n_ctx_blocks, mode):
    first, lastb = _seq_edges(n_ctx_blocks)
    x = p_ref[...]
    prev_row = jnp.where(first, 0.0, prev_ref[HALO - 1:HALO, :])
    next_row = jnp.where(lastb, 0.0, next_ref[0:1, :])
    rid = lax.broadcasted_iota(jnp.int32, x.shape, 0)
    x_prev = jnp.where(rid == 0, prev_row, pltpu.roll(x, 1, axis=0))
    x_next = jnp.where(rid == PREP_ROWS - 1, next_row, pltpu.roll(x, PREP_ROWS - 1, axis=0))
    xm = x + (0.5 * (x_prev + x_next) - x) * mu_ref[...]
    w = BRANCH_W
    r, k, v, lo = xm[:, 0:w], xm[:, w:2 * w], xm[:, 2 * w:3 * w], xm[:, 3 * w:4 * w]
    ones = ones_ref[...]
    th, sg = jnp.tanh(lo), _sigmoid(lo)
    gate = _mm(sg, g2_ref[...], mode=mode)
    kx = k * kk_ref[...]
    kk = kx * lax.rsqrt(_mm(kx * kx, ones, mode="f32") + 1e-6)
    sh_ref[:, 0:w] = r
    sh_ref[:, w:2 * w] = v
    sh_ref[:, 2 * w:3 * w] = -kk
    bonus = jnp.zeros_like(v)
    for d, dp_ref in enumerate((dp0_ref, dp1_ref)):
        w_raw = w0_ref[d:d + 1, :] + _mm(th, w2_ref[d], mode=mode)
        logw = -jnp.exp(-_softplus(-w_raw) - 0.5)
        a = _sigmoid(a0_ref[d:d + 1, :] + _mm(lo, a2_ref[d], mode=mode))
        kd = k * (1.0 + (a - 1.0) * ka_ref[...])
        dp_ref[:, 0:w] = logw
        dp_ref[:, w:2 * w] = kd
        dp_ref[:, 2 * w:3 * w] = kk * a
        bonus = bonus + _mm(r * kd * rk_ref[...], ones, mode="f32") * v
    fin_ref[:, 0:w] = gate
    fin_ref[:, w:2 * w] = bonus


def _rwkv_prep(p, n_ctx_rows, mu, w2p, a2p, g2p, w0, a0, kk, ka, rk, mode="bf16"):
    batch, rows, width = p.shape
    w = BRANCH_W
    ones = jnp.asarray(_head_block_ones())
    consts = [mu, w2p, a2p, g2p, w0, a0, kk, ka, rk, ones]
    out = lambda n: jax.ShapeDtypeStruct((batch, rows, n * w), F32)
    ospec = lambda n: pl.BlockSpec((None, PREP_ROWS, n * w), lambda b, i: (b, i, 0))
    return pl.pallas_call(
        functools.partial(_rwkv_prep_kernel, n_ctx_blocks=n_ctx_rows // PREP_ROWS, mode=mode),
        grid=(batch, rows // PREP_ROWS),
        in_specs=_halo_specs(width, rows) + [_vec_spec(c.shape) for c in consts],
        out_specs=[ospec(3), ospec(3), ospec(3), ospec(2)],
        out_shape=[out(3), out(3), out(3), out(2)],
        compiler_params=_cparams(("parallel", "parallel")),
        name="rwkv_prep",
    )(p, p, p, *consts)


def _gdn_prep_kernel(p_ref, prev_ref, next_ref, conv_ref, alog_ref, dt_ref, eb_ref, ea_ref, ones_ref,
                     sh_ref, dp0_ref, dp1_ref, fin_ref, *, n_ctx_blocks):
    first, lastb = _seq_edges(n_ctx_blocks)
    w = BRANCH_W
    x = p_ref[:, 0:3 * w]
    top = jnp.where(first, 0.0, prev_ref[:, 0:3 * w])
    bot = jnp.where(lastb, 0.0, next_ref[:, 0:3 * w])
    xe = jnp.concatenate([top, x, bot], axis=0)
    ext = PREP_ROWS + 2 * HALO
    acc = jnp.zeros_like(x)
    for s in range(B_CONV):
        shift = (B_CONV // 2 - s) % ext
        rolled = xe if shift == 0 else pltpu.roll(xe, shift, axis=0)
        acc = acc + rolled[HALO:HALO + PREP_ROWS] * conv_ref[s:s + 1, :]
    qkv = _silu(acc)
    ones = ones_ref[...]

    def l2n(t):
        return t * lax.rsqrt(_mm(t * t, ones, mode="f32") + 1e-6)

    q = l2n(qkv[:, 0:w]) * (HEAD ** -0.5)
    k = l2n(qkv[:, w:2 * w])
    v = qkv[:, 2 * w:3 * w]
    sh_ref[:, 0:w] = q
    sh_ref[:, w:2 * w] = v
    sh_ref[:, 2 * w:3 * w] = k
    sr = p_ref[:, 4 * w:4 * w + 128]
    beta_all = _sigmoid(sr)
    g_all = -jnp.exp(alog_ref[...]) * _softplus(sr + dt_ref[...])
    for d, dp_ref in enumerate((dp0_ref, dp1_ref)):
        beta = _mm(beta_all, eb_ref[d], mode="f32")
        g = _mm(g_all, ea_ref[d], mode="f32")
        kb = k * beta
        dp_ref[:, 0:w] = g
        dp_ref[:, w:2 * w] = kb
        dp_ref[:, 2 * w:3 * w] = -jnp.exp(g) * kb
    fin_ref[...] = p_ref[:, 3 * w:4 * w]


def _gdn_prep(p, n_ctx_rows, conv_w, alog_vec, dt_vec, eb, ea):
    batch, rows, width = p.shape
    w = BRANCH_W
    ones = jnp.asarray(_head_block_ones())
    consts = [conv_w, alog_vec, dt_vec, eb, ea, ones]
    out = lambda n: jax.ShapeDtypeStruct((batch, rows, n * w), F32)
    ospec = lambda n: pl.BlockSpec((None, PREP_ROWS, n * w), lambda b, i: (b, i, 0))
    return pl.pallas_call(
        functools.partial(_gdn_prep_kernel, n_ctx_blocks=n_ctx_rows // PREP_ROWS),
        grid=(batch, rows // PREP_ROWS),
        in_specs=_halo_specs(width, rows) + [_vec_spec(c.shape) for c in consts],
        out_specs=[ospec(3), ospec(3), ospec(3), ospec(1)],
        out_shape=[out(3), out(3), out(3), out(1)],
        compiler_params=_cparams(("parallel", "parallel")),
        name="gdn_prep",
    )(p, p, p, *consts)


def _gla_prep_kernel(p_ref, gw_ref, gb_ref, lane_ref, sh_ref, dp0_ref, dp1_ref, fin_ref, *, mode):
    w = BRANCH_W
    sh_ref[:, 0:w] = p_ref[:, 0:w] * (C_DK ** -0.5)
    sh_ref[:, w:2 * w] = p_ref[:, 2 * w:3 * w]
    sh_ref[:, 2 * w:3 * w] = p_ref[:, w:2 * w]
    lo = p_ref[:, 4 * w:4 * w + 128]
    for d, dp_ref in enumerate((dp0_ref, dp1_ref)):
        z = _mm(lo, gw_ref[d], mode=mode) + gb_ref[d:d + 1, :]
        dp_ref[...] = (-_softplus(-z) / C_GATE_NORM) * lane_ref[...]
    fin_ref[...] = p_ref[:, 3 * w:4 * w]


def _gla_prep(p, gwp, gbp, lane_mask, mode="bf16"):
    batch, rows, width = p.shape
    w = BRANCH_W
    consts = [gwp, gbp, lane_mask]
    out = lambda n: jax.ShapeDtypeStruct((batch, rows, n * w), F32)
    ospec = lambda n: pl.BlockSpec((None, PREP_ROWS, n * w), lambda b, i: (b, i, 0))
    return pl.pallas_call(
        functools.partial(_gla_prep_kernel, mode=mode),
        grid=(batch, rows // PREP_ROWS),
        in_specs=[pl.BlockSpec((None, PREP_ROWS, width), lambda b, i: (b, i, 0))]
        + [_vec_spec(c.shape) for c in consts],
        out_specs=[ospec(3), ospec(1), ospec(1), ospec(1)],
        out_shape=[out(3), out(1), out(1), out(1)],
        compiler_params=_cparams(("parallel", "parallel")),
        name="gla_prep",
    )(p, *consts)


def _attn_kernel(sink_ref, q_ref, kc_ref, vc_ref, kp_ref, ko_ref, kn_ref, vp_ref, vo_ref, vn_ref,
                 co_ref, so_ref, cp_ref, sp_ref, cn_ref, sn_ref, o_ref, *, n_ctx_rows, n_lat_rows):
    t = pl.program_id(1)
    blk = ATT_BLOCK

    def rope(x, cos, sin):
        width = x.shape[-1]
        lane = lax.broadcasted_iota(jnp.int32, x.shape, 1)
        swapped = jnp.where(lane % 32 < 16, pltpu.roll(x, width - 16, axis=1), pltpu.roll(x, 16, axis=1))
        return x * cos[:, 0:width] + swapped * sin[:, 0:width]

    q = rope(q_ref[...], co_ref[...], so_ref[...])
    k_band = [rope(kp_ref[...], cp_ref[...], sp_ref[...]),
              rope(ko_ref[...], co_ref[...], so_ref[...]),
              rope(kn_ref[...], cn_ref[...], sn_ref[...])]
    v_band = [vp_ref[...], vo_ref[...], vn_ref[...]]

    qi = lax.broadcasted_iota(jnp.int32, (2 * blk, blk), 0) % blk
    ki = lax.broadcasted_iota(jnp.int32, (2 * blk, blk), 1)
    qpos = t * blk + qi - n_ctx_rows
    valid = []
    for j in range(3):
        kpos = (t - 1 + j) * blk + ki - n_ctx_rows
        valid.append((jnp.abs(qpos - kpos) <= WINDOW) & (kpos >= 0) & (kpos < n_lat_rows) & (qpos >= 0))
    lane = lax.broadcasted_iota(jnp.int32, (blk, 2 * HEAD), 1)
    row2 = lax.broadcasted_iota(jnp.int32, (2 * blk, 1), 0)
    scale = HEAD ** -0.5
    for g in range(D_KV_HEADS):
        cols = slice(g * 2 * HEAD, (g + 1) * 2 * HEAD)
        qg = q[:, cols]
        qs = jnp.concatenate([jnp.where(lane < HEAD, qg, 0.0), jnp.where(lane >= HEAD, qg, 0.0)], axis=0)
        s_ctx = _mm(qs, kc_ref[:, cols], NT) * scale
        s_band = [jnp.where(valid[j], _mm(qs, k_band[j][:, cols], NT) * scale, -1e30) for j in range(3)]
        sink = jnp.where(row2 < blk, sink_ref[2 * g], sink_ref[2 * g + 1])
        m = jnp.maximum(jnp.max(s_ctx, axis=-1, keepdims=True), sink)
        for s in s_band:
            m = jnp.maximum(m, jnp.max(s, axis=-1, keepdims=True))
        p_ctx = jnp.exp(s_ctx - m)
        den = jnp.sum(p_ctx, axis=-1, keepdims=True) + jnp.exp(sink - m)
        acc = _mm(p_ctx, vc_ref[:, cols], NN)
        for j in range(3):
            pj = jnp.exp(s_band[j] - m)
            den = den + jnp.sum(pj, axis=-1, keepdims=True)
            acc = acc + _mm(pj, v_band[j][:, cols], NN)
        og = acc / den
        o_ref[:, cols] = jnp.where(lane < HEAD, og[0:blk], og[blk:2 * blk])


def _attention(p, sink, cos, sin, n_ctx_rows):
    batch, rows, _ = p.shape
    blk = ATT_BLOCK
    nb = rows // blk
    w = BRANCH_W

    def band(col, off):
        def index(b, t):
            return (b, jnp.clip(t + off, 0, nb - 1), col)
        return pl.BlockSpec((None, blk, w), index)

    def tab(off):
        return pl.BlockSpec((blk, w), lambda b, t: (jnp.clip(t + off, 0, nb - 1), 0))

    ctx = lambda col: pl.BlockSpec((None, n_ctx_rows, w), lambda b, t: (b, 0, col))
    kern = functools.partial(_attn_kernel, n_ctx_rows=n_ctx_rows, n_lat_rows=rows - n_ctx_rows)
    return pl.pallas_call(
        kern,
        grid=(batch, nb),
        in_specs=[pl.BlockSpec(memory_space=pltpu.SMEM), band(0, 0), ctx(1), ctx(2),
                  band(1, -1), band(1, 0), band(1, 1), band(2, -1), band(2, 0), band(2, 1),
                  tab(0), tab(0), tab(-1), tab(-1), tab(1), tab(1)],
        out_specs=pl.BlockSpec((None, blk, w), lambda b, t: (b, t, 0)),
        out_shape=jax.ShapeDtypeStruct((batch, rows, w), F32),
        compiler_params=_cparams(("parallel", "parallel")),
        name="window_attn",
    )(sink, p, p, p, p, p, p, p, p, p, cos, sin, cos, sin, cos, sin)


def _merge_kernel(x_ref, sc_ref, csc_ref, sh_ref, csh_ref, gm_ref, cgm_ref, g0_ref, g1_ref,
                  ya_ref, yb_ref, yc_ref, yd_ref, wg_ref, gb_ref, wb_ref, wo_ref, o_ref, *, n_ctx_rows):
    x = x_ref[...]
    is_ctx = _ctx_rows(x.shape[0], n_ctx_rows)
    scale = jnp.where(is_ctx, csc_ref[...], sc_ref[...])
    shift = jnp.where(is_ctx, csh_ref[...], sh_ref[...])
    gmod = jnp.where(is_ctx, cgm_ref[...], gm_ref[...])
    h = _prenorm(x, g0_ref[...], scale, shift).astype(BF16)
    acc = jnp.zeros(x.shape, F32)
    for i, y_ref in enumerate((ya_ref, yb_ref, yc_ref, yd_ref)):
        pre = jnp.dot(h, wg_ref[:, i * D_MODEL:(i + 1) * D_MODEL], preferred_element_type=F32)
        gate = _sigmoid(pre + gb_ref[i:i + 1, :])
        acc = acc + gate * jnp.dot(y_ref[...].astype(BF16), wb_ref[i], preferred_element_type=F32)
    out = jnp.dot(acc.astype(BF16), wo_ref[...], preferred_element_type=F32)
    ms = jnp.mean(out * out, axis=-1, keepdims=True)
    o_ref[...] = x + gmod * (out * lax.rsqrt(ms + NORM_EPS) * g1_ref[...])


def _merge(xs, mod, gain0, gain1, ys, wg, gate_b, wb, wo, n_ctx_rows):
    batch, rows, d = xs.shape
    tm = _row_tile(rows, 272)
    tile = lambda width: pl.BlockSpec((None, tm, width), lambda b, i: (b, i, 0))
    consts = [wg, gate_b, wb, wo]
    return pl.pallas_call(
        functools.partial(_merge_kernel, n_ctx_rows=n_ctx_rows),
        grid=(batch, rows // tm),
        in_specs=[tile(d)] + _mod_specs(1, batch) + _mod_specs(0, batch) + _mod_specs(2, batch)
        + [_vec_spec(gain0.shape), _vec_spec(gain1.shape)]
        + [tile(BRANCH_W)] * 4 + [_vec_spec(c.shape) for c in consts],
        out_specs=tile(d),
        out_shape=jax.ShapeDtypeStruct((batch, rows, d), F32),
        compiler_params=_cparams(("parallel", "parallel")),
        name="merge_out",
    )(xs, *([mod] * 6), gain0, gain1, *ys, *consts)


def _ffn_kernel(x_ref, sc_ref, csc_ref, sh_ref, csh_ref, gm_ref, cgm_ref, g2_ref, g3_ref, w1g_ref, w1u_ref, w2_ref,
                o_ref, h_ref, acc_ref, *, n_ctx_rows):
    j = pl.program_id(2)
    is_ctx = _ctx_rows(x_ref.shape[0], n_ctx_rows)

    @pl.when(j == 0)
    def _():
        scale = jnp.where(is_ctx, csc_ref[...], sc_ref[...])
        shift = jnp.where(is_ctx, csh_ref[...], sh_ref[...])
        h_ref[...] = _prenorm(x_ref[...], g2_ref[...], scale, shift).astype(BF16)
        acc_ref[...] = jnp.zeros_like(acc_ref)

    h = h_ref[...]
    gt = jnp.dot(h, w1g_ref[...], preferred_element_type=F32)
    up = jnp.dot(h, w1u_ref[...], preferred_element_type=F32)
    acc_ref[...] += jnp.dot((_silu(gt) * up).astype(BF16), w2_ref[...], preferred_element_type=F32)

    @pl.when(j == pl.num_programs(2) - 1)
    def _():
        out = acc_ref[...]
        ms = jnp.mean(out * out, axis=-1, keepdims=True)
        gmod = jnp.where(is_ctx, cgm_ref[...], gm_ref[...])
        o_ref[...] = x_ref[...] + gmod * (out * lax.rsqrt(ms + NORM_EPS) * g3_ref[...])


def _ffn(xs, mod, gain2, gain3, w1, w2, n_ctx_rows, th=1408):
    batch, rows, d = xs.shape
    hidden = w2.shape[0]
    tm = _row_tile(rows, 544)
    nh = hidden // th
    tile = pl.BlockSpec((None, tm, d), lambda b, i, j: (b, i, 0))
    return pl.pallas_call(
        functools.partial(_ffn_kernel, n_ctx_rows=n_ctx_rows),
        grid=(batch, rows // tm, nh),
        in_specs=[tile] + _mod_specs(4, batch) + _mod_specs(3, batch) + _mod_specs(5, batch)
        + [pl.BlockSpec((1, d), lambda b, i, j: (0, 0)), pl.BlockSpec((1, d), lambda b, i, j: (0, 0)),
                  pl.BlockSpec((d, th), lambda b, i, j: (0, j)),
                  pl.BlockSpec((d, th), lambda b, i, j: (0, nh + j)),
                  pl.BlockSpec((th, d), lambda b, i, j: (j, 0))],
        out_specs=tile,
        out_shape=jax.ShapeDtypeStruct((batch, rows, d), F32),
        scratch_shapes=[pltpu.VMEM((tm, d), BF16), pltpu.VMEM((tm, d), F32)],
        compiler_params=_cparams(("parallel", "parallel", "arbitrary")),
        name="swiglu",
    )(xs, *([mod] * 6), gain2, gain3, w1, w1, w2)


def _pad_cols(w, width):
    return jnp.pad(w, ((0, 0), (0, width - w.shape[1])))


def _layer_weights(w_in, mu, w2, a2, g2, conv, a_log, dt_bias, gw2, gb):
    wts = {}
    off_b, off_c, off_d, off_g = A_IN, A_IN + B_IN, A_IN + B_IN + C_IN, A_IN + B_IN + C_IN + D_IN
    w = BRANCH_W
    wts["wa"] = _pad_cols(w_in[:, 0:A_IN], 4 * w).astype(BF16)
    wts["mu"] = _pad_cols(mu[None, :], 4 * w)
    lo = np.cumsum((0,) + A_LORA)
    place = lambda m, r0: jnp.zeros((w, w), F32).at[r0:r0 + m.shape[0], :].set(m)
    wts["w2p"] = jnp.stack([place(w2[0], lo[0]), place(w2[1], lo[1])])
    wts["a2p"] = jnp.stack([place(a2[0], lo[2]), place(a2[1], lo[3])])
    wts["g2p"] = place(g2, lo[4])

    wb = w_in[:, off_b:off_b + B_IN]
    wts["wb"] = jnp.concatenate([wb[:, 0:3 * w], wb[:, 3 * w + 16:], _pad_cols(wb[:, 3 * w:3 * w + 16], 128)],
                                axis=1).astype(BF16)
    nh = N_HEADS
    expand = np.zeros((4, 128, w), np.float32)
    for grp in range(4):
        for h in range(nh):
            expand[grp, grp * nh + h, h * HEAD:(h + 1) * HEAD] = 1.0
    wts["eb"] = jnp.asarray(expand[0:2])
    wts["ea"] = jnp.asarray(expand[2:4])
    vec = lambda t: jnp.zeros((1, 128), F32).at[0, 2 * nh:4 * nh].set(t.reshape(-1))
    wts["alog"] = vec(a_log)
    wts["dt"] = vec(dt_bias)
    wts["conv"] = conv

    wc = w_in[:, off_c:off_c + C_IN]
    pad_heads = lambda m: jnp.pad(m.reshape(m.shape[0], nh, C_DK),
                                  ((0, 0), (0, 0), (0, HEAD - C_DK))).reshape(m.shape[0], w)
    qc, kc, vc = wc[:, 0:C_QK], wc[:, C_QK:2 * C_QK], wc[:, 2 * C_QK:2 * C_QK + w]
    loc = wc[:, 2 * C_QK + w:2 * C_QK + w + 2 * C_GATE_R]
    gc = wc[:, 2 * C_QK + w + 2 * C_GATE_R:]
    wts["wc"] = jnp.concatenate([pad_heads(qc), pad_heads(kc), vc, gc, _pad_cols(loc, 128)], axis=1).astype(BF16)
    gwp = jnp.zeros((2, 128, w), F32)
    for d in range(2):
        gwp = gwp.at[d, d * C_GATE_R:(d + 1) * C_GATE_R, :].set(pad_heads(gw2[d]))
    wts["gwp"] = gwp
    wts["gbp"] = pad_heads(gb)
    wts["glane"] = jnp.asarray((np.arange(w) % HEAD < C_DK).astype(np.float32))[None, :]

    wd = w_in[:, off_d:off_d + D_IN]
    qd = wd[:, 0:w]
    dup = lambda m: jnp.concatenate([m[:, 0:HEAD], m[:, 0:HEAD], m[:, HEAD:], m[:, HEAD:]], axis=1)
    wts["wd"] = jnp.concatenate([qd, dup(wd[:, w:w + 2 * HEAD]), dup(wd[:, w + 2 * HEAD:])], axis=1).astype(BF16)
    wts["wg"] = w_in[:, off_g:].astype(BF16)
    return wts


def _rope_tables(n_ctx_rows, n_lat_rows):
    quarter = HEAD // 4
    inv = ROPE_BASE ** (-np.arange(quarter, dtype=np.float32) / quarter)
    pos = np.arange(n_lat_rows)
    rows = (pos // GRID_W).astype(np.float32)
    cols = (pos % GRID_W).astype(np.float32)
    inv = jnp.asarray(inv)
    ang_r = jnp.asarray(rows)[:, None] * inv[None, :]
    ang_c = jnp.asarray(cols)[:, None] * inv[None, :]
    cos = jnp.concatenate([jnp.cos(ang_r)] * 2 + [jnp.cos(ang_c)] * 2, axis=1)
    sin = jnp.concatenate([-jnp.sin(ang_r), jnp.sin(ang_r), -jnp.sin(ang_c), jnp.sin(ang_c)], axis=1)
    cos = jnp.concatenate([jnp.ones((n_ctx_rows, HEAD), F32), cos], axis=0)
    sin = jnp.concatenate([jnp.zeros((n_ctx_rows, HEAD), F32), sin], axis=0)
    return jnp.tile(cos, (1, N_HEADS)), jnp.tile(sin, (1, N_HEADS))


RWKV_SLOTS = {"r": ("sh", 0), "v": ("sh", 1), "a": ("sh", 2), "w": ("dp", 0), "k": ("dp", 1), "b": ("dp", 2)}
GDN_SLOTS = RWKV_SLOTS
GLA_SLOTS = {"r": ("sh", 0), "v": ("sh", 1), "k": ("sh", 2), "w": ("dp", 0)}


def kernel(x, c, ctx, c_ctx, ada_w, ada_b, norm_g, w_in, gate_b, w_branch, w_out, rwkv_mu, rwkv_w0, rwkv_w2, rwkv_a0, rwkv_a2, rwkv_g2, rwkv_kk, rwkv_ka, rwkv_rk, rwkv_ln_g, rwkv_ln_b, gdn_conv, gdn_a_log, gdn_dt_bias, gdn_norm_g, gla_gw2, gla_gb, gla_norm_g, attn_sink, ffn_w1, ffn_w2):
    batch, n_lat, d = x.shape
    n_ctx = ctx.shape[1]
    depth = ada_w.shape[0]
    assert n_ctx % PREP_ROWS == 0 and n_lat % PREP_ROWS == 0 and d == D_MODEL

    mod_rows = 8 * ((batch + 1 + 7) // 8)
    c_rows = jnp.concatenate([c, c_ctx[None, :], jnp.zeros((mod_rows - batch - 1, d), F32)], axis=0)
    mod_all = _modulation(c_rows, ada_w, ada_b)
    cos, sin = _rope_tables(n_ctx, n_lat)

    xs = jnp.concatenate([ctx, x], axis=1)
    row = lambda t: t.reshape(1, -1)
    for l in range(depth):
        mod = mod_all[l].reshape(mod_rows, 1, 6 * d)
        ng = norm_g[l]
        wts = _layer_weights(w_in[l], rwkv_mu[l], rwkv_w2[l], rwkv_a2[l], rwkv_g2[l], gdn_conv[l],
                             gdn_a_log[l], gdn_dt_bias[l], gla_gw2[l], gla_gb[l])
        proj = lambda w: _project(xs, mod, row(ng[0]), w, n_ctx, 1, 0)

        sh, dp0, dp1, fin = _rwkv_prep(proj(wts["wa"]), n_ctx, wts["mu"], wts["w2p"], wts["a2p"], wts["g2p"],
                                       rwkv_w0[l], rwkv_a0[l], row(rwkv_kk[l]), row(rwkv_ka[l]),
                                       row(rwkv_rk[l]))
        ya = _bidir_scan(sh, (dp0, dp1), n_ctx, fin, row(rwkv_ln_g[l]), row(rwkv_ln_b[l]), finish="groupnorm",
                         scalar_decay=False, lowrank=True, slots=RWKV_SLOTS)

        sh, dp0, dp1, fin = _gdn_prep(proj(wts["wb"]), n_ctx, wts["conv"], wts["alog"], wts["dt"], wts["eb"],
                                      wts["ea"])
        gnorm = row(jnp.tile(gdn_norm_g[l], N_HEADS))
        yb = _bidir_scan(sh, (dp0, dp1), n_ctx, fin, gnorm, gnorm, finish="rms",
                         scalar_decay=True, lowrank=True, slots=GDN_SLOTS)

        sh, dp0, dp1, fin = _gla_prep(proj(wts["wc"]), wts["gwp"], wts["gbp"], wts["glane"])
        cnorm = row(jnp.tile(gla_norm_g[l], N_HEADS))
        yc = _bidir_scan(sh, (dp0, dp1), n_ctx, fin, cnorm, cnorm, finish="rms",
                         scalar_decay=False, lowrank=False, slots=GLA_SLOTS)

        yd = _attention(proj(wts["wd"]), attn_sink[l], cos, sin, n_ctx)

        xs = _merge(xs, mod, row(ng[0]), row(ng[1]), (ya, yb, yc, yd), wts["wg"], gate_b[l],
                    w_branch[l].astype(BF16), w_out[l].astype(BF16), n_ctx)
        xs = _ffn(xs, mod, row(ng[2]), row(ng[3]), ffn_w1[l].astype(BF16), ffn_w2[l].astype(BF16), n_ctx)
    return xs[:, n_ctx:, :]
```

```python
import functools
import math

import numpy as np
import jax
import jax.numpy as jnp
from jax import lax
from jax.experimental import pallas as pl
from jax.experimental.pallas import tpu as pltpu

F32 = jnp.float32
BF16 = jnp.bfloat16

D_MODEL = 1024
N_BRANCH = 4
BRANCH_W = 256
HEAD = 64
N_HEADS = BRANCH_W // HEAD
NORM_EPS = 1e-6
A_GN_EPS = 64e-5
A_LORA = (32, 32, 32, 32, 64)
A_IN = 3 * BRANCH_W + sum(A_LORA)
B_IN = 4 * BRANCH_W + 4 * N_HEADS
C_DK = 32
C_QK = N_HEADS * C_DK
C_GATE_R = 16
C_GATE_NORM = 16.0
C_IN = 2 * C_QK + 2 * BRANCH_W + 2 * C_GATE_R
D_KV_HEADS = 2
D_IN = BRANCH_W + 2 * D_KV_HEADS * HEAD
B_CONV = 7
WINDOW = 128
ROPE_BASE = 10000.0
GRID_W = 64
FFN_HIDDEN = 2816

CHUNK = 64
PREP_ROWS = 256
HALO = 8
ATT_BLOCK = 128
VMEM_LIMIT = 48 * 1024 * 1024

NN = (((1,), (0,)), ((), ()))
NT = (((1,), (1,)), ((), ()))
TN = (((0,), (0,)), ((), ()))


def _mm(a, b, dims=NN, mode="bf16"):
    if mode == "f32":
        return lax.dot_general(a, b, dims, precision=lax.Precision.HIGHEST, preferred_element_type=F32)
    if mode == "x3":
        ah = a.astype(BF16)
        al = (a - ah.astype(F32)).astype(BF16)
        bh = b.astype(BF16)
        bl = (b - bh.astype(F32)).astype(BF16)
        dot = functools.partial(lax.dot_general, dimension_numbers=dims, preferred_element_type=F32)
        return dot(ah, bh) + (dot(ah, bl) + dot(al, bh))
    return lax.dot_general(a.astype(BF16), b.astype(BF16), dims, preferred_element_type=F32)


def _sigmoid(x):
    return 1.0 / (1.0 + jnp.exp(-x))


def _silu(x):
    return x * _sigmoid(x)


def _softplus(x):
    return jnp.maximum(x, 0.0) + jnp.log1p(jnp.exp(-jnp.abs(x)))


def _cparams(sem):
    return pltpu.CompilerParams(dimension_semantics=sem, vmem_limit_bytes=VMEM_LIMIT)


def _mod_kernel(c_ref, w_ref, b_ref, o_ref):
    c = c_ref[...]
    o_ref[...] = _mm(_silu(c), w_ref[...], mode="f32") + b_ref[...]


def _modulation(c_rows, ada_w, ada_b):
    depth, d, n = ada_w.shape
    rows = c_rows.shape[0]
    tn = 1024
    return pl.pallas_call(
        _mod_kernel,
        grid=(depth, n // tn),
        in_specs=[
            pl.BlockSpec((rows, d), lambda l, j: (0, 0)),
            pl.BlockSpec((None, d, tn), lambda l, j: (l, 0, j)),
            pl.BlockSpec((None, 1, tn), lambda l, j: (l, 0, j)),
        ],
        out_specs=pl.BlockSpec((None, rows, tn), lambda l, j: (l, 0, j)),
        out_shape=jax.ShapeDtypeStruct((depth, rows, n), F32),
        compiler_params=_cparams(("parallel", "parallel")),
        name="adaln_mod",
    )(c_rows, ada_w, ada_b.reshape(depth, 1, n))


def _row_tile(rows, target):
    return max(t for t in range(8, target + 1, 8) if rows % t == 0)


def _mod_specs(which, batch):
    lat = pl.BlockSpec((None, 1, D_MODEL), lambda b, i, *_: (b, 0, which))
    ctx = pl.BlockSpec((None, 1, D_MODEL), lambda b, i, *_: (batch, 0, which))
    return [lat, ctx]


def _ctx_rows(tm, n_ctx_rows):
    row = pl.program_id(1) * tm + lax.broadcasted_iota(jnp.int32, (tm, 1), 0)
    return row < n_ctx_rows


def _prenorm(x, gain, scale, shift):
    ms = jnp.mean(x * x, axis=-1, keepdims=True)
    y = x * lax.rsqrt(ms + NORM_EPS) * gain
    return y * (1.0 + scale) + shift


def _proj_kernel(x_ref, sc_ref, csc_ref, sh_ref, csh_ref, g_ref, w_ref, o_ref, h_ref, *, n_ctx_rows):
    @pl.when(pl.program_id(2) == 0)
    def _():
        is_ctx = _ctx_rows(x_ref.shape[0], n_ctx_rows)
        scale = jnp.where(is_ctx, csc_ref[...], sc_ref[...])
        shift = jnp.where(is_ctx, csh_ref[...], sh_ref[...])
        h_ref[...] = _prenorm(x_ref[...], g_ref[...], scale, shift).astype(BF16)

    o_ref[...] = jnp.dot(h_ref[...], w_ref[...], preferred_element_type=F32)


def _project(xs, mod, gain, w, n_ctx_rows, sc_idx, sh_idx):
    batch, rows, d = xs.shape
    n = w.shape[1]
    tm = _row_tile(rows, 1088)
    tn = n if n <= 1152 else next(t for t in (1024, 512, 384, 256, 128) if n % t == 0)
    return pl.pallas_call(
        functools.partial(_proj_kernel, n_ctx_rows=n_ctx_rows),
        grid=(batch, rows // tm, n // tn),
        in_specs=[pl.BlockSpec((None, tm, d), lambda b, i, j: (b, i, 0))]
        + _mod_specs(sc_idx, batch) + _mod_specs(sh_idx, batch)
        + [pl.BlockSpec((1, d), lambda b, i, j: (0, 0)),
           pl.BlockSpec((d, tn), lambda b, i, j: (0, j))],
        out_specs=pl.BlockSpec((None, tm, tn), lambda b, i, j: (b, i, j)),
        out_shape=jax.ShapeDtypeStruct((batch, rows, n), F32),
        scratch_shapes=[pltpu.VMEM((tm, d), BF16)],
        compiler_params=_cparams(("parallel", "parallel", "arbitrary")),
        name="prenorm_proj",
    )(xs, mod, mod, mod, mod, gain, w)


def _head_block_ones():
    idx = np.arange(BRANCH_W)
    return (idx[:, None] // HEAD == idx[None, :] // HEAD).astype(np.float32)


N_LEVELS = 6


def _level_mask(ri, ci, s, reverse):
    b = 1 << s
    blk = (ri // (2 * b)) == (ci // (2 * b))
    hi_r, hi_c = (ri // b) % 2 == 1, (ci // b) % 2 == 1
    return blk & ((hi_c & ~hi_r) if reverse else (hi_r & ~hi_c))


SCAN_HEADS = 2
GROUP_W = SCAN_HEADS * HEAD
N_GROUPS = N_HEADS // SCAN_HEADS


def _scan_masks(reverse):
    n = SCAN_HEADS * CHUNK
    ri = np.arange(CHUNK)[:, None]
    ci = np.arange(n)[None, :] % CHUNK
    strict, incl = (ci > ri, ci >= ri) if reverse else (ci < ri, ci <= ri)
    compact = [strict, incl, ci == ri] + [_level_mask(ri, ci, s, reverse) for s in range(N_LEVELS)]
    r = np.arange(n)[:, None]
    c = np.arange(n)[None, :]
    block = [(r // CHUNK) == (c // CHUNK), r == c]
    return np.stack(compact).astype(np.float32), np.stack(block).astype(np.float32)


EXACT_TERMS = 3


def _tri(reverse):
    i = np.arange(CHUNK)
    m = (i[None, :] >= i[:, None]) if reverse else (i[None, :] <= i[:, None])
    return np.tile(m.astype(np.float32), (1, EXACT_TERMS))


def _split_terms(x, terms):
    out = []
    for _ in range(terms - 1):
        part = x.astype(BF16)
        out.append(part)
        x = x - part.astype(F32)
    out.append(x.astype(BF16))
    return out


def _mm_exact(a, b, dims=NN, split="a", terms=EXACT_TERMS):
    assert dims == NN
    if split == "a":
        lhs = jnp.concatenate(_split_terms(a, terms), axis=1)
        rhs = jnp.concatenate([b.astype(BF16)] * terms, axis=0)
    else:
        lhs = a.astype(BF16)
        rhs = jnp.concatenate(_split_terms(b, terms), axis=0)
    return lax.dot_general(lhs, rhs, dims, preferred_element_type=F32)


C_STRICT, C_INCL, C_EYE, C_LVL0 = 0, 1, 2, 3
B_SAME, B_EYE = 0, 1
SCAN_BATCH = 8
assert CHUNK == HEAD and (1 << N_LEVELS) == CHUNK


def _scan_chunk(vals, st, tri, ones, cm_ref, bm_ref, same_bf, *, reverse, scalar_decay, lowrank, mm_mode):
    same = bm_ref[B_SAME]

    def get(name):
        return vals[name]

    def expand(x):
        return jnp.concatenate([x.astype(BF16)] * SCAN_HEADS, axis=0) * same_bf

    def keep(x, k):
        return jnp.where(cm_ref[k] > 0.5, x, 0.0)

    logw = get("w")
    r, k, v = get("r"), get("k"), get("v")
    cum = _mm_exact(tri, logw, split="b")
    cum_x = cum - logw
    last = 0 if reverse else CHUNK - 1
    total = cum[last:last + 1, :]
    to_end = jnp.exp(total - cum)

    if scalar_decay:
        diag = jnp.concatenate([cum] * SCAN_HEADS, axis=0) * bm_ref[B_EYE]
        cum_row = _mm_exact(ones, diag, split="b")
        d_ii = jnp.exp(jnp.where(cm_ref[C_INCL] > 0.5, cum - cum_row, -1e30))
        r_q, k_q = r, k
    else:
        ref_row = cum[CHUNK // 2:CHUNK // 2 + 1, :]
        p_inv = jnp.exp(ref_row - cum)
        r_q, k_q = r * jnp.exp(cum - ref_row), k * p_inv

    k_e, v_e = expand(k_q), expand(v)
    r_abs = r * jnp.exp(cum)
    if not lowrank:
        s_k = _mm(r_q, k_e, NT, mm_mode)
        a_rk = s_k * d_ii if scalar_decay else keep(s_k, C_INCL)
        y = _mm(r_abs, st, NT, mm_mode) + _mm(a_rk, v_e, NN, mm_mode)
        upd = _mm(v, k * to_end, TN, mm_mode)
        return y, st * jnp.exp(total) + upd * same

    a, b = get("a"), get("b")
    if scalar_decay:
        d_xi = jnp.exp(jnp.where(cm_ref[C_STRICT] > 0.5, cum_x - cum_row, -1e30))
        a_q, b_q = a, b
    else:
        a_q, b_q = a * jnp.exp(cum_x - ref_row), b * p_inv
    lhs = jnp.concatenate([a_q, r_q], axis=0)
    s_b = _mm(lhs, expand(b_q), NT, mm_mode)
    s_k = _mm(lhs, k_e, NT, mm_mode)
    if scalar_decay:
        a_ab, a_rb = s_b[0:CHUNK] * d_xi, s_b[CHUNK:] * d_ii
        a_ak, a_rk = s_k[0:CHUNK] * d_xi, s_k[CHUNK:] * d_ii
    else:
        a_ab, a_rb = keep(s_b[0:CHUNK], C_STRICT), keep(s_b[CHUNK:], C_INCL)
        a_ak, a_rk = keep(s_k[0:CHUNK], C_STRICT), keep(s_k[CHUNK:], C_INCL)

    inv = cm_ref[C_EYE] + a_ab * cm_ref[C_LVL0]
    for s in range(1, N_LEVELS):
        c_s = expand(a_ab * cm_ref[C_LVL0 + s])
        inv = inv + _mm(_mm(inv, c_s, NN, mm_mode), expand(inv), NN, mm_mode)

    a_abs = a * jnp.exp(cum_x)
    from_state = _mm(jnp.concatenate([a_abs, r_abs], axis=0), st, NT, mm_mode)
    from_v = _mm(jnp.concatenate([a_ak, a_rk], axis=0), v_e, NN, mm_mode)
    both = from_state + from_v
    z = _mm(inv, expand(both[0:CHUNK]), NN, mm_mode)
    y = both[CHUNK:] + _mm(a_rb, expand(z), NN, mm_mode)
    upd = _mm(jnp.concatenate([v, z], axis=0), jnp.concatenate([k * to_end, b * to_end], axis=0), TN, mm_mode)
    return y, st * jnp.exp(total) + upd * same


def _scan_kernel(*refs, reverse, scalar_decay, lowrank, slots, finish, mm_mode):
    it = iter(refs)
    sh_ref, dp_ref, cm_ref, bm_ref, tri_ref = next(it), next(it), next(it), next(it), next(it)
    if finish:
        ob_ref, fin_ref, p1_ref, p2_ref, avg_ref = next(it), next(it), next(it), next(it), next(it)
    o_ref, st_ref = next(it), next(it)

    @pl.when(pl.program_id(1) == 0)
    def _():
        st_ref[...] = jnp.zeros_like(st_ref)

    nb = sh_ref.shape[0]
    n = nb * N_GROUPS

    def groups(name):
        src, idx = slots[name]
        ref = sh_ref if src == "sh" else dp_ref
        parts = [ref[:, :, idx * BRANCH_W + g * GROUP_W:idx * BRANCH_W + (g + 1) * GROUP_W]
                 for g in range(N_GROUPS)]
        return jnp.stack(parts, axis=1).reshape(n, CHUNK, GROUP_W)

    vals = {name: groups(name) for name in slots}
    tri = jnp.broadcast_to(tri_ref[...], (n,) + tri_ref.shape)
    ones = jnp.ones((n, CHUNK, EXACT_TERMS * SCAN_HEADS * CHUNK), F32)
    chunk = functools.partial(_scan_chunk, cm_ref=cm_ref, bm_ref=bm_ref, same_bf=bm_ref[B_SAME].astype(BF16),
                              reverse=reverse, scalar_decay=scalar_decay, lowrank=lowrank, mm_mode=mm_mode)
    y, st_new = jax.vmap(chunk)(vals, st_ref[...], tri, ones)
    st_ref[...] = st_new
    y = y.reshape(nb, N_GROUPS, CHUNK, GROUP_W)
    y = jnp.concatenate([y[:, g] for g in range(N_GROUPS)], axis=-1)
    if not finish:
        o_ref[...] = y
        return

    avg = avg_ref[...]
    for i in range(nb):
        yi = y[i] + ob_ref[i]
        if finish == "groupnorm":
            gate, bonus = fin_ref[i, :, 0:BRANCH_W], fin_ref[i, :, BRANCH_W:2 * BRANCH_W]
            cen = yi - _mm_exact(yi, avg)
            var = _mm_exact(cen * cen, avg)
            yn = cen * lax.rsqrt(var + A_GN_EPS) * p1_ref[...] + p2_ref[...]
            o_ref[i] = (yn + bonus) * gate
        else:
            ms = _mm_exact(yi * yi, avg)
            o_ref[i] = yi * lax.rsqrt(ms + NORM_EPS) * p1_ref[...] * _silu(fin_ref[i])


def _scan(sh, dp, n_ctx_rows, *, reverse, scalar_decay, lowrank, slots, finish=None, fin_args=None,
          mm_mode="bf16"):
    batch, rows, shw = sh.shape
    dpw = dp.shape[-1]
    nc, nctx = rows // CHUNK, n_ctx_rows // CHUNK
    nb = math.gcd(batch, SCAN_BATCH)

    if reverse:
        def chunk(n):
            return jnp.where(n < nctx, nctx - 1 - n, nc - 1 + nctx - n)
    else:
        def chunk(n):
            return n

    def row_spec(width):
        return pl.BlockSpec((nb, CHUNK, width), lambda b, n: (b, chunk(n), 0))

    def const_spec(shape):
        zeros = (0,) * len(shape)
        return pl.BlockSpec(shape, lambda b, n: zeros)

    cmask, bmask = (jnp.asarray(m) for m in _scan_masks(reverse))
    tri = jnp.asarray(_tri(reverse))
    in_specs = [row_spec(shw), row_spec(dpw), const_spec(cmask.shape), const_spec(bmask.shape),
                const_spec(tri.shape)]
    args = [sh, dp, cmask, bmask, tri]
    if finish:
        ob, fin, p1, p2 = fin_args
        avg = jnp.asarray(_head_block_ones() / HEAD)
        in_specs += [row_spec(BRANCH_W), row_spec(fin.shape[-1]), const_spec(p1.shape), const_spec(p2.shape),
                     const_spec(avg.shape)]
        args += [ob, fin, p1, p2, avg]
    kern = functools.partial(_scan_kernel, reverse=reverse, scalar_decay=scalar_decay, lowrank=lowrank,
                             slots=slots, finish=finish, mm_mode=mm_mode)
    return pl.pallas_call(
        kern,
        grid=(batch // nb, nc),
        in_specs=in_specs,
        out_specs=row_spec(BRANCH_W),
        out_shape=jax.ShapeDtypeStruct((batch, rows, BRANCH_W), F32),
        scratch_shapes=[pltpu.VMEM((nb * N_GROUPS, GROUP_W, GROUP_W), F32)],
        compiler_params=_cparams(("parallel", "arbitrary")),
        name="dplr_scan_" + ("bwd" if reverse else "fwd"),
    )(*args)


def _bidir_scan(sh, dps, n_ctx_rows, fin, p1, p2, *, finish, **kw):
    ob = _scan(sh, dps[1], n_ctx_rows, reverse=True, **kw)
    return _scan(sh, dps[0], n_ctx_rows, reverse=False, finish=finish, fin_args=(ob, fin, p1, p2), **kw)


def _halo_specs(width, rows, col):
    per = PREP_ROWS // HALO
    last = rows // HALO - 1
    own = pl.BlockSpec((None, PREP_ROWS, width), lambda b, i: (b, i, col))
    prev = pl.BlockSpec((None, HALO, width), lambda b, i: (b, jnp.maximum(i * per - 1, 0), col))
    nxt = pl.BlockSpec((None, HALO, width), lambda b, i: (b, jnp.minimum((i + 1) * per, last), col))
    return [own, prev, nxt]


def _seq_edges(n_ctx_blocks):
    i = pl.program_id(1)
    first = (i == 0) | (i == n_ctx_blocks)
    lastb = (i == n_ctx_blocks - 1) | (i == pl.num_programs(1) - 1)
    return first, lastb


def _vec_spec(shape):
    zeros = (0,) * len(shape)
    return pl.BlockSpec(shape, lambda b, i: zeros)


def _rwkv_prep_kernel(p_ref, prev_ref, next_ref, mu_ref, w2_ref, a2_ref, g2_ref, w0_ref, a0_ref, kk_ref,
                      ka_ref, rk_ref, ones_ref, sh_ref, dp0_ref, dp1_ref, fin_ref, *, n_ctx_blocks, mode):
    first, lastb = _seq_edges(n_ctx_blocks)
    x = p_ref[...]
    prev_row = jnp.where(first, 0.0, prev_ref[HALO - 1:HALO, :])
    next_row = jnp.where(lastb, 0.0, next_ref[0:1, :])
    rid = lax.broadcasted_iota(jnp.int32, x.shape, 0)
    x_prev = jnp.where(rid == 0, prev_row, pltpu.roll(x, 1, axis=0))
    x_next = jnp.where(rid == PREP_ROWS - 1, next_row, pltpu.roll(x, PREP_ROWS - 1, axis=0))
    xm = x + (0.5 * (x_prev + x_next) - x) * mu_ref[...]
    w = BRANCH_W
    r, k, v, lo = xm[:, 0:w], xm[:, w:2 * w], xm[:, 2 * w:3 * w], xm[:, 3 * w:4 * w]
    ones = ones_ref[...]
    th, sg = jnp.tanh(lo), _sigmoid(lo)
    gate = _mm(sg, g2_ref[...], mode=mode)
    kx = k * kk_ref[...]
    kk = kx * lax.rsqrt(_mm_exact(kx * kx, ones) + 1e-6)
    sh_ref[:, 0:w] = r
    sh_ref[:, w:2 * w] = v
    sh_ref[:, 2 * w:3 * w] = -kk
    bonus = jnp.zeros_like(v)
    for d, dp_ref in enumerate((dp0_ref, dp1_ref)):
        w_raw = w0_ref[d:d + 1, :] + _mm(th, w2_ref[d], mode=mode)
        logw = -jnp.exp(-_softplus(-w_raw) - 0.5)
        a = _sigmoid(a0_ref[d:d + 1, :] + _mm(lo, a2_ref[d], mode=mode))
        kd = k * (1.0 + (a - 1.0) * ka_ref[...])
        dp_ref[:, 0:w] = logw
        dp_ref[:, w:2 * w] = kd
        dp_ref[:, 2 * w:3 * w] = kk * a
        bonus = bonus + _mm_exact(r * kd * rk_ref[...], ones) * v
    fin_ref[:, 0:w] = gate
    fin_ref[:, w:2 * w] = bonus


def _rwkv_prep(p, n_ctx_rows, mu, w2p, a2p, g2p, w0, a0, kk, ka, rk, mode="bf16", width=None, col=0):
    batch, rows, full = p.shape
    width = width or full
    w = BRANCH_W
    ones = jnp.asarray(_head_block_ones())
    consts = [mu, w2p, a2p, g2p, w0, a0, kk, ka, rk, ones]
    out = lambda n: jax.ShapeDtypeStruct((batch, rows, n * w), F32)
    ospec = lambda n: pl.BlockSpec((None, PREP_ROWS, n * w), lambda b, i: (b, i, 0))
    return pl.pallas_call(
        functools.partial(_rwkv_prep_kernel, n_ctx_blocks=n_ctx_rows // PREP_ROWS, mode=mode),
        grid=(batch, rows // PREP_ROWS),
        in_specs=_halo_specs(width, rows, col) + [_vec_spec(c.shape) for c in consts],
        out_specs=[ospec(3), ospec(3), ospec(3), ospec(2)],
        out_shape=[out(3), out(3), out(3), out(2)],
        compiler_params=_cparams(("parallel", "parallel")),
        name="rwkv_prep",
    )(p, p, p, *consts)


def _gdn_prep_kernel(p_ref, prev_ref, next_ref, conv_ref, alog_ref, dt_ref, eb_ref, ea_ref, ones_ref,
                     sh_ref, dp0_ref, dp1_ref, fin_ref, *, n_ctx_blocks):
    first, lastb = _seq_edges(n_ctx_blocks)
    w = BRANCH_W
    x = p_ref[:, 0:3 * w]
    top = jnp.where(first, 0.0, prev_ref[:, 0:3 * w])
    bot = jnp.where(lastb, 0.0, next_ref[:, 0:3 * w])
    xe = jnp.concatenate([top, x, bot], axis=0)
    ext = PREP_ROWS + 2 * HALO
    acc = jnp.zeros_like(x)
    for s in range(B_CONV):
        shift = (B_CONV // 2 - s) % ext
        rolled = xe if shift == 0 else pltpu.roll(xe, shift, axis=0)
        acc = acc + rolled[HALO:HALO + PREP_ROWS] * conv_ref[s:s + 1, :]
    qkv = _silu(acc)
    ones = ones_ref[...]

    def l2n(t):
        return t * lax.rsqrt(_mm_exact(t * t, ones) + 1e-6)

    q = l2n(qkv[:, 0:w]) * (HEAD ** -0.5)
    k = l2n(qkv[:, w:2 * w])
    v = qkv[:, 2 * w:3 * w]
    sh_ref[:, 0:w] = q
    sh_ref[:, w:2 * w] = v
    sh_ref[:, 2 * w:3 * w] = k
    sr = p_ref[:, 4 * w:4 * w + 128]
    beta_all = _sigmoid(sr)
    g_all = -jnp.exp(alog_ref[...]) * _softplus(sr + dt_ref[...])
    for d, dp_ref in enumerate((dp0_ref, dp1_ref)):
        beta = _mm_exact(beta_all, eb_ref[d])
        g = _mm_exact(g_all, ea_ref[d])
        kb = k * beta
        dp_ref[:, 0:w] = g
        dp_ref[:, w:2 * w] = kb
        dp_ref[:, 2 * w:3 * w] = -jnp.exp(g) * kb
    fin_ref[...] = p_ref[:, 3 * w:4 * w]


def _gdn_prep(p, n_ctx_rows, conv_w, alog_vec, dt_vec, eb, ea, width=None, col=0):
    batch, rows, full = p.shape
    width = width or full
    w = BRANCH_W
    ones = jnp.asarray(_head_block_ones())
    consts = [conv_w, alog_vec, dt_vec, eb, ea, ones]
    out = lambda n: jax.ShapeDtypeStruct((batch, rows, n * w), F32)
    ospec = lambda n: pl.BlockSpec((None, PREP_ROWS, n * w), lambda b, i: (b, i, 0))
    return pl.pallas_call(
        functools.partial(_gdn_prep_kernel, n_ctx_blocks=n_ctx_rows // PREP_ROWS),
        grid=(batch, rows // PREP_ROWS),
        in_specs=_halo_specs(width, rows, col) + [_vec_spec(c.shape) for c in consts],
        out_specs=[ospec(3), ospec(3), ospec(3), ospec(1)],
        out_shape=[out(3), out(3), out(3), out(1)],
        compiler_params=_cparams(("parallel", "parallel")),
        name="gdn_prep",
    )(p, p, p, *consts)


def _gla_prep_kernel(p_ref, gw_ref, gb_ref, lane_ref, sh_ref, dp0_ref, dp1_ref, fin_ref, *, mode):
    w = BRANCH_W
    sh_ref[:, 0:w] = p_ref[:, 0:w] * (C_DK ** -0.5)
    sh_ref[:, w:2 * w] = p_ref[:, 2 * w:3 * w]
    sh_ref[:, 2 * w:3 * w] = p_ref[:, w:2 * w]
    lo = p_ref[:, 4 * w:4 * w + 128]
    for d, dp_ref in enumerate((dp0_ref, dp1_ref)):
        z = _mm(lo, gw_ref[d], mode=mode) + gb_ref[d:d + 1, :]
        dp_ref[...] = (-_softplus(-z) / C_GATE_NORM) * lane_ref[...]
    fin_ref[...] = p_ref[:, 3 * w:4 * w]


def _gla_prep(p, gwp, gbp, lane_mask, mode="bf16", width=None, col=0):
    batch, rows, full = p.shape
    width = width or full
    w = BRANCH_W
    consts = [gwp, gbp, lane_mask]
    out = lambda n: jax.ShapeDtypeStruct((batch, rows, n * w), F32)
    ospec = lambda n: pl.BlockSpec((None, PREP_ROWS, n * w), lambda b, i: (b, i, 0))
    return pl.pallas_call(
        functools.partial(_gla_prep_kernel, mode=mode),
        grid=(batch, rows // PREP_ROWS),
        in_specs=[pl.BlockSpec((None, PREP_ROWS, width), lambda b, i: (b, i, col))]
        + [_vec_spec(c.shape) for c in consts],
        out_specs=[ospec(3), ospec(1), ospec(1), ospec(1)],
        out_shape=[out(3), out(1), out(1), out(1)],
        compiler_params=_cparams(("parallel", "parallel")),
        name="gla_prep",
    )(p, *consts)


def _attn_kernel(sink_ref, q_ref, kc_ref, vc_ref, kp_ref, ko_ref, kn_ref, vp_ref, vo_ref, vn_ref,
                 co_ref, so_ref, cp_ref, sp_ref, cn_ref, sn_ref, o_ref, *, n_ctx_rows, n_lat_rows):
    t = pl.program_id(1)
    blk = ATT_BLOCK

    def rope(x, cos, sin):
        width = x.shape[-1]
        lane = lax.broadcasted_iota(jnp.int32, x.shape, 1)
        swapped = jnp.where(lane % 32 < 16, pltpu.roll(x, width - 16, axis=1), pltpu.roll(x, 16, axis=1))
        return x * cos[:, 0:width] + swapped * sin[:, 0:width]

    q = rope(q_ref[...], co_ref[...], so_ref[...])
    k_band = [rope(kp_ref[...], cp_ref[...], sp_ref[...]),
              rope(ko_ref[...], co_ref[...], so_ref[...]),
              rope(kn_ref[...], cn_ref[...], sn_ref[...])]
    v_band = [vp_ref[...], vo_ref[...], vn_ref[...]]

    qi = lax.broadcasted_iota(jnp.int32, (2 * blk, blk), 0) % blk
    ki = lax.broadcasted_iota(jnp.int32, (2 * blk, blk), 1)
    qpos = t * blk + qi - n_ctx_rows
    valid = []
    for j in range(3):
        kpos = (t - 1 + j) * blk + ki - n_ctx_rows
        valid.append((jnp.abs(qpos - kpos) <= WINDOW) & (kpos >= 0) & (kpos < n_lat_rows) & (qpos >= 0))
    lane = lax.broadcasted_iota(jnp.int32, (blk, 2 * HEAD), 1)
    row2 = lax.broadcasted_iota(jnp.int32, (2 * blk, 1), 0)
    scale = HEAD ** -0.5
    for g in range(D_KV_HEADS):
        cols = slice(g * 2 * HEAD, (g + 1) * 2 * HEAD)
        qg = q[:, cols]
        qs = jnp.concatenate([jnp.where(lane < HEAD, qg, 0.0), jnp.where(lane >= HEAD, qg, 0.0)], axis=0)
        s_ctx = _mm(qs, kc_ref[:, cols], NT) * scale
        s_band = [jnp.where(valid[j], _mm(qs, k_band[j][:, cols], NT) * scale, -1e30) for j in range(3)]
        sink = jnp.where(row2 < blk, sink_ref[2 * g], sink_ref[2 * g + 1])
        m = jnp.maximum(jnp.max(s_ctx, axis=-1, keepdims=True), sink)
        for s in s_band:
            m = jnp.maximum(m, jnp.max(s, axis=-1, keepdims=True))
        p_ctx = jnp.exp(s_ctx - m)
        den = jnp.sum(p_ctx, axis=-1, keepdims=True) + jnp.exp(sink - m)
        acc = _mm(p_ctx, vc_ref[:, cols], NN)
        for j in range(3):
            pj = jnp.exp(s_band[j] - m)
            den = den + jnp.sum(pj, axis=-1, keepdims=True)
            acc = acc + _mm(pj, v_band[j][:, cols], NN)
        og = acc / den
        o_ref[:, cols] = jnp.where(lane < HEAD, og[0:blk], og[blk:2 * blk])


def _attention(p, sink, cos, sin, n_ctx_rows, col0=0):
    batch, rows, _ = p.shape
    blk = ATT_BLOCK
    nb = rows // blk
    w = BRANCH_W

    def band(col, off):
        def index(b, t):
            return (b, jnp.clip(t + off, 0, nb - 1), col0 + col)
        return pl.BlockSpec((None, blk, w), index)

    def tab(off):
        return pl.BlockSpec((blk, w), lambda b, t: (jnp.clip(t + off, 0, nb - 1), 0))

    ctx = lambda col: pl.BlockSpec((None, n_ctx_rows, w), lambda b, t: (b, 0, col0 + col))
    kern = functools.partial(_attn_kernel, n_ctx_rows=n_ctx_rows, n_lat_rows=rows - n_ctx_rows)
    return pl.pallas_call(
        kern,
        grid=(batch, nb),
        in_specs=[pl.BlockSpec(memory_space=pltpu.SMEM), band(0, 0), ctx(1), ctx(2),
                  band(1, -1), band(1, 0), band(1, 1), band(2, -1), band(2, 0), band(2, 1),
                  tab(0), tab(0), tab(-1), tab(-1), tab(1), tab(1)],
        out_specs=pl.BlockSpec((None, blk, w), lambda b, t: (b, t, 0)),
        out_shape=jax.ShapeDtypeStruct((batch, rows, w), F32),
        compiler_params=_cparams(("parallel", "parallel")),
        name="window_attn",
    )(sink, p, p, p, p, p, p, p, p, p, cos, sin, cos, sin, cos, sin)


def _merge_kernel(x_ref, sc_ref, csc_ref, sh_ref, csh_ref, gm_ref, cgm_ref, g0_ref, g1_ref,
                  ya_ref, yb_ref, yc_ref, yd_ref, wg_ref, gb_ref, wb_ref, wo_ref, o_ref, *, n_ctx_rows):
    x = x_ref[...]
    is_ctx = _ctx_rows(x.shape[0], n_ctx_rows)
    scale = jnp.where(is_ctx, csc_ref[...], sc_ref[...])
    shift = jnp.where(is_ctx, csh_ref[...], sh_ref[...])
    gmod = jnp.where(is_ctx, cgm_ref[...], gm_ref[...])
    h = _prenorm(x, g0_ref[...], scale, shift).astype(BF16)
    acc = jnp.zeros(x.shape, F32)
    for i, y_ref in enumerate((ya_ref, yb_ref, yc_ref, yd_ref)):
        pre = jnp.dot(h, wg_ref[:, i * D_MODEL:(i + 1) * D_MODEL], preferred_element_type=F32)
        gate = _sigmoid(pre + gb_ref[i:i + 1, :])
        acc = acc + gate * jnp.dot(y_ref[...].astype(BF16), wb_ref[i], preferred_element_type=F32)
    out = jnp.dot(acc.astype(BF16), wo_ref[...], preferred_element_type=F32)
    ms = jnp.mean(out * out, axis=-1, keepdims=True)
    o_ref[...] = x + gmod * (out * lax.rsqrt(ms + NORM_EPS) * g1_ref[...])


def _merge(xs, mod, gain0, gain1, ys, wg, gate_b, wb, wo, n_ctx_rows):
    batch, rows, d = xs.shape
    tm = _row_tile(rows, 272)
    tile = lambda width: pl.BlockSpec((None, tm, width), lambda b, i: (b, i, 0))
    consts = [wg, gate_b, wb, wo]
    return pl.pallas_call(
        functools.partial(_merge_kernel, n_ctx_rows=n_ctx_rows),
        grid=(batch, rows // tm),
        in_specs=[tile(d)] + _mod_specs(1, batch) + _mod_specs(0, batch) + _mod_specs(2, batch)
        + [_vec_spec(gain0.shape), _vec_spec(gain1.shape)]
        + [tile(BRANCH_W)] * 4 + [_vec_spec(c.shape) for c in consts],
        out_specs=tile(d),
        out_shape=jax.ShapeDtypeStruct((batch, rows, d), F32),
        compiler_params=_cparams(("parallel", "parallel")),
        name="merge_out",
    )(xs, *([mod] * 6), gain0, gain1, *ys, *consts)


def _ffn_kernel(x_ref, sc_ref, csc_ref, sh_ref, csh_ref, gm_ref, cgm_ref, g2_ref, g3_ref, w1g_ref, w1u_ref, w2_ref,
                o_ref, h_ref, acc_ref, *, n_ctx_rows):
    j = pl.program_id(2)
    is_ctx = _ctx_rows(x_ref.shape[0], n_ctx_rows)

    @pl.when(j == 0)
    def _():
        scale = jnp.where(is_ctx, csc_ref[...], sc_ref[...])
        shift = jnp.where(is_ctx, csh_ref[...], sh_ref[...])
        h_ref[...] = _prenorm(x_ref[...], g2_ref[...], scale, shift).astype(BF16)
        acc_ref[...] = jnp.zeros_like(acc_ref)

    h = h_ref[...]
    gt = jnp.dot(h, w1g_ref[...], preferred_element_type=F32)
    up = jnp.dot(h, w1u_ref[...], preferred_element_type=F32)
    acc_ref[...] += jnp.dot((_silu(gt) * up).astype(BF16), w2_ref[...], preferred_element_type=F32)

    @pl.when(j == pl.num_programs(2) - 1)
    def _():
        out = acc_ref[...]
        ms = jnp.mean(out * out, axis=-1, keepdims=True)
        gmod = jnp.where(is_ctx, cgm_ref[...], gm_ref[...])
        o_ref[...] = x_ref[...] + gmod * (out * lax.rsqrt(ms + NORM_EPS) * g3_ref[...])


def _ffn(xs, mod, gain2, gain3, w1, w2, n_ctx_rows, th=1408):
    batch, rows, d = xs.shape
    hidden = w2.shape[0]
    tm = _row_tile(rows, 544)
    nh = hidden // th
    tile = pl.BlockSpec((None, tm, d), lambda b, i, j: (b, i, 0))
    return pl.pallas_call(
        functools.partial(_ffn_kernel, n_ctx_rows=n_ctx_rows),
        grid=(batch, rows // tm, nh),
        in_specs=[tile] + _mod_specs(4, batch) + _mod_specs(3, batch) + _mod_specs(5, batch)
        + [pl.BlockSpec((1, d), lambda b, i, j: (0, 0)), pl.BlockSpec((1, d), lambda b, i, j: (0, 0)),
                  pl.BlockSpec((d, th), lambda b, i, j: (0, j)),
                  pl.BlockSpec((d, th), lambda b, i, j: (0, nh + j)),
                  pl.BlockSpec((th, d), lambda b, i, j: (j, 0))],
        out_specs=tile,
        out_shape=jax.ShapeDtypeStruct((batch, rows, d), F32),
        scratch_shapes=[pltpu.VMEM((tm, d), BF16), pltpu.VMEM((tm, d), F32)],
        compiler_params=_cparams(("parallel", "parallel", "arbitrary")),
        name="swiglu",
    )(xs, *([mod] * 6), gain2, gain3, w1, w1, w2)


def _pad_cols(w, width):
    return jnp.pad(w, ((0, 0), (0, width - w.shape[1])))


def _layer_weights(w_in, mu, w2, a2, g2, conv, a_log, dt_bias, gw2, gb):
    wts = {}
    off_b, off_c, off_d, off_g = A_IN, A_IN + B_IN, A_IN + B_IN + C_IN, A_IN + B_IN + C_IN + D_IN
    w = BRANCH_W
    wts["wa"] = _pad_cols(w_in[:, 0:A_IN], 4 * w).astype(BF16)
    wts["mu"] = _pad_cols(mu[None, :], 4 * w)
    lo = np.cumsum((0,) + A_LORA)
    place = lambda m, r0: jnp.zeros((w, w), F32).at[r0:r0 + m.shape[0], :].set(m)
    wts["w2p"] = jnp.stack([place(w2[0], lo[0]), place(w2[1], lo[1])])
    wts["a2p"] = jnp.stack([place(a2[0], lo[2]), place(a2[1], lo[3])])
    wts["g2p"] = place(g2, lo[4])

    wb = w_in[:, off_b:off_b + B_IN]
    wts["wb"] = jnp.concatenate([wb[:, 0:3 * w], wb[:, 3 * w + 16:], _pad_cols(wb[:, 3 * w:3 * w + 16], 128)],
                                axis=1).astype(BF16)
    nh = N_HEADS
    expand = np.zeros((4, 128, w), np.float32)
    for grp in range(4):
        for h in range(nh):
            expand[grp, grp * nh + h, h * HEAD:(h + 1) * HEAD] = 1.0
    wts["eb"] = jnp.asarray(expand[0:2])
    wts["ea"] = jnp.asarray(expand[2:4])
    vec = lambda t: jnp.zeros((1, 128), F32).at[0, 2 * nh:4 * nh].set(t.reshape(-1))
    wts["alog"] = vec(a_log)
    wts["dt"] = vec(dt_bias)
    wts["conv"] = conv

    wc = w_in[:, off_c:off_c + C_IN]
    pad_heads = lambda m: jnp.pad(m.reshape(m.shape[0], nh, C_DK),
                                  ((0, 0), (0, 0), (0, HEAD - C_DK))).reshape(m.shape[0], w)
    qc, kc, vc = wc[:, 0:C_QK], wc[:, C_QK:2 * C_QK], wc[:, 2 * C_QK:2 * C_QK + w]
    loc = wc[:, 2 * C_QK + w:2 * C_QK + w + 2 * C_GATE_R]
    gc = wc[:, 2 * C_QK + w + 2 * C_GATE_R:]
    wts["wc"] = jnp.concatenate([pad_heads(qc), pad_heads(kc), vc, gc, _pad_cols(loc, 128)], axis=1).astype(BF16)
    gwp = jnp.zeros((2, 128, w), F32)
    for d in range(2):
        gwp = gwp.at[d, d * C_GATE_R:(d + 1) * C_GATE_R, :].set(pad_heads(gw2[d]))
    wts["gwp"] = gwp
    wts["gbp"] = pad_heads(gb)
    wts["glane"] = jnp.asarray((np.arange(w) % HEAD < C_DK).astype(np.float32))[None, :]

    wd = w_in[:, off_d:off_d + D_IN]
    qd = wd[:, 0:w]
    dup = lambda m: jnp.concatenate([m[:, 0:HEAD], m[:, 0:HEAD], m[:, HEAD:], m[:, HEAD:]], axis=1)
    wts["wd"] = jnp.concatenate([qd, dup(wd[:, w:w + 2 * HEAD]), dup(wd[:, w + 2 * HEAD:])], axis=1).astype(BF16)
    wts["wg"] = w_in[:, off_g:].astype(BF16)
    return wts


def _rope_tables(n_ctx_rows, n_lat_rows):
    quarter = HEAD // 4
    inv = ROPE_BASE ** (-np.arange(quarter, dtype=np.float32) / quarter)
    pos = np.arange(n_lat_rows)
    rows = (pos // GRID_W).astype(np.float32)
    cols = (pos % GRID_W).astype(np.float32)
    inv = jnp.asarray(inv)
    ang_r = jnp.asarray(rows)[:, None] * inv[None, :]
    ang_c = jnp.asarray(cols)[:, None] * inv[None, :]
    cos = jnp.concatenate([jnp.cos(ang_r)] * 2 + [jnp.cos(ang_c)] * 2, axis=1)
    sin = jnp.concatenate([-jnp.sin(ang_r), jnp.sin(ang_r), -jnp.sin(ang_c), jnp.sin(ang_c)], axis=1)
    cos = jnp.concatenate([jnp.ones((n_ctx_rows, HEAD), F32), cos], axis=0)
    sin = jnp.concatenate([jnp.zeros((n_ctx_rows, HEAD), F32), sin], axis=0)
    return jnp.tile(cos, (1, N_HEADS)), jnp.tile(sin, (1, N_HEADS))


RWKV_SLOTS = {"r": ("sh", 0), "v": ("sh", 1), "a": ("sh", 2), "w": ("dp", 0), "k": ("dp", 1), "b": ("dp", 2)}
GDN_SLOTS = RWKV_SLOTS
GLA_SLOTS = {"r": ("sh", 0), "v": ("sh", 1), "k": ("sh", 2), "w": ("dp", 0)}


def kernel(x, c, ctx, c_ctx, ada_w, ada_b, norm_g, w_in, gate_b, w_branch, w_out, rwkv_mu, rwkv_w0, rwkv_w2, rwkv_a0, rwkv_a2, rwkv_g2, rwkv_kk, rwkv_ka, rwkv_rk, rwkv_ln_g, rwkv_ln_b, gdn_conv, gdn_a_log, gdn_dt_bias, gdn_norm_g, gla_gw2, gla_gb, gla_norm_g, attn_sink, ffn_w1, ffn_w2):
    batch, n_lat, d = x.shape
    n_ctx = ctx.shape[1]
    depth = ada_w.shape[0]
    assert n_ctx % PREP_ROWS == 0 and n_lat % PREP_ROWS == 0 and d == D_MODEL

    mod_rows = 8 * ((batch + 1 + 7) // 8)
    c_rows = jnp.concatenate([c, c_ctx[None, :], jnp.zeros((mod_rows - batch - 1, d), F32)], axis=0)
    mod_all = _modulation(c_rows, ada_w, ada_b)
    cos, sin = _rope_tables(n_ctx, n_lat)

    xs = jnp.concatenate([ctx, x], axis=1)
    row = lambda t: t.reshape(1, -1)
    for l in range(depth):
        mod = mod_all[l].reshape(mod_rows, 1, 6 * d)
        ng = norm_g[l]
        wts = _layer_weights(w_in[l], rwkv_mu[l], rwkv_w2[l], rwkv_a2[l], rwkv_g2[l], gdn_conv[l],
                             gdn_a_log[l], gdn_dt_bias[l], gla_gw2[l], gla_gb[l])
        w_all = jnp.concatenate([wts["wb"], wts["wc"], wts["wd"], wts["wa"]], axis=1)
        p = _project(xs, mod, row(ng[0]), w_all, n_ctx, 1, 0)
        wb_w, wc_w, wd_w, wa_w = (wts[k].shape[1] for k in ("wb", "wc", "wd", "wa"))
        assert wc_w == wb_w and (2 * wb_w) % BRANCH_W == 0 and (2 * wb_w + wd_w) % wa_w == 0

        sh, dp0, dp1, fin = _rwkv_prep(p, n_ctx, wts["mu"], wts["w2p"], wts["a2p"], wts["g2p"],
                                       rwkv_w0[l], rwkv_a0[l], row(rwkv_kk[l]), row(rwkv_ka[l]),
                                       row(rwkv_rk[l]), width=wa_w, col=(2 * wb_w + wd_w) // wa_w)
        ya = _bidir_scan(sh, (dp0, dp1), n_ctx, fin, row(rwkv_ln_g[l]), row(rwkv_ln_b[l]), finish="groupnorm",
                         scalar_decay=False, lowrank=True, slots=RWKV_SLOTS)

        sh, dp0, dp1, fin = _gdn_prep(p, n_ctx, wts["conv"], wts["alog"], wts["dt"], wts["eb"],
                                      wts["ea"], width=wb_w, col=0)
        gnorm = row(jnp.tile(gdn_norm_g[l], N_HEADS))
        yb = _bidir_scan(sh, (dp0, dp1), n_ctx, fin, gnorm, gnorm, finish="rms",
                         scalar_decay=True, lowrank=True, slots=GDN_SLOTS)

        sh, dp0, dp1, fin = _gla_prep(p, wts["gwp"], wts["gbp"], wts["glane"], width=wc_w, col=1)
        cnorm = row(jnp.tile(gla_norm_g[l], N_HEADS))
        yc = _bidir_scan(sh, (dp0, dp1), n_ctx, fin, cnorm, cnorm, finish="rms",
                         scalar_decay=False, lowrank=False, slots=GLA_SLOTS)

        yd = _attention(p, attn_sink[l], cos, sin, n_ctx, col0=2 * wb_w // BRANCH_W)

        xs = _merge(xs, mod, row(ng[0]), row(ng[1]), (ya, yb, yc, yd), wts["wg"], gate_b[l],
                    w_branch[l].astype(BF16), w_out[l].astype(BF16), n_ctx)
        xs = _ffn(xs, mod, row(ng[2]), row(ng[3]), ffn_w1[l].astype(BF16), ffn_w2[l].astype(BF16), n_ctx)
    return xs[:, n_ctx:, :]
```

```python
import functools
import math

import numpy as np
import jax
import jax.numpy as jnp
from jax import lax
from jax.experimental import pallas as pl
from jax.experimental.pallas import tpu as pltpu

F32 = jnp.float32
BF16 = jnp.bfloat16

D_MODEL = 1024
N_BRANCH = 4
BRANCH_W = 256
HEAD = 64
N_HEADS = BRANCH_W // HEAD
NORM_EPS = 1e-6
A_GN_EPS = 64e-5
A_LORA = (32, 32, 32, 32, 64)
A_IN = 3 * BRANCH_W + sum(A_LORA)
B_IN = 4 * BRANCH_W + 4 * N_HEADS
C_DK = 32
C_QK = N_HEADS * C_DK
C_GATE_R = 16
C_GATE_NORM = 16.0
C_IN = 2 * C_QK + 2 * BRANCH_W + 2 * C_GATE_R
D_KV_HEADS = 2
D_IN = BRANCH_W + 2 * D_KV_HEADS * HEAD
B_CONV = 7
WINDOW = 128
ROPE_BASE = 10000.0
GRID_W = 64
FFN_HIDDEN = 2816

CHUNK = 64
PREP_ROWS = 256
HALO = 16
ATT_BLOCK = 128
VMEM_LIMIT = 48 * 1024 * 1024

NN = (((1,), (0,)), ((), ()))
NT = (((1,), (1,)), ((), ()))
TN = (((0,), (0,)), ((), ()))


def _mm(a, b, dims=NN, mode="bf16"):
    if mode == "f32":
        return lax.dot_general(a, b, dims, precision=lax.Precision.HIGHEST, preferred_element_type=F32)
    if mode == "x3":
        ah = a.astype(BF16)
        al = (a - ah.astype(F32)).astype(BF16)
        bh = b.astype(BF16)
        bl = (b - bh.astype(F32)).astype(BF16)
        dot = functools.partial(lax.dot_general, dimension_numbers=dims, preferred_element_type=F32)
        return dot(ah, bh) + (dot(ah, bl) + dot(al, bh))
    return lax.dot_general(a.astype(BF16), b.astype(BF16), dims, preferred_element_type=F32)


def _sigmoid(x):
    return 1.0 / (1.0 + jnp.exp(-x))


def _silu(x):
    return x * _sigmoid(x)


def _softplus(x):
    return jnp.maximum(x, 0.0) + jnp.log1p(jnp.exp(-jnp.abs(x)))


def _cparams(sem):
    return pltpu.CompilerParams(dimension_semantics=sem, vmem_limit_bytes=VMEM_LIMIT)


def _mod_kernel(c_ref, w_ref, b_ref, o_ref):
    c = c_ref[...]
    o_ref[...] = _mm(_silu(c), w_ref[...], mode="f32") + b_ref[...]


def _modulation(c_rows, ada_w, ada_b):
    depth, d, n = ada_w.shape
    rows = c_rows.shape[0]
    tn = 1024
    return pl.pallas_call(
        _mod_kernel,
        grid=(depth, n // tn),
        in_specs=[
            pl.BlockSpec((rows, d), lambda l, j: (0, 0)),
            pl.BlockSpec((None, d, tn), lambda l, j: (l, 0, j)),
            pl.BlockSpec((None, 1, tn), lambda l, j: (l, 0, j)),
        ],
        out_specs=pl.BlockSpec((None, rows, tn), lambda l, j: (l, 0, j)),
        out_shape=jax.ShapeDtypeStruct((depth, rows, n), F32),
        compiler_params=_cparams(("parallel", "parallel")),
        name="adaln_mod",
    )(c_rows, ada_w, ada_b.reshape(depth, 1, n))


def _row_tile(rows, target):
    return max(t for t in range(8, target + 1, 8) if rows % t == 0)


def _mod_specs(which, batch):
    lat = pl.BlockSpec((None, 1, D_MODEL), lambda b, i, *_: (b, 0, which))
    ctx = pl.BlockSpec((None, 1, D_MODEL), lambda b, i, *_: (batch, 0, which))
    return [lat, ctx]


def _ctx_rows(tm, n_ctx_rows):
    row = pl.program_id(1) * tm + lax.broadcasted_iota(jnp.int32, (tm, 1), 0)
    return row < n_ctx_rows


def _prenorm(x, gain, scale, shift):
    ms = jnp.mean(x * x, axis=-1, keepdims=True)
    y = x * lax.rsqrt(ms + NORM_EPS) * gain
    return y * (1.0 + scale) + shift


PROJ_CHUNK = 1024


def _proj_kernel(x_ref, sc_ref, csc_ref, sh_ref, csh_ref, g_ref, w_ref, o_ref, *, n_ctx_rows):
    is_ctx = _ctx_rows(x_ref.shape[0], n_ctx_rows)
    scale = jnp.where(is_ctx, csc_ref[...], sc_ref[...])
    shift = jnp.where(is_ctx, csh_ref[...], sh_ref[...])
    h = _prenorm(x_ref[...], g_ref[...], scale, shift).astype(BF16)
    n = w_ref.shape[1]
    for lo in range(0, n, PROJ_CHUNK):
        hi = min(lo + PROJ_CHUNK, n)
        o_ref[:, lo:hi] = jnp.dot(h, w_ref[:, lo:hi], preferred_element_type=F32).astype(o_ref.dtype)


def _project(xs, mod, gain, w, n_ctx_rows, sc_idx, sh_idx):
    batch, rows, d = xs.shape
    n = w.shape[1]
    tm = _row_tile(rows, 544)
    return pl.pallas_call(
        functools.partial(_proj_kernel, n_ctx_rows=n_ctx_rows),
        grid=(batch, rows // tm),
        in_specs=[pl.BlockSpec((None, tm, d), lambda b, i: (b, i, 0))]
        + _mod_specs(sc_idx, batch) + _mod_specs(sh_idx, batch)
        + [_vec_spec(gain.shape), _vec_spec(w.shape)],
        out_specs=pl.BlockSpec((None, tm, n), lambda b, i: (b, i, 0)),
        out_shape=jax.ShapeDtypeStruct((batch, rows, n), BF16),
        compiler_params=_cparams(("parallel", "parallel")),
        name="prenorm_proj",
    )(xs, mod, mod, mod, mod, gain, w)


def _head_block_ones():
    idx = np.arange(BRANCH_W)
    return (idx[:, None] // HEAD == idx[None, :] // HEAD).astype(np.float32)


N_LEVELS = 6


def _level_mask(ri, ci, s, reverse):
    b = 1 << s
    blk = (ri // (2 * b)) == (ci // (2 * b))
    hi_r, hi_c = (ri // b) % 2 == 1, (ci // b) % 2 == 1
    return blk & ((hi_c & ~hi_r) if reverse else (hi_r & ~hi_c))


SCAN_HEADS = 2
GROUP_W = SCAN_HEADS * HEAD
N_GROUPS = N_HEADS // SCAN_HEADS


def _scan_masks(reverse):
    n = SCAN_HEADS * CHUNK
    ri = np.arange(CHUNK)[:, None]
    ci = np.arange(n)[None, :] % CHUNK
    strict, incl = (ci > ri, ci >= ri) if reverse else (ci < ri, ci <= ri)
    compact = [strict, incl, ci == ri] + [_level_mask(ri, ci, s, reverse) for s in range(N_LEVELS)]
    r = np.arange(n)[:, None]
    c = np.arange(n)[None, :]
    block = [(r // CHUNK) == (c // CHUNK), r == c]
    return np.stack(compact).astype(np.float32), np.stack(block).astype(np.float32)


EXACT_TERMS = 3


def _tri(reverse):
    i = np.arange(CHUNK)
    m = (i[None, :] >= i[:, None]) if reverse else (i[None, :] <= i[:, None])
    return np.tile(m.astype(np.float32), (1, EXACT_TERMS))


def _split_terms(x, terms):
    out = []
    for _ in range(terms - 1):
        part = x.astype(BF16)
        out.append(part)
        x = x - part.astype(F32)
    out.append(x.astype(BF16))
    return out


def _mm_exact(a, b, dims=NN, split="a", terms=EXACT_TERMS):
    assert dims == NN
    if split == "a":
        lhs = jnp.concatenate(_split_terms(a, terms), axis=1)
        rhs = jnp.concatenate([b.astype(BF16)] * terms, axis=0)
    else:
        lhs = a.astype(BF16)
        rhs = jnp.concatenate(_split_terms(b, terms), axis=0)
    return lax.dot_general(lhs, rhs, dims, preferred_element_type=F32)


C_STRICT, C_INCL, C_EYE, C_LVL0 = 0, 1, 2, 3
B_SAME, B_EYE = 0, 1
SCAN_BATCH = 8
assert CHUNK == HEAD and (1 << N_LEVELS) == CHUNK


def _scan_chunk(vals, st, tri, ones, cm_ref, bm_ref, same_bf, *, reverse, scalar_decay, lowrank, mm_mode):
    same = bm_ref[B_SAME]

    def get(name):
        return vals[name]

    def expand(x):
        return jnp.concatenate([x.astype(BF16)] * SCAN_HEADS, axis=0) * same_bf

    def keep(x, k):
        return jnp.where(cm_ref[k] > 0.5, x, 0.0)

    logw = get("w")
    r, k, v = get("r"), get("k"), get("v")
    cum = _mm_exact(tri, logw, split="b")
    cum_x = cum - logw
    last = 0 if reverse else CHUNK - 1
    total = cum[last:last + 1, :]
    to_end = jnp.exp(total - cum)

    if scalar_decay:
        diag = jnp.concatenate([cum] * SCAN_HEADS, axis=0) * bm_ref[B_EYE]
        cum_row = _mm_exact(ones, diag, split="b")
        d_ii = jnp.exp(jnp.where(cm_ref[C_INCL] > 0.5, cum - cum_row, -1e30))
        r_q, k_q = r, k
    else:
        ref_row = cum[CHUNK // 2:CHUNK // 2 + 1, :]
        p_inv = jnp.exp(ref_row - cum)
        r_q, k_q = r * jnp.exp(cum - ref_row), k * p_inv

    k_e, v_e = expand(k_q), expand(v)
    r_abs = r * jnp.exp(cum)
    if not lowrank:
        s_k = _mm(r_q, k_e, NT, mm_mode)
        a_rk = s_k * d_ii if scalar_decay else keep(s_k, C_INCL)
        y = _mm(r_abs, st, NT, mm_mode) + _mm(a_rk, v_e, NN, mm_mode)
        upd = _mm(v, k * to_end, TN, mm_mode)
        return y, st * jnp.exp(total) + upd * same

    a, b = get("a"), get("b")
    if scalar_decay:
        d_xi = jnp.exp(jnp.where(cm_ref[C_STRICT] > 0.5, cum_x - cum_row, -1e30))
        a_q, b_q = a, b
    else:
        a_q, b_q = a * jnp.exp(cum_x - ref_row), b * p_inv
    lhs = jnp.concatenate([a_q, r_q], axis=0)
    s_b = _mm(lhs, expand(b_q), NT, mm_mode)
    s_k = _mm(lhs, k_e, NT, mm_mode)
    if scalar_decay:
        a_ab, a_rb = s_b[0:CHUNK] * d_xi, s_b[CHUNK:] * d_ii
        a_ak, a_rk = s_k[0:CHUNK] * d_xi, s_k[CHUNK:] * d_ii
    else:
        a_ab, a_rb = keep(s_b[0:CHUNK], C_STRICT), keep(s_b[CHUNK:], C_INCL)
        a_ak, a_rk = keep(s_k[0:CHUNK], C_STRICT), keep(s_k[CHUNK:], C_INCL)

    inv = cm_ref[C_EYE] + a_ab * cm_ref[C_LVL0]
    for s in range(1, N_LEVELS):
        c_s = expand(a_ab * cm_ref[C_LVL0 + s])
        inv = inv + _mm(_mm(inv, c_s, NN, mm_mode), expand(inv), NN, mm_mode)

    a_abs = a * jnp.exp(cum_x)
    from_state = _mm(jnp.concatenate([a_abs, r_abs], axis=0), st, NT, mm_mode)
    from_v = _mm(jnp.concatenate([a_ak, a_rk], axis=0), v_e, NN, mm_mode)
    both = from_state + from_v
    z = _mm(inv, expand(both[0:CHUNK]), NN, mm_mode)
    y = both[CHUNK:] + _mm(a_rb, expand(z), NN, mm_mode)
    upd = _mm(jnp.concatenate([v, z], axis=0), jnp.concatenate([k * to_end, b * to_end], axis=0), TN, mm_mode)
    return y, st * jnp.exp(total) + upd * same


def _scan_kernel(*refs, reverse, scalar_decay, lowrank, slots, sources, finish, mm_mode):
    it = iter(refs)
    src_refs = {name: next(it) for name in sources}
    sh_ref = src_refs[sources[0]]
    cm_ref, bm_ref, tri_ref = next(it), next(it), next(it)
    if finish:
        ob_ref, fin_ref, p1_ref, p2_ref, avg_ref = next(it), next(it), next(it), next(it), next(it)
    o_ref, st_ref = next(it), next(it)

    @pl.when(pl.program_id(1) == 0)
    def _():
        st_ref[...] = jnp.zeros_like(st_ref)

    nb = sh_ref.shape[0]
    n = nb * N_GROUPS

    def groups(name):
        src, idx = slots[name]
        ref = src_refs[src]
        parts = [ref[:, :, idx * BRANCH_W + g * GROUP_W:idx * BRANCH_W + (g + 1) * GROUP_W].astype(F32)
                 for g in range(N_GROUPS)]
        return jnp.stack(parts, axis=1).reshape(n, CHUNK, GROUP_W)

    vals = {name: groups(name) for name in slots}
    tri = jnp.broadcast_to(tri_ref[...], (n,) + tri_ref.shape)
    ones = jnp.ones((n, CHUNK, EXACT_TERMS * SCAN_HEADS * CHUNK), F32)
    chunk = functools.partial(_scan_chunk, cm_ref=cm_ref, bm_ref=bm_ref, same_bf=bm_ref[B_SAME].astype(BF16),
                              reverse=reverse, scalar_decay=scalar_decay, lowrank=lowrank, mm_mode=mm_mode)
    y, st_new = jax.vmap(chunk)(vals, st_ref[...], tri, ones)
    st_ref[...] = st_new
    y = y.reshape(nb, N_GROUPS, CHUNK, GROUP_W)
    y = jnp.concatenate([y[:, g] for g in range(N_GROUPS)], axis=-1)
    if not finish:
        o_ref[...] = y
        return

    avg = avg_ref[...]
    for i in range(nb):
        yi = y[i] + ob_ref[i]
        if finish == "groupnorm":
            gate, bonus = fin_ref[i, :, 0:BRANCH_W], fin_ref[i, :, BRANCH_W:2 * BRANCH_W]
            cen = yi - _mm_exact(yi, avg)
            var = _mm_exact(cen * cen, avg)
            yn = cen * lax.rsqrt(var + A_GN_EPS) * p1_ref[...] + p2_ref[...]
            o_ref[i] = (yn + bonus) * gate
        else:
            ms = _mm_exact(yi * yi, avg)
            o_ref[i] = yi * lax.rsqrt(ms + NORM_EPS) * p1_ref[...] * _silu(fin_ref[i])


def _scan(srcs, n_ctx_rows, *, reverse, scalar_decay, lowrank, slots, finish=None, fin_args=None,
          mm_mode="bf16"):
    sources = tuple(srcs)
    batch, rows, _ = srcs[sources[0]].shape
    nc, nctx = rows // CHUNK, n_ctx_rows // CHUNK
    nb = math.gcd(batch, SCAN_BATCH)

    if reverse:
        def chunk(n):
            return jnp.where(n < nctx, nctx - 1 - n, nc - 1 + nctx - n)
    else:
        def chunk(n):
            return n

    def row_spec(width):
        return pl.BlockSpec((nb, CHUNK, width), lambda b, n: (b, chunk(n), 0))

    def const_spec(shape):
        zeros = (0,) * len(shape)
        return pl.BlockSpec(shape, lambda b, n: zeros)

    cmask, bmask = (jnp.asarray(m) for m in _scan_masks(reverse))
    tri = jnp.asarray(_tri(reverse))
    in_specs = [row_spec(srcs[s].shape[-1]) for s in sources]
    in_specs += [const_spec(cmask.shape), const_spec(bmask.shape), const_spec(tri.shape)]
    args = [srcs[s] for s in sources] + [cmask, bmask, tri]
    if finish:
        ob, fin, p1, p2 = fin_args
        avg = jnp.asarray(_head_block_ones() / HEAD)
        in_specs += [row_spec(BRANCH_W), row_spec(fin.shape[-1]), const_spec(p1.shape), const_spec(p2.shape),
                     const_spec(avg.shape)]
        args += [ob, fin, p1, p2, avg]
    kern = functools.partial(_scan_kernel, reverse=reverse, scalar_decay=scalar_decay, lowrank=lowrank,
                             slots=slots, sources=sources, finish=finish, mm_mode=mm_mode)
    return pl.pallas_call(
        kern,
        grid=(batch // nb, nc),
        in_specs=in_specs,
        out_specs=row_spec(BRANCH_W),
        out_shape=jax.ShapeDtypeStruct((batch, rows, BRANCH_W), F32),
        scratch_shapes=[pltpu.VMEM((nb * N_GROUPS, GROUP_W, GROUP_W), F32)],
        compiler_params=_cparams(("parallel", "arbitrary")),
        name="dplr_scan_" + ("bwd" if reverse else "fwd"),
    )(*args)


def _bidir_scan(sh, dks, dws, n_ctx_rows, fin, p1, p2, *, finish, **kw):
    def srcs(d):
        out = {"sh": sh, "dw": dws[d]}
        if dks is not None:
            out["dk"] = dks[d]
        return out

    ob = _scan(srcs(1), n_ctx_rows, reverse=True, **kw)
    return _scan(srcs(0), n_ctx_rows, reverse=False, finish=finish, fin_args=(ob, fin, p1, p2), **kw)


def _halo_specs(width, rows, col):
    per = PREP_ROWS // HALO
    last = rows // HALO - 1
    own = pl.BlockSpec((None, PREP_ROWS, width), lambda b, i: (b, i, col))
    prev = pl.BlockSpec((None, HALO, width), lambda b, i: (b, jnp.maximum(i * per - 1, 0), col))
    nxt = pl.BlockSpec((None, HALO, width), lambda b, i: (b, jnp.minimum((i + 1) * per, last), col))
    return [own, prev, nxt]


def _seq_edges(n_ctx_blocks):
    i = pl.program_id(1)
    first = (i == 0) | (i == n_ctx_blocks)
    lastb = (i == n_ctx_blocks - 1) | (i == pl.num_programs(1) - 1)
    return first, lastb


def _vec_spec(shape):
    zeros = (0,) * len(shape)
    return pl.BlockSpec(shape, lambda b, i: zeros)


def _rwkv_prep_kernel(p_ref, prev_ref, next_ref, mu_ref, w2_ref, a2_ref, g2_ref, w0_ref, a0_ref, kk_ref,
                      ka_ref, rk_ref, ones_ref, sh_ref, dk0_ref, dk1_ref, dw0_ref, dw1_ref, fin_ref, *,
                      n_ctx_blocks, mode):
    first, lastb = _seq_edges(n_ctx_blocks)
    x = p_ref[...].astype(F32)
    prev_row = jnp.where(first, 0.0, prev_ref[...].astype(F32)[HALO - 1:HALO, :])
    next_row = jnp.where(lastb, 0.0, next_ref[...].astype(F32)[0:1, :])
    rid = lax.broadcasted_iota(jnp.int32, x.shape, 0)
    x_prev = jnp.where(rid == 0, prev_row, pltpu.roll(x, 1, axis=0))
    x_next = jnp.where(rid == PREP_ROWS - 1, next_row, pltpu.roll(x, PREP_ROWS - 1, axis=0))
    xm = x + (0.5 * (x_prev + x_next) - x) * mu_ref[...]
    w = BRANCH_W
    r, k, v, lo = xm[:, 0:w], xm[:, w:2 * w], xm[:, 2 * w:3 * w], xm[:, 3 * w:4 * w]
    ones = ones_ref[...]
    th, sg = jnp.tanh(lo), _sigmoid(lo)
    gate = _mm(sg, g2_ref[...], mode=mode)
    kx = k * kk_ref[...]
    kk = kx * lax.rsqrt(_mm_exact(kx * kx, ones) + 1e-6)
    sh_ref[:, 0:w] = r.astype(BF16)
    sh_ref[:, w:2 * w] = v.astype(BF16)
    sh_ref[:, 2 * w:3 * w] = (-kk).astype(BF16)
    bonus = jnp.zeros_like(v)
    for d, (dk_ref, dw_ref) in enumerate(((dk0_ref, dw0_ref), (dk1_ref, dw1_ref))):
        w_raw = w0_ref[d:d + 1, :] + _mm(th, w2_ref[d], mode=mode)
        dw_ref[...] = -jnp.exp(-_softplus(-w_raw) - 0.5)
        a = _sigmoid(a0_ref[d:d + 1, :] + _mm(lo, a2_ref[d], mode=mode))
        kd = k * (1.0 + (a - 1.0) * ka_ref[...])
        dk_ref[:, 0:w] = kd.astype(BF16)
        dk_ref[:, w:2 * w] = (kk * a).astype(BF16)
        bonus = bonus + _mm_exact(r * kd * rk_ref[...], ones) * v
    fin_ref[:, 0:w] = gate
    fin_ref[:, w:2 * w] = bonus


def _prep_outs(batch, rows, spec):
    w = BRANCH_W
    specs = [pl.BlockSpec((None, PREP_ROWS, n * w), lambda b, i: (b, i, 0)) for n, _ in spec]
    shapes = [jax.ShapeDtypeStruct((batch, rows, n * w), dt) for n, dt in spec]
    return specs, shapes


def _rwkv_prep(p, n_ctx_rows, mu, w2p, a2p, g2p, w0, a0, kk, ka, rk, mode="bf16", width=None, col=0):
    batch, rows, full = p.shape
    width = width or full
    ones = jnp.asarray(_head_block_ones())
    consts = [mu, w2p, a2p, g2p, w0, a0, kk, ka, rk, ones]
    out_specs, out_shape = _prep_outs(batch, rows, [(3, BF16), (2, BF16), (2, BF16), (1, F32), (1, F32), (2, F32)])
    sh, dk0, dk1, dw0, dw1, fin = pl.pallas_call(
        functools.partial(_rwkv_prep_kernel, n_ctx_blocks=n_ctx_rows // PREP_ROWS, mode=mode),
        grid=(batch, rows // PREP_ROWS),
        in_specs=_halo_specs(width, rows, col) + [_vec_spec(c.shape) for c in consts],
        out_specs=out_specs,
        out_shape=out_shape,
        compiler_params=_cparams(("parallel", "parallel")),
        name="rwkv_prep",
    )(p, p, p, *consts)
    return sh, (dk0, dk1), (dw0, dw1), fin


def _gdn_prep_kernel(p_ref, prev_ref, next_ref, conv_ref, alog_ref, dt_ref, eb_ref, ea_ref, ones_ref,
                     sh_ref, dk0_ref, dk1_ref, dw0_ref, dw1_ref, fin_ref, *, n_ctx_blocks):
    first, lastb = _seq_edges(n_ctx_blocks)
    w = BRANCH_W
    x = p_ref[:, 0:3 * w].astype(F32)
    top = jnp.where(first, 0.0, prev_ref[:, 0:3 * w].astype(F32))
    bot = jnp.where(lastb, 0.0, next_ref[:, 0:3 * w].astype(F32))
    xe = jnp.concatenate([top, x, bot], axis=0)
    ext = PREP_ROWS + 2 * HALO
    acc = jnp.zeros_like(x)
    for s in range(B_CONV):
        shift = (B_CONV // 2 - s) % ext
        rolled = xe if shift == 0 else pltpu.roll(xe, shift, axis=0)
        acc = acc + rolled[HALO:HALO + PREP_ROWS] * conv_ref[s:s + 1, :]
    qkv = _silu(acc)
    ones = ones_ref[...]

    def l2n(t):
        return t * lax.rsqrt(_mm_exact(t * t, ones) + 1e-6)

    q = l2n(qkv[:, 0:w]) * (HEAD ** -0.5)
    k = l2n(qkv[:, w:2 * w])
    v = qkv[:, 2 * w:3 * w]
    sh_ref[:, 0:w] = q.astype(BF16)
    sh_ref[:, w:2 * w] = v.astype(BF16)
    sh_ref[:, 2 * w:3 * w] = k.astype(BF16)
    sr = p_ref[:, 4 * w:4 * w + 128].astype(F32)
    beta_all = _sigmoid(sr)
    g_all = -jnp.exp(alog_ref[...]) * _softplus(sr + dt_ref[...])
    for d, (dk_ref, dw_ref) in enumerate(((dk0_ref, dw0_ref), (dk1_ref, dw1_ref))):
        beta = _mm_exact(beta_all, eb_ref[d])
        g = _mm_exact(g_all, ea_ref[d])
        kb = k * beta
        dw_ref[...] = g
        dk_ref[:, 0:w] = kb.astype(BF16)
        dk_ref[:, w:2 * w] = (-jnp.exp(g) * kb).astype(BF16)
    fin_ref[...] = p_ref[:, 3 * w:4 * w].astype(F32)


def _gdn_prep(p, n_ctx_rows, conv_w, alog_vec, dt_vec, eb, ea, width=None, col=0):
    batch, rows, full = p.shape
    width = width or full
    ones = jnp.asarray(_head_block_ones())
    consts = [conv_w, alog_vec, dt_vec, eb, ea, ones]
    out_specs, out_shape = _prep_outs(batch, rows, [(3, BF16), (2, BF16), (2, BF16), (1, F32), (1, F32), (1, F32)])
    sh, dk0, dk1, dw0, dw1, fin = pl.pallas_call(
        functools.partial(_gdn_prep_kernel, n_ctx_blocks=n_ctx_rows // PREP_ROWS),
        grid=(batch, rows // PREP_ROWS),
        in_specs=_halo_specs(width, rows, col) + [_vec_spec(c.shape) for c in consts],
        out_specs=out_specs,
        out_shape=out_shape,
        compiler_params=_cparams(("parallel", "parallel")),
        name="gdn_prep",
    )(p, p, p, *consts)
    return sh, (dk0, dk1), (dw0, dw1), fin


def _gla_prep_kernel(p_ref, gw_ref, gb_ref, lane_ref, sh_ref, dw0_ref, dw1_ref, fin_ref, *, mode):
    w = BRANCH_W
    sh_ref[:, 0:w] = (p_ref[:, 0:w].astype(F32) * (C_DK ** -0.5)).astype(BF16)
    sh_ref[:, w:2 * w] = p_ref[:, 2 * w:3 * w]
    sh_ref[:, 2 * w:3 * w] = p_ref[:, w:2 * w]
    lo = p_ref[:, 4 * w:4 * w + 128]
    for d, dw_ref in enumerate((dw0_ref, dw1_ref)):
        z = _mm(lo, gw_ref[d], mode=mode) + gb_ref[d:d + 1, :]
        dw_ref[...] = (-_softplus(-z) / C_GATE_NORM) * lane_ref[...]
    fin_ref[...] = p_ref[:, 3 * w:4 * w].astype(F32)


def _gla_prep(p, gwp, gbp, lane_mask, mode="bf16", width=None, col=0):
    batch, rows, full = p.shape
    width = width or full
    consts = [gwp, gbp, lane_mask]
    out_specs, out_shape = _prep_outs(batch, rows, [(3, BF16), (1, F32), (1, F32), (1, F32)])
    sh, dw0, dw1, fin = pl.pallas_call(
        functools.partial(_gla_prep_kernel, mode=mode),
        grid=(batch, rows // PREP_ROWS),
        in_specs=[pl.BlockSpec((None, PREP_ROWS, width), lambda b, i: (b, i, col))]
        + [_vec_spec(c.shape) for c in consts],
        out_specs=out_specs,
        out_shape=out_shape,
        compiler_params=_cparams(("parallel", "parallel")),
        name="gla_prep",
    )(p, *consts)
    return sh, None, (dw0, dw1), fin


NEG_BIG = -1e30


def _window_bias():
    iq = np.arange(2 * ATT_BLOCK)[:, None] % ATT_BLOCK
    ik = np.arange(ATT_BLOCK)[None, :]
    ok = np.stack([ik >= iq, np.ones_like(ik >= iq), ik <= iq])
    assert WINDOW == ATT_BLOCK
    return np.where(ok, 0.0, NEG_BIG).astype(np.float32)


def _attn_kernel(sink_ref, q_ref, kc_ref, vc_ref, kp_ref, ko_ref, kn_ref, vp_ref, vo_ref, vn_ref,
                 co_ref, so_ref, cp_ref, sp_ref, cn_ref, sn_ref, bias_ref, o_ref, *, n_ctx_rows, n_lat_rows):
    t = pl.program_id(1)
    blk = ATT_BLOCK

    def rope(x_ref, cos, sin):
        x = x_ref[...].astype(F32)
        width = x.shape[-1]
        lane = lax.broadcasted_iota(jnp.int32, x.shape, 1)
        swapped = jnp.where(lane % 32 < 16, pltpu.roll(x, width - 16, axis=1), pltpu.roll(x, 16, axis=1))
        return x * cos[:, 0:width] + swapped * sin[:, 0:width]

    q = rope(q_ref, co_ref[...], so_ref[...]) * (HEAD ** -0.5)
    k_band = [rope(kp_ref, cp_ref[...], sp_ref[...]).astype(BF16),
              rope(ko_ref, co_ref[...], so_ref[...]).astype(BF16),
              rope(kn_ref, cn_ref[...], sn_ref[...]).astype(BF16)]
    v_band = [vp_ref, vo_ref, vn_ref]

    q_lat = t * blk >= n_ctx_rows
    in_seq = [q_lat & ((t - 1) * blk >= n_ctx_rows), q_lat, q_lat & ((t + 1) * blk < n_ctx_rows + n_lat_rows)]
    bias = [jnp.where(in_seq[j], bias_ref[j], NEG_BIG) for j in range(3)]
    lane = lax.broadcasted_iota(jnp.int32, (blk, 2 * HEAD), 1)
    row2 = lax.broadcasted_iota(jnp.int32, (2 * blk, 1), 0)
    for g in range(D_KV_HEADS):
        cols = slice(g * 2 * HEAD, (g + 1) * 2 * HEAD)
        qg = q[:, cols]
        qs = jnp.concatenate([jnp.where(lane < HEAD, qg, 0.0), jnp.where(lane >= HEAD, qg, 0.0)], axis=0)
        qs = qs.astype(BF16)
        s_ctx = _mm(qs, kc_ref[:, cols], NT)
        s_band = [_mm(qs, k_band[j][:, cols], NT) + bias[j] for j in range(3)]
        sink = jnp.where(row2 < blk, sink_ref[2 * g], sink_ref[2 * g + 1])
        m = jnp.maximum(jnp.max(s_ctx, axis=-1, keepdims=True), sink)
        for s in s_band:
            m = jnp.maximum(m, jnp.max(s, axis=-1, keepdims=True))
        p_ctx = jnp.exp(s_ctx - m)
        den = jnp.sum(p_ctx, axis=-1, keepdims=True) + jnp.exp(sink - m)
        acc = _mm(p_ctx, vc_ref[:, cols], NN)
        for j in range(3):
            pj = jnp.exp(s_band[j] - m)
            den = den + jnp.sum(pj, axis=-1, keepdims=True)
            acc = acc + _mm(pj, v_band[j][:, cols], NN)
        og = acc / den
        o_ref[:, cols] = jnp.where(lane < HEAD, og[0:blk], og[blk:2 * blk])


def _attention(p, sink, cos, sin, n_ctx_rows, col0=0):
    batch, rows, _ = p.shape
    blk = ATT_BLOCK
    nb = rows // blk
    w = BRANCH_W

    def band(col, off):
        def index(b, t):
            return (b, jnp.clip(t + off, 0, nb - 1), col0 + col)
        return pl.BlockSpec((None, blk, w), index)

    def tab(off):
        return pl.BlockSpec((blk, w), lambda b, t: (jnp.clip(t + off, 0, nb - 1), 0))

    ctx = lambda col: pl.BlockSpec((None, n_ctx_rows, w), lambda b, t: (b, 0, col0 + col))
    kern = functools.partial(_attn_kernel, n_ctx_rows=n_ctx_rows, n_lat_rows=rows - n_ctx_rows)
    bias = jnp.asarray(_window_bias())
    assert n_ctx_rows % blk == 0
    return pl.pallas_call(
        kern,
        grid=(batch, nb),
        in_specs=[pl.BlockSpec(memory_space=pltpu.SMEM), band(0, 0), ctx(1), ctx(2),
                  band(1, -1), band(1, 0), band(1, 1), band(2, -1), band(2, 0), band(2, 1),
                  tab(0), tab(0), tab(-1), tab(-1), tab(1), tab(1), _vec_spec(bias.shape)],
        out_specs=pl.BlockSpec((None, blk, w), lambda b, t: (b, t, 0)),
        out_shape=jax.ShapeDtypeStruct((batch, rows, w), F32),
        compiler_params=_cparams(("parallel", "parallel")),
        name="window_attn",
    )(sink, p, p, p, p, p, p, p, p, p, cos, sin, cos, sin, cos, sin, bias)


def _merge_kernel(x_ref, sc_ref, csc_ref, sh_ref, csh_ref, gm_ref, cgm_ref, g0_ref, g1_ref,
                  ya_ref, yb_ref, yc_ref, yd_ref, wg_ref, gb_ref, wb_ref, wo_ref, o_ref, *, n_ctx_rows):
    x = x_ref[...]
    is_ctx = _ctx_rows(x.shape[0], n_ctx_rows)
    scale = jnp.where(is_ctx, csc_ref[...], sc_ref[...])
    shift = jnp.where(is_ctx, csh_ref[...], sh_ref[...])
    gmod = jnp.where(is_ctx, cgm_ref[...], gm_ref[...])
    h = _prenorm(x, g0_ref[...], scale, shift).astype(BF16)
    acc = jnp.zeros(x.shape, F32)
    for i, y_ref in enumerate((ya_ref, yb_ref, yc_ref, yd_ref)):
        pre = jnp.dot(h, wg_ref[:, i * D_MODEL:(i + 1) * D_MODEL], preferred_element_type=F32)
        gate = _sigmoid(pre + gb_ref[i:i + 1, :])
        acc = acc + gate * jnp.dot(y_ref[...].astype(BF16), wb_ref[i], preferred_element_type=F32)
    out = jnp.dot(acc.astype(BF16), wo_ref[...], preferred_element_type=F32)
    ms = jnp.mean(out * out, axis=-1, keepdims=True)
    o_ref[...] = x + gmod * (out * lax.rsqrt(ms + NORM_EPS) * g1_ref[...])


def _merge(xs, mod, gain0, gain1, ys, wg, gate_b, wb, wo, n_ctx_rows):
    batch, rows, d = xs.shape
    tm = _row_tile(rows, 272)
    tile = lambda width: pl.BlockSpec((None, tm, width), lambda b, i: (b, i, 0))
    consts = [wg, gate_b, wb, wo]
    return pl.pallas_call(
        functools.partial(_merge_kernel, n_ctx_rows=n_ctx_rows),
        grid=(batch, rows // tm),
        in_specs=[tile(d)] + _mod_specs(1, batch) + _mod_specs(0, batch) + _mod_specs(2, batch)
        + [_vec_spec(gain0.shape), _vec_spec(gain1.shape)]
        + [tile(BRANCH_W)] * 4 + [_vec_spec(c.shape) for c in consts],
        out_specs=tile(d),
        out_shape=jax.ShapeDtypeStruct((batch, rows, d), F32),
        compiler_params=_cparams(("parallel", "parallel")),
        name="merge_out",
    )(xs, *([mod] * 6), gain0, gain1, *ys, *consts)


FFN_CHUNK = 512


def _ffn_kernel(x_ref, sc_ref, csc_ref, sh_ref, csh_ref, gm_ref, cgm_ref, g2_ref, g3_ref, w1_ref, w2_ref,
                o_ref, *, n_ctx_rows):
    x = x_ref[...]
    is_ctx = _ctx_rows(x.shape[0], n_ctx_rows)
    scale = jnp.where(is_ctx, csc_ref[...], sc_ref[...])
    shift = jnp.where(is_ctx, csh_ref[...], sh_ref[...])
    h = _prenorm(x, g2_ref[...], scale, shift).astype(BF16)
    hidden = w2_ref.shape[0]
    out = jnp.zeros(x.shape, F32)
    for lo in range(0, hidden, FFN_CHUNK):
        hi = min(lo + FFN_CHUNK, hidden)
        gt = jnp.dot(h, w1_ref[:, lo:hi], preferred_element_type=F32)
        up = jnp.dot(h, w1_ref[:, hidden + lo:hidden + hi], preferred_element_type=F32)
        out = out + jnp.dot((_silu(gt) * up).astype(BF16), w2_ref[lo:hi, :], preferred_element_type=F32)
    ms = jnp.mean(out * out, axis=-1, keepdims=True)
    gmod = jnp.where(is_ctx, cgm_ref[...], gm_ref[...])
    o_ref[...] = x + gmod * (out * lax.rsqrt(ms + NORM_EPS) * g3_ref[...])


def _ffn(xs, mod, gain2, gain3, w1, w2, n_ctx_rows):
    batch, rows, d = xs.shape
    tm = _row_tile(rows, 544)
    tile = pl.BlockSpec((None, tm, d), lambda b, i: (b, i, 0))
    consts = [gain2, gain3, w1, w2]
    return pl.pallas_call(
        functools.partial(_ffn_kernel, n_ctx_rows=n_ctx_rows),
        grid=(batch, rows // tm),
        in_specs=[tile] + _mod_specs(4, batch) + _mod_specs(3, batch) + _mod_specs(5, batch)
        + [_vec_spec(c.shape) for c in consts],
        out_specs=tile,
        out_shape=jax.ShapeDtypeStruct((batch, rows, d), F32),
        compiler_params=_cparams(("parallel", "parallel")),
        name="swiglu",
    )(xs, *([mod] * 6), *consts)


def _pad_cols(w, width):
    return jnp.pad(w, ((0, 0), (0, width - w.shape[1])))


def _layer_weights(w_in, mu, w2, a2, g2, conv, a_log, dt_bias, gw2, gb):
    wts = {}
    off_b, off_c, off_d, off_g = A_IN, A_IN + B_IN, A_IN + B_IN + C_IN, A_IN + B_IN + C_IN + D_IN
    w = BRANCH_W
    wts["wa"] = _pad_cols(w_in[:, 0:A_IN], 4 * w).astype(BF16)
    wts["mu"] = _pad_cols(mu[None, :], 4 * w)
    lo = np.cumsum((0,) + A_LORA)
    place = lambda m, r0: jnp.zeros((w, w), F32).at[r0:r0 + m.shape[0], :].set(m)
    wts["w2p"] = jnp.stack([place(w2[0], lo[0]), place(w2[1], lo[1])])
    wts["a2p"] = jnp.stack([place(a2[0], lo[2]), place(a2[1], lo[3])])
    wts["g2p"] = place(g2, lo[4])

    wb = w_in[:, off_b:off_b + B_IN]
    wts["wb"] = jnp.concatenate([wb[:, 0:3 * w], wb[:, 3 * w + 16:], _pad_cols(wb[:, 3 * w:3 * w + 16], 128)],
                                axis=1).astype(BF16)
    nh = N_HEADS
    expand = np.zeros((4, 128, w), np.float32)
    for grp in range(4):
        for h in range(nh):
            expand[grp, grp * nh + h, h * HEAD:(h + 1) * HEAD] = 1.0
    wts["eb"] = jnp.asarray(expand[0:2])
    wts["ea"] = jnp.asarray(expand[2:4])
    vec = lambda t: jnp.zeros((1, 128), F32).at[0, 2 * nh:4 * nh].set(t.reshape(-1))
    wts["alog"] = vec(a_log)
    wts["dt"] = vec(dt_bias)
    wts["conv"] = conv

    wc = w_in[:, off_c:off_c + C_IN]
    pad_heads = lambda m: jnp.pad(m.reshape(m.shape[0], nh, C_DK),
                                  ((0, 0), (0, 0), (0, HEAD - C_DK))).reshape(m.shape[0], w)
    qc, kc, vc = wc[:, 0:C_QK], wc[:, C_QK:2 * C_QK], wc[:, 2 * C_QK:2 * C_QK + w]
    loc = wc[:, 2 * C_QK + w:2 * C_QK + w + 2 * C_GATE_R]
    gc = wc[:, 2 * C_QK + w + 2 * C_GATE_R:]
    wts["wc"] = jnp.concatenate([pad_heads(qc), pad_heads(kc), vc, gc, _pad_cols(loc, 128)], axis=1).astype(BF16)
    gwp = jnp.zeros((2, 128, w), F32)
    for d in range(2):
        gwp = gwp.at[d, d * C_GATE_R:(d + 1) * C_GATE_R, :].set(pad_heads(gw2[d]))
    wts["gwp"] = gwp
    wts["gbp"] = pad_heads(gb)
    wts["glane"] = jnp.asarray((np.arange(w) % HEAD < C_DK).astype(np.float32))[None, :]

    wd = w_in[:, off_d:off_d + D_IN]
    qd = wd[:, 0:w]
    dup = lambda m: jnp.concatenate([m[:, 0:HEAD], m[:, 0:HEAD], m[:, HEAD:], m[:, HEAD:]], axis=1)
    wts["wd"] = jnp.concatenate([qd, dup(wd[:, w:w + 2 * HEAD]), dup(wd[:, w + 2 * HEAD:])], axis=1).astype(BF16)
    wts["wg"] = w_in[:, off_g:].astype(BF16)
    return wts


def _rope_tables(n_ctx_rows, n_lat_rows):
    quarter = HEAD // 4
    inv = ROPE_BASE ** (-np.arange(quarter, dtype=np.float32) / quarter)
    pos = np.arange(n_lat_rows)
    rows = (pos // GRID_W).astype(np.float32)
    cols = (pos % GRID_W).astype(np.float32)
    inv = jnp.asarray(inv)
    ang_r = jnp.asarray(rows)[:, None] * inv[None, :]
    ang_c = jnp.asarray(cols)[:, None] * inv[None, :]
    cos = jnp.concatenate([jnp.cos(ang_r)] * 2 + [jnp.cos(ang_c)] * 2, axis=1)
    sin = jnp.concatenate([-jnp.sin(ang_r), jnp.sin(ang_r), -jnp.sin(ang_c), jnp.sin(ang_c)], axis=1)
    cos = jnp.concatenate([jnp.ones((n_ctx_rows, HEAD), F32), cos], axis=0)
    sin = jnp.concatenate([jnp.zeros((n_ctx_rows, HEAD), F32), sin], axis=0)
    return jnp.tile(cos, (1, N_HEADS)), jnp.tile(sin, (1, N_HEADS))


RWKV_SLOTS = {"r": ("sh", 0), "v": ("sh", 1), "a": ("sh", 2), "w": ("dw", 0), "k": ("dk", 0), "b": ("dk", 1)}
GDN_SLOTS = RWKV_SLOTS
GLA_SLOTS = {"r": ("sh", 0), "v": ("sh", 1), "k": ("sh", 2), "w": ("dw", 0)}


def kernel(x, c, ctx, c_ctx, ada_w, ada_b, norm_g, w_in, gate_b, w_branch, w_out, rwkv_mu, rwkv_w0, rwkv_w2, rwkv_a0, rwkv_a2, rwkv_g2, rwkv_kk, rwkv_ka, rwkv_rk, rwkv_ln_g, rwkv_ln_b, gdn_conv, gdn_a_log, gdn_dt_bias, gdn_norm_g, gla_gw2, gla_gb, gla_norm_g, attn_sink, ffn_w1, ffn_w2):
    batch, n_lat, d = x.shape
    n_ctx = ctx.shape[1]
    depth = ada_w.shape[0]
    assert n_ctx % PREP_ROWS == 0 and n_lat % PREP_ROWS == 0 and d == D_MODEL

    mod_rows = 8 * ((batch + 1 + 7) // 8)
    c_rows = jnp.concatenate([c, c_ctx[None, :], jnp.zeros((mod_rows - batch - 1, d), F32)], axis=0)
    mod_all = _modulation(c_rows, ada_w, ada_b)
    cos, sin = _rope_tables(n_ctx, n_lat)

    xs = jnp.concatenate([ctx, x], axis=1)
    row = lambda t: t.reshape(1, -1)
    for l in range(depth):
        mod = mod_all[l].reshape(mod_rows, 1, 6 * d)
        ng = norm_g[l]
        wts = _layer_weights(w_in[l], rwkv_mu[l], rwkv_w2[l], rwkv_a2[l], rwkv_g2[l], gdn_conv[l],
                             gdn_a_log[l], gdn_dt_bias[l], gla_gw2[l], gla_gb[l])
        w_all = jnp.concatenate([wts["wb"], wts["wc"], wts["wd"], wts["wa"]], axis=1)
        p = _project(xs, mod, row(ng[0]), w_all, n_ctx, 1, 0)
        wb_w, wc_w, wd_w, wa_w = (wts[k].shape[1] for k in ("wb", "wc", "wd", "wa"))
        assert wc_w == wb_w and (2 * wb_w) % BRANCH_W == 0 and (2 * wb_w + wd_w) % wa_w == 0

        sh, dks, dws, fin = _rwkv_prep(p, n_ctx, wts["mu"], wts["w2p"], wts["a2p"], wts["g2p"],
                                       rwkv_w0[l], rwkv_a0[l], row(rwkv_kk[l]), row(rwkv_ka[l]),
                                       row(rwkv_rk[l]), width=wa_w, col=(2 * wb_w + wd_w) // wa_w)
        ya = _bidir_scan(sh, dks, dws, n_ctx, fin, row(rwkv_ln_g[l]), row(rwkv_ln_b[l]), finish="groupnorm",
                         scalar_decay=False, lowrank=True, slots=RWKV_SLOTS)

        sh, dks, dws, fin = _gdn_prep(p, n_ctx, wts["conv"], wts["alog"], wts["dt"], wts["eb"],
                                      wts["ea"], width=wb_w, col=0)
        gnorm = row(jnp.tile(gdn_norm_g[l], N_HEADS))
        yb = _bidir_scan(sh, dks, dws, n_ctx, fin, gnorm, gnorm, finish="rms",
                         scalar_decay=True, lowrank=True, slots=GDN_SLOTS)

        sh, dks, dws, fin = _gla_prep(p, wts["gwp"], wts["gbp"], wts["glane"], width=wc_w, col=1)
        cnorm = row(jnp.tile(gla_norm_g[l], N_HEADS))
        yc = _bidir_scan(sh, dks, dws, n_ctx, fin, cnorm, cnorm, finish="rms",
                         scalar_decay=False, lowrank=False, slots=GLA_SLOTS)

        yd = _attention(p, attn_sink[l], cos, sin, n_ctx, col0=2 * wb_w // BRANCH_W)

        xs = _merge(xs, mod, row(ng[0]), row(ng[1]), (ya, yb, yc, yd), wts["wg"], gate_b[l],
                    w_branch[l].astype(BF16), w_out[l].astype(BF16), n_ctx)
        xs = _ffn(xs, mod, row(ng[2]), row(ng[3]), ffn_w1[l].astype(BF16), ffn_w2[l].astype(BF16), n_ctx)
    return xs[:, n_ctx:, :]
```

```python
import functools
import math

import numpy as np
import jax
import jax.numpy as jnp
from jax import lax
from jax.experimental import pallas as pl
from jax.experimental.pallas import tpu as pltpu

F32 = jnp.float32
BF16 = jnp.bfloat16

D_MODEL = 1024
N_BRANCH = 4
BRANCH_W = 256
HEAD = 64
N_HEADS = BRANCH_W // HEAD
NORM_EPS = 1e-6
A_GN_EPS = 64e-5
A_LORA = (32, 32, 32, 32, 64)
A_IN = 3 * BRANCH_W + sum(A_LORA)
B_IN = 4 * BRANCH_W + 4 * N_HEADS
C_DK = 32
C_QK = N_HEADS * C_DK
C_GATE_R = 16
C_GATE_NORM = 16.0
C_IN = 2 * C_QK + 2 * BRANCH_W + 2 * C_GATE_R
D_KV_HEADS = 2
D_IN = BRANCH_W + 2 * D_KV_HEADS * HEAD
B_CONV = 7
WINDOW = 128
ROPE_BASE = 10000.0
GRID_W = 64
FFN_HIDDEN = 2816

CHUNK = 64
PREP_ROWS = 256
HALO = 16
ATT_BLOCK = 128
VMEM_LIMIT = 48 * 1024 * 1024

NN = (((1,), (0,)), ((), ()))
NT = (((1,), (1,)), ((), ()))
TN = (((0,), (0,)), ((), ()))


def _mm(a, b, dims=NN, mode="bf16"):
    if mode == "f32":
        return lax.dot_general(a, b, dims, precision=lax.Precision.HIGHEST, preferred_element_type=F32)
    if mode == "x3":
        ah = a.astype(BF16)
        al = (a - ah.astype(F32)).astype(BF16)
        bh = b.astype(BF16)
        bl = (b - bh.astype(F32)).astype(BF16)
        dot = functools.partial(lax.dot_general, dimension_numbers=dims, preferred_element_type=F32)
        return dot(ah, bh) + (dot(ah, bl) + dot(al, bh))
    return lax.dot_general(a.astype(BF16), b.astype(BF16), dims, preferred_element_type=F32)


def _sigmoid(x):
    return 1.0 / (1.0 + jnp.exp(-x))


def _silu(x):
    return x * _sigmoid(x)


def _softplus(x):
    return jnp.maximum(x, 0.0) + jnp.log1p(jnp.exp(-jnp.abs(x)))


def _cparams(sem):
    return pltpu.CompilerParams(dimension_semantics=sem, vmem_limit_bytes=VMEM_LIMIT)


def _mod_kernel(c_ref, w_ref, b_ref, o_ref):
    c = c_ref[...]
    o_ref[...] = _mm(_silu(c), w_ref[...], mode="f32") + b_ref[...]


def _modulation(c_rows, ada_w, ada_b):
    depth, d, n = ada_w.shape
    rows = c_rows.shape[0]
    tn = 1024
    return pl.pallas_call(
        _mod_kernel,
        grid=(depth, n // tn),
        in_specs=[
            pl.BlockSpec((rows, d), lambda l, j: (0, 0)),
            pl.BlockSpec((None, d, tn), lambda l, j: (l, 0, j)),
            pl.BlockSpec((None, 1, tn), lambda l, j: (l, 0, j)),
        ],
        out_specs=pl.BlockSpec((None, rows, tn), lambda l, j: (l, 0, j)),
        out_shape=jax.ShapeDtypeStruct((depth, rows, n), F32),
        compiler_params=_cparams(("parallel", "parallel")),
        name="adaln_mod",
    )(c_rows, ada_w, ada_b.reshape(depth, 1, n))


def _row_tile(rows, target):
    return max(t for t in range(8, target + 1, 8) if rows % t == 0)


def _mod_specs(which, batch):
    lat = pl.BlockSpec((None, 1, D_MODEL), lambda b, i, *_: (b, 0, which))
    ctx = pl.BlockSpec((None, 1, D_MODEL), lambda b, i, *_: (batch, 0, which))
    return [lat, ctx]


def _ctx_rows(tm, n_ctx_rows):
    row = pl.program_id(1) * tm + lax.broadcasted_iota(jnp.int32, (tm, 1), 0)
    return row < n_ctx_rows


def _prenorm(x, gain, scale, shift):
    ms = jnp.mean(x * x, axis=-1, keepdims=True)
    y = x * lax.rsqrt(ms + NORM_EPS) * gain
    return y * (1.0 + scale) + shift


PROJ_CHUNK = 1024


def _rope(x, cos, sin):
    width = x.shape[-1]
    lane = lax.broadcasted_iota(jnp.int32, x.shape, 1)
    swapped = jnp.where(lane % 32 < 16, pltpu.roll(x, width - 16, axis=1), pltpu.roll(x, 16, axis=1))
    return x * cos + swapped * sin


def _proj_kernel(x_ref, sc_ref, csc_ref, sh_ref, csh_ref, g_ref, w_ref, cos_ref, sin_ref, o_ref, *,
                 n_ctx_rows, rope_cols):
    is_ctx = _ctx_rows(x_ref.shape[0], n_ctx_rows)
    scale = jnp.where(is_ctx, csc_ref[...], sc_ref[...])
    shift = jnp.where(is_ctx, csh_ref[...], sh_ref[...])
    h = _prenorm(x_ref[...], g_ref[...], scale, shift).astype(BF16)
    n = w_ref.shape[1]
    w = BRANCH_W
    for lo in range(0, n, PROJ_CHUNK):
        hi = min(lo + PROJ_CHUNK, n)
        res = jnp.dot(h, w_ref[:, lo:hi], preferred_element_type=F32)
        for c0 in range(lo, hi, w):
            part = res[:, c0 - lo:c0 - lo + w]
            if rope_cols[0] <= c0 < rope_cols[1]:
                part = _rope(part, cos_ref[...], sin_ref[...])
            o_ref[:, c0:c0 + w] = part.astype(o_ref.dtype)


def _project(xs, mod, gain, w, n_ctx_rows, sc_idx, sh_idx, cos, sin, rope_cols):
    batch, rows, d = xs.shape
    n = w.shape[1]
    tm = _row_tile(rows, 544)
    assert n % BRANCH_W == 0 and rope_cols[0] % BRANCH_W == 0 and rope_cols[1] % BRANCH_W == 0
    tab = pl.BlockSpec((tm, BRANCH_W), lambda b, i: (i, 0))
    return pl.pallas_call(
        functools.partial(_proj_kernel, n_ctx_rows=n_ctx_rows, rope_cols=rope_cols),
        grid=(batch, rows // tm),
        in_specs=[pl.BlockSpec((None, tm, d), lambda b, i: (b, i, 0))]
        + _mod_specs(sc_idx, batch) + _mod_specs(sh_idx, batch)
        + [_vec_spec(gain.shape), _vec_spec(w.shape), tab, tab],
        out_specs=pl.BlockSpec((None, tm, n), lambda b, i: (b, i, 0)),
        out_shape=jax.ShapeDtypeStruct((batch, rows, n), BF16),
        compiler_params=_cparams(("parallel", "parallel")),
        name="prenorm_proj",
    )(xs, mod, mod, mod, mod, gain, w, cos, sin)


def _head_block_ones():
    idx = np.arange(BRANCH_W)
    return (idx[:, None] // HEAD == idx[None, :] // HEAD).astype(np.float32)


N_LEVELS = 6


def _level_mask(ri, ci, s, reverse):
    b = 1 << s
    blk = (ri // (2 * b)) == (ci // (2 * b))
    hi_r, hi_c = (ri // b) % 2 == 1, (ci // b) % 2 == 1
    return blk & ((hi_c & ~hi_r) if reverse else (hi_r & ~hi_c))


SCAN_HEADS = 2
GROUP_W = SCAN_HEADS * HEAD
N_GROUPS = N_HEADS // SCAN_HEADS


def _scan_masks(reverse):
    n = SCAN_HEADS * CHUNK
    ri = np.arange(CHUNK)[:, None]
    ci = np.arange(n)[None, :] % CHUNK
    strict, incl = (ci > ri, ci >= ri) if reverse else (ci < ri, ci <= ri)
    compact = [strict, incl, ci == ri] + [_level_mask(ri, ci, s, reverse) for s in range(N_LEVELS)]
    r = np.arange(n)[:, None]
    c = np.arange(n)[None, :]
    block = [(r // CHUNK) == (c // CHUNK), r == c]
    return np.stack(compact).astype(np.float32), np.stack(block).astype(np.float32)


EXACT_TERMS = 3
NORM_TERMS = 2


def _tri(reverse):
    i = np.arange(CHUNK)
    m = (i[None, :] >= i[:, None]) if reverse else (i[None, :] <= i[:, None])
    return np.tile(m.astype(np.float32), (1, EXACT_TERMS))


def _split_terms(x, terms):
    out = []
    for _ in range(terms - 1):
        part = x.astype(BF16)
        out.append(part)
        x = x - part.astype(F32)
    out.append(x.astype(BF16))
    return out


def _mm_exact(a, b, dims=NN, split="a", terms=EXACT_TERMS):
    assert dims == NN
    if split == "a":
        lhs = jnp.concatenate(_split_terms(a, terms), axis=1)
        rhs = jnp.concatenate([b.astype(BF16)] * terms, axis=0)
    else:
        lhs = a.astype(BF16)
        rhs = jnp.concatenate(_split_terms(b, terms), axis=0)
    return lax.dot_general(lhs, rhs, dims, preferred_element_type=F32)


C_STRICT, C_INCL, C_EYE, C_LVL0 = 0, 1, 2, 3
B_SAME, B_EYE = 0, 1
SCAN_BATCH = 8
SCAN_ROWS = 256
assert CHUNK == HEAD and (1 << N_LEVELS) == CHUNK


def _scan_chunk(vals, st, tri, ones, cm_ref, bm_ref, same_bf, *, reverse, scalar_decay, lowrank, mm_mode):
    same = bm_ref[B_SAME]

    def get(name):
        return vals[name]

    def expand(x):
        return jnp.concatenate([x.astype(BF16)] * SCAN_HEADS, axis=0) * same_bf

    def keep(x, k):
        return jnp.where(cm_ref[k] > 0.5, x, 0.0)

    logw = get("w")
    r, k, v = get("r"), get("k"), get("v")
    cum = _mm_exact(tri, logw, split="b")
    cum_x = cum - logw
    last = 0 if reverse else CHUNK - 1
    total = cum[last:last + 1, :]
    to_end = jnp.exp(total - cum)

    if scalar_decay:
        diag = jnp.concatenate([cum] * SCAN_HEADS, axis=0) * bm_ref[B_EYE]
        cum_row = _mm_exact(ones, diag, split="b")
        d_ii = jnp.exp(jnp.where(cm_ref[C_INCL] > 0.5, cum - cum_row, -1e30))
        r_q, k_q = r, k
    else:
        ref_row = cum[CHUNK // 2:CHUNK // 2 + 1, :]
        p_inv = jnp.exp(ref_row - cum)
        r_q, k_q = r * jnp.exp(cum - ref_row), k * p_inv

    k_e, v_e = expand(k_q), expand(v)
    r_abs = r * jnp.exp(cum)
    if not lowrank:
        s_k = _mm(r_q, k_e, NT, mm_mode)
        a_rk = s_k * d_ii if scalar_decay else keep(s_k, C_INCL)
        y = _mm(r_abs, st, NT, mm_mode) + _mm(a_rk, v_e, NN, mm_mode)
        upd = _mm(v, k * to_end, TN, mm_mode)
        return y, st * jnp.exp(total) + upd * same

    a, b = get("a"), get("b")
    if scalar_decay:
        d_xi = jnp.exp(jnp.where(cm_ref[C_STRICT] > 0.5, cum_x - cum_row, -1e30))
        a_q, b_q = a, b
    else:
        a_q, b_q = a * jnp.exp(cum_x - ref_row), b * p_inv
    lhs = jnp.concatenate([a_q, r_q], axis=0)
    s_b = _mm(lhs, expand(b_q), NT, mm_mode)
    s_k = _mm(lhs, k_e, NT, mm_mode)
    if scalar_decay:
        a_ab, a_rb = s_b[0:CHUNK] * d_xi, s_b[CHUNK:] * d_ii
        a_ak, a_rk = s_k[0:CHUNK] * d_xi, s_k[CHUNK:] * d_ii
    else:
        a_ab, a_rb = keep(s_b[0:CHUNK], C_STRICT), keep(s_b[CHUNK:], C_INCL)
        a_ak, a_rk = keep(s_k[0:CHUNK], C_STRICT), keep(s_k[CHUNK:], C_INCL)

    inv = cm_ref[C_EYE] + a_ab * cm_ref[C_LVL0]
    for s in range(1, N_LEVELS):
        c_s = expand(a_ab * cm_ref[C_LVL0 + s])
        inv = inv + _mm(_mm(inv, c_s, NN, mm_mode), expand(inv), NN, mm_mode)

    a_abs = a * jnp.exp(cum_x)
    from_state = _mm(jnp.concatenate([a_abs, r_abs], axis=0), st, NT, mm_mode)
    from_v = _mm(jnp.concatenate([a_ak, a_rk], axis=0), v_e, NN, mm_mode)
    both = from_state + from_v
    z = _mm(inv, expand(both[0:CHUNK]), NN, mm_mode)
    y = both[CHUNK:] + _mm(a_rb, expand(z), NN, mm_mode)
    upd = _mm(jnp.concatenate([v, z], axis=0), jnp.concatenate([k * to_end, b * to_end], axis=0), TN, mm_mode)
    return y, st * jnp.exp(total) + upd * same


def _scan_kernel(*refs, reverse, scalar_decay, lowrank, slots, sources, finish, mm_mode):
    it = iter(refs)
    src_refs = {name: next(it) for name in sources}
    sh_ref = src_refs[sources[0]]
    cm_ref, bm_ref, tri_ref = next(it), next(it), next(it)
    if finish:
        ob_ref, fin_ref, p1_ref, p2_ref, avg_ref = next(it), next(it), next(it), next(it), next(it)
    o_ref, st_ref = next(it), next(it)

    @pl.when(pl.program_id(1) == 0)
    def _():
        st_ref[...] = jnp.zeros_like(st_ref)

    nb, block_rows = sh_ref.shape[0], sh_ref.shape[1]
    n_chunks = block_rows // CHUNK
    n = nb * N_GROUPS
    tri = jnp.broadcast_to(tri_ref[...], (n,) + tri_ref.shape)
    ones = jnp.ones((n, CHUNK, EXACT_TERMS * SCAN_HEADS * CHUNK), F32)
    chunk = functools.partial(_scan_chunk, cm_ref=cm_ref, bm_ref=bm_ref, same_bf=bm_ref[B_SAME].astype(BF16),
                              reverse=reverse, scalar_decay=scalar_decay, lowrank=lowrank, mm_mode=mm_mode)

    def step(c, carry):
        rows = pl.ds(pl.multiple_of((n_chunks - 1 - c if reverse else c) * CHUNK, CHUNK), CHUNK)

        def groups(name):
            src, idx = slots[name]
            ref = src_refs[src]
            parts = [ref[:, rows, idx * BRANCH_W + g * GROUP_W:idx * BRANCH_W + (g + 1) * GROUP_W].astype(F32)
                     for g in range(N_GROUPS)]
            return jnp.stack(parts, axis=1).reshape(n, CHUNK, GROUP_W)

        vals = {name: groups(name) for name in slots}
        y, st_new = jax.vmap(chunk)(vals, st_ref[...], tri, ones)
        st_ref[...] = st_new
        y = y.reshape(nb, N_GROUPS, CHUNK, GROUP_W)
        y = jnp.concatenate([y[:, g] for g in range(N_GROUPS)], axis=-1)
        if not finish:
            o_ref[:, rows, :] = y
            return carry

        y = (y + ob_ref[:, rows, :]).reshape(nb * CHUNK, BRANCH_W)
        avg = avg_ref[...]
        if finish == "groupnorm":
            gate = fin_ref[:, rows, 0:BRANCH_W].reshape(y.shape)
            bonus = fin_ref[:, rows, BRANCH_W:2 * BRANCH_W].reshape(y.shape)
            cen = y - _mm_exact(y, avg)
            var = _mm_exact(cen * cen, avg)
            yn = cen * lax.rsqrt(var + A_GN_EPS) * p1_ref[...] + p2_ref[...]
            out = (yn + bonus) * gate
        else:
            ms = _mm_exact(y * y, avg)
            out = y * lax.rsqrt(ms + NORM_EPS) * p1_ref[...] * _silu(fin_ref[:, rows, :].reshape(y.shape))
        o_ref[:, rows, :] = out.reshape(nb, CHUNK, BRANCH_W)
        return carry

    lax.fori_loop(0, n_chunks, step, 0)


def _scan(srcs, n_ctx_rows, *, reverse, scalar_decay, lowrank, slots, finish=None, fin_args=None,
          mm_mode="bf16"):
    sources = tuple(srcs)
    batch, rows, _ = srcs[sources[0]].shape
    nc, nctx = rows // SCAN_ROWS, n_ctx_rows // SCAN_ROWS
    nb = math.gcd(batch, SCAN_BATCH)

    if reverse:
        def chunk(n):
            return jnp.where(n < nctx, nctx - 1 - n, nc - 1 + nctx - n)
    else:
        def chunk(n):
            return n

    def row_spec(width):
        return pl.BlockSpec((nb, SCAN_ROWS, width), lambda b, n: (b, chunk(n), 0))

    def const_spec(shape):
        zeros = (0,) * len(shape)
        return pl.BlockSpec(shape, lambda b, n: zeros)

    cmask, bmask = (jnp.asarray(m) for m in _scan_masks(reverse))
    tri = jnp.asarray(_tri(reverse))
    in_specs = [row_spec(srcs[s].shape[-1]) for s in sources]
    in_specs += [const_spec(cmask.shape), const_spec(bmask.shape), const_spec(tri.shape)]
    args = [srcs[s] for s in sources] + [cmask, bmask, tri]
    if finish:
        ob, fin, p1, p2 = fin_args
        avg = jnp.asarray(_head_block_ones() / HEAD)
        in_specs += [row_spec(BRANCH_W), row_spec(fin.shape[-1]), const_spec(p1.shape), const_spec(p2.shape),
                     const_spec(avg.shape)]
        args += [ob, fin, p1, p2, avg]
    kern = functools.partial(_scan_kernel, reverse=reverse, scalar_decay=scalar_decay, lowrank=lowrank,
                             slots=slots, sources=sources, finish=finish, mm_mode=mm_mode)
    return pl.pallas_call(
        kern,
        grid=(batch // nb, nc),
        in_specs=in_specs,
        out_specs=row_spec(BRANCH_W),
        out_shape=jax.ShapeDtypeStruct((batch, rows, BRANCH_W), F32),
        scratch_shapes=[pltpu.VMEM((nb * N_GROUPS, GROUP_W, GROUP_W), F32)],
        compiler_params=_cparams(("parallel", "arbitrary")),
        name="dplr_scan_" + ("bwd" if reverse else "fwd"),
    )(*args)


def _bidir_scan(sh, dks, dws, n_ctx_rows, fin, p1, p2, *, finish, **kw):
    def srcs(d):
        out = {"sh": sh, "dw": dws[d]}
        if dks is not None:
            out["dk"] = dks[d]
        return out

    ob = _scan(srcs(1), n_ctx_rows, reverse=True, **kw)
    return _scan(srcs(0), n_ctx_rows, reverse=False, finish=finish, fin_args=(ob, fin, p1, p2), **kw)


def _halo_specs(width, rows, col):
    per = PREP_ROWS // HALO
    last = rows // HALO - 1
    own = pl.BlockSpec((None, PREP_ROWS, width), lambda b, i: (b, i, col))
    prev = pl.BlockSpec((None, HALO, width), lambda b, i: (b, jnp.maximum(i * per - 1, 0), col))
    nxt = pl.BlockSpec((None, HALO, width), lambda b, i: (b, jnp.minimum((i + 1) * per, last), col))
    return [own, prev, nxt]


def _seq_edges(n_ctx_blocks):
    i = pl.program_id(1)
    first = (i == 0) | (i == n_ctx_blocks)
    lastb = (i == n_ctx_blocks - 1) | (i == pl.num_programs(1) - 1)
    return first, lastb


def _vec_spec(shape):
    zeros = (0,) * len(shape)
    return pl.BlockSpec(shape, lambda b, i: zeros)


def _shift_matrix():
    m = np.zeros((PREP_ROWS, PREP_ROWS + 2 * HALO), np.float32)
    t = np.arange(PREP_ROWS)
    m[t[1:], t[1:] - 1] = 0.5
    m[t[:-1], t[:-1] + 1] = 0.5
    m[0, PREP_ROWS + HALO - 1] = 0.5
    m[PREP_ROWS - 1, PREP_ROWS + HALO] = 0.5
    return m


def _rwkv_prep_kernel(p_ref, prev_ref, next_ref, shift_ref, mu_ref, w2_ref, a2_ref, g2_ref, w0_ref, a0_ref, kk_ref,
                      ka_ref, rk_ref, ones_ref, sh_ref, dk0_ref, dk1_ref, dw0_ref, dw1_ref, fin_ref, *,
                      n_ctx_blocks, mode):
    first, lastb = _seq_edges(n_ctx_blocks)
    x = p_ref[...]
    zero = jnp.zeros((), x.dtype)
    xe = jnp.concatenate([x, jnp.where(first, zero, prev_ref[...]), jnp.where(lastb, zero, next_ref[...])], axis=0)
    around = jnp.dot(shift_ref[...], xe, preferred_element_type=F32)
    x = x.astype(F32)
    xm = x + (around - x) * mu_ref[...]
    w = BRANCH_W
    r, k, v, lo = xm[:, 0:w], xm[:, w:2 * w], xm[:, 2 * w:3 * w], xm[:, 3 * w:4 * w]
    ones = ones_ref[...]
    th, sg = jnp.tanh(lo), _sigmoid(lo)
    gate = _mm(sg, g2_ref[...], mode=mode)
    kx = k * kk_ref[...]
    kk = kx * lax.rsqrt(_mm_exact(kx * kx, ones, terms=NORM_TERMS) + 1e-6)
    sh_ref[:, 0:w] = r.astype(BF16)
    sh_ref[:, w:2 * w] = v.astype(BF16)
    sh_ref[:, 2 * w:3 * w] = (-kk).astype(BF16)
    bonus = jnp.zeros_like(v)
    for d, (dk_ref, dw_ref) in enumerate(((dk0_ref, dw0_ref), (dk1_ref, dw1_ref))):
        w_raw = w0_ref[d:d + 1, :] + _mm(th, w2_ref[d], mode=mode)
        dw_ref[...] = -math.exp(-0.5) * _sigmoid(w_raw)
        a = _sigmoid(a0_ref[d:d + 1, :] + _mm(lo, a2_ref[d], mode=mode))
        kd = k * (1.0 + (a - 1.0) * ka_ref[...])
        dk_ref[:, 0:w] = kd.astype(BF16)
        dk_ref[:, w:2 * w] = (kk * a).astype(BF16)
        bonus = bonus + _mm_exact(r * kd * rk_ref[...], ones) * v
    fin_ref[:, 0:w] = gate
    fin_ref[:, w:2 * w] = bonus


def _prep_outs(batch, rows, spec):
    w = BRANCH_W
    specs = [pl.BlockSpec((None, PREP_ROWS, n * w), lambda b, i: (b, i, 0)) for n, _ in spec]
    shapes = [jax.ShapeDtypeStruct((batch, rows, n * w), dt) for n, dt in spec]
    return specs, shapes


def _rwkv_prep(p, n_ctx_rows, mu, w2p, a2p, g2p, w0, a0, kk, ka, rk, mode="bf16", width=None, col=0):
    batch, rows, full = p.shape
    width = width or full
    ones = jnp.asarray(_head_block_ones())
    assert p.dtype == BF16
    consts = [jnp.asarray(_shift_matrix(), BF16), mu, w2p, a2p, g2p, w0, a0, kk, ka, rk, ones]
    out_specs, out_shape = _prep_outs(batch, rows, [(3, BF16), (2, BF16), (2, BF16), (1, F32), (1, F32), (2, F32)])
    sh, dk0, dk1, dw0, dw1, fin = pl.pallas_call(
        functools.partial(_rwkv_prep_kernel, n_ctx_blocks=n_ctx_rows // PREP_ROWS, mode=mode),
        grid=(batch, rows // PREP_ROWS),
        in_specs=_halo_specs(width, rows, col) + [_vec_spec(c.shape) for c in consts],
        out_specs=out_specs,
        out_shape=out_shape,
        compiler_params=_cparams(("parallel", "parallel")),
        name="rwkv_prep",
    )(p, p, p, *consts)
    return sh, (dk0, dk1), (dw0, dw1), fin


def _gdn_prep_kernel(p_ref, prev_ref, next_ref, conv_ref, alog_ref, dt_ref, eb_ref, ea_ref, ones_ref,
                     sh_ref, dk0_ref, dk1_ref, dw0_ref, dw1_ref, fin_ref, *, n_ctx_blocks):
    first, lastb = _seq_edges(n_ctx_blocks)
    w = BRANCH_W
    x = p_ref[:, 0:3 * w].astype(F32)
    top = jnp.where(first, 0.0, prev_ref[:, 0:3 * w].astype(F32))
    bot = jnp.where(lastb, 0.0, next_ref[:, 0:3 * w].astype(F32))
    xe = jnp.concatenate([top, x, bot], axis=0)
    ext = PREP_ROWS + 2 * HALO
    acc = jnp.zeros_like(x)
    for s in range(B_CONV):
        shift = (B_CONV // 2 - s) % ext
        rolled = xe if shift == 0 else pltpu.roll(xe, shift, axis=0)
        acc = acc + rolled[HALO:HALO + PREP_ROWS] * conv_ref[s:s + 1, :]
    qkv = _silu(acc)
    ones = ones_ref[...]

    def l2n(t):
        return t * lax.rsqrt(_mm_exact(t * t, ones, terms=NORM_TERMS) + 1e-6)

    q = l2n(qkv[:, 0:w]) * (HEAD ** -0.5)
    k = l2n(qkv[:, w:2 * w])
    v = qkv[:, 2 * w:3 * w]
    sh_ref[:, 0:w] = q.astype(BF16)
    sh_ref[:, w:2 * w] = v.astype(BF16)
    sh_ref[:, 2 * w:3 * w] = k.astype(BF16)
    sr = p_ref[:, 4 * w:4 * w + 128].astype(F32)
    beta_all = _sigmoid(sr)
    g_all = -jnp.exp(alog_ref[...]) * _softplus(sr + dt_ref[...])
    for d, (dk_ref, dw_ref) in enumerate(((dk0_ref, dw0_ref), (dk1_ref, dw1_ref))):
        beta = _mm_exact(beta_all, eb_ref[d])
        g = _mm_exact(g_all, ea_ref[d])
        kb = k * beta
        dw_ref[...] = g
        dk_ref[:, 0:w] = kb.astype(BF16)
        dk_ref[:, w:2 * w] = (-jnp.exp(g) * kb).astype(BF16)
    fin_ref[...] = p_ref[:, 3 * w:4 * w].astype(F32)


def _gdn_prep(p, n_ctx_rows, conv_w, alog_vec, dt_vec, eb, ea, width=None, col=0):
    batch, rows, full = p.shape
    width = width or full
    ones = jnp.asarray(_head_block_ones())
    consts = [conv_w, alog_vec, dt_vec, eb, ea, ones]
    out_specs, out_shape = _prep_outs(batch, rows, [(3, BF16), (2, BF16), (2, BF16), (1, F32), (1, F32), (1, F32)])
    sh, dk0, dk1, dw0, dw1, fin = pl.pallas_call(
        functools.partial(_gdn_prep_kernel, n_ctx_blocks=n_ctx_rows // PREP_ROWS),
        grid=(batch, rows // PREP_ROWS),
        in_specs=_halo_specs(width, rows, col) + [_vec_spec(c.shape) for c in consts],
        out_specs=out_specs,
        out_shape=out_shape,
        compiler_params=_cparams(("parallel", "parallel")),
        name="gdn_prep",
    )(p, p, p, *consts)
    return sh, (dk0, dk1), (dw0, dw1), fin


def _gla_prep_kernel(p_ref, gw_ref, gb_ref, lane_ref, sh_ref, dw0_ref, dw1_ref, fin_ref, *, mode):
    w = BRANCH_W
    sh_ref[:, 0:w] = (p_ref[:, 0:w].astype(F32) * (C_DK ** -0.5)).astype(BF16)
    sh_ref[:, w:2 * w] = p_ref[:, 2 * w:3 * w]
    sh_ref[:, 2 * w:3 * w] = p_ref[:, w:2 * w]
    lo = p_ref[:, 4 * w:4 * w + 128]
    for d, dw_ref in enumerate((dw0_ref, dw1_ref)):
        z = _mm(lo, gw_ref[d], mode=mode) + gb_ref[d:d + 1, :]
        dw_ref[...] = (-_softplus(-z) / C_GATE_NORM) * lane_ref[...]
    fin_ref[...] = p_ref[:, 3 * w:4 * w].astype(F32)


def _gla_prep(p, gwp, gbp, lane_mask, mode="bf16", width=None, col=0):
    batch, rows, full = p.shape
    width = width or full
    consts = [gwp, gbp, lane_mask]
    out_specs, out_shape = _prep_outs(batch, rows, [(3, BF16), (1, F32), (1, F32), (1, F32)])
    sh, dw0, dw1, fin = pl.pallas_call(
        functools.partial(_gla_prep_kernel, mode=mode),
        grid=(batch, rows // PREP_ROWS),
        in_specs=[pl.BlockSpec((None, PREP_ROWS, width), lambda b, i: (b, i, col))]
        + [_vec_spec(c.shape) for c in consts],
        out_specs=out_specs,
        out_shape=out_shape,
        compiler_params=_cparams(("parallel", "parallel")),
        name="gla_prep",
    )(p, *consts)
    return sh, None, (dw0, dw1), fin


NEG_BIG = -1e30


def _window_bias():
    iq = np.arange(2 * ATT_BLOCK)[:, None] % ATT_BLOCK
    ik = np.arange(ATT_BLOCK)[None, :]
    ok = np.stack([ik >= iq, np.ones_like(ik >= iq), ik <= iq])
    assert WINDOW == ATT_BLOCK
    return np.where(ok, 0.0, NEG_BIG).astype(np.float32)


def _attn_kernel(sink_ref, q_ref, kc_ref, vc_ref, kp_ref, ko_ref, kn_ref, vp_ref, vo_ref, vn_ref,
                 bias_ref, o_ref, *, n_ctx_rows, n_lat_rows):
    t = pl.program_id(1)
    blk = ATT_BLOCK
    q = q_ref[...] * (HEAD ** -0.5)
    k_band = [kp_ref, ko_ref, kn_ref]
    v_band = [vp_ref, vo_ref, vn_ref]

    q_lat = t * blk >= n_ctx_rows
    in_seq = [q_lat & ((t - 1) * blk >= n_ctx_rows), q_lat, q_lat & ((t + 1) * blk < n_ctx_rows + n_lat_rows)]
    bias = [jnp.where(in_seq[j], bias_ref[j], NEG_BIG) for j in range(3)]
    lane = lax.broadcasted_iota(jnp.int32, (blk, 2 * HEAD), 1)
    row2 = lax.broadcasted_iota(jnp.int32, (2 * blk, 1), 0)
    for g in range(D_KV_HEADS):
        cols = slice(g * 2 * HEAD, (g + 1) * 2 * HEAD)
        qg = q[:, cols]
        zero = jnp.zeros((), qg.dtype)
        qs = jnp.concatenate([jnp.where(lane < HEAD, qg, zero), jnp.where(lane >= HEAD, qg, zero)], axis=0)
        s_ctx = _mm(qs, kc_ref[:, cols], NT)
        s_band = [_mm(qs, k_band[j][:, cols], NT) + bias[j] for j in range(3)]
        sink = jnp.where(row2 < blk, sink_ref[2 * g], sink_ref[2 * g + 1])
        m = jnp.maximum(jnp.max(s_ctx, axis=-1, keepdims=True), sink)
        for s in s_band:
            m = jnp.maximum(m, jnp.max(s, axis=-1, keepdims=True))
        p_ctx = jnp.exp(s_ctx - m)
        den = jnp.sum(p_ctx, axis=-1, keepdims=True) + jnp.exp(sink - m)
        acc = _mm(p_ctx, vc_ref[:, cols], NN)
        for j in range(3):
            pj = jnp.exp(s_band[j] - m)
            den = den + jnp.sum(pj, axis=-1, keepdims=True)
            acc = acc + _mm(pj, v_band[j][:, cols], NN)
        og = acc / den
        o_ref[:, cols] = jnp.where(lane < HEAD, og[0:blk], og[blk:2 * blk])


def _attention(p, sink, n_ctx_rows, col0=0):
    batch, rows, _ = p.shape
    blk = ATT_BLOCK
    nb = rows // blk
    w = BRANCH_W

    def band(col, off):
        def index(b, t):
            return (b, jnp.clip(t + off, 0, nb - 1), col0 + col)
        return pl.BlockSpec((None, blk, w), index)

    ctx = lambda col: pl.BlockSpec((None, n_ctx_rows, w), lambda b, t: (b, 0, col0 + col))
    kern = functools.partial(_attn_kernel, n_ctx_rows=n_ctx_rows, n_lat_rows=rows - n_ctx_rows)
    bias = jnp.asarray(_window_bias())
    assert n_ctx_rows % blk == 0
    return pl.pallas_call(
        kern,
        grid=(batch, nb),
        in_specs=[pl.BlockSpec(memory_space=pltpu.SMEM), band(0, 0), ctx(1), ctx(2),
                  band(1, -1), band(1, 0), band(1, 1), band(2, -1), band(2, 0), band(2, 1),
                  _vec_spec(bias.shape)],
        out_specs=pl.BlockSpec((None, blk, w), lambda b, t: (b, t, 0)),
        out_shape=jax.ShapeDtypeStruct((batch, rows, w), F32),
        compiler_params=_cparams(("parallel", "parallel")),
        name="window_attn",
    )(sink, p, p, p, p, p, p, p, p, p, bias)


def _merge_kernel(x_ref, sc_ref, csc_ref, sh_ref, csh_ref, gm_ref, cgm_ref, g0_ref, g1_ref,
                  ya_ref, yb_ref, yc_ref, yd_ref, wg_ref, gb_ref, wb_ref, wo_ref, o_ref, *, n_ctx_rows):
    x = x_ref[...]
    is_ctx = _ctx_rows(x.shape[0], n_ctx_rows)
    scale = jnp.where(is_ctx, csc_ref[...], sc_ref[...])
    shift = jnp.where(is_ctx, csh_ref[...], sh_ref[...])
    gmod = jnp.where(is_ctx, cgm_ref[...], gm_ref[...])
    h = _prenorm(x, g0_ref[...], scale, shift).astype(BF16)
    acc = jnp.zeros(x.shape, F32)
    for i, y_ref in enumerate((ya_ref, yb_ref, yc_ref, yd_ref)):
        pre = jnp.dot(h, wg_ref[:, i * D_MODEL:(i + 1) * D_MODEL], preferred_element_type=F32)
        gate = _sigmoid(pre + gb_ref[i:i + 1, :])
        acc = acc + gate * jnp.dot(y_ref[...].astype(BF16), wb_ref[i], preferred_element_type=F32)
    out = jnp.dot(acc.astype(BF16), wo_ref[...], preferred_element_type=F32)
    ms = jnp.mean(out * out, axis=-1, keepdims=True)
    o_ref[...] = x + gmod * (out * lax.rsqrt(ms + NORM_EPS) * g1_ref[...])


def _merge(xs, mod, gain0, gain1, ys, wg, gate_b, wb, wo, n_ctx_rows):
    batch, rows, d = xs.shape
    tm = _row_tile(rows, 272)
    tile = lambda width: pl.BlockSpec((None, tm, width), lambda b, i: (b, i, 0))
    consts = [wg, gate_b, wb, wo]
    return pl.pallas_call(
        functools.partial(_merge_kernel, n_ctx_rows=n_ctx_rows),
        grid=(batch, rows // tm),
        in_specs=[tile(d)] + _mod_specs(1, batch) + _mod_specs(0, batch) + _mod_specs(2, batch)
        + [_vec_spec(gain0.shape), _vec_spec(gain1.shape)]
        + [tile(BRANCH_W)] * 4 + [_vec_spec(c.shape) for c in consts],
        out_specs=tile(d),
        out_shape=jax.ShapeDtypeStruct((batch, rows, d), F32),
        compiler_params=_cparams(("parallel", "parallel")),
        name="merge_out",
    )(xs, *([mod] * 6), gain0, gain1, *ys, *consts)


FFN_CHUNK = 512


def _ffn_kernel(x_ref, sc_ref, csc_ref, sh_ref, csh_ref, gm_ref, cgm_ref, g2_ref, g3_ref, w1_ref, w2_ref,
                o_ref, *, n_ctx_rows):
    x = x_ref[...]
    is_ctx = _ctx_rows(x.shape[0], n_ctx_rows)
    scale = jnp.where(is_ctx, csc_ref[...], sc_ref[...])
    shift = jnp.where(is_ctx, csh_ref[...], sh_ref[...])
    h = _prenorm(x, g2_ref[...], scale, shift).astype(BF16)
    hidden = w2_ref.shape[0]
    out = jnp.zeros(x.shape, F32)
    for lo in range(0, hidden, FFN_CHUNK):
        hi = min(lo + FFN_CHUNK, hidden)
        gt = jnp.dot(h, w1_ref[:, lo:hi], preferred_element_type=F32)
        up = jnp.dot(h, w1_ref[:, hidden + lo:hidden + hi], preferred_element_type=F32)
        out = out + jnp.dot((_silu(gt) * up).astype(BF16), w2_ref[lo:hi, :], preferred_element_type=F32)
    ms = jnp.mean(out * out, axis=-1, keepdims=True)
    gmod = jnp.where(is_ctx, cgm_ref[...], gm_ref[...])
    o_ref[...] = x + gmod * (out * lax.rsqrt(ms + NORM_EPS) * g3_ref[...])


def _ffn(xs, mod, gain2, gain3, w1, w2, n_ctx_rows):
    batch, rows, d = xs.shape
    tm = _row_tile(rows, 544)
    tile = pl.BlockSpec((None, tm, d), lambda b, i: (b, i, 0))
    consts = [gain2, gain3, w1, w2]
    return pl.pallas_call(
        functools.partial(_ffn_kernel, n_ctx_rows=n_ctx_rows),
        grid=(batch, rows // tm),
        in_specs=[tile] + _mod_specs(4, batch) + _mod_specs(3, batch) + _mod_specs(5, batch)
        + [_vec_spec(c.shape) for c in consts],
        out_specs=tile,
        out_shape=jax.ShapeDtypeStruct((batch, rows, d), F32),
        compiler_params=_cparams(("parallel", "parallel")),
        name="swiglu",
    )(xs, *([mod] * 6), *consts)


def _pad_cols(w, width):
    return jnp.pad(w, ((0, 0), (0, width - w.shape[1])))


def _layer_weights(w_in, mu, w2, a2, g2, conv, a_log, dt_bias, gw2, gb):
    wts = {}
    off_b, off_c, off_d, off_g = A_IN, A_IN + B_IN, A_IN + B_IN + C_IN, A_IN + B_IN + C_IN + D_IN
    w = BRANCH_W
    wts["wa"] = _pad_cols(w_in[:, 0:A_IN], 4 * w).astype(BF16)
    wts["mu"] = _pad_cols(mu[None, :], 4 * w)
    lo = np.cumsum((0,) + A_LORA)
    place = lambda m, r0: jnp.zeros((w, w), F32).at[r0:r0 + m.shape[0], :].set(m)
    wts["w2p"] = jnp.stack([place(w2[0], lo[0]), place(w2[1], lo[1])])
    wts["a2p"] = jnp.stack([place(a2[0], lo[2]), place(a2[1], lo[3])])
    wts["g2p"] = place(g2, lo[4])

    wb = w_in[:, off_b:off_b + B_IN]
    wts["wb"] = jnp.concatenate([wb[:, 0:3 * w], wb[:, 3 * w + 16:], _pad_cols(wb[:, 3 * w:3 * w + 16], 128)],
                                axis=1).astype(BF16)
    nh = N_HEADS
    expand = np.zeros((4, 128, w), np.float32)
    for grp in range(4):
        for h in range(nh):
            expand[grp, grp * nh + h, h * HEAD:(h + 1) * HEAD] = 1.0
    wts["eb"] = jnp.asarray(expand[0:2])
    wts["ea"] = jnp.asarray(expand[2:4])
    vec = lambda t: jnp.zeros((1, 128), F32).at[0, 2 * nh:4 * nh].set(t.reshape(-1))
    wts["alog"] = vec(a_log)
    wts["dt"] = vec(dt_bias)
    wts["conv"] = conv

    wc = w_in[:, off_c:off_c + C_IN]
    pad_heads = lambda m: jnp.pad(m.reshape(m.shape[0], nh, C_DK),
                                  ((0, 0), (0, 0), (0, HEAD - C_DK))).reshape(m.shape[0], w)
    qc, kc, vc = wc[:, 0:C_QK], wc[:, C_QK:2 * C_QK], wc[:, 2 * C_QK:2 * C_QK + w]
    loc = wc[:, 2 * C_QK + w:2 * C_QK + w + 2 * C_GATE_R]
    gc = wc[:, 2 * C_QK + w + 2 * C_GATE_R:]
    wts["wc"] = jnp.concatenate([pad_heads(qc), pad_heads(kc), vc, gc, _pad_cols(loc, 128)], axis=1).astype(BF16)
    gwp = jnp.zeros((2, 128, w), F32)
    for d in range(2):
        gwp = gwp.at[d, d * C_GATE_R:(d + 1) * C_GATE_R, :].set(pad_heads(gw2[d]))
    wts["gwp"] = gwp
    wts["gbp"] = pad_heads(gb)
    wts["glane"] = jnp.asarray((np.arange(w) % HEAD < C_DK).astype(np.float32))[None, :]

    wd = w_in[:, off_d:off_d + D_IN]
    qd = wd[:, 0:w]
    dup = lambda m: jnp.concatenate([m[:, 0:HEAD], m[:, 0:HEAD], m[:, HEAD:], m[:, HEAD:]], axis=1)
    wts["wd"] = jnp.concatenate([qd, dup(wd[:, w:w + 2 * HEAD]), dup(wd[:, w + 2 * HEAD:])], axis=1).astype(BF16)
    wts["wg"] = w_in[:, off_g:].astype(BF16)
    return wts


def _rope_tables(n_ctx_rows, n_lat_rows):
    quarter = HEAD // 4
    inv = ROPE_BASE ** (-np.arange(quarter, dtype=np.float32) / quarter)
    pos = np.arange(n_lat_rows)
    rows = (pos // GRID_W).astype(np.float32)
    cols = (pos % GRID_W).astype(np.float32)
    inv = jnp.asarray(inv)
    ang_r = jnp.asarray(rows)[:, None] * inv[None, :]
    ang_c = jnp.asarray(cols)[:, None] * inv[None, :]
    cos = jnp.concatenate([jnp.cos(ang_r)] * 2 + [jnp.cos(ang_c)] * 2, axis=1)
    sin = jnp.concatenate([-jnp.sin(ang_r), jnp.sin(ang_r), -jnp.sin(ang_c), jnp.sin(ang_c)], axis=1)
    cos = jnp.concatenate([jnp.ones((n_ctx_rows, HEAD), F32), cos], axis=0)
    sin = jnp.concatenate([jnp.zeros((n_ctx_rows, HEAD), F32), sin], axis=0)
    return jnp.tile(cos, (1, N_HEADS)), jnp.tile(sin, (1, N_HEADS))


RWKV_SLOTS = {"r": ("sh", 0), "v": ("sh", 1), "a": ("sh", 2), "w": ("dw", 0), "k": ("dk", 0), "b": ("dk", 1)}
GDN_SLOTS = RWKV_SLOTS
GLA_SLOTS = {"r": ("sh", 0), "v": ("sh", 1), "k": ("sh", 2), "w": ("dw", 0)}


def kernel(x, c, ctx, c_ctx, ada_w, ada_b, norm_g, w_in, gate_b, w_branch, w_out, rwkv_mu, rwkv_w0, rwkv_w2, rwkv_a0, rwkv_a2, rwkv_g2, rwkv_kk, rwkv_ka, rwkv_rk, rwkv_ln_g, rwkv_ln_b, gdn_conv, gdn_a_log, gdn_dt_bias, gdn_norm_g, gla_gw2, gla_gb, gla_norm_g, attn_sink, ffn_w1, ffn_w2):
    batch, n_lat, d = x.shape
    n_ctx = ctx.shape[1]
    depth = ada_w.shape[0]
    assert n_ctx % PREP_ROWS == 0 and n_lat % PREP_ROWS == 0 and d == D_MODEL

    mod_rows = 8 * ((batch + 1 + 7) // 8)
    c_rows = jnp.concatenate([c, c_ctx[None, :], jnp.zeros((mod_rows - batch - 1, d), F32)], axis=0)
    mod_all = _modulation(c_rows, ada_w, ada_b)
    cos, sin = _rope_tables(n_ctx, n_lat)

    xs = jnp.concatenate([ctx, x], axis=1)
    row = lambda t: t.reshape(1, -1)
    for l in range(depth):
        mod = mod_all[l].reshape(mod_rows, 1, 6 * d)
        ng = norm_g[l]
        wts = _layer_weights(w_in[l], rwkv_mu[l], rwkv_w2[l], rwkv_a2[l], rwkv_g2[l], gdn_conv[l],
                             gdn_a_log[l], gdn_dt_bias[l], gla_gw2[l], gla_gb[l])
        w_all = jnp.concatenate([wts["wb"], wts["wc"], wts["wd"], wts["wa"]], axis=1)
        wb_w, wc_w, wd_w, wa_w = (wts[k].shape[1] for k in ("wb", "wc", "wd", "wa"))
        assert wc_w == wb_w and (2 * wb_w) % BRANCH_W == 0 and (2 * wb_w + wd_w) % wa_w == 0
        p = _project(xs, mod, row(ng[0]), w_all, n_ctx, 1, 0, cos, sin,
                     rope_cols=(2 * wb_w, 2 * wb_w + 2 * BRANCH_W))

        sh, dks, dws, fin = _rwkv_prep(p, n_ctx, wts["mu"], wts["w2p"], wts["a2p"], wts["g2p"],
                                       rwkv_w0[l], rwkv_a0[l], row(rwkv_kk[l]), row(rwkv_ka[l]),
                                       row(rwkv_rk[l]), width=wa_w, col=(2 * wb_w + wd_w) // wa_w)
        ya = _bidir_scan(sh, dks, dws, n_ctx, fin, row(rwkv_ln_g[l]), row(rwkv_ln_b[l]), finish="groupnorm",
                         scalar_decay=False, lowrank=True, slots=RWKV_SLOTS)

        sh, dks, dws, fin = _gdn_prep(p, n_ctx, wts["conv"], wts["alog"], wts["dt"], wts["eb"],
                                      wts["ea"], width=wb_w, col=0)
        gnorm = row(jnp.tile(gdn_norm_g[l], N_HEADS))
        yb = _bidir_scan(sh, dks, dws, n_ctx, fin, gnorm, gnorm, finish="rms",
                         scalar_decay=True, lowrank=True, slots=GDN_SLOTS)

        sh, dks, dws, fin = _gla_prep(p, wts["gwp"], wts["gbp"], wts["glane"], width=wc_w, col=1)
        cnorm = row(jnp.tile(gla_norm_g[l], N_HEADS))
        yc = _bidir_scan(sh, dks, dws, n_ctx, fin, cnorm, cnorm, finish="rms",
                         scalar_decay=False, lowrank=False, slots=GLA_SLOTS)

        yd = _attention(p, attn_sink[l], n_ctx, col0=2 * wb_w // BRANCH_W)

        xs = _merge(xs, mod, row(ng[0]), row(ng[1]), (ya, yb, yc, yd), wts["wg"], gate_b[l],
                    w_branch[l].astype(BF16), w_out[l].astype(BF16), n_ctx)
        xs = _ffn(xs, mod, row(ng[2]), row(ng[3]), ffn_w1[l].astype(BF16), ffn_w2[l].astype(BF16), n_ctx)
    return xs[:, n_ctx:, :]
```

```python
import functools
import math

import numpy as np
import jax
import jax.numpy as jnp
from jax import lax
from jax.experimental import pallas as pl
from jax.experimental.pallas import tpu as pltpu

F32 = jnp.float32
BF16 = jnp.bfloat16

D_MODEL = 1024
N_BRANCH = 4
BRANCH_W = 256
HEAD = 64
N_HEADS = BRANCH_W // HEAD
NORM_EPS = 1e-6
A_GN_EPS = 64e-5
A_LORA = (32, 32, 32, 32, 64)
A_IN = 3 * BRANCH_W + sum(A_LORA)
B_IN = 4 * BRANCH_W + 4 * N_HEADS
C_DK = 32
C_QK = N_HEADS * C_DK
C_GATE_R = 16
C_GATE_NORM = 16.0
C_IN = 2 * C_QK + 2 * BRANCH_W + 2 * C_GATE_R
D_KV_HEADS = 2
D_IN = BRANCH_W + 2 * D_KV_HEADS * HEAD
B_CONV = 7
WINDOW = 128
ROPE_BASE = 10000.0
GRID_W = 64
FFN_HIDDEN = 2816

CHUNK = 64
PREP_ROWS = 256
HALO = 16
ATT_BLOCK = 128
VMEM_LIMIT = 48 * 1024 * 1024

NN = (((1,), (0,)), ((), ()))
NT = (((1,), (1,)), ((), ()))
TN = (((0,), (0,)), ((), ()))


def _mm(a, b, dims=NN, mode="bf16"):
    if mode == "f32":
        return lax.dot_general(a, b, dims, precision=lax.Precision.HIGHEST, preferred_element_type=F32)
    if mode == "x3":
        ah = a.astype(BF16)
        al = (a - ah.astype(F32)).astype(BF16)
        bh = b.astype(BF16)
        bl = (b - bh.astype(F32)).astype(BF16)
        dot = functools.partial(lax.dot_general, dimension_numbers=dims, preferred_element_type=F32)
        return dot(ah, bh) + (dot(ah, bl) + dot(al, bh))
    return lax.dot_general(a.astype(BF16), b.astype(BF16), dims, preferred_element_type=F32)


def _sigmoid(x):
    return 1.0 / (1.0 + jnp.exp(-x))


def _silu(x):
    return x * _sigmoid(x)


def _softplus(x):
    return jnp.maximum(x, 0.0) + jnp.log1p(jnp.exp(-jnp.abs(x)))


def _cparams(sem):
    return pltpu.CompilerParams(dimension_semantics=sem, vmem_limit_bytes=VMEM_LIMIT)


def _mod_kernel(c_ref, w_ref, b_ref, o_ref):
    c = c_ref[...]
    o_ref[...] = _mm(_silu(c), w_ref[...], mode="f32") + b_ref[...]


def _modulation(c_rows, ada_w, ada_b):
    depth, d, n = ada_w.shape
    rows = c_rows.shape[0]
    tn = 1024
    return pl.pallas_call(
        _mod_kernel,
        grid=(depth, n // tn),
        in_specs=[
            pl.BlockSpec((rows, d), lambda l, j: (0, 0)),
            pl.BlockSpec((None, d, tn), lambda l, j: (l, 0, j)),
            pl.BlockSpec((None, 1, tn), lambda l, j: (l, 0, j)),
        ],
        out_specs=pl.BlockSpec((None, rows, tn), lambda l, j: (l, 0, j)),
        out_shape=jax.ShapeDtypeStruct((depth, rows, n), F32),
        compiler_params=_cparams(("parallel", "parallel")),
        name="adaln_mod",
    )(c_rows, ada_w, ada_b.reshape(depth, 1, n))


def _row_tile(rows, target):
    return max(t for t in range(8, target + 1, 8) if rows % t == 0)


def _mod_specs(which, batch):
    lat = pl.BlockSpec((None, 1, D_MODEL), lambda b, i, *_: (b, 0, which))
    ctx = pl.BlockSpec((None, 1, D_MODEL), lambda b, i, *_: (batch, 0, which))
    return [lat, ctx]


def _ctx_rows(tm, n_lat_rows):
    row = pl.program_id(1) * tm + lax.broadcasted_iota(jnp.int32, (tm, 1), 0)
    return row >= n_lat_rows


def _prenorm(x, gain, scale, shift):
    ms = jnp.mean(x * x, axis=-1, keepdims=True)
    y = x * lax.rsqrt(ms + NORM_EPS) * gain
    return y * (1.0 + scale) + shift


PROJ_CHUNK = 1024


def _rope(x, cos, sin):
    width = x.shape[-1]
    lane = lax.broadcasted_iota(jnp.int32, x.shape, 1)
    swapped = jnp.where(lane % 32 < 16, pltpu.roll(x, width - 16, axis=1), pltpu.roll(x, 16, axis=1))
    return x * cos + swapped * sin


def _proj_kernel(x_ref, sc_ref, csc_ref, sh_ref, csh_ref, g_ref, w_ref, cos_ref, sin_ref, o_ref, *,
                 n_lat_rows, rope_cols):
    is_ctx = _ctx_rows(x_ref.shape[0], n_lat_rows)
    scale = jnp.where(is_ctx, csc_ref[...], sc_ref[...])
    shift = jnp.where(is_ctx, csh_ref[...], sh_ref[...])
    h = _prenorm(x_ref[...], g_ref[...], scale, shift).astype(BF16)
    n = w_ref.shape[1]
    w = BRANCH_W
    for lo in range(0, n, PROJ_CHUNK):
        hi = min(lo + PROJ_CHUNK, n)
        res = jnp.dot(h, w_ref[:, lo:hi], preferred_element_type=F32)
        for c0 in range(lo, hi, w):
            part = res[:, c0 - lo:c0 - lo + w]
            if rope_cols[0] <= c0 < rope_cols[1]:
                part = _rope(part, cos_ref[...], sin_ref[...])
            o_ref[:, c0:c0 + w] = part.astype(o_ref.dtype)


def _project(xs, mod, gain, w, n_lat_rows, sc_idx, sh_idx, cos, sin, rope_cols):
    batch, rows, d = xs.shape
    n = w.shape[1]
    tm = _row_tile(rows, 544)
    assert n % BRANCH_W == 0 and rope_cols[0] % BRANCH_W == 0 and rope_cols[1] % BRANCH_W == 0
    tab = pl.BlockSpec((tm, BRANCH_W), lambda b, i: (i, 0))
    return pl.pallas_call(
        functools.partial(_proj_kernel, n_lat_rows=n_lat_rows, rope_cols=rope_cols),
        grid=(batch, rows // tm),
        in_specs=[pl.BlockSpec((None, tm, d), lambda b, i: (b, i, 0))]
        + _mod_specs(sc_idx, batch) + _mod_specs(sh_idx, batch)
        + [_vec_spec(gain.shape), _vec_spec(w.shape), tab, tab],
        out_specs=pl.BlockSpec((None, tm, n), lambda b, i: (b, i, 0)),
        out_shape=jax.ShapeDtypeStruct((batch, rows, n), BF16),
        compiler_params=_cparams(("parallel", "parallel")),
        name="prenorm_proj",
    )(xs, mod, mod, mod, mod, gain, w, cos, sin)


def _head_block_ones():
    idx = np.arange(BRANCH_W)
    return (idx[:, None] // HEAD == idx[None, :] // HEAD).astype(np.float32)


N_LEVELS = 6


def _level_mask(ri, ci, s, reverse):
    b = 1 << s
    blk = (ri // (2 * b)) == (ci // (2 * b))
    hi_r, hi_c = (ri // b) % 2 == 1, (ci // b) % 2 == 1
    return blk & ((hi_c & ~hi_r) if reverse else (hi_r & ~hi_c))


SCAN_HEADS = 2
GROUP_W = SCAN_HEADS * HEAD
N_GROUPS = N_HEADS // SCAN_HEADS


def _scan_masks(reverse):
    n = SCAN_HEADS * CHUNK
    ri = np.arange(CHUNK)[:, None]
    ci = np.arange(n)[None, :] % CHUNK
    strict, incl = (ci > ri, ci >= ri) if reverse else (ci < ri, ci <= ri)
    compact = [strict, incl, ci == ri] + [_level_mask(ri, ci, s, reverse) for s in range(N_LEVELS)]
    r = np.arange(n)[:, None]
    c = np.arange(n)[None, :]
    block = [(r // CHUNK) == (c // CHUNK), r == c]
    return np.stack(compact).astype(np.float32), np.stack(block).astype(np.float32)


EXACT_TERMS = 3
NORM_TERMS = 2


def _tri(reverse):
    i = np.arange(CHUNK)
    m = (i[None, :] >= i[:, None]) if reverse else (i[None, :] <= i[:, None])
    return np.tile(m.astype(np.float32), (1, EXACT_TERMS))


def _split_terms(x, terms):
    out = []
    for _ in range(terms - 1):
        part = x.astype(BF16)
        out.append(part)
        x = x - part.astype(F32)
    out.append(x.astype(BF16))
    return out


def _mm_exact(a, b, dims=NN, split="a", terms=EXACT_TERMS):
    assert dims == NN
    if split == "a":
        lhs = jnp.concatenate(_split_terms(a, terms), axis=1)
        rhs = jnp.concatenate([b.astype(BF16)] * terms, axis=0)
    else:
        lhs = a.astype(BF16)
        rhs = jnp.concatenate(_split_terms(b, terms), axis=0)
    return lax.dot_general(lhs, rhs, dims, preferred_element_type=F32)


C_STRICT, C_INCL, C_EYE, C_LVL0 = 0, 1, 2, 3
B_SAME, B_EYE = 0, 1
SCAN_BATCH = 8
SCAN_ROWS = 256
assert CHUNK == HEAD and (1 << N_LEVELS) == CHUNK


def _scan_chunk(vals, st, tri, ones, cm_ref, bm_ref, same_bf, *, reverse, scalar_decay, lowrank, mm_mode):
    same = bm_ref[B_SAME]

    def get(name):
        return vals[name]

    def expand(x):
        return jnp.concatenate([x.astype(BF16)] * SCAN_HEADS, axis=0) * same_bf

    def keep(x, k):
        return jnp.where(cm_ref[k] > 0.5, x, 0.0)

    logw = get("w")
    r, k, v = get("r"), get("k"), get("v")
    cum = _mm_exact(tri, logw, split="b")
    cum_x = cum - logw
    last = 0 if reverse else CHUNK - 1
    total = cum[last:last + 1, :]
    to_end = jnp.exp(total - cum)

    if scalar_decay:
        diag = jnp.concatenate([cum] * SCAN_HEADS, axis=0) * bm_ref[B_EYE]
        cum_row = _mm_exact(ones, diag, split="b")
        d_ii = jnp.exp(jnp.where(cm_ref[C_INCL] > 0.5, cum - cum_row, -1e30))
        r_q, k_q = r, k
    else:
        ref_row = cum[CHUNK // 2:CHUNK // 2 + 1, :]
        p_inv = jnp.exp(ref_row - cum)
        r_q, k_q = r * jnp.exp(cum - ref_row), k * p_inv

    k_e, v_e = expand(k_q), expand(v)
    r_abs = r * jnp.exp(cum)
    if not lowrank:
        s_k = _mm(r_q, k_e, NT, mm_mode)
        a_rk = s_k * d_ii if scalar_decay else keep(s_k, C_INCL)
        y = _mm(r_abs, st, NT, mm_mode) + _mm(a_rk, v_e, NN, mm_mode)
        upd = _mm(v, k * to_end, TN, mm_mode)
        return y, st * jnp.exp(total) + upd * same

    a, b = get("a"), get("b")
    if scalar_decay:
        d_xi = jnp.exp(jnp.where(cm_ref[C_STRICT] > 0.5, cum_x - cum_row, -1e30))
        a_q, b_q = a, b
    else:
        a_q, b_q = a * jnp.exp(cum_x - ref_row), b * p_inv
    lhs = jnp.concatenate([a_q, r_q], axis=0)
    s_b = _mm(lhs, expand(b_q), NT, mm_mode)
    s_k = _mm(lhs, k_e, NT, mm_mode)
    if scalar_decay:
        a_ab, a_rb = s_b[0:CHUNK] * d_xi, s_b[CHUNK:] * d_ii
        a_ak, a_rk = s_k[0:CHUNK] * d_xi, s_k[CHUNK:] * d_ii
    else:
        a_ab, a_rb = keep(s_b[0:CHUNK], C_STRICT), keep(s_b[CHUNK:], C_INCL)
        a_ak, a_rk = keep(s_k[0:CHUNK], C_STRICT), keep(s_k[CHUNK:], C_INCL)

    inv = cm_ref[C_EYE] + a_ab * cm_ref[C_LVL0]
    for s in range(1, N_LEVELS):
        c_s = expand(a_ab * cm_ref[C_LVL0 + s])
        inv = inv + _mm(_mm(inv, c_s, NN, mm_mode), expand(inv), NN, mm_mode)

    a_abs = a * jnp.exp(cum_x)
    from_state = _mm(jnp.concatenate([a_abs, r_abs], axis=0), st, NT, mm_mode)
    from_v = _mm(jnp.concatenate([a_ak, a_rk], axis=0), v_e, NN, mm_mode)
    both = from_state + from_v
    z = _mm(inv, expand(both[0:CHUNK]), NN, mm_mode)
    y = both[CHUNK:] + _mm(a_rb, expand(z), NN, mm_mode)
    upd = _mm(jnp.concatenate([v, z], axis=0), jnp.concatenate([k * to_end, b * to_end], axis=0), TN, mm_mode)
    return y, st * jnp.exp(total) + upd * same


def _scan_kernel(*refs, reverse, scalar_decay, lowrank, slots, sources, finish, mm_mode, lora, r_scale):
    it = iter(refs)
    src_refs = {name: next(it) for name in sources}
    sh_ref = src_refs[sources[0]]
    cm_ref, bm_ref, tri_ref = next(it), next(it), next(it)
    if lora:
        lo_ref, gw_ref, gb_ref, lane_ref = next(it), next(it), next(it), next(it)
    if finish:
        ob_ref, fin_ref, p1_ref, p2_ref, avg_ref = next(it), next(it), next(it), next(it), next(it)
    o_ref, st_ref = next(it), next(it)

    @pl.when(pl.program_id(1) == 0)
    def _():
        st_ref[...] = jnp.zeros_like(st_ref)

    nb, block_rows = sh_ref.shape[0], sh_ref.shape[1]
    n_chunks = block_rows // CHUNK
    n = nb * N_GROUPS
    tri = jnp.broadcast_to(tri_ref[...], (n,) + tri_ref.shape)
    ones = jnp.ones((n, CHUNK, EXACT_TERMS * SCAN_HEADS * CHUNK), F32)
    chunk = functools.partial(_scan_chunk, cm_ref=cm_ref, bm_ref=bm_ref, same_bf=bm_ref[B_SAME].astype(BF16),
                              reverse=reverse, scalar_decay=scalar_decay, lowrank=lowrank, mm_mode=mm_mode)

    def step(c, carry):
        rows = pl.ds(pl.multiple_of((n_chunks - 1 - c if reverse else c) * CHUNK, CHUNK), CHUNK)

        def groups(name):
            src, idx = slots[name]
            ref = src_refs[src]
            parts = [ref[:, rows, idx * BRANCH_W + g * GROUP_W:idx * BRANCH_W + (g + 1) * GROUP_W].astype(F32)
                     for g in range(N_GROUPS)]
            return jnp.stack(parts, axis=1).reshape(n, CHUNK, GROUP_W)

        vals = {name: groups(name) for name in slots}
        if lora:
            lo = lo_ref[:, rows, :].reshape(nb * CHUNK, lo_ref.shape[-1])
            z = _mm(lo, gw_ref[...], mode=mm_mode) + gb_ref[...]
            logw = ((-_softplus(-z) / C_GATE_NORM) * lane_ref[...]).reshape(nb, CHUNK, BRANCH_W)
            parts = [logw[:, :, g * GROUP_W:(g + 1) * GROUP_W] for g in range(N_GROUPS)]
            vals["w"] = jnp.stack(parts, axis=1).reshape(n, CHUNK, GROUP_W)
        if r_scale != 1.0:
            vals["r"] = vals["r"] * r_scale
        y, st_new = jax.vmap(chunk)(vals, st_ref[...], tri, ones)
        st_ref[...] = st_new
        y = y.reshape(nb, N_GROUPS, CHUNK, GROUP_W)
        y = jnp.concatenate([y[:, g] for g in range(N_GROUPS)], axis=-1)
        if not finish:
            o_ref[:, rows, :] = y
            return carry

        y = (y + ob_ref[:, rows, :]).reshape(nb * CHUNK, BRANCH_W)
        avg = avg_ref[...]
        if finish == "groupnorm":
            gate = fin_ref[:, rows, 0:BRANCH_W].reshape(y.shape)
            bonus = fin_ref[:, rows, BRANCH_W:2 * BRANCH_W].reshape(y.shape)
            cen = y - _mm_exact(y, avg)
            var = _mm_exact(cen * cen, avg)
            yn = cen * lax.rsqrt(var + A_GN_EPS) * p1_ref[...] + p2_ref[...]
            out = (yn + bonus) * gate
        else:
            ms = _mm_exact(y * y, avg)
            gate = fin_ref[:, rows, :].astype(F32).reshape(y.shape)
            out = y * lax.rsqrt(ms + NORM_EPS) * p1_ref[...] * _silu(gate)
        o_ref[:, rows, :] = out.reshape(nb, CHUNK, BRANCH_W)
        return carry

    lax.fori_loop(0, n_chunks, step, 0)


def _view(x):
    return x if isinstance(x, tuple) else (x, x.shape[-1], 0)


def _scan(srcs, n_lat_rows, *, reverse, scalar_decay, lowrank, slots, finish=None, fin_args=None,
          mm_mode="bf16", lora=None, r_scale=1.0):
    sources = tuple(srcs)
    views = [_view(srcs[s]) for s in sources]
    batch, rows, _ = views[0][0].shape
    nc, nlat = rows // SCAN_ROWS, n_lat_rows // SCAN_ROWS
    nctx = nc - nlat
    nb = math.gcd(batch, SCAN_BATCH)

    if reverse:
        def chunk(n):
            return nc - 1 - n
    else:
        def chunk(n):
            return jnp.where(n < nctx, nlat + n, n - nctx)

    def row_spec(width, col=0):
        return pl.BlockSpec((nb, SCAN_ROWS, width), lambda b, n: (b, chunk(n), col))

    def const_spec(shape):
        zeros = (0,) * len(shape)
        return pl.BlockSpec(shape, lambda b, n: zeros)

    cmask, bmask = (jnp.asarray(m) for m in _scan_masks(reverse))
    tri = jnp.asarray(_tri(reverse))
    in_specs = [row_spec(w, c) for _, w, c in views]
    in_specs += [const_spec(cmask.shape), const_spec(bmask.shape), const_spec(tri.shape)]
    args = [a for a, _, _ in views] + [cmask, bmask, tri]
    if lora:
        (lo, lo_w, lo_c), gw, gb, lane = _view(lora[0]), lora[1], lora[2], lora[3]
        in_specs += [row_spec(lo_w, lo_c), const_spec(gw.shape), const_spec(gb.shape), const_spec(lane.shape)]
        args += [lo, gw, gb, lane]
    if finish:
        ob, fin, p1, p2 = fin_args
        fin, fin_w, fin_c = _view(fin)
        avg = jnp.asarray(_head_block_ones() / HEAD)
        in_specs += [row_spec(BRANCH_W), row_spec(fin_w, fin_c), const_spec(p1.shape), const_spec(p2.shape),
                     const_spec(avg.shape)]
        args += [ob, fin, p1, p2, avg]
    kern = functools.partial(_scan_kernel, reverse=reverse, scalar_decay=scalar_decay, lowrank=lowrank,
                             slots=slots, sources=sources, finish=finish, mm_mode=mm_mode, lora=bool(lora),
                             r_scale=r_scale)
    return pl.pallas_call(
        kern,
        grid=(batch // nb, nc),
        in_specs=in_specs,
        out_specs=row_spec(BRANCH_W),
        out_shape=jax.ShapeDtypeStruct((batch, rows, BRANCH_W), F32),
        scratch_shapes=[pltpu.VMEM((nb * N_GROUPS, GROUP_W, GROUP_W), F32)],
        compiler_params=_cparams(("parallel", "arbitrary")),
        name="dplr_scan_" + ("bwd" if reverse else "fwd"),
    )(*args)


def _gla_scan(p, n_lat_rows, gwp, gbp, lane, norm_g):
    w = BRANCH_W
    srcs = {"sh": (p, 3 * w, 0)}
    lo = (p, gwp.shape[1], 4 * w // gwp.shape[1])
    kw = dict(scalar_decay=False, lowrank=False, slots=GLA_SLOTS, r_scale=C_DK ** -0.5)
    ob = _scan(srcs, n_lat_rows, reverse=True, lora=(lo, gwp[1], gbp[1:2], lane), **kw)
    return _scan(srcs, n_lat_rows, reverse=False, lora=(lo, gwp[0], gbp[0:1], lane), finish="rms",
                 fin_args=(ob, (p, w, 3), norm_g, norm_g), **kw)


def _bidir_scan(sh, dks, dws, n_lat_rows, fin, p1, p2, *, finish, **kw):
    def srcs(d):
        out = {"sh": sh, "dw": dws[d]}
        if dks is not None:
            out["dk"] = dks[d]
        return out

    ob = _scan(srcs(1), n_lat_rows, reverse=True, **kw)
    return _scan(srcs(0), n_lat_rows, reverse=False, finish=finish, fin_args=(ob, fin, p1, p2), **kw)


def _halo_specs(width, rows, col):
    per = PREP_ROWS // HALO
    last = rows // HALO - 1
    own = pl.BlockSpec((None, PREP_ROWS, width), lambda b, i: (b, i, col))
    prev = pl.BlockSpec((None, HALO, width), lambda b, i: (b, jnp.maximum(i * per - 1, 0), col))
    nxt = pl.BlockSpec((None, HALO, width), lambda b, i: (b, jnp.minimum((i + 1) * per, last), col))
    return [own, prev, nxt]


def _seq_edges(n_lat_blocks):
    i = pl.program_id(1)
    first = (i == 0) | (i == n_lat_blocks)
    lastb = (i == n_lat_blocks - 1) | (i == pl.num_programs(1) - 1)
    return first, lastb


def _vec_spec(shape):
    zeros = (0,) * len(shape)
    return pl.BlockSpec(shape, lambda b, i: zeros)


def _shift_matrix(offset):
    m = np.zeros((PREP_ROWS, PREP_ROWS + 2 * HALO), np.float32)
    t = np.arange(PREP_ROWS)
    src = t + offset
    col = np.where(src < 0, PREP_ROWS + HALO + src, np.where(src >= PREP_ROWS, HALO + src, src))
    m[t, col] = 1.0
    return m


def _rwkv_prep_kernel(p_ref, prev_ref, next_ref, shift_ref, mu_ref, w2_ref, a2_ref, g2_ref, w0_ref, a0_ref, kk_ref,
                      ka_ref, rk_ref, ones_ref, sh_ref, dk0_ref, dk1_ref, dw0_ref, dw1_ref, fin_ref, *,
                      n_lat_blocks, mode):
    first, lastb = _seq_edges(n_lat_blocks)
    x = p_ref[...]
    zero = jnp.zeros((), x.dtype)
    xe = jnp.concatenate([x, jnp.where(first, zero, prev_ref[...]), jnp.where(lastb, zero, next_ref[...])], axis=0)
    around = jnp.dot(shift_ref[...], xe, preferred_element_type=F32)
    x = x.astype(F32)
    xm = x + (around - x) * mu_ref[...]
    w = BRANCH_W
    r, k, v, lo = xm[:, 0:w], xm[:, w:2 * w], xm[:, 2 * w:3 * w], xm[:, 3 * w:4 * w]
    ones = ones_ref[...]
    th, sg = jnp.tanh(lo), _sigmoid(lo)
    gate = _mm(sg, g2_ref[...], mode=mode)
    kx = k * kk_ref[...]
    kk = kx * lax.rsqrt(_mm_exact(kx * kx, ones, terms=NORM_TERMS) + 1e-6)
    sh_ref[:, 0:w] = r.astype(BF16)
    sh_ref[:, w:2 * w] = v.astype(BF16)
    sh_ref[:, 2 * w:3 * w] = (-kk).astype(BF16)
    bonus = jnp.zeros_like(v)
    for d, (dk_ref, dw_ref) in enumerate(((dk0_ref, dw0_ref), (dk1_ref, dw1_ref))):
        w_raw = w0_ref[d:d + 1, :] + _mm(th, w2_ref[d], mode=mode)
        dw_ref[...] = -math.exp(-0.5) * _sigmoid(w_raw)
        a = _sigmoid(a0_ref[d:d + 1, :] + _mm(lo, a2_ref[d], mode=mode))
        kd = k * (1.0 + (a - 1.0) * ka_ref[...])
        dk_ref[:, 0:w] = kd.astype(BF16)
        dk_ref[:, w:2 * w] = (kk * a).astype(BF16)
        bonus = bonus + _mm_exact(r * kd * rk_ref[...], ones) * v
    fin_ref[:, 0:w] = gate
    fin_ref[:, w:2 * w] = bonus


def _prep_outs(batch, rows, spec):
    w = BRANCH_W
    specs = [pl.BlockSpec((None, PREP_ROWS, n * w), lambda b, i: (b, i, 0)) for n, _ in spec]
    shapes = [jax.ShapeDtypeStruct((batch, rows, n * w), dt) for n, dt in spec]
    return specs, shapes


def _rwkv_prep(p, n_lat_rows, mu, w2p, a2p, g2p, w0, a0, kk, ka, rk, mode="bf16", width=None, col=0):
    batch, rows, full = p.shape
    width = width or full
    ones = jnp.asarray(_head_block_ones())
    assert p.dtype == BF16
    around = 0.5 * (_shift_matrix(-1) + _shift_matrix(1))
    consts = [jnp.asarray(around, BF16), mu, w2p, a2p, g2p, w0, a0, kk, ka, rk, ones]
    out_specs, out_shape = _prep_outs(batch, rows, [(3, BF16), (2, BF16), (2, BF16), (1, F32), (1, F32), (2, F32)])
    sh, dk0, dk1, dw0, dw1, fin = pl.pallas_call(
        functools.partial(_rwkv_prep_kernel, n_lat_blocks=n_lat_rows // PREP_ROWS, mode=mode),
        grid=(batch, rows // PREP_ROWS),
        in_specs=_halo_specs(width, rows, col) + [_vec_spec(c.shape) for c in consts],
        out_specs=out_specs,
        out_shape=out_shape,
        compiler_params=_cparams(("parallel", "parallel")),
        name="rwkv_prep",
    )(p, p, p, *consts)
    return sh, (dk0, dk1), (dw0, dw1), fin


def _gdn_prep_kernel(p_ref, prev_ref, next_ref, conv_ref, alog_ref, dt_ref, eb_ref, ea_ref, ones_ref,
                     sh_ref, dk0_ref, dk1_ref, dw0_ref, dw1_ref, fin_ref, *, n_lat_blocks):
    first, lastb = _seq_edges(n_lat_blocks)
    w = BRANCH_W
    x = p_ref[:, 0:3 * w].astype(F32)
    top = jnp.where(first, 0.0, prev_ref[:, 0:3 * w].astype(F32))
    bot = jnp.where(lastb, 0.0, next_ref[:, 0:3 * w].astype(F32))
    xe = jnp.concatenate([top, x, bot], axis=0)
    ext = PREP_ROWS + 2 * HALO
    acc = jnp.zeros_like(x)
    for s in range(B_CONV):
        shift = (B_CONV // 2 - s) % ext
        rolled = xe if shift == 0 else pltpu.roll(xe, shift, axis=0)
        acc = acc + rolled[HALO:HALO + PREP_ROWS] * conv_ref[s:s + 1, :]
    qkv = _silu(acc)
    ones = ones_ref[...]

    def l2n(t):
        return t * lax.rsqrt(_mm_exact(t * t, ones, terms=NORM_TERMS) + 1e-6)

    q = l2n(qkv[:, 0:w]) * (HEAD ** -0.5)
    k = l2n(qkv[:, w:2 * w])
    v = qkv[:, 2 * w:3 * w]
    sh_ref[:, 0:w] = q.astype(BF16)
    sh_ref[:, w:2 * w] = v.astype(BF16)
    sh_ref[:, 2 * w:3 * w] = k.astype(BF16)
    sr = p_ref[:, 4 * w:4 * w + 128].astype(F32)
    beta_all = _sigmoid(sr)
    g_all = -jnp.exp(alog_ref[...]) * _softplus(sr + dt_ref[...])
    for d, (dk_ref, dw_ref) in enumerate(((dk0_ref, dw0_ref), (dk1_ref, dw1_ref))):
        beta = _mm_exact(beta_all, eb_ref[d])
        g = _mm_exact(g_all, ea_ref[d])
        kb = k * beta
        dw_ref[...] = g
        dk_ref[:, 0:w] = kb.astype(BF16)
        dk_ref[:, w:2 * w] = (-jnp.exp(g) * kb).astype(BF16)
    fin_ref[...] = p_ref[:, 3 * w:4 * w].astype(F32)


def _gdn_prep(p, n_lat_rows, conv_w, alog_vec, dt_vec, eb, ea, width=None, col=0):
    batch, rows, full = p.shape
    width = width or full
    ones = jnp.asarray(_head_block_ones())
    consts = [conv_w, alog_vec, dt_vec, eb, ea, ones]
    out_specs, out_shape = _prep_outs(batch, rows, [(3, BF16), (2, BF16), (2, BF16), (1, F32), (1, F32), (1, F32)])
    sh, dk0, dk1, dw0, dw1, fin = pl.pallas_call(
        functools.partial(_gdn_prep_kernel, n_lat_blocks=n_lat_rows // PREP_ROWS),
        grid=(batch, rows // PREP_ROWS),
        in_specs=_halo_specs(width, rows, col) + [_vec_spec(c.shape) for c in consts],
        out_specs=out_specs,
        out_shape=out_shape,
        compiler_params=_cparams(("parallel", "parallel")),
        name="gdn_prep",
    )(p, p, p, *consts)
    return sh, (dk0, dk1), (dw0, dw1), fin


NEG_BIG = -1e30


def _window_bias():
    iq = np.arange(2 * ATT_BLOCK)[:, None] % ATT_BLOCK
    ik = np.arange(ATT_BLOCK)[None, :]
    ok = np.stack([ik >= iq, np.ones_like(ik >= iq), ik <= iq])
    assert WINDOW == ATT_BLOCK
    return np.where(ok, 0.0, NEG_BIG).astype(np.float32)


def _attn_kernel(sink_ref, q_ref, kc_ref, vc_ref, kp_ref, ko_ref, kn_ref, vp_ref, vo_ref, vn_ref,
                 bias_ref, o_ref, *, n_lat_rows):
    t = pl.program_id(1)
    blk = ATT_BLOCK
    q = q_ref[...] * (HEAD ** -0.5)
    k_band = [kp_ref, ko_ref, kn_ref]
    v_band = [vp_ref, vo_ref, vn_ref]

    q_lat = t * blk < n_lat_rows
    in_seq = [q_lat & (t >= 1), q_lat, q_lat & ((t + 1) * blk < n_lat_rows)]
    bias = [jnp.where(in_seq[j], bias_ref[j], NEG_BIG) for j in range(3)]
    lane = lax.broadcasted_iota(jnp.int32, (blk, 2 * HEAD), 1)
    row2 = lax.broadcasted_iota(jnp.int32, (2 * blk, 1), 0)
    for g in range(D_KV_HEADS):
        cols = slice(g * 2 * HEAD, (g + 1) * 2 * HEAD)
        qg = q[:, cols]
        zero = jnp.zeros((), qg.dtype)
        qs = jnp.concatenate([jnp.where(lane < HEAD, qg, zero), jnp.where(lane >= HEAD, qg, zero)], axis=0)
        s_ctx = _mm(qs, kc_ref[:, cols], NT)
        s_band = [_mm(qs, k_band[j][:, cols], NT) + bias[j] for j in range(3)]
        sink = jnp.where(row2 < blk, sink_ref[2 * g], sink_ref[2 * g + 1])
        m = jnp.maximum(jnp.max(s_ctx, axis=-1, keepdims=True), sink)
        for s in s_band:
            m = jnp.maximum(m, jnp.max(s, axis=-1, keepdims=True))
        p_ctx = jnp.exp(s_ctx - m)
        den = jnp.sum(p_ctx, axis=-1, keepdims=True) + jnp.exp(sink - m)
        acc = _mm(p_ctx, vc_ref[:, cols], NN)
        for j in range(3):
            pj = jnp.exp(s_band[j] - m)
            den = den + jnp.sum(pj, axis=-1, keepdims=True)
            acc = acc + _mm(pj, v_band[j][:, cols], NN)
        og = acc / den
        o_ref[:, cols] = jnp.where(lane < HEAD, og[0:blk], og[blk:2 * blk])


def _attention(p, sink, n_lat_rows, out_rows, col0=0):
    batch, rows, _ = p.shape
    blk = ATT_BLOCK
    nb = rows // blk
    w = BRANCH_W
    n_ctx_rows = rows - n_lat_rows
    assert n_lat_rows % n_ctx_rows == 0 and n_ctx_rows % blk == 0 and out_rows % blk == 0

    def band(col, off):
        def index(b, t):
            return (b, jnp.clip(t + off, 0, nb - 1), col0 + col)
        return pl.BlockSpec((None, blk, w), index)

    ctx = lambda col: pl.BlockSpec((None, n_ctx_rows, w), lambda b, t: (b, n_lat_rows // n_ctx_rows, col0 + col))
    kern = functools.partial(_attn_kernel, n_lat_rows=n_lat_rows)
    bias = jnp.asarray(_window_bias())
    return pl.pallas_call(
        kern,
        grid=(batch, out_rows // blk),
        in_specs=[pl.BlockSpec(memory_space=pltpu.SMEM), band(0, 0), ctx(1), ctx(2),
                  band(1, -1), band(1, 0), band(1, 1), band(2, -1), band(2, 0), band(2, 1),
                  _vec_spec(bias.shape)],
        out_specs=pl.BlockSpec((None, blk, w), lambda b, t: (b, t, 0)),
        out_shape=jax.ShapeDtypeStruct((batch, out_rows, w), F32),
        compiler_params=_cparams(("parallel", "parallel")),
        name="window_attn",
    )(sink, p, p, p, p, p, p, p, p, p, bias)


def _merge_kernel(x_ref, sc_ref, csc_ref, sh_ref, csh_ref, gm_ref, cgm_ref, g0_ref, g1_ref,
                  ya_ref, yb_ref, yc_ref, yd_ref, wg_ref, gb_ref, wb_ref, wo_ref, o_ref, *, n_lat_rows):
    x = x_ref[...]
    is_ctx = _ctx_rows(x.shape[0], n_lat_rows)
    scale = jnp.where(is_ctx, csc_ref[...], sc_ref[...])
    shift = jnp.where(is_ctx, csh_ref[...], sh_ref[...])
    gmod = jnp.where(is_ctx, cgm_ref[...], gm_ref[...])
    h = _prenorm(x, g0_ref[...], scale, shift).astype(BF16)
    acc = jnp.zeros(x.shape, F32)
    for i, y_ref in enumerate((ya_ref, yb_ref, yc_ref, yd_ref)):
        pre = jnp.dot(h, wg_ref[:, i * D_MODEL:(i + 1) * D_MODEL], preferred_element_type=F32)
        gate = _sigmoid(pre + gb_ref[i:i + 1, :])
        acc = acc + gate * jnp.dot(y_ref[...].astype(BF16), wb_ref[i], preferred_element_type=F32)
    out = jnp.dot(acc.astype(BF16), wo_ref[...], preferred_element_type=F32)
    ms = jnp.mean(out * out, axis=-1, keepdims=True)
    o_ref[...] = x + gmod * (out * lax.rsqrt(ms + NORM_EPS) * g1_ref[...])


def _merge(xs, mod, gain0, gain1, ys, wg, gate_b, wb, wo, n_lat_rows, out_rows):
    batch, _, d = xs.shape
    rows = out_rows
    tm = _row_tile(rows, 272)
    tile = lambda width: pl.BlockSpec((None, tm, width), lambda b, i: (b, i, 0))
    consts = [wg, gate_b, wb, wo]
    return pl.pallas_call(
        functools.partial(_merge_kernel, n_lat_rows=n_lat_rows),
        grid=(batch, rows // tm),
        in_specs=[tile(d)] + _mod_specs(1, batch) + _mod_specs(0, batch) + _mod_specs(2, batch)
        + [_vec_spec(gain0.shape), _vec_spec(gain1.shape)]
        + [tile(BRANCH_W)] * 4 + [_vec_spec(c.shape) for c in consts],
        out_specs=tile(d),
        out_shape=jax.ShapeDtypeStruct((batch, rows, d), F32),
        compiler_params=_cparams(("parallel", "parallel")),
        name="merge_out",
    )(xs, *([mod] * 6), gain0, gain1, *ys, *consts)


FFN_CHUNK = 512


def _ffn_kernel(x_ref, sc_ref, csc_ref, sh_ref, csh_ref, gm_ref, cgm_ref, g2_ref, g3_ref, w1_ref, w2_ref,
                o_ref, *, n_lat_rows):
    x = x_ref[...]
    is_ctx = _ctx_rows(x.shape[0], n_lat_rows)
    scale = jnp.where(is_ctx, csc_ref[...], sc_ref[...])
    shift = jnp.where(is_ctx, csh_ref[...], sh_ref[...])
    h = _prenorm(x, g2_ref[...], scale, shift).astype(BF16)
    hidden = w2_ref.shape[0]
    out = jnp.zeros(x.shape, F32)
    for lo in range(0, hidden, FFN_CHUNK):
        hi = min(lo + FFN_CHUNK, hidden)
        gt = jnp.dot(h, w1_ref[:, lo:hi], preferred_element_type=F32)
        up = jnp.dot(h, w1_ref[:, hidden + lo:hidden + hi], preferred_element_type=F32)
        out = out + jnp.dot((_silu(gt) * up).astype(BF16), w2_ref[lo:hi, :], preferred_element_type=F32)
    ms = jnp.mean(out * out, axis=-1, keepdims=True)
    gmod = jnp.where(is_ctx, cgm_ref[...], gm_ref[...])
    o_ref[...] = x + gmod * (out * lax.rsqrt(ms + NORM_EPS) * g3_ref[...])


def _ffn(xs, mod, gain2, gain3, w1, w2, n_lat_rows):
    batch, rows, d = xs.shape
    tm = _row_tile(rows, 544)
    tile = pl.BlockSpec((None, tm, d), lambda b, i: (b, i, 0))
    consts = [gain2, gain3, w1, w2]
    return pl.pallas_call(
        functools.partial(_ffn_kernel, n_lat_rows=n_lat_rows),
        grid=(batch, rows // tm),
        in_specs=[tile] + _mod_specs(4, batch) + _mod_specs(3, batch) + _mod_specs(5, batch)
        + [_vec_spec(c.shape) for c in consts],
        out_specs=tile,
        out_shape=jax.ShapeDtypeStruct((batch, rows, d), F32),
        compiler_params=_cparams(("parallel", "parallel")),
        name="swiglu",
    )(xs, *([mod] * 6), *consts)


def _pad_cols(w, width):
    return jnp.pad(w, ((0, 0), (0, width - w.shape[1])))


def _layer_weights(w_in, mu, w2, a2, g2, conv, a_log, dt_bias, gw2, gb):
    wts = {}
    off_b, off_c, off_d, off_g = A_IN, A_IN + B_IN, A_IN + B_IN + C_IN, A_IN + B_IN + C_IN + D_IN
    w = BRANCH_W
    wts["wa"] = _pad_cols(w_in[:, 0:A_IN], 4 * w).astype(BF16)
    wts["mu"] = _pad_cols(mu[None, :], 4 * w)
    lo = np.cumsum((0,) + A_LORA)
    place = lambda m, r0: jnp.zeros((w, w), F32).at[r0:r0 + m.shape[0], :].set(m)
    wts["w2p"] = jnp.stack([place(w2[0], lo[0]), place(w2[1], lo[1])])
    wts["a2p"] = jnp.stack([place(a2[0], lo[2]), place(a2[1], lo[3])])
    wts["g2p"] = place(g2, lo[4])

    wb = w_in[:, off_b:off_b + B_IN]
    wts["wb"] = jnp.concatenate([wb[:, 0:3 * w], wb[:, 3 * w + 16:], _pad_cols(wb[:, 3 * w:3 * w + 16], 128)],
                                axis=1).astype(BF16)
    nh = N_HEADS
    expand = np.zeros((4, 128, w), np.float32)
    for grp in range(4):
        for h in range(nh):
            expand[grp, grp * nh + h, h * HEAD:(h + 1) * HEAD] = 1.0
    wts["eb"] = jnp.asarray(expand[0:2])
    wts["ea"] = jnp.asarray(expand[2:4])
    vec = lambda t: jnp.zeros((1, 128), F32).at[0, 2 * nh:4 * nh].set(t.reshape(-1))
    wts["alog"] = vec(a_log)
    wts["dt"] = vec(dt_bias)
    wts["conv"] = conv

    wc = w_in[:, off_c:off_c + C_IN]
    pad_heads = lambda m: jnp.pad(m.reshape(m.shape[0], nh, C_DK),
                                  ((0, 0), (0, 0), (0, HEAD - C_DK))).reshape(m.shape[0], w)
    qc, kc, vc = wc[:, 0:C_QK], wc[:, C_QK:2 * C_QK], wc[:, 2 * C_QK:2 * C_QK + w]
    loc = wc[:, 2 * C_QK + w:2 * C_QK + w + 2 * C_GATE_R]
    gc = wc[:, 2 * C_QK + w + 2 * C_GATE_R:]
    wts["wc"] = jnp.concatenate([pad_heads(qc), pad_heads(kc), vc, gc, _pad_cols(loc, 128)], axis=1).astype(BF16)
    gwp = jnp.zeros((2, 128, w), F32)
    for d in range(2):
        gwp = gwp.at[d, d * C_GATE_R:(d + 1) * C_GATE_R, :].set(pad_heads(gw2[d]))
    wts["gwp"] = gwp
    wts["gbp"] = pad_heads(gb)
    wts["glane"] = jnp.asarray((np.arange(w) % HEAD < C_DK).astype(np.float32))[None, :]

    wd = w_in[:, off_d:off_d + D_IN]
    qd = wd[:, 0:w]
    dup = lambda m: jnp.concatenate([m[:, 0:HEAD], m[:, 0:HEAD], m[:, HEAD:], m[:, HEAD:]], axis=1)
    wts["wd"] = jnp.concatenate([qd, dup(wd[:, w:w + 2 * HEAD]), dup(wd[:, w + 2 * HEAD:])], axis=1).astype(BF16)
    wts["wg"] = w_in[:, off_g:].astype(BF16)
    return wts


def _rope_tables(n_lat_rows, n_ctx_rows):
    quarter = HEAD // 4
    inv = ROPE_BASE ** (-np.arange(quarter, dtype=np.float32) / quarter)
    pos = np.arange(n_lat_rows)
    rows = (pos // GRID_W).astype(np.float32)
    cols = (pos % GRID_W).astype(np.float32)
    inv = jnp.asarray(inv)
    ang_r = jnp.asarray(rows)[:, None] * inv[None, :]
    ang_c = jnp.asarray(cols)[:, None] * inv[None, :]
    cos = jnp.concatenate([jnp.cos(ang_r)] * 2 + [jnp.cos(ang_c)] * 2, axis=1)
    sin = jnp.concatenate([-jnp.sin(ang_r), jnp.sin(ang_r), -jnp.sin(ang_c), jnp.sin(ang_c)], axis=1)
    cos = jnp.concatenate([cos, jnp.ones((n_ctx_rows, HEAD), F32)], axis=0)
    sin = jnp.concatenate([sin, jnp.zeros((n_ctx_rows, HEAD), F32)], axis=0)
    return jnp.tile(cos, (1, N_HEADS)), jnp.tile(sin, (1, N_HEADS))


RWKV_SLOTS = {"r": ("sh", 0), "v": ("sh", 1), "a": ("sh", 2), "w": ("dw", 0), "k": ("dk", 0), "b": ("dk", 1)}
GDN_SLOTS = RWKV_SLOTS
GLA_SLOTS = {"r": ("sh", 0), "k": ("sh", 1), "v": ("sh", 2)}


def kernel(x, c, ctx, c_ctx, ada_w, ada_b, norm_g, w_in, gate_b, w_branch, w_out, rwkv_mu, rwkv_w0, rwkv_w2, rwkv_a0, rwkv_a2, rwkv_g2, rwkv_kk, rwkv_ka, rwkv_rk, rwkv_ln_g, rwkv_ln_b, gdn_conv, gdn_a_log, gdn_dt_bias, gdn_norm_g, gla_gw2, gla_gb, gla_norm_g, attn_sink, ffn_w1, ffn_w2):
    batch, n_lat, d = x.shape
    n_ctx = ctx.shape[1]
    depth = ada_w.shape[0]
    assert n_ctx % PREP_ROWS == 0 and n_lat % PREP_ROWS == 0 and d == D_MODEL

    mod_rows = 8 * ((batch + 1 + 7) // 8)
    c_rows = jnp.concatenate([c, c_ctx[None, :], jnp.zeros((mod_rows - batch - 1, d), F32)], axis=0)
    mod_all = _modulation(c_rows, ada_w, ada_b)
    cos, sin = _rope_tables(n_lat, n_ctx)

    xs = jnp.concatenate([x, ctx], axis=1)
    rows = n_lat + n_ctx
    row = lambda t: t.reshape(1, -1)
    for l in range(depth):
        out_rows = n_lat if l == depth - 1 else rows
        mod = mod_all[l].reshape(mod_rows, 1, 6 * d)
        ng = norm_g[l]
        wts = _layer_weights(w_in[l], rwkv_mu[l], rwkv_w2[l], rwkv_a2[l], rwkv_g2[l], gdn_conv[l],
                             gdn_a_log[l], gdn_dt_bias[l], gla_gw2[l], gla_gb[l])
        w_all = jnp.concatenate([wts["wc"], wts["wb"], wts["wd"], wts["wa"]], axis=1)
        wb_w, wc_w, wd_w, wa_w = (wts[k].shape[1] for k in ("wb", "wc", "wd", "wa"))
        assert wc_w == wb_w and (2 * wb_w) % BRANCH_W == 0 and (2 * wb_w + wd_w) % wa_w == 0
        p = _project(xs, mod, row(ng[0]), w_all, n_lat, 1, 0, cos, sin,
                     rope_cols=(2 * wb_w, 2 * wb_w + 2 * BRANCH_W))

        sh, dks, dws, fin = _rwkv_prep(p, n_lat, wts["mu"], wts["w2p"], wts["a2p"], wts["g2p"],
                                       rwkv_w0[l], rwkv_a0[l], row(rwkv_kk[l]), row(rwkv_ka[l]),
                                       row(rwkv_rk[l]), width=wa_w, col=(2 * wb_w + wd_w) // wa_w)
        ya = _bidir_scan(sh, dks, dws, n_lat, fin, row(rwkv_ln_g[l]), row(rwkv_ln_b[l]), finish="groupnorm",
                         scalar_decay=False, lowrank=True, slots=RWKV_SLOTS)

        sh, dks, dws, fin = _gdn_prep(p, n_lat, wts["conv"], wts["alog"], wts["dt"], wts["eb"],
                                      wts["ea"], width=wb_w, col=1)
        gnorm = row(jnp.tile(gdn_norm_g[l], N_HEADS))
        yb = _bidir_scan(sh, dks, dws, n_lat, fin, gnorm, gnorm, finish="rms",
                         scalar_decay=True, lowrank=True, slots=GDN_SLOTS)

        yc = _gla_scan(p, n_lat, wts["gwp"], wts["gbp"], wts["glane"], row(jnp.tile(gla_norm_g[l], N_HEADS)))

        yd = _attention(p, attn_sink[l], n_lat, out_rows, col0=2 * wb_w // BRANCH_W)

        xs = _merge(xs, mod, row(ng[0]), row(ng[1]), (ya, yb, yc, yd), wts["wg"], gate_b[l],
                    w_branch[l].astype(BF16), w_out[l].astype(BF16), n_lat, out_rows)
        xs = _ffn(xs, mod, row(ng[2]), row(ng[3]), ffn_w1[l].astype(BF16), ffn_w2[l].astype(BF16), n_lat)
    return xs
```

```python
import functools
import math

import numpy as np
import jax
import jax.numpy as jnp
from jax import lax
from jax.experimental import pallas as pl
from jax.experimental.pallas import tpu as pltpu

F32 = jnp.float32
BF16 = jnp.bfloat16

D_MODEL = 1024
N_BRANCH = 4
BRANCH_W = 256
HEAD = 64
N_HEADS = BRANCH_W // HEAD
NORM_EPS = 1e-6
A_GN_EPS = 64e-5
A_LORA = (32, 32, 32, 32, 64)
A_IN = 3 * BRANCH_W + sum(A_LORA)
B_IN = 4 * BRANCH_W + 4 * N_HEADS
C_DK = 32
C_QK = N_HEADS * C_DK
C_GATE_R = 16
C_GATE_NORM = 16.0
C_IN = 2 * C_QK + 2 * BRANCH_W + 2 * C_GATE_R
D_KV_HEADS = 2
D_IN = BRANCH_W + 2 * D_KV_HEADS * HEAD
B_CONV = 7
WINDOW = 128
ROPE_BASE = 10000.0
GRID_W = 64
FFN_HIDDEN = 2816

CHUNK = 64
PREP_ROWS = 256
HALO = 16
ATT_BLOCK = 128
VMEM_LIMIT = 48 * 1024 * 1024

NN = (((1,), (0,)), ((), ()))
NT = (((1,), (1,)), ((), ()))
TN = (((0,), (0,)), ((), ()))


def _mm(a, b, dims=NN, mode="bf16"):
    if mode == "f32":
        return lax.dot_general(a, b, dims, precision=lax.Precision.HIGHEST, preferred_element_type=F32)
    if mode == "x3":
        ah = a.astype(BF16)
        al = (a - ah.astype(F32)).astype(BF16)
        bh = b.astype(BF16)
        bl = (b - bh.astype(F32)).astype(BF16)
        dot = functools.partial(lax.dot_general, dimension_numbers=dims, preferred_element_type=F32)
        return dot(ah, bh) + (dot(ah, bl) + dot(al, bh))
    return lax.dot_general(a.astype(BF16), b.astype(BF16), dims, preferred_element_type=F32)


def _sigmoid(x):
    return 1.0 / (1.0 + jnp.exp(-x))


def _silu(x):
    return x * _sigmoid(x)


def _softplus(x):
    return jnp.maximum(x, 0.0) + jnp.log1p(jnp.exp(-jnp.abs(x)))


def _cparams(sem):
    return pltpu.CompilerParams(dimension_semantics=sem, vmem_limit_bytes=VMEM_LIMIT)


def _mod_kernel(c_ref, w_ref, b_ref, o_ref):
    c = c_ref[...]
    o_ref[...] = _mm(_silu(c), w_ref[...], mode="f32") + b_ref[...]


def _modulation(c_rows, ada_w, ada_b):
    depth, d, n = ada_w.shape
    rows = c_rows.shape[0]
    tn = 1024
    return pl.pallas_call(
        _mod_kernel,
        grid=(depth, n // tn),
        in_specs=[
            pl.BlockSpec((rows, d), lambda l, j: (0, 0)),
            pl.BlockSpec((None, d, tn), lambda l, j: (l, 0, j)),
            pl.BlockSpec((None, 1, tn), lambda l, j: (l, 0, j)),
        ],
        out_specs=pl.BlockSpec((None, rows, tn), lambda l, j: (l, 0, j)),
        out_shape=jax.ShapeDtypeStruct((depth, rows, n), F32),
        compiler_params=_cparams(("parallel", "parallel")),
        name="adaln_mod",
    )(c_rows, ada_w, ada_b.reshape(depth, 1, n))


def _row_tile(rows, target):
    return max(t for t in range(8, target + 1, 8) if rows % t == 0)


def _mod_specs(which, batch):
    lat = pl.BlockSpec((None, 1, D_MODEL), lambda b, i, *_: (b, 0, which))
    ctx = pl.BlockSpec((None, 1, D_MODEL), lambda b, i, *_: (batch, 0, which))
    return [lat, ctx]


def _ctx_rows(tm, n_lat_rows):
    row = pl.program_id(1) * tm + lax.broadcasted_iota(jnp.int32, (tm, 1), 0)
    return row >= n_lat_rows


def _prenorm(x, gain, scale, shift):
    ms = jnp.mean(x * x, axis=-1, keepdims=True)
    y = x * lax.rsqrt(ms + NORM_EPS) * gain
    return y * (1.0 + scale) + shift


def _rope(x, cos, sin):
    width = x.shape[-1]
    lane = lax.broadcasted_iota(jnp.int32, x.shape, 1)
    swapped = jnp.where(lane % 32 < 16, pltpu.roll(x, width - 16, axis=1), pltpu.roll(x, 16, axis=1))
    return x * cos + swapped * sin


def _head_block_ones():
    idx = np.arange(BRANCH_W)
    return (idx[:, None] // HEAD == idx[None, :] // HEAD).astype(np.float32)


N_LEVELS = 6


def _level_mask(ri, ci, s, reverse):
    b = 1 << s
    blk = (ri // (2 * b)) == (ci // (2 * b))
    hi_r, hi_c = (ri // b) % 2 == 1, (ci // b) % 2 == 1
    return blk & ((hi_c & ~hi_r) if reverse else (hi_r & ~hi_c))


SCAN_HEADS = 2
GROUP_W = SCAN_HEADS * HEAD
N_GROUPS = N_HEADS // SCAN_HEADS


def _scan_masks(reverse):
    n = SCAN_HEADS * CHUNK
    ri = np.arange(CHUNK)[:, None]
    ci = np.arange(n)[None, :] % CHUNK
    strict, incl = (ci > ri, ci >= ri) if reverse else (ci < ri, ci <= ri)
    compact = [strict, incl, ci == ri] + [_level_mask(ri, ci, s, reverse) for s in range(N_LEVELS)]
    r = np.arange(n)[:, None]
    c = np.arange(n)[None, :]
    block = [(r // CHUNK) == (c // CHUNK), r == c]
    return np.stack(compact).astype(np.float32), np.stack(block).astype(np.float32)


EXACT_TERMS = 3
NORM_TERMS = 2


def _tri(reverse):
    i = np.arange(CHUNK)
    m = (i[None, :] >= i[:, None]) if reverse else (i[None, :] <= i[:, None])
    return np.tile(m.astype(np.float32), (1, EXACT_TERMS))


def _split_terms(x, terms):
    out = []
    for _ in range(terms - 1):
        part = x.astype(BF16)
        out.append(part)
        x = x - part.astype(F32)
    out.append(x.astype(BF16))
    return out


def _mm_exact(a, b, dims=NN, split="a", terms=EXACT_TERMS):
    assert dims == NN
    if split == "a":
        lhs = jnp.concatenate(_split_terms(a, terms), axis=1)
        rhs = jnp.concatenate([b.astype(BF16)] * terms, axis=0)
    else:
        lhs = a.astype(BF16)
        rhs = jnp.concatenate(_split_terms(b, terms), axis=0)
    return lax.dot_general(lhs, rhs, dims, preferred_element_type=F32)


C_STRICT, C_INCL, C_EYE, C_LVL0 = 0, 1, 2, 3
B_SAME, B_EYE = 0, 1
SCAN_BATCH = 8
SCAN_ROWS = 256
assert CHUNK == HEAD and (1 << N_LEVELS) == CHUNK


def _scan_chunk(vals, st, tri, ones, cm_ref, bm_ref, same_bf, *, reverse, scalar_decay, lowrank, mm_mode):
    same = bm_ref[B_SAME]

    def get(name):
        return vals[name]

    def expand(x):
        return jnp.concatenate([x.astype(BF16)] * SCAN_HEADS, axis=0) * same_bf

    def keep(x, k):
        return jnp.where(cm_ref[k] > 0.5, x, 0.0)

    logw = get("w")
    r, k, v = get("r"), get("k"), get("v")
    cum = _mm_exact(tri, logw, split="b")
    cum_x = cum - logw
    last = 0 if reverse else CHUNK - 1
    total = cum[last:last + 1, :]
    to_end = jnp.exp(total - cum)

    if scalar_decay:
        diag = jnp.concatenate([cum] * SCAN_HEADS, axis=0) * bm_ref[B_EYE]
        cum_row = _mm_exact(ones, diag, split="b")
        d_ii = jnp.exp(jnp.where(cm_ref[C_INCL] > 0.5, cum - cum_row, -1e30))
        r_q, k_q = r, k
    else:
        ref_row = cum[CHUNK // 2:CHUNK // 2 + 1, :]
        p_inv = jnp.exp(ref_row - cum)
        r_q, k_q = r * jnp.exp(cum - ref_row), k * p_inv

    k_e, v_e = expand(k_q), expand(v)
    r_abs = r * jnp.exp(cum)
    if not lowrank:
        s_k = _mm(r_q, k_e, NT, mm_mode)
        a_rk = s_k * d_ii if scalar_decay else keep(s_k, C_INCL)
        y = _mm(r_abs, st, NT, mm_mode) + _mm(a_rk, v_e, NN, mm_mode)
        upd = _mm(v, k * to_end, TN, mm_mode)
        return y, st * jnp.exp(total) + upd * same

    a, b = get("a"), get("b")
    if scalar_decay:
        d_xi = jnp.exp(jnp.where(cm_ref[C_STRICT] > 0.5, cum_x - cum_row, -1e30))
        a_q, b_q = a, b
    else:
        a_q, b_q = a * jnp.exp(cum_x - ref_row), b * p_inv
    lhs = jnp.concatenate([a_q, r_q], axis=0)
    s_b = _mm(lhs, expand(b_q), NT, mm_mode)
    s_k = _mm(lhs, k_e, NT, mm_mode)
    if scalar_decay:
        a_ab, a_rb = s_b[0:CHUNK] * d_xi, s_b[CHUNK:] * d_ii
        a_ak, a_rk = s_k[0:CHUNK] * d_xi, s_k[CHUNK:] * d_ii
    else:
        a_ab, a_rb = keep(s_b[0:CHUNK], C_STRICT), keep(s_b[CHUNK:], C_INCL)
        a_ak, a_rk = keep(s_k[0:CHUNK], C_STRICT), keep(s_k[CHUNK:], C_INCL)

    inv = cm_ref[C_EYE] + a_ab * cm_ref[C_LVL0]
    for s in range(1, N_LEVELS):
        c_s = expand(a_ab * cm_ref[C_LVL0 + s])
        inv = inv + _mm(_mm(inv, c_s, NN, mm_mode), expand(inv), NN, mm_mode)

    a_abs = a * jnp.exp(cum_x)
    from_state = _mm(jnp.concatenate([a_abs, r_abs], axis=0), st, NT, mm_mode)
    from_v = _mm(jnp.concatenate([a_ak, a_rk], axis=0), v_e, NN, mm_mode)
    both = from_state + from_v
    z = _mm(inv, expand(both[0:CHUNK]), NN, mm_mode)
    y = both[CHUNK:] + _mm(a_rb, expand(z), NN, mm_mode)
    upd = _mm(jnp.concatenate([v, z], axis=0), jnp.concatenate([k * to_end, b * to_end], axis=0), TN, mm_mode)
    return y, st * jnp.exp(total) + upd * same


def _scan_kernel(*refs, reverse, scalar_decay, lowrank, slots, sources, finish, mm_mode, lora, r_scale):
    it = iter(refs)
    src_refs = {name: next(it) for name in sources}
    sh_ref = src_refs[sources[0]]
    cm_ref, bm_ref, tri_ref = next(it), next(it), next(it)
    if lora:
        lo_ref, gw_ref, gb_ref, lane_ref = next(it), next(it), next(it), next(it)
    if finish:
        ob_ref, fin_ref, p1_ref, p2_ref, avg_ref = next(it), next(it), next(it), next(it), next(it)
    o_ref, st_ref = next(it), next(it)

    @pl.when(pl.program_id(1) == 0)
    def _():
        st_ref[...] = jnp.zeros_like(st_ref)

    nb, block_rows = sh_ref.shape[0], sh_ref.shape[1]
    n_chunks = block_rows // CHUNK
    n = nb * N_GROUPS
    tri = jnp.broadcast_to(tri_ref[...], (n,) + tri_ref.shape)
    ones = jnp.ones((n, CHUNK, EXACT_TERMS * SCAN_HEADS * CHUNK), F32)
    chunk = functools.partial(_scan_chunk, cm_ref=cm_ref, bm_ref=bm_ref, same_bf=bm_ref[B_SAME].astype(BF16),
                              reverse=reverse, scalar_decay=scalar_decay, lowrank=lowrank, mm_mode=mm_mode)

    def step(c, carry):
        rows = pl.ds(pl.multiple_of((n_chunks - 1 - c if reverse else c) * CHUNK, CHUNK), CHUNK)

        def groups(name):
            src, idx = slots[name]
            ref = src_refs[src]
            parts = [ref[:, rows, idx * BRANCH_W + g * GROUP_W:idx * BRANCH_W + (g + 1) * GROUP_W].astype(F32)
                     for g in range(N_GROUPS)]
            return jnp.stack(parts, axis=1).reshape(n, CHUNK, GROUP_W)

        vals = {name: groups(name) for name in slots}
        if lora:
            lo = lo_ref[:, rows, :].reshape(nb * CHUNK, lo_ref.shape[-1])
            z = _mm(lo, gw_ref[...], mode=mm_mode) + gb_ref[...]
            logw = ((-_softplus(-z) / C_GATE_NORM) * lane_ref[...]).reshape(nb, CHUNK, BRANCH_W)
            parts = [logw[:, :, g * GROUP_W:(g + 1) * GROUP_W] for g in range(N_GROUPS)]
            vals["w"] = jnp.stack(parts, axis=1).reshape(n, CHUNK, GROUP_W)
        if r_scale != 1.0:
            vals["r"] = vals["r"] * r_scale
        y, st_new = jax.vmap(chunk)(vals, st_ref[...], tri, ones)
        st_ref[...] = st_new
        y = y.reshape(nb, N_GROUPS, CHUNK, GROUP_W)
        y = jnp.concatenate([y[:, g] for g in range(N_GROUPS)], axis=-1)
        if not finish:
            o_ref[:, rows, :] = y
            return carry

        y = (y + ob_ref[:, rows, :]).reshape(nb * CHUNK, BRANCH_W)
        avg = avg_ref[...]
        if finish == "groupnorm":
            gate = fin_ref[:, rows, 0:BRANCH_W].reshape(y.shape)
            bonus = fin_ref[:, rows, BRANCH_W:2 * BRANCH_W].reshape(y.shape)
            cen = y - _mm_exact(y, avg)
            var = _mm_exact(cen * cen, avg)
            yn = cen * lax.rsqrt(var + A_GN_EPS) * p1_ref[...] + p2_ref[...]
            out = (yn + bonus) * gate
        else:
            ms = _mm_exact(y * y, avg)
            gate = fin_ref[:, rows, :].astype(F32).reshape(y.shape)
            out = y * lax.rsqrt(ms + NORM_EPS) * p1_ref[...] * _silu(gate)
        o_ref[:, rows, :] = out.reshape(nb, CHUNK, BRANCH_W)
        return carry

    lax.fori_loop(0, n_chunks, step, 0)


def _view(x):
    return x if isinstance(x, tuple) else (x, x.shape[-1], 0)


def _scan(srcs, n_lat_rows, *, reverse, scalar_decay, lowrank, slots, finish=None, fin_args=None,
          mm_mode="bf16", lora=None, r_scale=1.0):
    sources = tuple(srcs)
    views = [_view(srcs[s]) for s in sources]
    batch, rows, _ = views[0][0].shape
    nc, nlat = rows // SCAN_ROWS, n_lat_rows // SCAN_ROWS
    nctx = nc - nlat
    nb = math.gcd(batch, SCAN_BATCH)

    if reverse:
        def chunk(n):
            return nc - 1 - n
    else:
        def chunk(n):
            return jnp.where(n < nctx, nlat + n, n - nctx)

    def row_spec(width, col=0):
        return pl.BlockSpec((nb, SCAN_ROWS, width), lambda b, n: (b, chunk(n), col))

    def const_spec(shape):
        zeros = (0,) * len(shape)
        return pl.BlockSpec(shape, lambda b, n: zeros)

    cmask, bmask = (jnp.asarray(m) for m in _scan_masks(reverse))
    tri = jnp.asarray(_tri(reverse))
    in_specs = [row_spec(w, c) for _, w, c in views]
    in_specs += [const_spec(cmask.shape), const_spec(bmask.shape), const_spec(tri.shape)]
    args = [a for a, _, _ in views] + [cmask, bmask, tri]
    if lora:
        (lo, lo_w, lo_c), gw, gb, lane = _view(lora[0]), lora[1], lora[2], lora[3]
        in_specs += [row_spec(lo_w, lo_c), const_spec(gw.shape), const_spec(gb.shape), const_spec(lane.shape)]
        args += [lo, gw, gb, lane]
    if finish:
        ob, fin, p1, p2 = fin_args
        fin, fin_w, fin_c = _view(fin)
        avg = jnp.asarray(_head_block_ones() / HEAD)
        in_specs += [row_spec(BRANCH_W), row_spec(fin_w, fin_c), const_spec(p1.shape), const_spec(p2.shape),
                     const_spec(avg.shape)]
        args += [ob, fin, p1, p2, avg]
    kern = functools.partial(_scan_kernel, reverse=reverse, scalar_decay=scalar_decay, lowrank=lowrank,
                             slots=slots, sources=sources, finish=finish, mm_mode=mm_mode, lora=bool(lora),
                             r_scale=r_scale)
    return pl.pallas_call(
        kern,
        grid=(batch // nb, nc),
        in_specs=in_specs,
        out_specs=row_spec(BRANCH_W),
        out_shape=jax.ShapeDtypeStruct((batch, rows, BRANCH_W), F32),
        scratch_shapes=[pltpu.VMEM((nb * N_GROUPS, GROUP_W, GROUP_W), F32)],
        compiler_params=_cparams(("parallel", "arbitrary")),
        name="dplr_scan_" + ("bwd" if reverse else "fwd"),
    )(*args)


def _gla_scan(p, n_lat_rows, gwp, gbp, lane, norm_g):
    w = BRANCH_W
    srcs = {"sh": (p, 3 * w, 0)}
    lo = (p, gwp.shape[1], 4 * w // gwp.shape[1])
    kw = dict(scalar_decay=False, lowrank=False, slots=GLA_SLOTS, r_scale=C_DK ** -0.5)
    ob = _scan(srcs, n_lat_rows, reverse=True, lora=(lo, gwp[1], gbp[1:2], lane), **kw)
    return _scan(srcs, n_lat_rows, reverse=False, lora=(lo, gwp[0], gbp[0:1], lane), finish="rms",
                 fin_args=(ob, (p, w, 3), norm_g, norm_g), **kw)


def _bidir_scan(sh, dks, dws, n_lat_rows, fin, p1, p2, *, finish, **kw):
    def srcs(d):
        out = {"sh": sh, "dw": dws[d]}
        if dks is not None:
            out["dk"] = dks[d]
        return out

    ob = _scan(srcs(1), n_lat_rows, reverse=True, **kw)
    return _scan(srcs(0), n_lat_rows, reverse=False, finish=finish, fin_args=(ob, fin, p1, p2), **kw)


def _halo_specs(width, rows, col):
    per = PREP_ROWS // HALO
    last = rows // HALO - 1
    own = pl.BlockSpec((None, PREP_ROWS, width), lambda b, i: (b, i, col))
    prev = pl.BlockSpec((None, HALO, width), lambda b, i: (b, jnp.maximum(i * per - 1, 0), col))
    nxt = pl.BlockSpec((None, HALO, width), lambda b, i: (b, jnp.minimum((i + 1) * per, last), col))
    return [own, prev, nxt]


def _seq_edges(n_lat_blocks):
    i = pl.program_id(1)
    first = (i == 0) | (i == n_lat_blocks)
    lastb = (i == n_lat_blocks - 1) | (i == pl.num_programs(1) - 1)
    return first, lastb


def _vec_spec(shape):
    zeros = (0,) * len(shape)
    return pl.BlockSpec(shape, lambda b, i: zeros)


def _shift_matrix(offset):
    m = np.zeros((PREP_ROWS, PREP_ROWS + 2 * HALO), np.float32)
    t = np.arange(PREP_ROWS)
    src = t + offset
    col = np.where(src < 0, PREP_ROWS + HALO + src, np.where(src >= PREP_ROWS, HALO + src, src))
    m[t, col] = 1.0
    return m


def _rwkv_prep(x, prev, nxt, first, lastb, shift_ref, mu_ref, w2_ref, a2_ref, g2_ref, w0_ref, a0_ref, kk_ref,
               ka_ref, rk_ref, ones_ref, sh_ref, dk0_ref, dk1_ref, dw0_ref, dw1_ref, fin_ref, mode="bf16"):
    zero = jnp.zeros((), x.dtype)
    xe = jnp.concatenate([x, jnp.where(first, zero, prev), jnp.where(lastb, zero, nxt)], axis=0)
    around = jnp.dot(shift_ref[...], xe, preferred_element_type=F32)
    x = x.astype(F32)
    xm = x + (around - x) * mu_ref[...]
    w = BRANCH_W
    r, k, v, lo = xm[:, 0:w], xm[:, w:2 * w], xm[:, 2 * w:3 * w], xm[:, 3 * w:4 * w]
    ones = ones_ref[...]
    th, sg = jnp.tanh(lo), _sigmoid(lo)
    gate = _mm(sg, g2_ref[...], mode=mode)
    kx = k * kk_ref[...]
    kk = kx * lax.rsqrt(_mm_exact(kx * kx, ones, terms=NORM_TERMS) + 1e-6)
    sh_ref[:, 0:w] = r.astype(BF16)
    sh_ref[:, w:2 * w] = v.astype(BF16)
    sh_ref[:, 2 * w:3 * w] = (-kk).astype(BF16)
    bonus = jnp.zeros_like(v)
    for d, (dk_ref, dw_ref) in enumerate(((dk0_ref, dw0_ref), (dk1_ref, dw1_ref))):
        w_raw = w0_ref[d:d + 1, :] + _mm(th, w2_ref[d], mode=mode)
        dw_ref[...] = -math.exp(-0.5) * _sigmoid(w_raw)
        a = _sigmoid(a0_ref[d:d + 1, :] + _mm(lo, a2_ref[d], mode=mode))
        kd = k * (1.0 + (a - 1.0) * ka_ref[...])
        dk_ref[:, 0:w] = kd.astype(BF16)
        dk_ref[:, w:2 * w] = (kk * a).astype(BF16)
        bonus = bonus + _mm_exact(r * kd * rk_ref[...], ones) * v
    fin_ref[:, 0:w] = gate
    fin_ref[:, w:2 * w] = bonus


def _gdn_prep(p, prev, nxt, first, lastb, conv_ref, alog_ref, dt_ref, eb_ref, ea_ref, ones_ref,
              sh_ref, dk0_ref, dk1_ref, dw0_ref, dw1_ref, fin_ref):
    w = BRANCH_W
    x = p[:, 0:3 * w].astype(F32)
    top = jnp.where(first, 0.0, prev[:, 0:3 * w].astype(F32))
    bot = jnp.where(lastb, 0.0, nxt[:, 0:3 * w].astype(F32))
    xe = jnp.concatenate([top, x, bot], axis=0)
    ext = PREP_ROWS + 2 * HALO
    acc = jnp.zeros_like(x)
    for s in range(B_CONV):
        shift = (B_CONV // 2 - s) % ext
        rolled = xe if shift == 0 else pltpu.roll(xe, shift, axis=0)
        acc = acc + rolled[HALO:HALO + PREP_ROWS] * conv_ref[s:s + 1, :]
    qkv = _silu(acc)
    ones = ones_ref[...]

    def l2n(t):
        return t * lax.rsqrt(_mm_exact(t * t, ones, terms=NORM_TERMS) + 1e-6)

    q = l2n(qkv[:, 0:w]) * (HEAD ** -0.5)
    k = l2n(qkv[:, w:2 * w])
    v = qkv[:, 2 * w:3 * w]
    sh_ref[:, 0:w] = q.astype(BF16)
    sh_ref[:, w:2 * w] = v.astype(BF16)
    sh_ref[:, 2 * w:3 * w] = k.astype(BF16)
    sr = p[:, 4 * w:4 * w + 128].astype(F32)
    beta_all = _sigmoid(sr)
    g_all = -jnp.exp(alog_ref[...]) * _softplus(sr + dt_ref[...])
    for d, (dk_ref, dw_ref) in enumerate(((dk0_ref, dw0_ref), (dk1_ref, dw1_ref))):
        beta = _mm_exact(beta_all, eb_ref[d])
        g = _mm_exact(g_all, ea_ref[d])
        kb = k * beta
        dw_ref[...] = g
        dk_ref[:, 0:w] = kb.astype(BF16)
        dk_ref[:, w:2 * w] = (-jnp.exp(g) * kb).astype(BF16)
    fin_ref[...] = p[:, 3 * w:4 * w].astype(F32)


SEG_GLA, SEG_GDN, SEG_ATT, SEG_RWKV = range(4)
N_RWKV_CONSTS, N_GDN_CONSTS = 11, 6
RWKV_OUTS = [(3, BF16), (2, BF16), (2, BF16), (1, F32), (1, F32), (2, F32)]
GDN_OUTS = [(3, BF16), (2, BF16), (2, BF16), (1, F32), (1, F32), (1, F32)]


def _front_kernel(*refs, n_lat_blocks, seg):
    it = iter(refs)
    x_ref, xp_ref, xn_ref, sc_ref, csc_ref, sh_ref, csh_ref, g_ref, w_ref, cos_ref, sin_ref = (
        next(it) for _ in range(11))
    rwkv_consts = [next(it) for _ in range(N_RWKV_CONSTS)]
    gdn_consts = [next(it) for _ in range(N_GDN_CONSTS)]
    pc_ref, pd_ref = next(it), next(it)
    rwkv_outs = [next(it) for _ in RWKV_OUTS]
    gdn_outs = [next(it) for _ in GDN_OUTS]

    first, lastb = _seq_edges(n_lat_blocks)
    is_ctx = pl.program_id(1) >= n_lat_blocks
    scale = jnp.where(is_ctx, csc_ref[...], sc_ref[...])
    shift = jnp.where(is_ctx, csh_ref[...], sh_ref[...])
    h = _prenorm(x_ref[...], g_ref[...], scale, shift).astype(BF16)
    halo = jnp.concatenate([xp_ref[...], xn_ref[...]], axis=0)
    h_halo = _prenorm(halo, g_ref[...], scale, shift).astype(BF16)

    def proj(rows, which):
        lo, hi = seg[which]
        return jnp.dot(rows, w_ref[:, lo:hi], preferred_element_type=F32)

    pc_ref[...] = proj(h, SEG_GLA).astype(BF16)
    att = proj(h, SEG_ATT)
    w = BRANCH_W
    for c0 in range(0, att.shape[1], w):
        part = att[:, c0:c0 + w]
        if c0 < 2 * w:
            part = _rope(part, cos_ref[...], sin_ref[...])
        pd_ref[:, c0:c0 + w] = part.astype(BF16)

    pb, pb_halo = proj(h, SEG_GDN).astype(BF16), proj(h_halo, SEG_GDN).astype(BF16)
    _gdn_prep(pb, pb_halo[0:HALO], pb_halo[HALO:], first, lastb, *gdn_consts, *gdn_outs)
    pa, pa_halo = proj(h, SEG_RWKV).astype(BF16), proj(h_halo, SEG_RWKV).astype(BF16)
    _rwkv_prep(pa, pa_halo[0:HALO], pa_halo[HALO:], first, lastb, *rwkv_consts, *rwkv_outs)


def _front(xs, mod, gain, w_all, seg, cos, sin, rwkv_consts, gdn_consts, n_lat_rows):
    batch, rows, d = xs.shape
    assert len(rwkv_consts) == N_RWKV_CONSTS and len(gdn_consts) == N_GDN_CONSTS
    tab = pl.BlockSpec((PREP_ROWS, BRANCH_W), lambda b, i: (i, 0))
    consts = list(rwkv_consts) + list(gdn_consts)
    outs = [(seg[s][1] - seg[s][0], BF16) for s in (SEG_GLA, SEG_ATT)]
    outs += [(n * BRANCH_W, dt) for n, dt in RWKV_OUTS + GDN_OUTS]
    out_specs = [pl.BlockSpec((None, PREP_ROWS, lanes), lambda b, i: (b, i, 0)) for lanes, _ in outs]
    out_shape = [jax.ShapeDtypeStruct((batch, rows, lanes), dt) for lanes, dt in outs]
    res = pl.pallas_call(
        functools.partial(_front_kernel, n_lat_blocks=n_lat_rows // PREP_ROWS, seg=seg),
        grid=(batch, rows // PREP_ROWS),
        in_specs=_halo_specs(d, rows, 0) + _mod_specs(1, batch) + _mod_specs(0, batch)
        + [_vec_spec(gain.shape), _vec_spec(w_all.shape), tab, tab] + [_vec_spec(c.shape) for c in consts],
        out_specs=out_specs,
        out_shape=out_shape,
        compiler_params=_cparams(("parallel", "parallel")),
        name="front_proj_prep",
    )(xs, xs, xs, mod, mod, mod, mod, gain, w_all, cos, sin, *consts)
    pc, pd = res[0], res[1]
    a_sh, a_dk0, a_dk1, a_dw0, a_dw1, a_fin = res[2:8]
    b_sh, b_dk0, b_dk1, b_dw0, b_dw1, b_fin = res[8:14]
    return pc, pd, (a_sh, (a_dk0, a_dk1), (a_dw0, a_dw1), a_fin), (b_sh, (b_dk0, b_dk1), (b_dw0, b_dw1), b_fin)


NEG_BIG = -1e30


def _window_bias():
    iq = np.arange(2 * ATT_BLOCK)[:, None] % ATT_BLOCK
    ik = np.arange(ATT_BLOCK)[None, :]
    ok = np.stack([ik >= iq, np.ones_like(ik >= iq), ik <= iq])
    assert WINDOW == ATT_BLOCK
    return np.where(ok, 0.0, NEG_BIG).astype(np.float32)


def _attn_kernel(sink_ref, q_ref, kc_ref, vc_ref, kp_ref, ko_ref, kn_ref, vp_ref, vo_ref, vn_ref,
                 bias_ref, o_ref, *, n_lat_rows):
    t = pl.program_id(1)
    blk = ATT_BLOCK
    q = q_ref[...] * (HEAD ** -0.5)
    k_band = [kp_ref, ko_ref, kn_ref]
    v_band = [vp_ref, vo_ref, vn_ref]

    q_lat = t * blk < n_lat_rows
    in_seq = [q_lat & (t >= 1), q_lat, q_lat & ((t + 1) * blk < n_lat_rows)]
    bias = [jnp.where(in_seq[j], bias_ref[j], NEG_BIG) for j in range(3)]
    lane = lax.broadcasted_iota(jnp.int32, (blk, 2 * HEAD), 1)
    row2 = lax.broadcasted_iota(jnp.int32, (2 * blk, 1), 0)
    for g in range(D_KV_HEADS):
        cols = slice(g * 2 * HEAD, (g + 1) * 2 * HEAD)
        qg = q[:, cols]
        zero = jnp.zeros((), qg.dtype)
        qs = jnp.concatenate([jnp.where(lane < HEAD, qg, zero), jnp.where(lane >= HEAD, qg, zero)], axis=0)
        s_ctx = _mm(qs, kc_ref[:, cols], NT)
        s_band = [_mm(qs, k_band[j][:, cols], NT) + bias[j] for j in range(3)]
        sink = jnp.where(row2 < blk, sink_ref[2 * g], sink_ref[2 * g + 1])
        m = jnp.maximum(jnp.max(s_ctx, axis=-1, keepdims=True), sink)
        for s in s_band:
            m = jnp.maximum(m, jnp.max(s, axis=-1, keepdims=True))
        p_ctx = jnp.exp(s_ctx - m)
        den = jnp.sum(p_ctx, axis=-1, keepdims=True) + jnp.exp(sink - m)
        acc = _mm(p_ctx, vc_ref[:, cols], NN)
        for j in range(3):
            pj = jnp.exp(s_band[j] - m)
            den = den + jnp.sum(pj, axis=-1, keepdims=True)
            acc = acc + _mm(pj, v_band[j][:, cols], NN)
        og = acc / den
        o_ref[:, cols] = jnp.where(lane < HEAD, og[0:blk], og[blk:2 * blk])


def _attention(p, sink, n_lat_rows, out_rows, col0=0):
    batch, rows, _ = p.shape
    blk = ATT_BLOCK
    nb = rows // blk
    w = BRANCH_W
    n_ctx_rows = rows - n_lat_rows
    assert n_lat_rows % n_ctx_rows == 0 and n_ctx_rows % blk == 0 and out_rows % blk == 0

    def band(col, off):
        def index(b, t):
            return (b, jnp.clip(t + off, 0, nb - 1), col0 + col)
        return pl.BlockSpec((None, blk, w), index)

    ctx = lambda col: pl.BlockSpec((None, n_ctx_rows, w), lambda b, t: (b, n_lat_rows // n_ctx_rows, col0 + col))
    kern = functools.partial(_attn_kernel, n_lat_rows=n_lat_rows)
    bias = jnp.asarray(_window_bias())
    return pl.pallas_call(
        kern,
        grid=(batch, out_rows // blk),
        in_specs=[pl.BlockSpec(memory_space=pltpu.SMEM), band(0, 0), ctx(1), ctx(2),
                  band(1, -1), band(1, 0), band(1, 1), band(2, -1), band(2, 0), band(2, 1),
                  _vec_spec(bias.shape)],
        out_specs=pl.BlockSpec((None, blk, w), lambda b, t: (b, t, 0)),
        out_shape=jax.ShapeDtypeStruct((batch, out_rows, w), F32),
        compiler_params=_cparams(("parallel", "parallel")),
        name="window_attn",
    )(sink, p, p, p, p, p, p, p, p, p, bias)


def _merge_kernel(x_ref, sc_ref, csc_ref, sh_ref, csh_ref, gm_ref, cgm_ref, g0_ref, g1_ref,
                  ya_ref, yb_ref, yc_ref, yd_ref, wg_ref, gb_ref, wb_ref, wo_ref, o_ref, *, n_lat_rows):
    x = x_ref[...]
    is_ctx = _ctx_rows(x.shape[0], n_lat_rows)
    scale = jnp.where(is_ctx, csc_ref[...], sc_ref[...])
    shift = jnp.where(is_ctx, csh_ref[...], sh_ref[...])
    gmod = jnp.where(is_ctx, cgm_ref[...], gm_ref[...])
    h = _prenorm(x, g0_ref[...], scale, shift).astype(BF16)
    acc = jnp.zeros(x.shape, F32)
    for i, y_ref in enumerate((ya_ref, yb_ref, yc_ref, yd_ref)):
        pre = jnp.dot(h, wg_ref[:, i * D_MODEL:(i + 1) * D_MODEL], preferred_element_type=F32)
        gate = _sigmoid(pre + gb_ref[i:i + 1, :])
        acc = acc + gate * jnp.dot(y_ref[...].astype(BF16), wb_ref[i], preferred_element_type=F32)
    out = jnp.dot(acc.astype(BF16), wo_ref[...], preferred_element_type=F32)
    ms = jnp.mean(out * out, axis=-1, keepdims=True)
    o_ref[...] = x + gmod * (out * lax.rsqrt(ms + NORM_EPS) * g1_ref[...])


def _merge(xs, mod, gain0, gain1, ys, wg, gate_b, wb, wo, n_lat_rows, out_rows):
    batch, _, d = xs.shape
    rows = out_rows
    tm = _row_tile(rows, 272)
    tile = lambda width: pl.BlockSpec((None, tm, width), lambda b, i: (b, i, 0))
    consts = [wg, gate_b, wb, wo]
    return pl.pallas_call(
        functools.partial(_merge_kernel, n_lat_rows=n_lat_rows),
        grid=(batch, rows // tm),
        in_specs=[tile(d)] + _mod_specs(1, batch) + _mod_specs(0, batch) + _mod_specs(2, batch)
        + [_vec_spec(gain0.shape), _vec_spec(gain1.shape)]
        + [tile(BRANCH_W)] * 4 + [_vec_spec(c.shape) for c in consts],
        out_specs=tile(d),
        out_shape=jax.ShapeDtypeStruct((batch, rows, d), F32),
        compiler_params=_cparams(("parallel", "parallel")),
        name="merge_out",
    )(xs, *([mod] * 6), gain0, gain1, *ys, *consts)


FFN_CHUNK = 512


def _ffn_kernel(x_ref, sc_ref, csc_ref, sh_ref, csh_ref, gm_ref, cgm_ref, g2_ref, g3_ref, w1_ref, w2_ref,
                o_ref, *, n_lat_rows):
    x = x_ref[...]
    is_ctx = _ctx_rows(x.shape[0], n_lat_rows)
    scale = jnp.where(is_ctx, csc_ref[...], sc_ref[...])
    shift = jnp.where(is_ctx, csh_ref[...], sh_ref[...])
    h = _prenorm(x, g2_ref[...], scale, shift).astype(BF16)
    hidden = w2_ref.shape[0]
    out = jnp.zeros(x.shape, F32)
    for lo in range(0, hidden, FFN_CHUNK):
        hi = min(lo + FFN_CHUNK, hidden)
        gt = jnp.dot(h, w1_ref[:, lo:hi], preferred_element_type=F32)
        up = jnp.dot(h, w1_ref[:, hidden + lo:hidden + hi], preferred_element_type=F32)
        out = out + jnp.dot((_silu(gt) * up).astype(BF16), w2_ref[lo:hi, :], preferred_element_type=F32)
    ms = jnp.mean(out * out, axis=-1, keepdims=True)
    gmod = jnp.where(is_ctx, cgm_ref[...], gm_ref[...])
    o_ref[...] = x + gmod * (out * lax.rsqrt(ms + NORM_EPS) * g3_ref[...])


def _ffn(xs, mod, gain2, gain3, w1, w2, n_lat_rows):
    batch, rows, d = xs.shape
    tm = _row_tile(rows, 544)
    tile = pl.BlockSpec((None, tm, d), lambda b, i: (b, i, 0))
    consts = [gain2, gain3, w1, w2]
    return pl.pallas_call(
        functools.partial(_ffn_kernel, n_lat_rows=n_lat_rows),
        grid=(batch, rows // tm),
        in_specs=[tile] + _mod_specs(4, batch) + _mod_specs(3, batch) + _mod_specs(5, batch)
        + [_vec_spec(c.shape) for c in consts],
        out_specs=tile,
        out_shape=jax.ShapeDtypeStruct((batch, rows, d), F32),
        compiler_params=_cparams(("parallel", "parallel")),
        name="swiglu",
    )(xs, *([mod] * 6), *consts)


def _pad_cols(w, width):
    return jnp.pad(w, ((0, 0), (0, width - w.shape[1])))


def _layer_weights(w_in, mu, w2, a2, g2, conv, a_log, dt_bias, gw2, gb):
    wts = {}
    off_b, off_c, off_d, off_g = A_IN, A_IN + B_IN, A_IN + B_IN + C_IN, A_IN + B_IN + C_IN + D_IN
    w = BRANCH_W
    wts["wa"] = _pad_cols(w_in[:, 0:A_IN], 4 * w).astype(BF16)
    wts["mu"] = _pad_cols(mu[None, :], 4 * w)
    lo = np.cumsum((0,) + A_LORA)
    place = lambda m, r0: jnp.zeros((w, w), F32).at[r0:r0 + m.shape[0], :].set(m)
    wts["w2p"] = jnp.stack([place(w2[0], lo[0]), place(w2[1], lo[1])])
    wts["a2p"] = jnp.stack([place(a2[0], lo[2]), place(a2[1], lo[3])])
    wts["g2p"] = place(g2, lo[4])

    wb = w_in[:, off_b:off_b + B_IN]
    wts["wb"] = jnp.concatenate([wb[:, 0:3 * w], wb[:, 3 * w + 16:], _pad_cols(wb[:, 3 * w:3 * w + 16], 128)],
                                axis=1).astype(BF16)
    nh = N_HEADS
    expand = np.zeros((4, 128, w), np.float32)
    for grp in range(4):
        for h in range(nh):
            expand[grp, grp * nh + h, h * HEAD:(h + 1) * HEAD] = 1.0
    wts["eb"] = jnp.asarray(expand[0:2])
    wts["ea"] = jnp.asarray(expand[2:4])
    vec = lambda t: jnp.zeros((1, 128), F32).at[0, 2 * nh:4 * nh].set(t.reshape(-1))
    wts["alog"] = vec(a_log)
    wts["dt"] = vec(dt_bias)
    wts["conv"] = conv

    wc = w_in[:, off_c:off_c + C_IN]
    pad_heads = lambda m: jnp.pad(m.reshape(m.shape[0], nh, C_DK),
                                  ((0, 0), (0, 0), (0, HEAD - C_DK))).reshape(m.shape[0], w)
    qc, kc, vc = wc[:, 0:C_QK], wc[:, C_QK:2 * C_QK], wc[:, 2 * C_QK:2 * C_QK + w]
    loc = wc[:, 2 * C_QK + w:2 * C_QK + w + 2 * C_GATE_R]
    gc = wc[:, 2 * C_QK + w + 2 * C_GATE_R:]
    wts["wc"] = jnp.concatenate([pad_heads(qc), pad_heads(kc), vc, gc, _pad_cols(loc, 128)], axis=1).astype(BF16)
    gwp = jnp.zeros((2, 128, w), F32)
    for d in range(2):
        gwp = gwp.at[d, d * C_GATE_R:(d + 1) * C_GATE_R, :].set(pad_heads(gw2[d]))
    wts["gwp"] = gwp
    wts["gbp"] = pad_heads(gb)
    wts["glane"] = jnp.asarray((np.arange(w) % HEAD < C_DK).astype(np.float32))[None, :]

    wd = w_in[:, off_d:off_d + D_IN]
    qd = wd[:, 0:w]
    dup = lambda m: jnp.concatenate([m[:, 0:HEAD], m[:, 0:HEAD], m[:, HEAD:], m[:, HEAD:]], axis=1)
    wts["wd"] = jnp.concatenate([qd, dup(wd[:, w:w + 2 * HEAD]), dup(wd[:, w + 2 * HEAD:])], axis=1).astype(BF16)
    wts["wg"] = w_in[:, off_g:].astype(BF16)
    return wts


def _rope_tables(n_lat_rows, n_ctx_rows):
    quarter = HEAD // 4
    inv = ROPE_BASE ** (-np.arange(quarter, dtype=np.float32) / quarter)
    pos = np.arange(n_lat_rows)
    rows = (pos // GRID_W).astype(np.float32)
    cols = (pos % GRID_W).astype(np.float32)
    inv = jnp.asarray(inv)
    ang_r = jnp.asarray(rows)[:, None] * inv[None, :]
    ang_c = jnp.asarray(cols)[:, None] * inv[None, :]
    cos = jnp.concatenate([jnp.cos(ang_r)] * 2 + [jnp.cos(ang_c)] * 2, axis=1)
    sin = jnp.concatenate([-jnp.sin(ang_r), jnp.sin(ang_r), -jnp.sin(ang_c), jnp.sin(ang_c)], axis=1)
    cos = jnp.concatenate([cos, jnp.ones((n_ctx_rows, HEAD), F32)], axis=0)
    sin = jnp.concatenate([sin, jnp.zeros((n_ctx_rows, HEAD), F32)], axis=0)
    return jnp.tile(cos, (1, N_HEADS)), jnp.tile(sin, (1, N_HEADS))


RWKV_SLOTS = {"r": ("sh", 0), "v": ("sh", 1), "a": ("sh", 2), "w": ("dw", 0), "k": ("dk", 0), "b": ("dk", 1)}
GDN_SLOTS = RWKV_SLOTS
GLA_SLOTS = {"r": ("sh", 0), "k": ("sh", 1), "v": ("sh", 2)}


def kernel(x, c, ctx, c_ctx, ada_w, ada_b, norm_g, w_in, gate_b, w_branch, w_out, rwkv_mu, rwkv_w0, rwkv_w2, rwkv_a0, rwkv_a2, rwkv_g2, rwkv_kk, rwkv_ka, rwkv_rk, rwkv_ln_g, rwkv_ln_b, gdn_conv, gdn_a_log, gdn_dt_bias, gdn_norm_g, gla_gw2, gla_gb, gla_norm_g, attn_sink, ffn_w1, ffn_w2):
    batch, n_lat, d = x.shape
    n_ctx = ctx.shape[1]
    depth = ada_w.shape[0]
    assert n_ctx % PREP_ROWS == 0 and n_lat % PREP_ROWS == 0 and d == D_MODEL

    mod_rows = 8 * ((batch + 1 + 7) // 8)
    c_rows = jnp.concatenate([c, c_ctx[None, :], jnp.zeros((mod_rows - batch - 1, d), F32)], axis=0)
    mod_all = _modulation(c_rows, ada_w, ada_b)
    cos, sin = _rope_tables(n_lat, n_ctx)

    xs = jnp.concatenate([x, ctx], axis=1)
    rows = n_lat + n_ctx
    row = lambda t: t.reshape(1, -1)
    for l in range(depth):
        out_rows = n_lat if l == depth - 1 else rows
        mod = mod_all[l].reshape(mod_rows, 1, 6 * d)
        ng = norm_g[l]
        wts = _layer_weights(w_in[l], rwkv_mu[l], rwkv_w2[l], rwkv_a2[l], rwkv_g2[l], gdn_conv[l],
                             gdn_a_log[l], gdn_dt_bias[l], gla_gw2[l], gla_gb[l])
        parts = [wts["wc"], wts["wb"], wts["wd"], wts["wa"]]
        w_all = jnp.concatenate(parts, axis=1)
        edges = np.cumsum([0] + [int(t.shape[1]) for t in parts])
        seg = tuple((int(edges[i]), int(edges[i + 1])) for i in range(len(parts)))
        ones = jnp.asarray(_head_block_ones())
        around = jnp.asarray(0.5 * (_shift_matrix(-1) + _shift_matrix(1)), BF16)
        rwkv_consts = [around, wts["mu"], wts["w2p"], wts["a2p"], wts["g2p"], rwkv_w0[l], rwkv_a0[l],
                       row(rwkv_kk[l]), row(rwkv_ka[l]), row(rwkv_rk[l]), ones]
        gdn_consts = [wts["conv"], wts["alog"], wts["dt"], wts["eb"], wts["ea"], ones]
        p_gla, p_att, rwkv_ops, gdn_ops = _front(xs, mod, row(ng[0]), w_all, seg, cos, sin, rwkv_consts,
                                                 gdn_consts, n_lat)

        sh, dks, dws, fin = rwkv_ops
        ya = _bidir_scan(sh, dks, dws, n_lat, fin, row(rwkv_ln_g[l]), row(rwkv_ln_b[l]), finish="groupnorm",
                         scalar_decay=False, lowrank=True, slots=RWKV_SLOTS)

        sh, dks, dws, fin = gdn_ops
        gnorm = row(jnp.tile(gdn_norm_g[l], N_HEADS))
        yb = _bidir_scan(sh, dks, dws, n_lat, fin, gnorm, gnorm, finish="rms",
                         scalar_decay=True, lowrank=True, slots=GDN_SLOTS)

        yc = _gla_scan(p_gla, n_lat, wts["gwp"], wts["gbp"], wts["glane"], row(jnp.tile(gla_norm_g[l], N_HEADS)))

        yd = _attention(p_att, attn_sink[l], n_lat, out_rows)

        xs = _merge(xs, mod, row(ng[0]), row(ng[1]), (ya, yb, yc, yd), wts["wg"], gate_b[l],
                    w_branch[l].astype(BF16), w_out[l].astype(BF16), n_lat, out_rows)
        xs = _ffn(xs, mod, row(ng[2]), row(ng[3]), ffn_w1[l].astype(BF16), ffn_w2[l].astype(BF16), n_lat)
    return xs
```

```python
import functools
import math

import numpy as np
import jax
import jax.numpy as jnp
from jax import lax
from jax.experimental import pallas as pl
from jax.experimental.pallas import tpu as pltpu

F32 = jnp.float32
BF16 = jnp.bfloat16

D_MODEL = 1024
N_BRANCH = 4
BRANCH_W = 256
HEAD = 64
N_HEADS = BRANCH_W // HEAD
NORM_EPS = 1e-6
A_GN_EPS = 64e-5
A_LORA = (32, 32, 32, 32, 64)
A_IN = 3 * BRANCH_W + sum(A_LORA)
B_IN = 4 * BRANCH_W + 4 * N_HEADS
C_DK = 32
C_QK = N_HEADS * C_DK
C_GATE_R = 16
C_GATE_NORM = 16.0
C_IN = 2 * C_QK + 2 * BRANCH_W + 2 * C_GATE_R
D_KV_HEADS = 2
D_IN = BRANCH_W + 2 * D_KV_HEADS * HEAD
B_CONV = 7
WINDOW = 128
ROPE_BASE = 10000.0
GRID_W = 64
FFN_HIDDEN = 2816

CHUNK = 64
PREP_ROWS = 256
HALO = 16
ATT_BLOCK = 128
VMEM_LIMIT = 48 * 1024 * 1024

NN = (((1,), (0,)), ((), ()))
NT = (((1,), (1,)), ((), ()))
TN = (((0,), (0,)), ((), ()))


def _mm(a, b, dims=NN, mode="bf16"):
    if mode == "f32":
        return lax.dot_general(a, b, dims, precision=lax.Precision.HIGHEST, preferred_element_type=F32)
    if mode == "x3":
        ah = a.astype(BF16)
        al = (a - ah.astype(F32)).astype(BF16)
        bh = b.astype(BF16)
        bl = (b - bh.astype(F32)).astype(BF16)
        dot = functools.partial(lax.dot_general, dimension_numbers=dims, preferred_element_type=F32)
        return dot(ah, bh) + (dot(ah, bl) + dot(al, bh))
    return lax.dot_general(a.astype(BF16), b.astype(BF16), dims, preferred_element_type=F32)


def _sigmoid(x):
    return 1.0 / (1.0 + jnp.exp(-x))


def _silu(x):
    return x * _sigmoid(x)


def _softplus(x):
    return jnp.maximum(x, 0.0) + jnp.log1p(jnp.exp(-jnp.abs(x)))


def _cparams(sem):
    return pltpu.CompilerParams(dimension_semantics=sem, vmem_limit_bytes=VMEM_LIMIT)


def _mod_kernel(c_ref, w_ref, b_ref, o_ref):
    c = c_ref[...]
    o_ref[...] = _mm(_silu(c), w_ref[...], mode="f32") + b_ref[...]


def _modulation(c_rows, ada_w, ada_b):
    depth, d, n = ada_w.shape
    rows = c_rows.shape[0]
    tn = 1024
    return pl.pallas_call(
        _mod_kernel,
        grid=(depth, n // tn),
        in_specs=[
            pl.BlockSpec((rows, d), lambda l, j: (0, 0)),
            pl.BlockSpec((None, d, tn), lambda l, j: (l, 0, j)),
            pl.BlockSpec((None, 1, tn), lambda l, j: (l, 0, j)),
        ],
        out_specs=pl.BlockSpec((None, rows, tn), lambda l, j: (l, 0, j)),
        out_shape=jax.ShapeDtypeStruct((depth, rows, n), F32),
        compiler_params=_cparams(("parallel", "parallel")),
        name="adaln_mod",
    )(c_rows, ada_w, ada_b.reshape(depth, 1, n))


def _row_tile(rows, target):
    return max(t for t in range(8, target + 1, 8) if rows % t == 0)


def _mod_specs(which, batch):
    lat = pl.BlockSpec((None, 1, D_MODEL), lambda b, i, *_: (b, 0, which))
    ctx = pl.BlockSpec((None, 1, D_MODEL), lambda b, i, *_: (batch, 0, which))
    return [lat, ctx]


def _ctx_rows(tm, n_lat_rows):
    row = pl.program_id(1) * tm + lax.broadcasted_iota(jnp.int32, (tm, 1), 0)
    return row >= n_lat_rows


def _prenorm(x, gain, scale, shift):
    ms = jnp.mean(x * x, axis=-1, keepdims=True)
    y = x * lax.rsqrt(ms + NORM_EPS) * gain
    return y * (1.0 + scale) + shift


def _rope(x, cos, sin):
    width = x.shape[-1]
    lane = lax.broadcasted_iota(jnp.int32, x.shape, 1)
    swapped = jnp.where(lane % 32 < 16, pltpu.roll(x, width - 16, axis=1), pltpu.roll(x, 16, axis=1))
    return x * cos + swapped * sin


def _head_block_ones():
    idx = np.arange(BRANCH_W)
    return (idx[:, None] // HEAD == idx[None, :] // HEAD).astype(np.float32)


N_LEVELS = 6


def _level_mask(ri, ci, s, reverse):
    b = 1 << s
    blk = (ri // (2 * b)) == (ci // (2 * b))
    hi_r, hi_c = (ri // b) % 2 == 1, (ci // b) % 2 == 1
    return blk & ((hi_c & ~hi_r) if reverse else (hi_r & ~hi_c))


SCAN_HEADS = 2
GROUP_W = SCAN_HEADS * HEAD
N_GROUPS = N_HEADS // SCAN_HEADS


def _scan_masks(reverse):
    n = SCAN_HEADS * CHUNK
    ri = np.arange(CHUNK)[:, None]
    ci = np.arange(n)[None, :] % CHUNK
    strict, incl = (ci > ri, ci >= ri) if reverse else (ci < ri, ci <= ri)
    compact = [strict, incl, ci == ri] + [_level_mask(ri, ci, s, reverse) for s in range(N_LEVELS)]
    r = np.arange(n)[:, None]
    c = np.arange(n)[None, :]
    block = [(r // CHUNK) == (c // CHUNK), r == c]
    return np.stack(compact).astype(np.float32), np.stack(block).astype(np.float32)


EXACT_TERMS = 3
NORM_TERMS = 2


def _tri(reverse):
    i = np.arange(CHUNK)
    m = (i[None, :] >= i[:, None]) if reverse else (i[None, :] <= i[:, None])
    return np.tile(m.astype(np.float32), (1, EXACT_TERMS))


def _split_terms(x, terms):
    out = []
    for _ in range(terms - 1):
        part = x.astype(BF16)
        out.append(part)
        x = x - part.astype(F32)
    out.append(x.astype(BF16))
    return out


def _mm_exact(a, b, dims=NN, split="a", terms=EXACT_TERMS):
    assert dims == NN
    if split == "a":
        lhs = jnp.concatenate(_split_terms(a, terms), axis=1)
        rhs = jnp.concatenate([b.astype(BF16)] * terms, axis=0)
    else:
        lhs = a.astype(BF16)
        rhs = jnp.concatenate(_split_terms(b, terms), axis=0)
    return lax.dot_general(lhs, rhs, dims, preferred_element_type=F32)


C_STRICT, C_INCL, C_EYE, C_LVL0 = 0, 1, 2, 3
B_SAME, B_EYE = 0, 1
SCAN_BATCH = 8
SCAN_ROWS = 256
assert CHUNK == HEAD and (1 << N_LEVELS) == CHUNK


def _scan_chunk(vals, st, tri, ones, cm_ref, bm_ref, same_bf, *, reverse, scalar_decay, lowrank, mm_mode):
    same = bm_ref[B_SAME]

    def get(name):
        return vals[name]

    def expand(x):
        return jnp.concatenate([x.astype(BF16)] * SCAN_HEADS, axis=0) * same_bf

    def keep(x, k):
        return jnp.where(cm_ref[k] > 0.5, x, 0.0)

    logw = get("w")
    r, k, v = get("r"), get("k"), get("v")
    cum = _mm_exact(tri, logw, split="b")
    cum_x = cum - logw
    last = 0 if reverse else CHUNK - 1
    total = cum[last:last + 1, :]
    to_end = jnp.exp(total - cum)

    if scalar_decay:
        diag = jnp.concatenate([cum] * SCAN_HEADS, axis=0) * bm_ref[B_EYE]
        cum_row = _mm_exact(ones, diag, split="b")
        d_ii = jnp.exp(jnp.where(cm_ref[C_INCL] > 0.5, cum - cum_row, -1e30))
        r_q, k_q = r, k
    else:
        ref_row = cum[CHUNK // 2:CHUNK // 2 + 1, :]
        p_inv = jnp.exp(ref_row - cum)
        r_q, k_q = r * jnp.exp(cum - ref_row), k * p_inv

    k_e, v_e = expand(k_q), expand(v)
    r_abs = r * jnp.exp(cum)
    if not lowrank:
        s_k = _mm(r_q, k_e, NT, mm_mode)
        a_rk = s_k * d_ii if scalar_decay else keep(s_k, C_INCL)
        y = _mm(r_abs, st, NT, mm_mode) + _mm(a_rk, v_e, NN, mm_mode)
        upd = _mm(v, k * to_end, TN, mm_mode)
        return y, st * jnp.exp(total) + upd * same

    a, b = get("a"), get("b")
    if scalar_decay:
        d_xi = jnp.exp(jnp.where(cm_ref[C_STRICT] > 0.5, cum_x - cum_row, -1e30))
        a_q, b_q = a, b
    else:
        a_q, b_q = a * jnp.exp(cum_x - ref_row), b * p_inv
    lhs = jnp.concatenate([a_q, r_q], axis=0)
    scores = _mm(lhs, jnp.concatenate([expand(b_q), k_e], axis=0), NT, mm_mode)
    s_b, s_k = scores[:, 0:GROUP_W], scores[:, GROUP_W:]
    if scalar_decay:
        a_ab, a_rb = s_b[0:CHUNK] * d_xi, s_b[CHUNK:] * d_ii
        a_ak, a_rk = s_k[0:CHUNK] * d_xi, s_k[CHUNK:] * d_ii
    else:
        a_ab, a_rb = keep(s_b[0:CHUNK], C_STRICT), keep(s_b[CHUNK:], C_INCL)
        a_ak, a_rk = keep(s_k[0:CHUNK], C_STRICT), keep(s_k[CHUNK:], C_INCL)

    inv = cm_ref[C_EYE] + a_ab * cm_ref[C_LVL0]
    for s in range(1, N_LEVELS):
        c_s = expand(a_ab * cm_ref[C_LVL0 + s])
        inv = inv + _mm(_mm(inv, c_s, NN, mm_mode), expand(inv), NN, mm_mode)

    a_abs = a * jnp.exp(cum_x)
    from_state = _mm(jnp.concatenate([a_abs, r_abs], axis=0), st, NT, mm_mode)
    from_v = _mm(jnp.concatenate([a_ak, a_rk], axis=0), v_e, NN, mm_mode)
    both = from_state + from_v
    z = _mm(inv, expand(both[0:CHUNK]), NN, mm_mode)
    y = both[CHUNK:] + _mm(a_rb, expand(z), NN, mm_mode)
    upd = _mm(jnp.concatenate([v, z], axis=0), jnp.concatenate([k * to_end, b * to_end], axis=0), TN, mm_mode)
    return y, st * jnp.exp(total) + upd * same


def _scan_kernel(*refs, reverse, scalar_decay, lowrank, slots, sources, finish, mm_mode, lora, r_scale):
    it = iter(refs)
    src_refs = {name: next(it) for name in sources}
    sh_ref = src_refs[sources[0]]
    cm_ref, bm_ref, tri_ref = next(it), next(it), next(it)
    if lora:
        lo_ref, gw_ref, gb_ref, lane_ref = next(it), next(it), next(it), next(it)
    if finish:
        ob_ref, fin_ref, p1_ref, p2_ref, avg_ref = next(it), next(it), next(it), next(it), next(it)
    o_ref, st_ref = next(it), next(it)

    @pl.when(pl.program_id(1) == 0)
    def _():
        st_ref[...] = jnp.zeros_like(st_ref)

    nb, block_rows = sh_ref.shape[0], sh_ref.shape[1]
    n_chunks = block_rows // CHUNK
    n = nb * N_GROUPS
    tri = jnp.broadcast_to(tri_ref[...], (n,) + tri_ref.shape)
    ones = jnp.ones((n, CHUNK, EXACT_TERMS * SCAN_HEADS * CHUNK), F32)
    chunk = functools.partial(_scan_chunk, cm_ref=cm_ref, bm_ref=bm_ref, same_bf=bm_ref[B_SAME].astype(BF16),
                              reverse=reverse, scalar_decay=scalar_decay, lowrank=lowrank, mm_mode=mm_mode)

    def step(c, carry):
        rows = pl.ds(pl.multiple_of((n_chunks - 1 - c if reverse else c) * CHUNK, CHUNK), CHUNK)

        def groups(name):
            src, idx = slots[name]
            ref = src_refs[src]
            parts = [ref[:, rows, idx * BRANCH_W + g * GROUP_W:idx * BRANCH_W + (g + 1) * GROUP_W].astype(F32)
                     for g in range(N_GROUPS)]
            return jnp.stack(parts, axis=1).reshape(n, CHUNK, GROUP_W)

        vals = {name: groups(name) for name in slots}
        if lora:
            lo = lo_ref[:, rows, :].reshape(nb * CHUNK, lo_ref.shape[-1])
            z = _mm(lo, gw_ref[...], mode=mm_mode) + gb_ref[...]
            logw = ((-_softplus(-z) / C_GATE_NORM) * lane_ref[...]).reshape(nb, CHUNK, BRANCH_W)
            parts = [logw[:, :, g * GROUP_W:(g + 1) * GROUP_W] for g in range(N_GROUPS)]
            vals["w"] = jnp.stack(parts, axis=1).reshape(n, CHUNK, GROUP_W)
        if r_scale != 1.0:
            vals["r"] = vals["r"] * r_scale
        y, st_new = jax.vmap(chunk)(vals, st_ref[...], tri, ones)
        st_ref[...] = st_new
        y = y.reshape(nb, N_GROUPS, CHUNK, GROUP_W)
        y = jnp.concatenate([y[:, g] for g in range(N_GROUPS)], axis=-1)
        if not finish:
            o_ref[:, rows, :] = y
            return carry

        y = (y + ob_ref[:, rows, :]).reshape(nb * CHUNK, BRANCH_W)
        avg = avg_ref[...]
        if finish == "groupnorm":
            gate = fin_ref[:, rows, 0:BRANCH_W].reshape(y.shape)
            bonus = fin_ref[:, rows, BRANCH_W:2 * BRANCH_W].reshape(y.shape)
            cen = y - _mm_exact(y, avg)
            var = _mm_exact(cen * cen, avg)
            yn = cen * lax.rsqrt(var + A_GN_EPS) * p1_ref[...] + p2_ref[...]
            out = (yn + bonus) * gate
        else:
            ms = _mm_exact(y * y, avg)
            gate = fin_ref[:, rows, :].astype(F32).reshape(y.shape)
            out = y * lax.rsqrt(ms + NORM_EPS) * p1_ref[...] * _silu(gate)
        o_ref[:, rows, :] = out.reshape(nb, CHUNK, BRANCH_W)
        return carry

    lax.fori_loop(0, n_chunks, step, 0)


def _view(x):
    return x if isinstance(x, tuple) else (x, x.shape[-1], 0)


def _scan(srcs, n_lat_rows, *, reverse, scalar_decay, lowrank, slots, finish=None, fin_args=None,
          mm_mode="bf16", lora=None, r_scale=1.0):
    sources = tuple(srcs)
    views = [_view(srcs[s]) for s in sources]
    batch, rows, _ = views[0][0].shape
    nc, nlat = rows // SCAN_ROWS, n_lat_rows // SCAN_ROWS
    nctx = nc - nlat
    nb = math.gcd(batch, SCAN_BATCH)

    if reverse:
        def chunk(n):
            return nc - 1 - n
    else:
        def chunk(n):
            return jnp.where(n < nctx, nlat + n, n - nctx)

    def row_spec(width, col=0):
        return pl.BlockSpec((nb, SCAN_ROWS, width), lambda b, n: (b, chunk(n), col))

    def const_spec(shape):
        zeros = (0,) * len(shape)
        return pl.BlockSpec(shape, lambda b, n: zeros)

    cmask, bmask = (jnp.asarray(m) for m in _scan_masks(reverse))
    tri = jnp.asarray(_tri(reverse))
    in_specs = [row_spec(w, c) for _, w, c in views]
    in_specs += [const_spec(cmask.shape), const_spec(bmask.shape), const_spec(tri.shape)]
    args = [a for a, _, _ in views] + [cmask, bmask, tri]
    if lora:
        (lo, lo_w, lo_c), gw, gb, lane = _view(lora[0]), lora[1], lora[2], lora[3]
        in_specs += [row_spec(lo_w, lo_c), const_spec(gw.shape), const_spec(gb.shape), const_spec(lane.shape)]
        args += [lo, gw, gb, lane]
    if finish:
        ob, fin, p1, p2 = fin_args
        fin, fin_w, fin_c = _view(fin)
        avg = jnp.asarray(_head_block_ones() / HEAD)
        in_specs += [row_spec(BRANCH_W), row_spec(fin_w, fin_c), const_spec(p1.shape), const_spec(p2.shape),
                     const_spec(avg.shape)]
        args += [ob, fin, p1, p2, avg]
    kern = functools.partial(_scan_kernel, reverse=reverse, scalar_decay=scalar_decay, lowrank=lowrank,
                             slots=slots, sources=sources, finish=finish, mm_mode=mm_mode, lora=bool(lora),
                             r_scale=r_scale)
    return pl.pallas_call(
        kern,
        grid=(batch // nb, nc),
        in_specs=in_specs,
        out_specs=row_spec(BRANCH_W),
        out_shape=jax.ShapeDtypeStruct((batch, rows, BRANCH_W), F32),
        scratch_shapes=[pltpu.VMEM((nb * N_GROUPS, GROUP_W, GROUP_W), F32)],
        compiler_params=_cparams(("parallel", "arbitrary")),
        name="dplr_scan_" + ("bwd" if reverse else "fwd"),
    )(*args)


def _gla_scan(p, n_lat_rows, gwp, gbp, lane, norm_g):
    w = BRANCH_W
    srcs = {"sh": (p, 3 * w, 0)}
    lo = (p, gwp.shape[1], 4 * w // gwp.shape[1])
    kw = dict(scalar_decay=False, lowrank=False, slots=GLA_SLOTS, r_scale=C_DK ** -0.5)
    ob = _scan(srcs, n_lat_rows, reverse=True, lora=(lo, gwp[1], gbp[1:2], lane), **kw)
    return _scan(srcs, n_lat_rows, reverse=False, lora=(lo, gwp[0], gbp[0:1], lane), finish="rms",
                 fin_args=(ob, (p, w, 3), norm_g, norm_g), **kw)


def _bidir_scan(sh, dks, dws, n_lat_rows, fin, p1, p2, *, finish, **kw):
    def srcs(d):
        out = {"sh": sh, "dw": dws[d]}
        if dks is not None:
            out["dk"] = dks[d]
        return out

    ob = _scan(srcs(1), n_lat_rows, reverse=True, **kw)
    return _scan(srcs(0), n_lat_rows, reverse=False, finish=finish, fin_args=(ob, fin, p1, p2), **kw)


def _halo_specs(width, rows, col):
    per = PREP_ROWS // HALO
    last = rows // HALO - 1
    own = pl.BlockSpec((None, PREP_ROWS, width), lambda b, i: (b, i, col))
    prev = pl.BlockSpec((None, HALO, width), lambda b, i: (b, jnp.maximum(i * per - 1, 0), col))
    nxt = pl.BlockSpec((None, HALO, width), lambda b, i: (b, jnp.minimum((i + 1) * per, last), col))
    return [own, prev, nxt]


def _seq_edges(n_lat_blocks):
    i = pl.program_id(1)
    first = (i == 0) | (i == n_lat_blocks)
    lastb = (i == n_lat_blocks - 1) | (i == pl.num_programs(1) - 1)
    return first, lastb


def _vec_spec(shape):
    zeros = (0,) * len(shape)
    return pl.BlockSpec(shape, lambda b, i: zeros)


def _shift_matrix(offset):
    m = np.zeros((PREP_ROWS, PREP_ROWS + 2 * HALO), np.float32)
    t = np.arange(PREP_ROWS)
    src = t + offset
    col = np.where(src < 0, PREP_ROWS + HALO + src, np.where(src >= PREP_ROWS, HALO + src, src))
    m[t, col] = 1.0
    return m


def _rwkv_prep(x, prev, nxt, first, lastb, shift_ref, mu_ref, w2_ref, a2_ref, g2_ref, w0_ref, a0_ref, kk_ref,
               ka_ref, rk_ref, ones_ref, sh_ref, dk0_ref, dk1_ref, dw0_ref, dw1_ref, fin_ref, mode="bf16"):
    zero = jnp.zeros((), x.dtype)
    xe = jnp.concatenate([x, jnp.where(first, zero, prev), jnp.where(lastb, zero, nxt)], axis=0)
    around = jnp.dot(shift_ref[...], xe, preferred_element_type=F32)
    x = x.astype(F32)
    xm = x + (around - x) * mu_ref[...]
    w = BRANCH_W
    r, k, v, lo = xm[:, 0:w], xm[:, w:2 * w], xm[:, 2 * w:3 * w], xm[:, 3 * w:4 * w]
    ones = ones_ref[...]
    th, sg = jnp.tanh(lo), _sigmoid(lo)
    gate = _mm(sg, g2_ref[...], mode=mode)
    kx = k * kk_ref[...]
    kk = kx * lax.rsqrt(_mm_exact(kx * kx, ones, terms=NORM_TERMS) + 1e-6)
    sh_ref[:, 0:w] = r.astype(BF16)
    sh_ref[:, w:2 * w] = v.astype(BF16)
    sh_ref[:, 2 * w:3 * w] = (-kk).astype(BF16)
    bonus = jnp.zeros_like(v)
    for d, (dk_ref, dw_ref) in enumerate(((dk0_ref, dw0_ref), (dk1_ref, dw1_ref))):
        w_raw = w0_ref[d:d + 1, :] + _mm(th, w2_ref[d], mode=mode)
        dw_ref[...] = -math.exp(-0.5) * _sigmoid(w_raw)
        a = _sigmoid(a0_ref[d:d + 1, :] + _mm(lo, a2_ref[d], mode=mode))
        kd = k * (1.0 + (a - 1.0) * ka_ref[...])
        dk_ref[:, 0:w] = kd.astype(BF16)
        dk_ref[:, w:2 * w] = (kk * a).astype(BF16)
        bonus = bonus + _mm_exact(r * kd * rk_ref[...], ones) * v
    fin_ref[:, 0:w] = gate
    fin_ref[:, w:2 * w] = bonus


def _gdn_prep(p, prev, nxt, first, lastb, conv_ref, alog_ref, dt_ref, eb_ref, ea_ref, ones_ref,
              sh_ref, dk0_ref, dk1_ref, dw0_ref, dw1_ref, fin_ref):
    w = BRANCH_W
    x = p[:, 0:3 * w].astype(F32)
    top = jnp.where(first, 0.0, prev[:, 0:3 * w].astype(F32))
    bot = jnp.where(lastb, 0.0, nxt[:, 0:3 * w].astype(F32))
    xe = jnp.concatenate([top, x, bot], axis=0)
    ext = PREP_ROWS + 2 * HALO
    acc = jnp.zeros_like(x)
    for s in range(B_CONV):
        shift = (B_CONV // 2 - s) % ext
        rolled = xe if shift == 0 else pltpu.roll(xe, shift, axis=0)
        acc = acc + rolled[HALO:HALO + PREP_ROWS] * conv_ref[s:s + 1, :]
    qkv = _silu(acc)
    ones = ones_ref[...]

    def l2n(t):
        return t * lax.rsqrt(_mm_exact(t * t, ones, terms=NORM_TERMS) + 1e-6)

    q = l2n(qkv[:, 0:w]) * (HEAD ** -0.5)
    k = l2n(qkv[:, w:2 * w])
    v = qkv[:, 2 * w:3 * w]
    sh_ref[:, 0:w] = q.astype(BF16)
    sh_ref[:, w:2 * w] = v.astype(BF16)
    sh_ref[:, 2 * w:3 * w] = k.astype(BF16)
    sr = p[:, 4 * w:4 * w + 128].astype(F32)
    beta_all = _sigmoid(sr)
    g_all = -jnp.exp(alog_ref[...]) * _softplus(sr + dt_ref[...])
    for d, (dk_ref, dw_ref) in enumerate(((dk0_ref, dw0_ref), (dk1_ref, dw1_ref))):
        beta = _mm_exact(beta_all, eb_ref[d])
        g = _mm_exact(g_all, ea_ref[d])
        kb = k * beta
        dw_ref[...] = g
        dk_ref[:, 0:w] = kb.astype(BF16)
        dk_ref[:, w:2 * w] = (-jnp.exp(g) * kb).astype(BF16)
    fin_ref[...] = p[:, 3 * w:4 * w].astype(F32)


SEG_GLA, SEG_GDN, SEG_ATT, SEG_RWKV = range(4)
N_RWKV_CONSTS, N_GDN_CONSTS = 11, 6
RWKV_OUTS = [(3, BF16), (2, BF16), (2, BF16), (1, F32), (1, F32), (2, F32)]
GDN_OUTS = [(3, BF16), (2, BF16), (2, BF16), (1, F32), (1, F32), (1, F32)]


def _front_kernel(*refs, n_lat_blocks, seg):
    it = iter(refs)
    x_ref, xp_ref, xn_ref, sc_ref, csc_ref, sh_ref, csh_ref, g_ref, w_ref, cos_ref, sin_ref = (
        next(it) for _ in range(11))
    rwkv_consts = [next(it) for _ in range(N_RWKV_CONSTS)]
    gdn_consts = [next(it) for _ in range(N_GDN_CONSTS)]
    pc_ref, pd_ref = next(it), next(it)
    rwkv_outs = [next(it) for _ in RWKV_OUTS]
    gdn_outs = [next(it) for _ in GDN_OUTS]

    first, lastb = _seq_edges(n_lat_blocks)
    is_ctx = pl.program_id(1) >= n_lat_blocks
    scale = jnp.where(is_ctx, csc_ref[...], sc_ref[...])
    shift = jnp.where(is_ctx, csh_ref[...], sh_ref[...])
    h = _prenorm(x_ref[...], g_ref[...], scale, shift).astype(BF16)
    halo = jnp.concatenate([xp_ref[...], xn_ref[...]], axis=0)
    h_halo = _prenorm(halo, g_ref[...], scale, shift).astype(BF16)

    def proj(rows, which):
        lo, hi = seg[which]
        return jnp.dot(rows, w_ref[:, lo:hi], preferred_element_type=F32)

    pc_ref[...] = proj(h, SEG_GLA).astype(BF16)
    att = proj(h, SEG_ATT)
    w = BRANCH_W
    for c0 in range(0, att.shape[1], w):
        part = att[:, c0:c0 + w]
        if c0 < 2 * w:
            part = _rope(part, cos_ref[...], sin_ref[...])
        pd_ref[:, c0:c0 + w] = part.astype(BF16)

    h_ext = jnp.concatenate([h, h_halo], axis=0)
    n = h.shape[0]
    pb = proj(h_ext, SEG_GDN).astype(BF16)
    _gdn_prep(pb[0:n], pb[n:n + HALO], pb[n + HALO:], first, lastb, *gdn_consts, *gdn_outs)
    pa = proj(h_ext, SEG_RWKV).astype(BF16)
    _rwkv_prep(pa[0:n], pa[n:n + HALO], pa[n + HALO:], first, lastb, *rwkv_consts, *rwkv_outs)


def _front(xs, mod, gain, w_all, seg, cos, sin, rwkv_consts, gdn_consts, n_lat_rows):
    batch, rows, d = xs.shape
    assert len(rwkv_consts) == N_RWKV_CONSTS and len(gdn_consts) == N_GDN_CONSTS
    tab = pl.BlockSpec((PREP_ROWS, BRANCH_W), lambda b, i: (i, 0))
    consts = list(rwkv_consts) + list(gdn_consts)
    outs = [(seg[s][1] - seg[s][0], BF16) for s in (SEG_GLA, SEG_ATT)]
    outs += [(n * BRANCH_W, dt) for n, dt in RWKV_OUTS + GDN_OUTS]
    out_specs = [pl.BlockSpec((None, PREP_ROWS, lanes), lambda b, i: (b, i, 0)) for lanes, _ in outs]
    out_shape = [jax.ShapeDtypeStruct((batch, rows, lanes), dt) for lanes, dt in outs]
    res = pl.pallas_call(
        functools.partial(_front_kernel, n_lat_blocks=n_lat_rows // PREP_ROWS, seg=seg),
        grid=(batch, rows // PREP_ROWS),
        in_specs=_halo_specs(d, rows, 0) + _mod_specs(1, batch) + _mod_specs(0, batch)
        + [_vec_spec(gain.shape), _vec_spec(w_all.shape), tab, tab] + [_vec_spec(c.shape) for c in consts],
        out_specs=out_specs,
        out_shape=out_shape,
        compiler_params=_cparams(("parallel", "parallel")),
        name="front_proj_prep",
    )(xs, xs, xs, mod, mod, mod, mod, gain, w_all, cos, sin, *consts)
    pc, pd = res[0], res[1]
    a_sh, a_dk0, a_dk1, a_dw0, a_dw1, a_fin = res[2:8]
    b_sh, b_dk0, b_dk1, b_dw0, b_dw1, b_fin = res[8:14]
    return pc, pd, (a_sh, (a_dk0, a_dk1), (a_dw0, a_dw1), a_fin), (b_sh, (b_dk0, b_dk1), (b_dw0, b_dw1), b_fin)


NEG_BIG = -1e30


def _window_bias():
    iq = np.arange(2 * ATT_BLOCK)[:, None] % ATT_BLOCK
    ik = np.arange(ATT_BLOCK)[None, :]
    ok = np.stack([ik >= iq, np.ones_like(ik >= iq), ik <= iq])
    assert WINDOW == ATT_BLOCK
    return np.where(ok, 0.0, NEG_BIG).astype(np.float32)


def _attn_kernel(sink_ref, q_ref, kc_ref, vc_ref, kp_ref, ko_ref, kn_ref, vp_ref, vo_ref, vn_ref,
                 bias_ref, o_ref, *, n_lat_rows):
    t = pl.program_id(1)
    blk = ATT_BLOCK
    q = q_ref[...] * (HEAD ** -0.5)
    k_band = [kp_ref, ko_ref, kn_ref]
    v_band = [vp_ref, vo_ref, vn_ref]

    q_lat = t * blk < n_lat_rows
    in_seq = [q_lat & (t >= 1), q_lat, q_lat & ((t + 1) * blk < n_lat_rows)]
    bias = [jnp.where(in_seq[j], bias_ref[j], NEG_BIG) for j in range(3)]
    lane = lax.broadcasted_iota(jnp.int32, (blk, 2 * HEAD), 1)
    row2 = lax.broadcasted_iota(jnp.int32, (2 * blk, 1), 0)
    for g in range(D_KV_HEADS):
        cols = slice(g * 2 * HEAD, (g + 1) * 2 * HEAD)
        qg = q[:, cols]
        zero = jnp.zeros((), qg.dtype)
        qs = jnp.concatenate([jnp.where(lane < HEAD, qg, zero), jnp.where(lane >= HEAD, qg, zero)], axis=0)
        s_ctx = _mm(qs, kc_ref[:, cols], NT)
        s_band = [_mm(qs, k_band[j][:, cols], NT) + bias[j] for j in range(3)]
        sink = jnp.where(row2 < blk, sink_ref[2 * g], sink_ref[2 * g + 1])
        m = jnp.maximum(jnp.max(s_ctx, axis=-1, keepdims=True), sink)
        for s in s_band:
            m = jnp.maximum(m, jnp.max(s, axis=-1, keepdims=True))
        p_ctx = jnp.exp(s_ctx - m)
        den = jnp.sum(p_ctx, axis=-1, keepdims=True) + jnp.exp(sink - m)
        acc = _mm(p_ctx, vc_ref[:, cols], NN)
        for j in range(3):
            pj = jnp.exp(s_band[j] - m)
            den = den + jnp.sum(pj, axis=-1, keepdims=True)
            acc = acc + _mm(pj, v_band[j][:, cols], NN)
        og = acc / den
        o_ref[:, cols] = jnp.where(lane < HEAD, og[0:blk], og[blk:2 * blk])


def _attention(p, sink, n_lat_rows, out_rows, col0=0):
    batch, rows, _ = p.shape
    blk = ATT_BLOCK
    nb = rows // blk
    w = BRANCH_W
    n_ctx_rows = rows - n_lat_rows
    assert n_lat_rows % n_ctx_rows == 0 and n_ctx_rows % blk == 0 and out_rows % blk == 0

    def band(col, off):
        def index(b, t):
            return (b, jnp.clip(t + off, 0, nb - 1), col0 + col)
        return pl.BlockSpec((None, blk, w), index)

    ctx = lambda col: pl.BlockSpec((None, n_ctx_rows, w), lambda b, t: (b, n_lat_rows // n_ctx_rows, col0 + col))
    kern = functools.partial(_attn_kernel, n_lat_rows=n_lat_rows)
    bias = jnp.asarray(_window_bias())
    return pl.pallas_call(
        kern,
        grid=(batch, out_rows // blk),
        in_specs=[pl.BlockSpec(memory_space=pltpu.SMEM), band(0, 0), ctx(1), ctx(2),
                  band(1, -1), band(1, 0), band(1, 1), band(2, -1), band(2, 0), band(2, 1),
                  _vec_spec(bias.shape)],
        out_specs=pl.BlockSpec((None, blk, w), lambda b, t: (b, t, 0)),
        out_shape=jax.ShapeDtypeStruct((batch, out_rows, w), F32),
        compiler_params=_cparams(("parallel", "parallel")),
        name="window_attn",
    )(sink, p, p, p, p, p, p, p, p, p, bias)


def _merge_kernel(x_ref, sc_ref, csc_ref, sh_ref, csh_ref, gm_ref, cgm_ref, g0_ref, g1_ref,
                  ya_ref, yb_ref, yc_ref, yd_ref, wg_ref, gb_ref, wb_ref, wo_ref, o_ref, *, n_lat_rows):
    x = x_ref[...]
    is_ctx = _ctx_rows(x.shape[0], n_lat_rows)
    scale = jnp.where(is_ctx, csc_ref[...], sc_ref[...])
    shift = jnp.where(is_ctx, csh_ref[...], sh_ref[...])
    gmod = jnp.where(is_ctx, cgm_ref[...], gm_ref[...])
    h = _prenorm(x, g0_ref[...], scale, shift).astype(BF16)
    acc = jnp.zeros(x.shape, F32)
    for i, y_ref in enumerate((ya_ref, yb_ref, yc_ref, yd_ref)):
        pre = jnp.dot(h, wg_ref[:, i * D_MODEL:(i + 1) * D_MODEL], preferred_element_type=F32)
        gate = _sigmoid(pre + gb_ref[i:i + 1, :])
        acc = acc + gate * jnp.dot(y_ref[...].astype(BF16), wb_ref[i], preferred_element_type=F32)
    out = jnp.dot(acc.astype(BF16), wo_ref[...], preferred_element_type=F32)
    ms = jnp.mean(out * out, axis=-1, keepdims=True)
    o_ref[...] = x + gmod * (out * lax.rsqrt(ms + NORM_EPS) * g1_ref[...])


def _merge(xs, mod, gain0, gain1, ys, wg, gate_b, wb, wo, n_lat_rows, out_rows):
    batch, _, d = xs.shape
    rows = out_rows
    tm = _row_tile(rows, 272)
    tile = lambda width: pl.BlockSpec((None, tm, width), lambda b, i: (b, i, 0))
    consts = [wg, gate_b, wb, wo]
    return pl.pallas_call(
        functools.partial(_merge_kernel, n_lat_rows=n_lat_rows),
        grid=(batch, rows // tm),
        in_specs=[tile(d)] + _mod_specs(1, batch) + _mod_specs(0, batch) + _mod_specs(2, batch)
        + [_vec_spec(gain0.shape), _vec_spec(gain1.shape)]
        + [tile(BRANCH_W)] * 4 + [_vec_spec(c.shape) for c in consts],
        out_specs=tile(d),
        out_shape=jax.ShapeDtypeStruct((batch, rows, d), F32),
        compiler_params=_cparams(("parallel", "parallel")),
        name="merge_out",
    )(xs, *([mod] * 6), gain0, gain1, *ys, *consts)


FFN_CHUNK = 512


def _ffn_kernel(x_ref, sc_ref, csc_ref, sh_ref, csh_ref, gm_ref, cgm_ref, g2_ref, g3_ref, w1_ref, w2_ref,
                o_ref, *, n_lat_rows):
    x = x_ref[...]
    is_ctx = _ctx_rows(x.shape[0], n_lat_rows)
    scale = jnp.where(is_ctx, csc_ref[...], sc_ref[...])
    shift = jnp.where(is_ctx, csh_ref[...], sh_ref[...])
    h = _prenorm(x, g2_ref[...], scale, shift).astype(BF16)
    hidden = w2_ref.shape[0]
    out = jnp.zeros(x.shape, F32)
    for lo in range(0, hidden, FFN_CHUNK):
        hi = min(lo + FFN_CHUNK, hidden)
        gt = jnp.dot(h, w1_ref[:, lo:hi], preferred_element_type=F32)
        up = jnp.dot(h, w1_ref[:, hidden + lo:hidden + hi], preferred_element_type=F32)
        out = out + jnp.dot((_silu(gt) * up).astype(BF16), w2_ref[lo:hi, :], preferred_element_type=F32)
    ms = jnp.mean(out * out, axis=-1, keepdims=True)
    gmod = jnp.where(is_ctx, cgm_ref[...], gm_ref[...])
    o_ref[...] = x + gmod * (out * lax.rsqrt(ms + NORM_EPS) * g3_ref[...])


def _ffn(xs, mod, gain2, gain3, w1, w2, n_lat_rows):
    batch, rows, d = xs.shape
    tm = _row_tile(rows, 544)
    tile = pl.BlockSpec((None, tm, d), lambda b, i: (b, i, 0))
    consts = [gain2, gain3, w1, w2]
    return pl.pallas_call(
        functools.partial(_ffn_kernel, n_lat_rows=n_lat_rows),
        grid=(batch, rows // tm),
        in_specs=[tile] + _mod_specs(4, batch) + _mod_specs(3, batch) + _mod_specs(5, batch)
        + [_vec_spec(c.shape) for c in consts],
        out_specs=tile,
        out_shape=jax.ShapeDtypeStruct((batch, rows, d), F32),
        compiler_params=_cparams(("parallel", "parallel")),
        name="swiglu",
    )(xs, *([mod] * 6), *consts)


def _pad_cols(w, width):
    return jnp.pad(w, ((0, 0), (0, width - w.shape[1])))


def _layer_weights(w_in, mu, w2, a2, g2, conv, a_log, dt_bias, gw2, gb):
    wts = {}
    off_b, off_c, off_d, off_g = A_IN, A_IN + B_IN, A_IN + B_IN + C_IN, A_IN + B_IN + C_IN + D_IN
    w = BRANCH_W
    wts["wa"] = _pad_cols(w_in[:, 0:A_IN], 4 * w).astype(BF16)
    wts["mu"] = _pad_cols(mu[None, :], 4 * w)
    lo = np.cumsum((0,) + A_LORA)
    place = lambda m, r0: jnp.zeros((w, w), F32).at[r0:r0 + m.shape[0], :].set(m)
    wts["w2p"] = jnp.stack([place(w2[0], lo[0]), place(w2[1], lo[1])])
    wts["a2p"] = jnp.stack([place(a2[0], lo[2]), place(a2[1], lo[3])])
    wts["g2p"] = place(g2, lo[4])

    wb = w_in[:, off_b:off_b + B_IN]
    wts["wb"] = jnp.concatenate([wb[:, 0:3 * w], wb[:, 3 * w + 16:], _pad_cols(wb[:, 3 * w:3 * w + 16], 128)],
                                axis=1).astype(BF16)
    nh = N_HEADS
    expand = np.zeros((4, 128, w), np.float32)
    for grp in range(4):
        for h in range(nh):
            expand[grp, grp * nh + h, h * HEAD:(h + 1) * HEAD] = 1.0
    wts["eb"] = jnp.asarray(expand[0:2])
    wts["ea"] = jnp.asarray(expand[2:4])
    vec = lambda t: jnp.zeros((1, 128), F32).at[0, 2 * nh:4 * nh].set(t.reshape(-1))
    wts["alog"] = vec(a_log)
    wts["dt"] = vec(dt_bias)
    wts["conv"] = conv

    wc = w_in[:, off_c:off_c + C_IN]
    pad_heads = lambda m: jnp.pad(m.reshape(m.shape[0], nh, C_DK),
                                  ((0, 0), (0, 0), (0, HEAD - C_DK))).reshape(m.shape[0], w)
    qc, kc, vc = wc[:, 0:C_QK], wc[:, C_QK:2 * C_QK], wc[:, 2 * C_QK:2 * C_QK + w]
    loc = wc[:, 2 * C_QK + w:2 * C_QK + w + 2 * C_GATE_R]
    gc = wc[:, 2 * C_QK + w + 2 * C_GATE_R:]
    wts["wc"] = jnp.concatenate([pad_heads(qc), pad_heads(kc), vc, gc, _pad_cols(loc, 128)], axis=1).astype(BF16)
    gwp = jnp.zeros((2, 128, w), F32)
    for d in range(2):
        gwp = gwp.at[d, d * C_GATE_R:(d + 1) * C_GATE_R, :].set(pad_heads(gw2[d]))
    wts["gwp"] = gwp
    wts["gbp"] = pad_heads(gb)
    wts["glane"] = jnp.asarray((np.arange(w) % HEAD < C_DK).astype(np.float32))[None, :]

    wd = w_in[:, off_d:off_d + D_IN]
    qd = wd[:, 0:w]
    dup = lambda m: jnp.concatenate([m[:, 0:HEAD], m[:, 0:HEAD], m[:, HEAD:], m[:, HEAD:]], axis=1)
    wts["wd"] = jnp.concatenate([qd, dup(wd[:, w:w + 2 * HEAD]), dup(wd[:, w + 2 * HEAD:])], axis=1).astype(BF16)
    wts["wg"] = w_in[:, off_g:].astype(BF16)
    return wts


def _rope_tables(n_lat_rows, n_ctx_rows):
    quarter = HEAD // 4
    inv = ROPE_BASE ** (-np.arange(quarter, dtype=np.float32) / quarter)
    pos = np.arange(n_lat_rows)
    rows = (pos // GRID_W).astype(np.float32)
    cols = (pos % GRID_W).astype(np.float32)
    inv = jnp.asarray(inv)
    ang_r = jnp.asarray(rows)[:, None] * inv[None, :]
    ang_c = jnp.asarray(cols)[:, None] * inv[None, :]
    cos = jnp.concatenate([jnp.cos(ang_r)] * 2 + [jnp.cos(ang_c)] * 2, axis=1)
    sin = jnp.concatenate([-jnp.sin(ang_r), jnp.sin(ang_r), -jnp.sin(ang_c), jnp.sin(ang_c)], axis=1)
    cos = jnp.concatenate([cos, jnp.ones((n_ctx_rows, HEAD), F32)], axis=0)
    sin = jnp.concatenate([sin, jnp.zeros((n_ctx_rows, HEAD), F32)], axis=0)
    return jnp.tile(cos, (1, N_HEADS)), jnp.tile(sin, (1, N_HEADS))


RWKV_SLOTS = {"r": ("sh", 0), "v": ("sh", 1), "a": ("sh", 2), "w": ("dw", 0), "k": ("dk", 0), "b": ("dk", 1)}
GDN_SLOTS = RWKV_SLOTS
GLA_SLOTS = {"r": ("sh", 0), "k": ("sh", 1), "v": ("sh", 2)}


def kernel(x, c, ctx, c_ctx, ada_w, ada_b, norm_g, w_in, gate_b, w_branch, w_out, rwkv_mu, rwkv_w0, rwkv_w2, rwkv_a0, rwkv_a2, rwkv_g2, rwkv_kk, rwkv_ka, rwkv_rk, rwkv_ln_g, rwkv_ln_b, gdn_conv, gdn_a_log, gdn_dt_bias, gdn_norm_g, gla_gw2, gla_gb, gla_norm_g, attn_sink, ffn_w1, ffn_w2):
    batch, n_lat, d = x.shape
    n_ctx = ctx.shape[1]
    depth = ada_w.shape[0]
    assert n_ctx % PREP_ROWS == 0 and n_lat % PREP_ROWS == 0 and d == D_MODEL

    mod_rows = 8 * ((batch + 1 + 7) // 8)
    c_rows = jnp.concatenate([c, c_ctx[None, :], jnp.zeros((mod_rows - batch - 1, d), F32)], axis=0)
    mod_all = _modulation(c_rows, ada_w, ada_b)
    cos, sin = _rope_tables(n_lat, n_ctx)

    xs = jnp.concatenate([x, ctx], axis=1)
    rows = n_lat + n_ctx
    row = lambda t: t.reshape(1, -1)
    for l in range(depth):
        out_rows = n_lat if l == depth - 1 else rows
        mod = mod_all[l].reshape(mod_rows, 1, 6 * d)
        ng = norm_g[l]
        wts = _layer_weights(w_in[l], rwkv_mu[l], rwkv_w2[l], rwkv_a2[l], rwkv_g2[l], gdn_conv[l],
                             gdn_a_log[l], gdn_dt_bias[l], gla_gw2[l], gla_gb[l])
        parts = [wts["wc"], wts["wb"], wts["wd"], wts["wa"]]
        w_all = jnp.concatenate(parts, axis=1)
        edges = np.cumsum([0] + [int(t.shape[1]) for t in parts])
        seg = tuple((int(edges[i]), int(edges[i + 1])) for i in range(len(parts)))
        ones = jnp.asarray(_head_block_ones())
        around = jnp.asarray(0.5 * (_shift_matrix(-1) + _shift_matrix(1)), BF16)
        rwkv_consts = [around, wts["mu"], wts["w2p"], wts["a2p"], wts["g2p"], rwkv_w0[l], rwkv_a0[l],
                       row(rwkv_kk[l]), row(rwkv_ka[l]), row(rwkv_rk[l]), ones]
        gdn_consts = [wts["conv"], wts["alog"], wts["dt"], wts["eb"], wts["ea"], ones]
        p_gla, p_att, rwkv_ops, gdn_ops = _front(xs, mod, row(ng[0]), w_all, seg, cos, sin, rwkv_consts,
                                                 gdn_consts, n_lat)

        sh, dks, dws, fin = rwkv_ops
        ya = _bidir_scan(sh, dks, dws, n_lat, fin, row(rwkv_ln_g[l]), row(rwkv_ln_b[l]), finish="groupnorm",
                         scalar_decay=False, lowrank=True, slots=RWKV_SLOTS)

        sh, dks, dws, fin = gdn_ops
        gnorm = row(jnp.tile(gdn_norm_g[l], N_HEADS))
        yb = _bidir_scan(sh, dks, dws, n_lat, fin, gnorm, gnorm, finish="rms",
                         scalar_decay=True, lowrank=True, slots=GDN_SLOTS)

        yc = _gla_scan(p_gla, n_lat, wts["gwp"], wts["gbp"], wts["glane"], row(jnp.tile(gla_norm_g[l], N_HEADS)))

        yd = _attention(p_att, attn_sink[l], n_lat, out_rows)

        xs = _merge(xs, mod, row(ng[0]), row(ng[1]), (ya, yb, yc, yd), wts["wg"], gate_b[l],
                    w_branch[l].astype(BF16), w_out[l].astype(BF16), n_lat, out_rows)
        xs = _ffn(xs, mod, row(ng[2]), row(ng[3]), ffn_w1[l].astype(BF16), ffn_w2[l].astype(BF16), n_lat)
    return xs
```

```python
import functools
import math

import numpy as np
import jax
import jax.numpy as jnp
from jax import lax
from jax.experimental import pallas as pl
from jax.experimental.pallas import tpu as pltpu

F32 = jnp.float32
BF16 = jnp.bfloat16

D_MODEL = 1024
N_BRANCH = 4
BRANCH_W = 256
HEAD = 64
N_HEADS = BRANCH_W // HEAD
NORM_EPS = 1e-6
A_GN_EPS = 64e-5
A_LORA = (32, 32, 32, 32, 64)
A_IN = 3 * BRANCH_W + sum(A_LORA)
B_IN = 4 * BRANCH_W + 4 * N_HEADS
C_DK = 32
C_QK = N_HEADS * C_DK
C_GATE_R = 16
C_GATE_NORM = 16.0
C_IN = 2 * C_QK + 2 * BRANCH_W + 2 * C_GATE_R
D_KV_HEADS = 2
D_IN = BRANCH_W + 2 * D_KV_HEADS * HEAD
B_CONV = 7
WINDOW = 128
ROPE_BASE = 10000.0
GRID_W = 64
FFN_HIDDEN = 2816

CHUNK = 64
PREP_ROWS = 256
HALO = 16
ATT_BLOCK = 128
VMEM_LIMIT = 48 * 1024 * 1024

NN = (((1,), (0,)), ((), ()))
NT = (((1,), (1,)), ((), ()))
TN = (((0,), (0,)), ((), ()))


def _mm(a, b, dims=NN, mode="bf16"):
    if mode == "f32":
        return lax.dot_general(a, b, dims, precision=lax.Precision.HIGHEST, preferred_element_type=F32)
    if mode == "x3":
        ah = a.astype(BF16)
        al = (a - ah.astype(F32)).astype(BF16)
        bh = b.astype(BF16)
        bl = (b - bh.astype(F32)).astype(BF16)
        dot = functools.partial(lax.dot_general, dimension_numbers=dims, preferred_element_type=F32)
        return dot(ah, bh) + (dot(ah, bl) + dot(al, bh))
    return lax.dot_general(a.astype(BF16), b.astype(BF16), dims, preferred_element_type=F32)


def _sigmoid(x):
    return 1.0 / (1.0 + jnp.exp(-x))


def _silu(x):
    return x * _sigmoid(x)


def _softplus(x):
    return jnp.maximum(x, 0.0) + jnp.log1p(jnp.exp(-jnp.abs(x)))


def _cparams(sem):
    return pltpu.CompilerParams(dimension_semantics=sem, vmem_limit_bytes=VMEM_LIMIT)


def _mod_kernel(c_ref, w_ref, b_ref, o_ref):
    c = c_ref[...]
    o_ref[...] = _mm(_silu(c), w_ref[...], mode="f32") + b_ref[...]


def _modulation(c_rows, ada_w, ada_b):
    depth, d, n = ada_w.shape
    rows = c_rows.shape[0]
    tn = 1024
    return pl.pallas_call(
        _mod_kernel,
        grid=(depth, n // tn),
        in_specs=[
            pl.BlockSpec((rows, d), lambda l, j: (0, 0)),
            pl.BlockSpec((None, d, tn), lambda l, j: (l, 0, j)),
            pl.BlockSpec((None, 1, tn), lambda l, j: (l, 0, j)),
        ],
        out_specs=pl.BlockSpec((None, rows, tn), lambda l, j: (l, 0, j)),
        out_shape=jax.ShapeDtypeStruct((depth, rows, n), F32),
        compiler_params=_cparams(("parallel", "parallel")),
        name="adaln_mod",
    )(c_rows, ada_w, ada_b.reshape(depth, 1, n))


def _row_tile(rows, target):
    return max(t for t in range(8, target + 1, 8) if rows % t == 0)


def _mod_specs(which, batch):
    lat = pl.BlockSpec((None, 1, D_MODEL), lambda b, i, *_: (b, 0, which))
    ctx = pl.BlockSpec((None, 1, D_MODEL), lambda b, i, *_: (batch, 0, which))
    return [lat, ctx]


def _ctx_rows(tm, n_lat_rows):
    row = pl.program_id(1) * tm + lax.broadcasted_iota(jnp.int32, (tm, 1), 0)
    return row >= n_lat_rows


def _prenorm(x, gain, scale, shift):
    ms = jnp.mean(x * x, axis=-1, keepdims=True)
    y = x * lax.rsqrt(ms + NORM_EPS) * gain
    return y * (1.0 + scale) + shift


def _rope(x, cos, sin):
    width = x.shape[-1]
    lane = lax.broadcasted_iota(jnp.int32, x.shape, 1)
    swapped = jnp.where(lane % 32 < 16, pltpu.roll(x, width - 16, axis=1), pltpu.roll(x, 16, axis=1))
    return x * cos + swapped * sin


def _head_block_ones():
    idx = np.arange(BRANCH_W)
    return (idx[:, None] // HEAD == idx[None, :] // HEAD).astype(np.float32)


N_LEVELS = 6


def _level_mask(ri, ci, s, reverse):
    b = 1 << s
    blk = (ri // (2 * b)) == (ci // (2 * b))
    hi_r, hi_c = (ri // b) % 2 == 1, (ci // b) % 2 == 1
    return blk & ((hi_c & ~hi_r) if reverse else (hi_r & ~hi_c))


SCAN_HEADS = 2
GROUP_W = SCAN_HEADS * HEAD
N_GROUPS = N_HEADS // SCAN_HEADS


def _scan_masks(reverse):
    n = SCAN_HEADS * CHUNK
    ri = np.arange(CHUNK)[:, None]
    ci = np.arange(n)[None, :] % CHUNK
    strict, incl = (ci > ri, ci >= ri) if reverse else (ci < ri, ci <= ri)
    compact = [strict, incl, ci == ri] + [_level_mask(ri, ci, s, reverse) for s in range(N_LEVELS)]
    r = np.arange(n)[:, None]
    c = np.arange(n)[None, :]
    block = [(r // CHUNK) == (c // CHUNK), r == c]
    return np.stack(compact).astype(np.float32), np.stack(block).astype(np.float32)


EXACT_TERMS = 3
NORM_TERMS = 2


def _tri(reverse):
    i = np.arange(CHUNK)
    m = (i[None, :] >= i[:, None]) if reverse else (i[None, :] <= i[:, None])
    return np.tile(m.astype(np.float32), (1, EXACT_TERMS))


def _split_terms(x, terms):
    out = []
    for _ in range(terms - 1):
        part = x.astype(BF16)
        out.append(part)
        x = x - part.astype(F32)
    out.append(x.astype(BF16))
    return out


def _mm_exact(a, b, dims=NN, split="a", terms=EXACT_TERMS):
    assert dims == NN
    if split == "a":
        lhs = jnp.concatenate(_split_terms(a, terms), axis=1)
        rhs = jnp.concatenate([b.astype(BF16)] * terms, axis=0)
    else:
        lhs = a.astype(BF16)
        rhs = jnp.concatenate(_split_terms(b, terms), axis=0)
    return lax.dot_general(lhs, rhs, dims, preferred_element_type=F32)


C_STRICT, C_INCL, C_EYE, C_LVL0 = 0, 1, 2, 3
B_SAME, B_EYE = 0, 1
SCAN_BATCH = 8
SCAN_ROWS = 256
assert CHUNK == HEAD and (1 << N_LEVELS) == CHUNK


def _scan_chunk(vals, st, tri, ones, cm_ref, bm_ref, same_bf, *, reverse, scalar_decay, lowrank, mm_mode):
    same = bm_ref[B_SAME]

    def get(name):
        return vals[name]

    def expand(x):
        return jnp.concatenate([x.astype(BF16)] * SCAN_HEADS, axis=0) * same_bf

    def keep(x, k):
        return jnp.where(cm_ref[k] > 0.5, x, 0.0)

    logw = get("w")
    r, k, v = get("r"), get("k"), get("v")
    cum = _mm_exact(tri, logw, split="b")
    cum_x = cum - logw
    last = 0 if reverse else CHUNK - 1
    total = cum[last:last + 1, :]
    to_end = jnp.exp(total - cum)

    if scalar_decay:
        diag = jnp.concatenate([cum] * SCAN_HEADS, axis=0) * bm_ref[B_EYE]
        cum_row = _mm_exact(ones, diag, split="b")
        d_ii = jnp.exp(jnp.where(cm_ref[C_INCL] > 0.5, cum - cum_row, -1e30))
        r_q, k_q = r, k
    else:
        ref_row = cum[CHUNK // 2:CHUNK // 2 + 1, :]
        p_inv = jnp.exp(ref_row - cum)
        r_q, k_q = r * jnp.exp(cum - ref_row), k * p_inv

    k_e, v_e = expand(k_q), expand(v)
    r_abs = r * jnp.exp(cum)
    if not lowrank:
        s_k = _mm(r_q, k_e, NT, mm_mode)
        a_rk = s_k * d_ii if scalar_decay else keep(s_k, C_INCL)
        y = _mm(r_abs, st, NT, mm_mode) + _mm(a_rk, v_e, NN, mm_mode)
        upd = _mm(v, k * to_end, TN, mm_mode)
        return y, st * jnp.exp(total) + upd * same

    a, b = get("a"), get("b")
    if scalar_decay:
        d_xi = jnp.exp(jnp.where(cm_ref[C_STRICT] > 0.5, cum_x - cum_row, -1e30))
        a_q, b_q = a, b
    else:
        a_q, b_q = a * jnp.exp(cum_x - ref_row), b * p_inv
    lhs = jnp.concatenate([a_q, r_q], axis=0)
    scores = _mm(lhs, jnp.concatenate([expand(b_q), k_e], axis=0), NT, mm_mode)
    s_b, s_k = scores[:, 0:GROUP_W], scores[:, GROUP_W:]
    if scalar_decay:
        a_ab, a_rb = s_b[0:CHUNK] * d_xi, s_b[CHUNK:] * d_ii
        a_ak, a_rk = s_k[0:CHUNK] * d_xi, s_k[CHUNK:] * d_ii
    else:
        a_ab, a_rb = keep(s_b[0:CHUNK], C_STRICT), keep(s_b[CHUNK:], C_INCL)
        a_ak, a_rk = keep(s_k[0:CHUNK], C_STRICT), keep(s_k[CHUNK:], C_INCL)

    inv = cm_ref[C_EYE] + a_ab * cm_ref[C_LVL0]
    for s in range(1, N_LEVELS):
        c_s = expand(a_ab * cm_ref[C_LVL0 + s])
        inv = inv + _mm(_mm(inv, c_s, NN, mm_mode), expand(inv), NN, mm_mode)

    a_abs = a * jnp.exp(cum_x)
    from_state = _mm(jnp.concatenate([a_abs, r_abs], axis=0), st, NT, mm_mode)
    from_v = _mm(jnp.concatenate([a_ak, a_rk], axis=0), v_e, NN, mm_mode)
    both = from_state + from_v
    z = _mm(inv, expand(both[0:CHUNK]), NN, mm_mode)
    y = both[CHUNK:] + _mm(a_rb, expand(z), NN, mm_mode)
    upd = _mm(jnp.concatenate([v, z], axis=0), jnp.concatenate([k * to_end, b * to_end], axis=0), TN, mm_mode)
    return y, st * jnp.exp(total) + upd * same


def _scan_kernel(*refs, reverse, scalar_decay, lowrank, slots, sources, finish, mm_mode, lora, r_scale):
    it = iter(refs)
    src_refs = {name: next(it) for name in sources}
    sh_ref = src_refs[sources[0]]
    cm_ref, bm_ref, tri_ref = next(it), next(it), next(it)
    if lora:
        lo_ref, gw_ref, gb_ref, lane_ref = next(it), next(it), next(it), next(it)
    if finish:
        ob_ref, fin_ref, p1_ref, p2_ref, avg_ref = next(it), next(it), next(it), next(it), next(it)
    o_ref, st_ref = next(it), next(it)

    @pl.when(pl.program_id(1) == 0)
    def _():
        st_ref[...] = jnp.zeros_like(st_ref)

    nb, block_rows = sh_ref.shape[0], sh_ref.shape[1]
    n_chunks = block_rows // CHUNK
    n = nb * N_GROUPS
    tri = jnp.broadcast_to(tri_ref[...], (n,) + tri_ref.shape)
    ones = jnp.ones((n, CHUNK, EXACT_TERMS * SCAN_HEADS * CHUNK), F32)
    chunk = functools.partial(_scan_chunk, cm_ref=cm_ref, bm_ref=bm_ref, same_bf=bm_ref[B_SAME].astype(BF16),
                              reverse=reverse, scalar_decay=scalar_decay, lowrank=lowrank, mm_mode=mm_mode)

    def step(c, carry):
        rows = pl.ds(pl.multiple_of((n_chunks - 1 - c if reverse else c) * CHUNK, CHUNK), CHUNK)

        def groups(name):
            src, idx = slots[name]
            ref = src_refs[src]
            parts = [ref[:, rows, idx * BRANCH_W + g * GROUP_W:idx * BRANCH_W + (g + 1) * GROUP_W].astype(F32)
                     for g in range(N_GROUPS)]
            return jnp.stack(parts, axis=1).reshape(n, CHUNK, GROUP_W)

        vals = {name: groups(name) for name in slots}
        if lora:
            lo = lo_ref[:, rows, :].reshape(nb * CHUNK, lo_ref.shape[-1])
            z = _mm(lo, gw_ref[...], mode=mm_mode) + gb_ref[...]
            logw = ((-_softplus(-z) / C_GATE_NORM) * lane_ref[...]).reshape(nb, CHUNK, BRANCH_W)
            parts = [logw[:, :, g * GROUP_W:(g + 1) * GROUP_W] for g in range(N_GROUPS)]
            vals["w"] = jnp.stack(parts, axis=1).reshape(n, CHUNK, GROUP_W)
        if r_scale != 1.0:
            vals["r"] = vals["r"] * r_scale
        y, st_new = jax.vmap(chunk)(vals, st_ref[...], tri, ones)
        st_ref[...] = st_new
        y = y.reshape(nb, N_GROUPS, CHUNK, GROUP_W)
        y = jnp.concatenate([y[:, g] for g in range(N_GROUPS)], axis=-1)
        if not finish:
            o_ref[:, rows, :] = y
            return carry

        y = (y + ob_ref[:, rows, :]).reshape(nb * CHUNK, BRANCH_W)
        avg = avg_ref[...]
        if finish == "groupnorm":
            gate = fin_ref[:, rows, 0:BRANCH_W].reshape(y.shape)
            bonus = fin_ref[:, rows, BRANCH_W:2 * BRANCH_W].reshape(y.shape)
            cen = y - _mm_exact(y, avg)
            var = _mm_exact(cen * cen, avg)
            yn = cen * lax.rsqrt(var + A_GN_EPS) * p1_ref[...] + p2_ref[...]
            out = (yn + bonus) * gate
        else:
            ms = _mm_exact(y * y, avg)
            gate = fin_ref[:, rows, :].astype(F32).reshape(y.shape)
            out = y * lax.rsqrt(ms + NORM_EPS) * p1_ref[...] * _silu(gate)
        o_ref[:, rows, :] = out.reshape(nb, CHUNK, BRANCH_W)
        return carry

    lax.fori_loop(0, n_chunks, step, 0)


def _view(x):
    return x if isinstance(x, tuple) else (x, x.shape[-1], 0)


def _scan(srcs, n_lat_rows, *, reverse, scalar_decay, lowrank, slots, finish=None, fin_args=None,
          mm_mode="bf16", lora=None, r_scale=1.0):
    sources = tuple(srcs)
    views = [_view(srcs[s]) for s in sources]
    batch, rows, _ = views[0][0].shape
    nc, nlat = rows // SCAN_ROWS, n_lat_rows // SCAN_ROWS
    nctx = nc - nlat
    nb = math.gcd(batch, SCAN_BATCH)

    if reverse:
        def chunk(n):
            return nc - 1 - n
    else:
        def chunk(n):
            return jnp.where(n < nctx, nlat + n, n - nctx)

    def row_spec(width, col=0):
        return pl.BlockSpec((nb, SCAN_ROWS, width), lambda b, n: (b, chunk(n), col))

    def const_spec(shape):
        zeros = (0,) * len(shape)
        return pl.BlockSpec(shape, lambda b, n: zeros)

    cmask, bmask = (jnp.asarray(m) for m in _scan_masks(reverse))
    tri = jnp.asarray(_tri(reverse))
    in_specs = [row_spec(w, c) for _, w, c in views]
    in_specs += [const_spec(cmask.shape), const_spec(bmask.shape), const_spec(tri.shape)]
    args = [a for a, _, _ in views] + [cmask, bmask, tri]
    if lora:
        (lo, lo_w, lo_c), gw, gb, lane = _view(lora[0]), lora[1], lora[2], lora[3]
        in_specs += [row_spec(lo_w, lo_c), const_spec(gw.shape), const_spec(gb.shape), const_spec(lane.shape)]
        args += [lo, gw, gb, lane]
    if finish:
        ob, fin, p1, p2 = fin_args
        fin, fin_w, fin_c = _view(fin)
        avg = jnp.asarray(_head_block_ones() / HEAD)
        in_specs += [row_spec(BRANCH_W), row_spec(fin_w, fin_c), const_spec(p1.shape), const_spec(p2.shape),
                     const_spec(avg.shape)]
        args += [ob, fin, p1, p2, avg]
    kern = functools.partial(_scan_kernel, reverse=reverse, scalar_decay=scalar_decay, lowrank=lowrank,
                             slots=slots, sources=sources, finish=finish, mm_mode=mm_mode, lora=bool(lora),
                             r_scale=r_scale)
    return pl.pallas_call(
        kern,
        grid=(batch // nb, nc),
        in_specs=in_specs,
        out_specs=row_spec(BRANCH_W),
        out_shape=jax.ShapeDtypeStruct((batch, rows, BRANCH_W), F32),
        scratch_shapes=[pltpu.VMEM((nb * N_GROUPS, GROUP_W, GROUP_W), F32)],
        compiler_params=_cparams(("parallel", "arbitrary")),
        name="dplr_scan_" + ("bwd" if reverse else "fwd"),
    )(*args)


def _gla_scan(p, n_lat_rows, gwp, gbp, lane, norm_g):
    w = BRANCH_W
    srcs = {"sh": (p, 3 * w, 0)}
    lo = (p, gwp.shape[1], 4 * w // gwp.shape[1])
    kw = dict(scalar_decay=False, lowrank=False, slots=GLA_SLOTS, r_scale=C_DK ** -0.5)
    ob = _scan(srcs, n_lat_rows, reverse=True, lora=(lo, gwp[1], gbp[1:2], lane), **kw)
    return _scan(srcs, n_lat_rows, reverse=False, lora=(lo, gwp[0], gbp[0:1], lane), finish="rms",
                 fin_args=(ob, (p, w, 3), norm_g, norm_g), **kw)


def _bidir_scan(sh, dks, dws, n_lat_rows, fin, p1, p2, *, finish, **kw):
    def srcs(d):
        out = {"sh": sh, "dw": dws[d]}
        if dks is not None:
            out["dk"] = dks[d]
        return out

    ob = _scan(srcs(1), n_lat_rows, reverse=True, **kw)
    return _scan(srcs(0), n_lat_rows, reverse=False, finish=finish, fin_args=(ob, fin, p1, p2), **kw)


def _halo_specs(width, rows, col):
    per = PREP_ROWS // HALO
    last = rows // HALO - 1
    own = pl.BlockSpec((None, PREP_ROWS, width), lambda b, i: (b, i, col))
    prev = pl.BlockSpec((None, HALO, width), lambda b, i: (b, jnp.maximum(i * per - 1, 0), col))
    nxt = pl.BlockSpec((None, HALO, width), lambda b, i: (b, jnp.minimum((i + 1) * per, last), col))
    return [own, prev, nxt]


def _seq_edges(n_lat_blocks):
    i = pl.program_id(1)
    first = (i == 0) | (i == n_lat_blocks)
    lastb = (i == n_lat_blocks - 1) | (i == pl.num_programs(1) - 1)
    return first, lastb


def _vec_spec(shape):
    zeros = (0,) * len(shape)
    return pl.BlockSpec(shape, lambda b, i: zeros)


def _shift_matrix(offset):
    m = np.zeros((PREP_ROWS, PREP_ROWS + 2 * HALO), np.float32)
    t = np.arange(PREP_ROWS)
    src = t + offset
    col = np.where(src < 0, PREP_ROWS + HALO + src, np.where(src >= PREP_ROWS, HALO + src, src))
    m[t, col] = 1.0
    return m


def _rwkv_prep(x, prev, nxt, first, lastb, shift_ref, mu_ref, w2_ref, a2_ref, g2_ref, w0_ref, a0_ref, kk_ref,
               ka_ref, rk_ref, ones_ref, sh_ref, dk0_ref, dk1_ref, dw0_ref, dw1_ref, fin_ref, mode="bf16"):
    zero = jnp.zeros((), x.dtype)
    xe = jnp.concatenate([x, jnp.where(first, zero, prev), jnp.where(lastb, zero, nxt)], axis=0)
    around = jnp.dot(shift_ref[...], xe, preferred_element_type=F32)
    x = x.astype(F32)
    xm = x + (around - x) * mu_ref[...]
    w = BRANCH_W
    r, k, v, lo = xm[:, 0:w], xm[:, w:2 * w], xm[:, 2 * w:3 * w], xm[:, 3 * w:4 * w]
    ones = ones_ref[...]
    th, sg = jnp.tanh(lo), _sigmoid(lo)
    gate = _mm(sg, g2_ref[...], mode=mode)
    kx = k * kk_ref[...]
    kk = kx * lax.rsqrt(_mm_exact(kx * kx, ones, terms=NORM_TERMS) + 1e-6)
    sh_ref[:, 0:w] = r.astype(BF16)
    sh_ref[:, w:2 * w] = v.astype(BF16)
    sh_ref[:, 2 * w:3 * w] = (-kk).astype(BF16)
    bonus = jnp.zeros_like(v)
    for d, (dk_ref, dw_ref) in enumerate(((dk0_ref, dw0_ref), (dk1_ref, dw1_ref))):
        w_raw = w0_ref[d:d + 1, :] + _mm(th, w2_ref[d], mode=mode)
        dw_ref[...] = -math.exp(-0.5) * _sigmoid(w_raw)
        a = _sigmoid(a0_ref[d:d + 1, :] + _mm(lo, a2_ref[d], mode=mode))
        kd = k * (1.0 + (a - 1.0) * ka_ref[...])
        dk_ref[:, 0:w] = kd.astype(BF16)
        dk_ref[:, w:2 * w] = (kk * a).astype(BF16)
        bonus = bonus + _mm_exact(r * kd * rk_ref[...], ones) * v
    fin_ref[:, 0:w] = gate
    fin_ref[:, w:2 * w] = bonus


def _gdn_prep(p, prev, nxt, first, lastb, conv_ref, alog_ref, dt_ref, eb_ref, ea_ref, ones_ref,
              sh_ref, dk0_ref, dk1_ref, dw0_ref, dw1_ref, fin_ref):
    w = BRANCH_W
    x = p[:, 0:3 * w].astype(F32)
    top = jnp.where(first, 0.0, prev[:, 0:3 * w].astype(F32))
    bot = jnp.where(lastb, 0.0, nxt[:, 0:3 * w].astype(F32))
    xe = jnp.concatenate([top, x, bot], axis=0)
    ext = PREP_ROWS + 2 * HALO
    acc = jnp.zeros_like(x)
    for s in range(B_CONV):
        shift = (B_CONV // 2 - s) % ext
        rolled = xe if shift == 0 else pltpu.roll(xe, shift, axis=0)
        acc = acc + rolled[HALO:HALO + PREP_ROWS] * conv_ref[s:s + 1, :]
    qkv = _silu(acc)
    ones = ones_ref[...]

    def l2n(t):
        return t * lax.rsqrt(_mm_exact(t * t, ones, terms=NORM_TERMS) + 1e-6)

    q = l2n(qkv[:, 0:w]) * (HEAD ** -0.5)
    k = l2n(qkv[:, w:2 * w])
    v = qkv[:, 2 * w:3 * w]
    sh_ref[:, 0:w] = q.astype(BF16)
    sh_ref[:, w:2 * w] = v.astype(BF16)
    sh_ref[:, 2 * w:3 * w] = k.astype(BF16)
    sr = p[:, 4 * w:4 * w + 128].astype(F32)
    beta_all = _sigmoid(sr)
    g_all = -jnp.exp(alog_ref[...]) * _softplus(sr + dt_ref[...])
    for d, (dk_ref, dw_ref) in enumerate(((dk0_ref, dw0_ref), (dk1_ref, dw1_ref))):
        beta = _mm_exact(beta_all, eb_ref[d])
        g = _mm_exact(g_all, ea_ref[d])
        kb = k * beta
        dw_ref[...] = g
        dk_ref[:, 0:w] = kb.astype(BF16)
        dk_ref[:, w:2 * w] = (-jnp.exp(g) * kb).astype(BF16)
    fin_ref[...] = p[:, 3 * w:4 * w].astype(F32)


SEG_GLA, SEG_GDN, SEG_ATT, SEG_RWKV = range(4)
N_RWKV_CONSTS, N_GDN_CONSTS = 11, 6
RWKV_OUTS = [(3, BF16), (2, BF16), (2, BF16), (1, F32), (1, F32), (2, F32)]
GDN_OUTS = [(3, BF16), (2, BF16), (2, BF16), (1, F32), (1, F32), (1, F32)]


def _front_kernel(*refs, n_lat_blocks):
    it = iter(refs)
    x_ref, xp_ref, xn_ref, sc_ref, csc_ref, sh_ref, csh_ref, g_ref = (next(it) for _ in range(8))
    w_refs = [next(it) for _ in range(4)]
    cos_ref, sin_ref = next(it), next(it)
    rwkv_consts = [next(it) for _ in range(N_RWKV_CONSTS)]
    gdn_consts = [next(it) for _ in range(N_GDN_CONSTS)]
    pc_ref, pd_ref = next(it), next(it)
    rwkv_outs = [next(it) for _ in RWKV_OUTS]
    gdn_outs = [next(it) for _ in GDN_OUTS]

    first, lastb = _seq_edges(n_lat_blocks)
    is_ctx = pl.program_id(1) >= n_lat_blocks
    scale = jnp.where(is_ctx, csc_ref[...], sc_ref[...])
    shift = jnp.where(is_ctx, csh_ref[...], sh_ref[...])
    h = _prenorm(x_ref[...], g_ref[...], scale, shift).astype(BF16)
    halo = jnp.concatenate([xp_ref[...], xn_ref[...]], axis=0)
    h_halo = _prenorm(halo, g_ref[...], scale, shift).astype(BF16)

    def proj(rows, which):
        return jnp.dot(rows, w_refs[which][...], preferred_element_type=F32)

    pc_ref[...] = proj(h, SEG_GLA).astype(BF16)
    att = proj(h, SEG_ATT)
    w = BRANCH_W
    for c0 in range(0, att.shape[1], w):
        part = att[:, c0:c0 + w]
        if c0 < 2 * w:
            part = _rope(part, cos_ref[...], sin_ref[...])
        pd_ref[:, c0:c0 + w] = part.astype(BF16)

    h_ext = jnp.concatenate([h, h_halo], axis=0)
    n = h.shape[0]
    pb = proj(h_ext, SEG_GDN).astype(BF16)
    _gdn_prep(pb[0:n], pb[n:n + HALO], pb[n + HALO:], first, lastb, *gdn_consts, *gdn_outs)
    pa = proj(h_ext, SEG_RWKV).astype(BF16)
    _rwkv_prep(pa[0:n], pa[n:n + HALO], pa[n + HALO:], first, lastb, *rwkv_consts, *rwkv_outs)


def _front(xs, mod, gain, weights, cos, sin, rwkv_consts, gdn_consts, n_lat_rows):
    batch, rows, d = xs.shape
    assert len(rwkv_consts) == N_RWKV_CONSTS and len(gdn_consts) == N_GDN_CONSTS and len(weights) == 4
    tab = pl.BlockSpec((PREP_ROWS, BRANCH_W), lambda b, i: (i, 0))
    consts = list(rwkv_consts) + list(gdn_consts)
    outs = [(weights[s].shape[1], BF16) for s in (SEG_GLA, SEG_ATT)]
    outs += [(n * BRANCH_W, dt) for n, dt in RWKV_OUTS + GDN_OUTS]
    out_specs = [pl.BlockSpec((None, PREP_ROWS, lanes), lambda b, i: (b, i, 0)) for lanes, _ in outs]
    out_shape = [jax.ShapeDtypeStruct((batch, rows, lanes), dt) for lanes, dt in outs]
    res = pl.pallas_call(
        functools.partial(_front_kernel, n_lat_blocks=n_lat_rows // PREP_ROWS),
        grid=(batch, rows // PREP_ROWS),
        in_specs=_halo_specs(d, rows, 0) + _mod_specs(1, batch) + _mod_specs(0, batch)
        + [_vec_spec(gain.shape)] + [_vec_spec(w.shape) for w in weights] + [tab, tab]
        + [_vec_spec(c.shape) for c in consts],
        out_specs=out_specs,
        out_shape=out_shape,
        compiler_params=_cparams(("parallel", "parallel")),
        name="front_proj_prep",
    )(xs, xs, xs, mod, mod, mod, mod, gain, *weights, cos, sin, *consts)
    pc, pd = res[0], res[1]
    a_sh, a_dk0, a_dk1, a_dw0, a_dw1, a_fin = res[2:8]
    b_sh, b_dk0, b_dk1, b_dw0, b_dw1, b_fin = res[8:14]
    return pc, pd, (a_sh, (a_dk0, a_dk1), (a_dw0, a_dw1), a_fin), (b_sh, (b_dk0, b_dk1), (b_dw0, b_dw1), b_fin)


NEG_BIG = -1e30


def _window_bias():
    iq = np.arange(2 * ATT_BLOCK)[:, None] % ATT_BLOCK
    ik = np.arange(ATT_BLOCK)[None, :]
    ok = np.stack([ik >= iq, np.ones_like(ik >= iq), ik <= iq])
    assert WINDOW == ATT_BLOCK
    return np.where(ok, 0.0, NEG_BIG).astype(np.float32)


ATT_STEP = 2


def _attn_kernel(*refs, n_lat_rows):
    sink_ref, q_ref, kc_ref, vc_ref = refs[0:4]
    nband = ATT_STEP + 2
    k_all, v_all = refs[4:4 + nband], refs[4 + nband:4 + 2 * nband]
    bias_ref, o_ref = refs[4 + 2 * nband], refs[5 + 2 * nband]
    blk = ATT_BLOCK
    lane = lax.broadcasted_iota(jnp.int32, (blk, 2 * HEAD), 1)
    row2 = lax.broadcasted_iota(jnp.int32, (2 * blk, 1), 0)
    for u in range(ATT_STEP):
        t = pl.program_id(1) * ATT_STEP + u
        rows = slice(u * blk, (u + 1) * blk)
        q = q_ref[rows, :] * (HEAD ** -0.5)
        k_band, v_band = k_all[u:u + 3], v_all[u:u + 3]
        q_lat = t * blk < n_lat_rows
        in_seq = [q_lat & (t >= 1), q_lat, q_lat & ((t + 1) * blk < n_lat_rows)]
        bias = [jnp.where(in_seq[j], bias_ref[j], NEG_BIG) for j in range(3)]
        for g in range(D_KV_HEADS):
            cols = slice(g * 2 * HEAD, (g + 1) * 2 * HEAD)
            qg = q[:, cols]
            zero = jnp.zeros((), qg.dtype)
            qs = jnp.concatenate([jnp.where(lane < HEAD, qg, zero), jnp.where(lane >= HEAD, qg, zero)], axis=0)
            s_ctx = _mm(qs, kc_ref[:, cols], NT)
            s_band = [_mm(qs, k_band[j][:, cols], NT) + bias[j] for j in range(3)]
            sink = jnp.where(row2 < blk, sink_ref[2 * g], sink_ref[2 * g + 1])
            m = jnp.maximum(jnp.max(s_ctx, axis=-1, keepdims=True), sink)
            for s in s_band:
                m = jnp.maximum(m, jnp.max(s, axis=-1, keepdims=True))
            p_ctx = jnp.exp(s_ctx - m)
            den = jnp.sum(p_ctx, axis=-1, keepdims=True) + jnp.exp(sink - m)
            acc = _mm(p_ctx, vc_ref[:, cols], NN)
            for j in range(3):
                pj = jnp.exp(s_band[j] - m)
                den = den + jnp.sum(pj, axis=-1, keepdims=True)
                acc = acc + _mm(pj, v_band[j][:, cols], NN)
            og = acc / den
            o_ref[rows, cols] = jnp.where(lane < HEAD, og[0:blk], og[blk:2 * blk])


def _attention(p, sink, n_lat_rows, out_rows, col0=0):
    batch, rows, _ = p.shape
    blk = ATT_BLOCK
    nb = rows // blk
    w = BRANCH_W
    n_ctx_rows = rows - n_lat_rows
    step = ATT_STEP * blk
    assert n_lat_rows % n_ctx_rows == 0 and n_ctx_rows % step == 0 and out_rows % step == 0

    def band(col, off):
        def index(b, t):
            return (b, jnp.clip(t * ATT_STEP + off, 0, nb - 1), col0 + col)
        return pl.BlockSpec((None, blk, w), index)

    offsets = range(-1, ATT_STEP + 1)
    ctx = lambda col: pl.BlockSpec((None, n_ctx_rows, w), lambda b, t: (b, n_lat_rows // n_ctx_rows, col0 + col))
    kern = functools.partial(_attn_kernel, n_lat_rows=n_lat_rows)
    bias = jnp.asarray(_window_bias())
    n_in = 3 + 2 * len(offsets)
    return pl.pallas_call(
        kern,
        grid=(batch, out_rows // step),
        in_specs=[pl.BlockSpec(memory_space=pltpu.SMEM),
                  pl.BlockSpec((None, step, w), lambda b, t: (b, t, col0)), ctx(1), ctx(2)]
        + [band(1, o) for o in offsets] + [band(2, o) for o in offsets] + [_vec_spec(bias.shape)],
        out_specs=pl.BlockSpec((None, step, w), lambda b, t: (b, t, 0)),
        out_shape=jax.ShapeDtypeStruct((batch, out_rows, w), F32),
        compiler_params=_cparams(("parallel", "parallel")),
        name="window_attn",
    )(sink, *([p] * n_in), bias)


def _merge_kernel(x_ref, sc_ref, csc_ref, sh_ref, csh_ref, gm_ref, cgm_ref, g0_ref, g1_ref,
                  ya_ref, yb_ref, yc_ref, yd_ref, wg_ref, gb_ref, wb_ref, wo_ref, o_ref, *, n_lat_rows):
    x = x_ref[...]
    is_ctx = _ctx_rows(x.shape[0], n_lat_rows)
    scale = jnp.where(is_ctx, csc_ref[...], sc_ref[...])
    shift = jnp.where(is_ctx, csh_ref[...], sh_ref[...])
    gmod = jnp.where(is_ctx, cgm_ref[...], gm_ref[...])
    h = _prenorm(x, g0_ref[...], scale, shift).astype(BF16)
    acc = jnp.zeros(x.shape, F32)
    for i, y_ref in enumerate((ya_ref, yb_ref, yc_ref, yd_ref)):
        pre = jnp.dot(h, wg_ref[:, i * D_MODEL:(i + 1) * D_MODEL], preferred_element_type=F32)
        gate = _sigmoid(pre + gb_ref[i:i + 1, :])
        acc = acc + gate * jnp.dot(y_ref[...].astype(BF16), wb_ref[i], preferred_element_type=F32)
    out = jnp.dot(acc.astype(BF16), wo_ref[...], preferred_element_type=F32)
    ms = jnp.mean(out * out, axis=-1, keepdims=True)
    o_ref[...] = x + gmod * (out * lax.rsqrt(ms + NORM_EPS) * g1_ref[...])


def _merge(xs, mod, gain0, gain1, ys, wg, gate_b, wb, wo, n_lat_rows, out_rows):
    batch, _, d = xs.shape
    rows = out_rows
    tm = _row_tile(rows, 272)
    tile = lambda width: pl.BlockSpec((None, tm, width), lambda b, i: (b, i, 0))
    consts = [wg, gate_b, wb, wo]
    return pl.pallas_call(
        functools.partial(_merge_kernel, n_lat_rows=n_lat_rows),
        grid=(batch, rows // tm),
        in_specs=[tile(d)] + _mod_specs(1, batch) + _mod_specs(0, batch) + _mod_specs(2, batch)
        + [_vec_spec(gain0.shape), _vec_spec(gain1.shape)]
        + [tile(BRANCH_W)] * 4 + [_vec_spec(c.shape) for c in consts],
        out_specs=tile(d),
        out_shape=jax.ShapeDtypeStruct((batch, rows, d), F32),
        compiler_params=_cparams(("parallel", "parallel")),
        name="merge_out",
    )(xs, *([mod] * 6), gain0, gain1, *ys, *consts)


FFN_CHUNK = 512


def _ffn_kernel(x_ref, sc_ref, csc_ref, sh_ref, csh_ref, gm_ref, cgm_ref, g2_ref, g3_ref, w1_ref, w2_ref,
                o_ref, *, n_lat_rows):
    x = x_ref[...]
    is_ctx = _ctx_rows(x.shape[0], n_lat_rows)
    scale = jnp.where(is_ctx, csc_ref[...], sc_ref[...])
    shift = jnp.where(is_ctx, csh_ref[...], sh_ref[...])
    h = _prenorm(x, g2_ref[...], scale, shift).astype(BF16)
    hidden = w2_ref.shape[0]
    out = jnp.zeros(x.shape, F32)
    for lo in range(0, hidden, FFN_CHUNK):
        hi = min(lo + FFN_CHUNK, hidden)
        gt = jnp.dot(h, w1_ref[:, lo:hi], preferred_element_type=F32)
        up = jnp.dot(h, w1_ref[:, hidden + lo:hidden + hi], preferred_element_type=F32)
        out = out + jnp.dot((_silu(gt) * up).astype(BF16), w2_ref[lo:hi, :], preferred_element_type=F32)
    ms = jnp.mean(out * out, axis=-1, keepdims=True)
    gmod = jnp.where(is_ctx, cgm_ref[...], gm_ref[...])
    o_ref[...] = x + gmod * (out * lax.rsqrt(ms + NORM_EPS) * g3_ref[...])


def _ffn(xs, mod, gain2, gain3, w1, w2, n_lat_rows):
    batch, rows, d = xs.shape
    tm = _row_tile(rows, 544)
    tile = pl.BlockSpec((None, tm, d), lambda b, i: (b, i, 0))
    consts = [gain2, gain3, w1, w2]
    return pl.pallas_call(
        functools.partial(_ffn_kernel, n_lat_rows=n_lat_rows),
        grid=(batch, rows // tm),
        in_specs=[tile] + _mod_specs(4, batch) + _mod_specs(3, batch) + _mod_specs(5, batch)
        + [_vec_spec(c.shape) for c in consts],
        out_specs=tile,
        out_shape=jax.ShapeDtypeStruct((batch, rows, d), F32),
        compiler_params=_cparams(("parallel", "parallel")),
        name="swiglu",
    )(xs, *([mod] * 6), *consts)


def _pad_cols(w, width):
    return jnp.pad(w, ((0, 0), (0, width - w.shape[1])))


def _layer_weights(w_in, mu, w2, a2, g2, conv, a_log, dt_bias, gw2, gb):
    wts = {}
    off_b, off_c, off_d, off_g = A_IN, A_IN + B_IN, A_IN + B_IN + C_IN, A_IN + B_IN + C_IN + D_IN
    w = BRANCH_W
    wts["wa"] = _pad_cols(w_in[:, 0:A_IN], 4 * w).astype(BF16)
    wts["mu"] = _pad_cols(mu[None, :], 4 * w)
    lo = np.cumsum((0,) + A_LORA)
    place = lambda m, r0: jnp.zeros((w, w), F32).at[r0:r0 + m.shape[0], :].set(m)
    wts["w2p"] = jnp.stack([place(w2[0], lo[0]), place(w2[1], lo[1])])
    wts["a2p"] = jnp.stack([place(a2[0], lo[2]), place(a2[1], lo[3])])
    wts["g2p"] = place(g2, lo[4])

    wb = w_in[:, off_b:off_b + B_IN]
    wts["wb"] = jnp.concatenate([wb[:, 0:3 * w], wb[:, 3 * w + 16:], _pad_cols(wb[:, 3 * w:3 * w + 16], 128)],
                                axis=1).astype(BF16)
    nh = N_HEADS
    expand = np.zeros((4, 128, w), np.float32)
    for grp in range(4):
        for h in range(nh):
            expand[grp, grp * nh + h, h * HEAD:(h + 1) * HEAD] = 1.0
    wts["eb"] = jnp.asarray(expand[0:2])
    wts["ea"] = jnp.asarray(expand[2:4])
    vec = lambda t: jnp.zeros((1, 128), F32).at[0, 2 * nh:4 * nh].set(t.reshape(-1))
    wts["alog"] = vec(a_log)
    wts["dt"] = vec(dt_bias)
    wts["conv"] = conv

    wc = w_in[:, off_c:off_c + C_IN]
    pad_heads = lambda m: jnp.pad(m.reshape(m.shape[0], nh, C_DK),
                                  ((0, 0), (0, 0), (0, HEAD - C_DK))).reshape(m.shape[0], w)
    qc, kc, vc = wc[:, 0:C_QK], wc[:, C_QK:2 * C_QK], wc[:, 2 * C_QK:2 * C_QK + w]
    loc = wc[:, 2 * C_QK + w:2 * C_QK + w + 2 * C_GATE_R]
    gc = wc[:, 2 * C_QK + w + 2 * C_GATE_R:]
    wts["wc"] = jnp.concatenate([pad_heads(qc), pad_heads(kc), vc, gc, _pad_cols(loc, 128)], axis=1).astype(BF16)
    gwp = jnp.zeros((2, 128, w), F32)
    for d in range(2):
        gwp = gwp.at[d, d * C_GATE_R:(d + 1) * C_GATE_R, :].set(pad_heads(gw2[d]))
    wts["gwp"] = gwp
    wts["gbp"] = pad_heads(gb)
    wts["glane"] = jnp.asarray((np.arange(w) % HEAD < C_DK).astype(np.float32))[None, :]

    wd = w_in[:, off_d:off_d + D_IN]
    qd = wd[:, 0:w]
    dup = lambda m: jnp.concatenate([m[:, 0:HEAD], m[:, 0:HEAD], m[:, HEAD:], m[:, HEAD:]], axis=1)
    wts["wd"] = jnp.concatenate([qd, dup(wd[:, w:w + 2 * HEAD]), dup(wd[:, w + 2 * HEAD:])], axis=1).astype(BF16)
    wts["wg"] = w_in[:, off_g:].astype(BF16)
    return wts


def _rope_tables(n_lat_rows, n_ctx_rows):
    quarter = HEAD // 4
    inv = ROPE_BASE ** (-np.arange(quarter, dtype=np.float32) / quarter)
    pos = np.arange(n_lat_rows)
    rows = (pos // GRID_W).astype(np.float32)
    cols = (pos % GRID_W).astype(np.float32)
    inv = jnp.asarray(inv)
    ang_r = jnp.asarray(rows)[:, None] * inv[None, :]
    ang_c = jnp.asarray(cols)[:, None] * inv[None, :]
    cos = jnp.concatenate([jnp.cos(ang_r)] * 2 + [jnp.cos(ang_c)] * 2, axis=1)
    sin = jnp.concatenate([-jnp.sin(ang_r), jnp.sin(ang_r), -jnp.sin(ang_c), jnp.sin(ang_c)], axis=1)
    cos = jnp.concatenate([cos, jnp.ones((n_ctx_rows, HEAD), F32)], axis=0)
    sin = jnp.concatenate([sin, jnp.zeros((n_ctx_rows, HEAD), F32)], axis=0)
    return jnp.tile(cos, (1, N_HEADS)), jnp.tile(sin, (1, N_HEADS))


RWKV_SLOTS = {"r": ("sh", 0), "v": ("sh", 1), "a": ("sh", 2), "w": ("dw", 0), "k": ("dk", 0), "b": ("dk", 1)}
GDN_SLOTS = RWKV_SLOTS
GLA_SLOTS = {"r": ("sh", 0), "k": ("sh", 1), "v": ("sh", 2)}


def kernel(x, c, ctx, c_ctx, ada_w, ada_b, norm_g, w_in, gate_b, w_branch, w_out, rwkv_mu, rwkv_w0, rwkv_w2, rwkv_a0, rwkv_a2, rwkv_g2, rwkv_kk, rwkv_ka, rwkv_rk, rwkv_ln_g, rwkv_ln_b, gdn_conv, gdn_a_log, gdn_dt_bias, gdn_norm_g, gla_gw2, gla_gb, gla_norm_g, attn_sink, ffn_w1, ffn_w2):
    batch, n_lat, d = x.shape
    n_ctx = ctx.shape[1]
    depth = ada_w.shape[0]
    assert n_ctx % PREP_ROWS == 0 and n_lat % PREP_ROWS == 0 and d == D_MODEL

    mod_rows = 8 * ((batch + 1 + 7) // 8)
    c_rows = jnp.concatenate([c, c_ctx[None, :], jnp.zeros((mod_rows - batch - 1, d), F32)], axis=0)
    mod_all = _modulation(c_rows, ada_w, ada_b)
    cos, sin = _rope_tables(n_lat, n_ctx)

    xs = jnp.concatenate([x, ctx], axis=1)
    rows = n_lat + n_ctx
    row = lambda t: t.reshape(1, -1)
    for l in range(depth):
        out_rows = n_lat if l == depth - 1 else rows
        mod = mod_all[l].reshape(mod_rows, 1, 6 * d)
        ng = norm_g[l]
        wts = _layer_weights(w_in[l], rwkv_mu[l], rwkv_w2[l], rwkv_a2[l], rwkv_g2[l], gdn_conv[l],
                             gdn_a_log[l], gdn_dt_bias[l], gla_gw2[l], gla_gb[l])
        ones = jnp.asarray(_head_block_ones())
        around = jnp.asarray(0.5 * (_shift_matrix(-1) + _shift_matrix(1)), BF16)
        rwkv_consts = [around, wts["mu"], wts["w2p"], wts["a2p"], wts["g2p"], rwkv_w0[l], rwkv_a0[l],
                       row(rwkv_kk[l]), row(rwkv_ka[l]), row(rwkv_rk[l]), ones]
        gdn_consts = [wts["conv"], wts["alog"], wts["dt"], wts["eb"], wts["ea"], ones]
        p_gla, p_att, rwkv_ops, gdn_ops = _front(xs, mod, row(ng[0]), (wts["wc"], wts["wb"], wts["wd"], wts["wa"]),
                                                 cos, sin, rwkv_consts, gdn_consts, n_lat)

        sh, dks, dws, fin = rwkv_ops
        ya = _bidir_scan(sh, dks, dws, n_lat, fin, row(rwkv_ln_g[l]), row(rwkv_ln_b[l]), finish="groupnorm",
                         scalar_decay=False, lowrank=True, slots=RWKV_SLOTS)

        sh, dks, dws, fin = gdn_ops
        gnorm = row(jnp.tile(gdn_norm_g[l], N_HEADS))
        yb = _bidir_scan(sh, dks, dws, n_lat, fin, gnorm, gnorm, finish="rms",
                         scalar_decay=True, lowrank=True, slots=GDN_SLOTS)

        yc = _gla_scan(p_gla, n_lat, wts["gwp"], wts["gbp"], wts["glane"], row(jnp.tile(gla_norm_g[l], N_HEADS)))

        yd = _attention(p_att, attn_sink[l], n_lat, out_rows)

        xs = _merge(xs, mod, row(ng[0]), row(ng[1]), (ya, yb, yc, yd), wts["wg"], gate_b[l],
                    w_branch[l].astype(BF16), w_out[l].astype(BF16), n_lat, out_rows)
        xs = _ffn(xs, mod, row(ng[2]), row(ng[3]), ffn_w1[l].astype(BF16), ffn_w2[l].astype(BF16), n_lat)
    return xs
```

```python
import functools
import math

import numpy as np
import jax
import jax.numpy as jnp
from jax import lax
from jax.experimental import pallas as pl
from jax.experimental.pallas import tpu as pltpu

F32 = jnp.float32
BF16 = jnp.bfloat16

D_MODEL = 1024
N_BRANCH = 4
BRANCH_W = 256
HEAD = 64
N_HEADS = BRANCH_W // HEAD
NORM_EPS = 1e-6
A_GN_EPS = 64e-5
A_LORA = (32, 32, 32, 32, 64)
A_IN = 3 * BRANCH_W + sum(A_LORA)
B_IN = 4 * BRANCH_W + 4 * N_HEADS
C_DK = 32
C_QK = N_HEADS * C_DK
C_GATE_R = 16
C_GATE_NORM = 16.0
C_IN = 2 * C_QK + 2 * BRANCH_W + 2 * C_GATE_R
D_KV_HEADS = 2
D_IN = BRANCH_W + 2 * D_KV_HEADS * HEAD
B_CONV = 7
WINDOW = 128
ROPE_BASE = 10000.0
GRID_W = 64
FFN_HIDDEN = 2816

CHUNK = 64
PREP_ROWS = 256
HALO = 16
ATT_BLOCK = 128
VMEM_LIMIT = 48 * 1024 * 1024

NN = (((1,), (0,)), ((), ()))
NT = (((1,), (1,)), ((), ()))
TN = (((0,), (0,)), ((), ()))


def _mm(a, b, dims=NN, mode="bf16"):
    if mode == "f32":
        return lax.dot_general(a, b, dims, precision=lax.Precision.HIGHEST, preferred_element_type=F32)
    if mode == "x3":
        ah = a.astype(BF16)
        al = (a - ah.astype(F32)).astype(BF16)
        bh = b.astype(BF16)
        bl = (b - bh.astype(F32)).astype(BF16)
        dot = functools.partial(lax.dot_general, dimension_numbers=dims, preferred_element_type=F32)
        return dot(ah, bh) + (dot(ah, bl) + dot(al, bh))
    return lax.dot_general(a.astype(BF16), b.astype(BF16), dims, preferred_element_type=F32)


def _sigmoid(x):
    return 1.0 / (1.0 + jnp.exp(-x))


def _silu(x):
    return x * _sigmoid(x)


def _softplus(x):
    return jnp.maximum(x, 0.0) + jnp.log1p(jnp.exp(-jnp.abs(x)))


def _cparams(sem):
    return pltpu.CompilerParams(dimension_semantics=sem, vmem_limit_bytes=VMEM_LIMIT)


def _mod_kernel(c_ref, w_ref, b_ref, o_ref):
    c = c_ref[...]
    o_ref[...] = _mm(_silu(c), w_ref[...], mode="f32") + b_ref[...]


def _modulation(c_rows, ada_w, ada_b):
    depth, d, n = ada_w.shape
    rows = c_rows.shape[0]
    tn = 1024
    return pl.pallas_call(
        _mod_kernel,
        grid=(depth, n // tn),
        in_specs=[
            pl.BlockSpec((rows, d), lambda l, j: (0, 0)),
            pl.BlockSpec((None, d, tn), lambda l, j: (l, 0, j)),
            pl.BlockSpec((None, 1, tn), lambda l, j: (l, 0, j)),
        ],
        out_specs=pl.BlockSpec((None, rows, tn), lambda l, j: (l, 0, j)),
        out_shape=jax.ShapeDtypeStruct((depth, rows, n), F32),
        compiler_params=_cparams(("parallel", "parallel")),
        name="adaln_mod",
    )(c_rows, ada_w, ada_b.reshape(depth, 1, n))


def _row_tile(rows, target):
    return max(t for t in range(8, target + 1, 8) if rows % t == 0)


def _mod_specs(which, batch):
    lat = pl.BlockSpec((None, 1, D_MODEL), lambda b, i, *_: (b, 0, which))
    ctx = pl.BlockSpec((None, 1, D_MODEL), lambda b, i, *_: (batch, 0, which))
    return [lat, ctx]


def _ctx_rows(tm, n_lat_rows):
    row = pl.program_id(1) * tm + lax.broadcasted_iota(jnp.int32, (tm, 1), 0)
    return row >= n_lat_rows


def _prenorm(x, gain, scale, shift):
    ms = jnp.mean(x * x, axis=-1, keepdims=True)
    y = x * lax.rsqrt(ms + NORM_EPS) * gain
    return y * (1.0 + scale) + shift


def _rope(x, cos, sin):
    width = x.shape[-1]
    lane = lax.broadcasted_iota(jnp.int32, x.shape, 1)
    swapped = jnp.where(lane % 32 < 16, pltpu.roll(x, width - 16, axis=1), pltpu.roll(x, 16, axis=1))
    return x * cos + swapped * sin


def _head_block_ones():
    idx = np.arange(BRANCH_W)
    return (idx[:, None] // HEAD == idx[None, :] // HEAD).astype(np.float32)


N_LEVELS = 6


def _level_mask(ri, ci, s, reverse):
    b = 1 << s
    blk = (ri // (2 * b)) == (ci // (2 * b))
    hi_r, hi_c = (ri // b) % 2 == 1, (ci // b) % 2 == 1
    return blk & ((hi_c & ~hi_r) if reverse else (hi_r & ~hi_c))


SCAN_HEADS = 2
GROUP_W = SCAN_HEADS * HEAD
N_GROUPS = N_HEADS // SCAN_HEADS


def _scan_masks(reverse):
    n = SCAN_HEADS * CHUNK
    ri = np.arange(CHUNK)[:, None]
    ci = np.arange(n)[None, :] % CHUNK
    strict, incl = (ci > ri, ci >= ri) if reverse else (ci < ri, ci <= ri)
    compact = [strict, incl, ci == ri] + [_level_mask(ri, ci, s, reverse) for s in range(N_LEVELS)]
    r = np.arange(n)[:, None]
    c = np.arange(n)[None, :]
    block = [(r // CHUNK) == (c // CHUNK), r == c]
    return np.stack(compact).astype(np.float32), np.stack(block).astype(np.float32)


EXACT_TERMS = 3
NORM_TERMS = 2


def _tri(reverse):
    i = np.arange(CHUNK)
    m = (i[None, :] >= i[:, None]) if reverse else (i[None, :] <= i[:, None])
    return np.tile(m.astype(np.float32), (1, EXACT_TERMS))


def _split_terms(x, terms):
    out = []
    for _ in range(terms - 1):
        part = x.astype(BF16)
        out.append(part)
        x = x - part.astype(F32)
    out.append(x.astype(BF16))
    return out


def _mm_exact(a, b, dims=NN, split="a", terms=EXACT_TERMS):
    assert dims == NN
    if split == "a":
        lhs = jnp.concatenate(_split_terms(a, terms), axis=1)
        rhs = jnp.concatenate([b.astype(BF16)] * terms, axis=0)
    else:
        lhs = a.astype(BF16)
        rhs = jnp.concatenate(_split_terms(b, terms), axis=0)
    return lax.dot_general(lhs, rhs, dims, preferred_element_type=F32)


C_STRICT, C_INCL, C_EYE, C_LVL0 = 0, 1, 2, 3
B_SAME, B_EYE = 0, 1
SCAN_BATCH = 8
SCAN_ROWS = 256
assert CHUNK == HEAD and (1 << N_LEVELS) == CHUNK


def _scan_chunk(vals, st, tri, ones, cm_ref, bm_ref, same_bf, *, reverse, scalar_decay, lowrank, mm_mode):
    same = bm_ref[B_SAME]

    def get(name):
        return vals[name]

    def expand(x):
        return jnp.concatenate([x.astype(BF16)] * SCAN_HEADS, axis=0) * same_bf

    def keep(x, k):
        return jnp.where(cm_ref[k] > 0.5, x, 0.0)

    logw = get("w")
    r, k, v = get("r"), get("k"), get("v")
    cum = _mm_exact(tri, logw, split="b")
    cum_x = cum - logw
    last = 0 if reverse else CHUNK - 1
    total = cum[last:last + 1, :]
    to_end = jnp.exp(total - cum)

    if scalar_decay:
        diag = jnp.concatenate([cum] * SCAN_HEADS, axis=0) * bm_ref[B_EYE]
        cum_row = _mm_exact(ones, diag, split="b")
        d_ii = jnp.exp(jnp.where(cm_ref[C_INCL] > 0.5, cum - cum_row, -1e30))
        r_q, k_q = r, k
    else:
        ref_row = cum[CHUNK // 2:CHUNK // 2 + 1, :]
        p_inv = jnp.exp(ref_row - cum)
        r_q, k_q = r * jnp.exp(cum - ref_row), k * p_inv

    k_e, v_e = expand(k_q), expand(v)
    r_abs = r * jnp.exp(cum)
    if not lowrank:
        s_k = _mm(r_q, k_e, NT, mm_mode)
        a_rk = s_k * d_ii if scalar_decay else keep(s_k, C_INCL)
        y = _mm(r_abs, st, NT, mm_mode) + _mm(a_rk, v_e, NN, mm_mode)
        upd = _mm(v, k * to_end, TN, mm_mode)
        return y, st * jnp.exp(total) + upd * same

    a, b = get("a"), get("b")
    if scalar_decay:
        d_xi = jnp.exp(jnp.where(cm_ref[C_STRICT] > 0.5, cum_x - cum_row, -1e30))
        a_q, b_q = a, b
    else:
        a_q, b_q = a * jnp.exp(cum_x - ref_row), b * p_inv
    lhs = jnp.concatenate([a_q, r_q], axis=0)
    scores = _mm(lhs, jnp.concatenate([expand(b_q), k_e], axis=0), NT, mm_mode)
    s_b, s_k = scores[:, 0:GROUP_W], scores[:, GROUP_W:]
    if scalar_decay:
        a_ab, a_rb = s_b[0:CHUNK] * d_xi, s_b[CHUNK:] * d_ii
        a_ak, a_rk = s_k[0:CHUNK] * d_xi, s_k[CHUNK:] * d_ii
    else:
        a_ab, a_rb = keep(s_b[0:CHUNK], C_STRICT), keep(s_b[CHUNK:], C_INCL)
        a_ak, a_rk = keep(s_k[0:CHUNK], C_STRICT), keep(s_k[CHUNK:], C_INCL)

    inv = cm_ref[C_EYE] + a_ab * cm_ref[C_LVL0]
    for s in range(1, N_LEVELS):
        c_s = expand(a_ab * cm_ref[C_LVL0 + s])
        inv = inv + _mm(_mm(inv, c_s, NN, mm_mode), expand(inv), NN, mm_mode)

    a_abs = a * jnp.exp(cum_x)
    from_state = _mm(jnp.concatenate([a_abs, r_abs], axis=0), st, NT, mm_mode)
    from_v = _mm(jnp.concatenate([a_ak, a_rk], axis=0), v_e, NN, mm_mode)
    both = from_state + from_v
    z = _mm(inv, expand(both[0:CHUNK]), NN, mm_mode)
    y = both[CHUNK:] + _mm(a_rb, expand(z), NN, mm_mode)
    upd = _mm(jnp.concatenate([v, z], axis=0), jnp.concatenate([k * to_end, b * to_end], axis=0), TN, mm_mode)
    return y, st * jnp.exp(total) + upd * same


def _scan_kernel(*refs, reverse, scalar_decay, lowrank, slots, sources, finish, mm_mode, lora, r_scale):
    it = iter(refs)
    src_refs = {name: next(it) for name in sources}
    sh_ref = src_refs[sources[0]]
    cm_ref, bm_ref, tri_ref = next(it), next(it), next(it)
    if lora:
        lo_ref, gw_ref, gb_ref, lane_ref = next(it), next(it), next(it), next(it)
    if finish:
        ob_ref, fin_ref, p1_ref, p2_ref, avg_ref = next(it), next(it), next(it), next(it), next(it)
    o_ref, st_ref = next(it), next(it)

    @pl.when(pl.program_id(1) == 0)
    def _():
        st_ref[...] = jnp.zeros_like(st_ref)

    nb, block_rows = sh_ref.shape[0], sh_ref.shape[1]
    n_chunks = block_rows // CHUNK
    n = nb * N_GROUPS
    tri = jnp.broadcast_to(tri_ref[...], (n,) + tri_ref.shape)
    ones = jnp.ones((n, CHUNK, EXACT_TERMS * SCAN_HEADS * CHUNK), F32)
    chunk = functools.partial(_scan_chunk, cm_ref=cm_ref, bm_ref=bm_ref, same_bf=bm_ref[B_SAME].astype(BF16),
                              reverse=reverse, scalar_decay=scalar_decay, lowrank=lowrank, mm_mode=mm_mode)

    def step(c, carry):
        rows = pl.ds(pl.multiple_of((n_chunks - 1 - c if reverse else c) * CHUNK, CHUNK), CHUNK)

        def groups(name):
            src, idx = slots[name]
            ref = src_refs[src]
            parts = [ref[:, rows, idx * BRANCH_W + g * GROUP_W:idx * BRANCH_W + (g + 1) * GROUP_W].astype(F32)
                     for g in range(N_GROUPS)]
            return jnp.stack(parts, axis=1).reshape(n, CHUNK, GROUP_W)

        vals = {name: groups(name) for name in slots}
        if lora:
            lo = lo_ref[:, rows, :].reshape(nb * CHUNK, lo_ref.shape[-1])
            z = _mm(lo, gw_ref[...], mode=mm_mode) + gb_ref[...]
            logw = ((-_softplus(-z) / C_GATE_NORM) * lane_ref[...]).reshape(nb, CHUNK, BRANCH_W)
            parts = [logw[:, :, g * GROUP_W:(g + 1) * GROUP_W] for g in range(N_GROUPS)]
            vals["w"] = jnp.stack(parts, axis=1).reshape(n, CHUNK, GROUP_W)
        if r_scale != 1.0:
            vals["r"] = vals["r"] * r_scale
        y, st_new = jax.vmap(chunk)(vals, st_ref[...], tri, ones)
        st_ref[...] = st_new
        y = y.reshape(nb, N_GROUPS, CHUNK, GROUP_W)
        y = jnp.concatenate([y[:, g] for g in range(N_GROUPS)], axis=-1)
        if not finish:
            o_ref[:, rows, :] = y
            return carry

        y = (y + ob_ref[:, rows, :]).reshape(nb * CHUNK, BRANCH_W)
        avg = avg_ref[...]
        if finish == "groupnorm":
            gate = fin_ref[:, rows, 0:BRANCH_W].reshape(y.shape)
            bonus = fin_ref[:, rows, BRANCH_W:2 * BRANCH_W].reshape(y.shape)
            cen = y - _mm_exact(y, avg)
            var = _mm_exact(cen * cen, avg)
            yn = cen * lax.rsqrt(var + A_GN_EPS) * p1_ref[...] + p2_ref[...]
            out = (yn + bonus) * gate
        else:
            ms = _mm_exact(y * y, avg)
            gate = fin_ref[:, rows, :].astype(F32).reshape(y.shape)
            out = y * lax.rsqrt(ms + NORM_EPS) * p1_ref[...] * _silu(gate)
        o_ref[:, rows, :] = out.reshape(nb, CHUNK, BRANCH_W)
        return carry

    lax.fori_loop(0, n_chunks, step, 0)


def _view(x):
    return x if isinstance(x, tuple) else (x, x.shape[-1], 0)


def _scan(srcs, n_lat_rows, *, reverse, scalar_decay, lowrank, slots, finish=None, fin_args=None,
          mm_mode="bf16", lora=None, r_scale=1.0):
    sources = tuple(srcs)
    views = [_view(srcs[s]) for s in sources]
    batch, rows, _ = views[0][0].shape
    nc, nlat = rows // SCAN_ROWS, n_lat_rows // SCAN_ROWS
    nctx = nc - nlat
    nb = math.gcd(batch, SCAN_BATCH)

    if reverse:
        def chunk(n):
            return nc - 1 - n
    else:
        def chunk(n):
            return jnp.where(n < nctx, nlat + n, n - nctx)

    def row_spec(width, col=0):
        return pl.BlockSpec((nb, SCAN_ROWS, width), lambda b, n: (b, chunk(n), col))

    def const_spec(shape):
        zeros = (0,) * len(shape)
        return pl.BlockSpec(shape, lambda b, n: zeros)

    cmask, bmask = (jnp.asarray(m) for m in _scan_masks(reverse))
    tri = jnp.asarray(_tri(reverse))
    in_specs = [row_spec(w, c) for _, w, c in views]
    in_specs += [const_spec(cmask.shape), const_spec(bmask.shape), const_spec(tri.shape)]
    args = [a for a, _, _ in views] + [cmask, bmask, tri]
    if lora:
        (lo, lo_w, lo_c), gw, gb, lane = _view(lora[0]), lora[1], lora[2], lora[3]
        in_specs += [row_spec(lo_w, lo_c), const_spec(gw.shape), const_spec(gb.shape), const_spec(lane.shape)]
        args += [lo, gw, gb, lane]
    if finish:
        ob, fin, p1, p2 = fin_args
        fin, fin_w, fin_c = _view(fin)
        avg = jnp.asarray(_head_block_ones() / HEAD)
        in_specs += [row_spec(BRANCH_W), row_spec(fin_w, fin_c), const_spec(p1.shape), const_spec(p2.shape),
                     const_spec(avg.shape)]
        args += [ob, fin, p1, p2, avg]
    kern = functools.partial(_scan_kernel, reverse=reverse, scalar_decay=scalar_decay, lowrank=lowrank,
                             slots=slots, sources=sources, finish=finish, mm_mode=mm_mode, lora=bool(lora),
                             r_scale=r_scale)
    return pl.pallas_call(
        kern,
        grid=(batch // nb, nc),
        in_specs=in_specs,
        out_specs=row_spec(BRANCH_W),
        out_shape=jax.ShapeDtypeStruct((batch, rows, BRANCH_W), F32),
        scratch_shapes=[pltpu.VMEM((nb * N_GROUPS, GROUP_W, GROUP_W), F32)],
        compiler_params=_cparams(("parallel", "arbitrary")),
        name="dplr_scan_" + ("bwd" if reverse else "fwd"),
    )(*args)


def _gla_scan(p, n_lat_rows, gwp, gbp, lane, norm_g):
    w = BRANCH_W
    srcs = {"sh": (p, 3 * w, 0)}
    lo = (p, gwp.shape[1], 4 * w // gwp.shape[1])
    kw = dict(scalar_decay=False, lowrank=False, slots=GLA_SLOTS, r_scale=C_DK ** -0.5)
    ob = _scan(srcs, n_lat_rows, reverse=True, lora=(lo, gwp[1], gbp[1:2], lane), **kw)
    return _scan(srcs, n_lat_rows, reverse=False, lora=(lo, gwp[0], gbp[0:1], lane), finish="rms",
                 fin_args=(ob, (p, w, 3), norm_g, norm_g), **kw)


def _bidir_scan(sh, dks, dws, n_lat_rows, fin, p1, p2, *, finish, **kw):
    def srcs(d):
        out = {"sh": sh, "dw": dws[d]}
        if dks is not None:
            out["dk"] = dks[d]
        return out

    ob = _scan(srcs(1), n_lat_rows, reverse=True, **kw)
    return _scan(srcs(0), n_lat_rows, reverse=False, finish=finish, fin_args=(ob, fin, p1, p2), **kw)


def _halo_specs(width, rows, col):
    per = PREP_ROWS // HALO
    last = rows // HALO - 1
    own = pl.BlockSpec((None, PREP_ROWS, width), lambda b, i: (b, i, col))
    prev = pl.BlockSpec((None, HALO, width), lambda b, i: (b, jnp.maximum(i * per - 1, 0), col))
    nxt = pl.BlockSpec((None, HALO, width), lambda b, i: (b, jnp.minimum((i + 1) * per, last), col))
    return [own, prev, nxt]


def _seq_edges(n_lat_blocks):
    i = pl.program_id(1)
    first = (i == 0) | (i == n_lat_blocks)
    lastb = (i == n_lat_blocks - 1) | (i == pl.num_programs(1) - 1)
    return first, lastb


def _vec_spec(shape):
    zeros = (0,) * len(shape)
    return pl.BlockSpec(shape, lambda b, i: zeros)


def _shift_matrix(offset):
    m = np.zeros((PREP_ROWS, PREP_ROWS + 2 * HALO), np.float32)
    t = np.arange(PREP_ROWS)
    src = t + offset
    col = np.where(src < 0, PREP_ROWS + HALO + src, np.where(src >= PREP_ROWS, HALO + src, src))
    m[t, col] = 1.0
    return m


def _rwkv_prep(x, prev, nxt, first, lastb, shift_ref, mu_ref, w2_ref, a2_ref, g2_ref, w0_ref, a0_ref, kk_ref,
               ka_ref, rk_ref, ones_ref, sh_ref, dk0_ref, dk1_ref, dw0_ref, dw1_ref, fin_ref, mode="bf16"):
    zero = jnp.zeros((), x.dtype)
    xe = jnp.concatenate([x, jnp.where(first, zero, prev), jnp.where(lastb, zero, nxt)], axis=0)
    around = jnp.dot(shift_ref[...], xe, preferred_element_type=F32)
    x = x.astype(F32)
    xm = x + (around - x) * mu_ref[...]
    w = BRANCH_W
    r, k, v, lo = xm[:, 0:w], xm[:, w:2 * w], xm[:, 2 * w:3 * w], xm[:, 3 * w:4 * w]
    ones = ones_ref[...]
    th, sg = jnp.tanh(lo), _sigmoid(lo)
    gate = _mm(sg, g2_ref[...], mode=mode)
    kx = k * kk_ref[...]
    kk = kx * lax.rsqrt(_mm_exact(kx * kx, ones, terms=NORM_TERMS) + 1e-6)
    sh_ref[:, 0:w] = r.astype(BF16)
    sh_ref[:, w:2 * w] = v.astype(BF16)
    sh_ref[:, 2 * w:3 * w] = (-kk).astype(BF16)
    bonus = jnp.zeros_like(v)
    for d, (dk_ref, dw_ref) in enumerate(((dk0_ref, dw0_ref), (dk1_ref, dw1_ref))):
        w_raw = w0_ref[d:d + 1, :] + _mm(th, w2_ref[d], mode=mode)
        dw_ref[...] = -math.exp(-0.5) * _sigmoid(w_raw)
        a = _sigmoid(a0_ref[d:d + 1, :] + _mm(lo, a2_ref[d], mode=mode))
        kd = k * (1.0 + (a - 1.0) * ka_ref[...])
        dk_ref[:, 0:w] = kd.astype(BF16)
        dk_ref[:, w:2 * w] = (kk * a).astype(BF16)
        bonus = bonus + _mm_exact(r * kd * rk_ref[...], ones) * v
    fin_ref[:, 0:w] = gate
    fin_ref[:, w:2 * w] = bonus


def _gdn_prep(p, prev, nxt, first, lastb, conv_ref, alog_ref, dt_ref, eb_ref, ea_ref, ones_ref,
              sh_ref, dk0_ref, dk1_ref, dw0_ref, dw1_ref, fin_ref):
    w = BRANCH_W
    x = p[:, 0:3 * w].astype(F32)
    top = jnp.where(first, 0.0, prev[:, 0:3 * w].astype(F32))
    bot = jnp.where(lastb, 0.0, nxt[:, 0:3 * w].astype(F32))
    xe = jnp.concatenate([top, x, bot], axis=0)
    ext = PREP_ROWS + 2 * HALO
    acc = jnp.zeros_like(x)
    for s in range(B_CONV):
        shift = (B_CONV // 2 - s) % ext
        rolled = xe if shift == 0 else pltpu.roll(xe, shift, axis=0)
        acc = acc + rolled[HALO:HALO + PREP_ROWS] * conv_ref[s:s + 1, :]
    qkv = _silu(acc)
    ones = ones_ref[...]

    def l2n(t):
        return t * lax.rsqrt(_mm_exact(t * t, ones, terms=NORM_TERMS) + 1e-6)

    q = l2n(qkv[:, 0:w]) * (HEAD ** -0.5)
    k = l2n(qkv[:, w:2 * w])
    v = qkv[:, 2 * w:3 * w]
    sh_ref[:, 0:w] = q.astype(BF16)
    sh_ref[:, w:2 * w] = v.astype(BF16)
    sh_ref[:, 2 * w:3 * w] = k.astype(BF16)
    sr = p[:, 4 * w:4 * w + 128].astype(F32)
    beta_all = _sigmoid(sr)
    g_all = -jnp.exp(alog_ref[...]) * _softplus(sr + dt_ref[...])
    for d, (dk_ref, dw_ref) in enumerate(((dk0_ref, dw0_ref), (dk1_ref, dw1_ref))):
        beta = _mm_exact(beta_all, eb_ref[d])
        g = _mm_exact(g_all, ea_ref[d])
        kb = k * beta
        dw_ref[...] = g
        dk_ref[:, 0:w] = kb.astype(BF16)
        dk_ref[:, w:2 * w] = (-jnp.exp(g) * kb).astype(BF16)
    fin_ref[...] = p[:, 3 * w:4 * w].astype(F32)


SEG_GLA, SEG_GDN, SEG_ATT, SEG_RWKV = range(4)
N_RWKV_CONSTS, N_GDN_CONSTS = 11, 6
RWKV_OUTS = [(3, BF16), (2, BF16), (2, BF16), (1, F32), (1, F32), (2, F32)]
GDN_OUTS = [(3, BF16), (2, BF16), (2, BF16), (1, F32), (1, F32), (1, F32)]


def _front_kernel(*refs, n_lat_blocks):
    it = iter(refs)
    x_ref, xp_ref, xn_ref, sc_ref, csc_ref, sh_ref, csh_ref, g_ref = (next(it) for _ in range(8))
    w_refs = [next(it) for _ in range(4)]
    cos_ref, sin_ref = next(it), next(it)
    rwkv_consts = [next(it) for _ in range(N_RWKV_CONSTS)]
    gdn_consts = [next(it) for _ in range(N_GDN_CONSTS)]
    pc_ref, pd_ref = next(it), next(it)
    rwkv_outs = [next(it) for _ in RWKV_OUTS]
    gdn_outs = [next(it) for _ in GDN_OUTS]

    first, lastb = _seq_edges(n_lat_blocks)
    is_ctx = pl.program_id(1) >= n_lat_blocks
    scale = jnp.where(is_ctx, csc_ref[...], sc_ref[...])
    shift = jnp.where(is_ctx, csh_ref[...], sh_ref[...])
    h = _prenorm(x_ref[...], g_ref[...], scale, shift).astype(BF16)
    halo = jnp.concatenate([xp_ref[...], xn_ref[...]], axis=0)
    h_halo = _prenorm(halo, g_ref[...], scale, shift).astype(BF16)

    def proj(rows, which):
        return jnp.dot(rows, w_refs[which][...], preferred_element_type=F32)

    pc_ref[...] = proj(h, SEG_GLA).astype(BF16)
    att = proj(h, SEG_ATT)
    w = BRANCH_W
    for c0 in range(0, att.shape[1], w):
        part = att[:, c0:c0 + w]
        if c0 < 2 * w:
            part = _rope(part, cos_ref[...], sin_ref[...])
        pd_ref[:, c0:c0 + w] = part.astype(BF16)

    h_ext = jnp.concatenate([h, h_halo], axis=0)
    n = h.shape[0]
    pb = proj(h_ext, SEG_GDN).astype(BF16)
    _gdn_prep(pb[0:n], pb[n:n + HALO], pb[n + HALO:], first, lastb, *gdn_consts, *gdn_outs)
    pa = proj(h_ext, SEG_RWKV).astype(BF16)
    _rwkv_prep(pa[0:n], pa[n:n + HALO], pa[n + HALO:], first, lastb, *rwkv_consts, *rwkv_outs)


def _front(xs, mod, gain, weights, cos, sin, rwkv_consts, gdn_consts, n_lat_rows):
    batch, rows, d = xs.shape
    assert len(rwkv_consts) == N_RWKV_CONSTS and len(gdn_consts) == N_GDN_CONSTS and len(weights) == 4
    tab = pl.BlockSpec((PREP_ROWS, BRANCH_W), lambda b, i: (i, 0))
    consts = list(rwkv_consts) + list(gdn_consts)
    outs = [(weights[s].shape[1], BF16) for s in (SEG_GLA, SEG_ATT)]
    outs += [(n * BRANCH_W, dt) for n, dt in RWKV_OUTS + GDN_OUTS]
    out_specs = [pl.BlockSpec((None, PREP_ROWS, lanes), lambda b, i: (b, i, 0)) for lanes, _ in outs]
    out_shape = [jax.ShapeDtypeStruct((batch, rows, lanes), dt) for lanes, dt in outs]
    res = pl.pallas_call(
        functools.partial(_front_kernel, n_lat_blocks=n_lat_rows // PREP_ROWS),
        grid=(batch, rows // PREP_ROWS),
        in_specs=_halo_specs(d, rows, 0) + _mod_specs(1, batch) + _mod_specs(0, batch)
        + [_vec_spec(gain.shape)] + [_vec_spec(w.shape) for w in weights] + [tab, tab]
        + [_vec_spec(c.shape) for c in consts],
        out_specs=out_specs,
        out_shape=out_shape,
        compiler_params=_cparams(("parallel", "parallel")),
        name="front_proj_prep",
    )(xs, xs, xs, mod, mod, mod, mod, gain, *weights, cos, sin, *consts)
    pc, pd = res[0], res[1]
    a_sh, a_dk0, a_dk1, a_dw0, a_dw1, a_fin = res[2:8]
    b_sh, b_dk0, b_dk1, b_dw0, b_dw1, b_fin = res[8:14]
    return pc, pd, (a_sh, (a_dk0, a_dk1), (a_dw0, a_dw1), a_fin), (b_sh, (b_dk0, b_dk1), (b_dw0, b_dw1), b_fin)


NEG_BIG = -1e30


def _window_bias():
    iq = np.arange(2 * ATT_BLOCK)[:, None] % ATT_BLOCK
    ik = np.arange(ATT_BLOCK)[None, :]
    ok = np.stack([ik >= iq, np.ones_like(ik >= iq), ik <= iq])
    assert WINDOW == ATT_BLOCK
    return np.where(ok, 0.0, NEG_BIG).astype(np.float32)


ATT_STEP = 2


N_ATT_REFS = 5 + 2 * (ATT_STEP + 2)


def _attn_rows(refs, n_lat_rows):
    sink_ref, q_ref, kc_ref, vc_ref = refs[0:4]
    nband = ATT_STEP + 2
    k_all, v_all = refs[4:4 + nband], refs[4 + nband:4 + 2 * nband]
    bias_ref = refs[4 + 2 * nband]
    blk = ATT_BLOCK
    lane = lax.broadcasted_iota(jnp.int32, (blk, 2 * HEAD), 1)
    row2 = lax.broadcasted_iota(jnp.int32, (2 * blk, 1), 0)
    out_rows = []
    for u in range(ATT_STEP):
        out_cols = []
        t = pl.program_id(1) * ATT_STEP + u
        rows = slice(u * blk, (u + 1) * blk)
        q = q_ref[rows, :] * (HEAD ** -0.5)
        k_band, v_band = k_all[u:u + 3], v_all[u:u + 3]
        q_lat = t * blk < n_lat_rows
        in_seq = [q_lat & (t >= 1), q_lat, q_lat & ((t + 1) * blk < n_lat_rows)]
        bias = [jnp.where(in_seq[j], bias_ref[j], NEG_BIG) for j in range(3)]
        for g in range(D_KV_HEADS):
            cols = slice(g * 2 * HEAD, (g + 1) * 2 * HEAD)
            qg = q[:, cols]
            zero = jnp.zeros((), qg.dtype)
            qs = jnp.concatenate([jnp.where(lane < HEAD, qg, zero), jnp.where(lane >= HEAD, qg, zero)], axis=0)
            s_ctx = _mm(qs, kc_ref[:, cols], NT)
            s_band = [_mm(qs, k_band[j][:, cols], NT) + bias[j] for j in range(3)]
            sink = jnp.where(row2 < blk, sink_ref[2 * g], sink_ref[2 * g + 1])
            m = jnp.maximum(jnp.max(s_ctx, axis=-1, keepdims=True), sink)
            for s in s_band:
                m = jnp.maximum(m, jnp.max(s, axis=-1, keepdims=True))
            p_ctx = jnp.exp(s_ctx - m)
            den = jnp.sum(p_ctx, axis=-1, keepdims=True) + jnp.exp(sink - m)
            acc = _mm(p_ctx, vc_ref[:, cols], NN)
            for j in range(3):
                pj = jnp.exp(s_band[j] - m)
                den = den + jnp.sum(pj, axis=-1, keepdims=True)
                acc = acc + _mm(pj, v_band[j][:, cols], NN)
            og = acc / den
            out_cols.append(jnp.where(lane < HEAD, og[0:blk], og[blk:2 * blk]))
        out_rows.append(jnp.concatenate(out_cols, axis=1))
    return jnp.concatenate(out_rows, axis=0)


ATT_ROWS = ATT_STEP * ATT_BLOCK


def _attention_inputs(p, sink, n_lat_rows):
    _, rows, _ = p.shape
    blk = ATT_BLOCK
    nb = rows // blk
    w = BRANCH_W
    n_ctx_rows = rows - n_lat_rows
    assert n_lat_rows % n_ctx_rows == 0 and n_ctx_rows % ATT_ROWS == 0

    def band(col, off):
        def index(b, t):
            return (b, jnp.clip(t * ATT_STEP + off, 0, nb - 1), col)
        return pl.BlockSpec((None, blk, w), index)

    offsets = range(-1, ATT_STEP + 1)
    ctx = lambda col: pl.BlockSpec((None, n_ctx_rows, w), lambda b, t: (b, n_lat_rows // n_ctx_rows, col))
    bias = jnp.asarray(_window_bias())
    specs = [pl.BlockSpec(memory_space=pltpu.SMEM), pl.BlockSpec((None, ATT_ROWS, w), lambda b, t: (b, t, 0)),
             ctx(1), ctx(2)]
    specs += [band(1, o) for o in offsets] + [band(2, o) for o in offsets] + [_vec_spec(bias.shape)]
    args = [sink] + [p] * (3 + 2 * len(offsets)) + [bias]
    assert len(specs) == N_ATT_REFS
    return specs, args


def _merge_kernel(*refs, n_lat_rows):
    x_ref, sc_ref, csc_ref, sh_ref, csh_ref, gm_ref, cgm_ref, g0_ref, g1_ref, ya_ref, yb_ref, yc_ref = refs[0:12]
    att_refs = refs[12:12 + N_ATT_REFS]
    wg_ref, gb_ref, wb_ref, wo_ref, o_ref = refs[12 + N_ATT_REFS:]
    x = x_ref[...]
    is_ctx = _ctx_rows(x.shape[0], n_lat_rows)
    scale = jnp.where(is_ctx, csc_ref[...], sc_ref[...])
    shift = jnp.where(is_ctx, csh_ref[...], sh_ref[...])
    gmod = jnp.where(is_ctx, cgm_ref[...], gm_ref[...])
    h = _prenorm(x, g0_ref[...], scale, shift).astype(BF16)
    branches = [ya_ref[...], yb_ref[...], yc_ref[...], _attn_rows(att_refs, n_lat_rows)]
    acc = jnp.zeros(x.shape, F32)
    for i, y in enumerate(branches):
        pre = jnp.dot(h, wg_ref[:, i * D_MODEL:(i + 1) * D_MODEL], preferred_element_type=F32)
        gate = _sigmoid(pre + gb_ref[i:i + 1, :])
        acc = acc + gate * jnp.dot(y.astype(BF16), wb_ref[i], preferred_element_type=F32)
    out = jnp.dot(acc.astype(BF16), wo_ref[...], preferred_element_type=F32)
    ms = jnp.mean(out * out, axis=-1, keepdims=True)
    o_ref[...] = x + gmod * (out * lax.rsqrt(ms + NORM_EPS) * g1_ref[...])


def _merge(xs, mod, gain0, gain1, ys, p_att, sink, wg, gate_b, wb, wo, n_lat_rows, out_rows):
    batch, _, d = xs.shape
    rows = out_rows
    tm = ATT_ROWS
    assert rows % tm == 0
    tile = lambda width: pl.BlockSpec((None, tm, width), lambda b, i: (b, i, 0))
    consts = [wg, gate_b, wb, wo]
    att_specs, att_args = _attention_inputs(p_att, sink, n_lat_rows)
    return pl.pallas_call(
        functools.partial(_merge_kernel, n_lat_rows=n_lat_rows),
        grid=(batch, rows // tm),
        in_specs=[tile(d)] + _mod_specs(1, batch) + _mod_specs(0, batch) + _mod_specs(2, batch)
        + [_vec_spec(gain0.shape), _vec_spec(gain1.shape)]
        + [tile(BRANCH_W)] * 3 + att_specs + [_vec_spec(c.shape) for c in consts],
        out_specs=tile(d),
        out_shape=jax.ShapeDtypeStruct((batch, rows, d), F32),
        compiler_params=_cparams(("parallel", "parallel")),
        name="attn_merge_out",
    )(xs, *([mod] * 6), gain0, gain1, *ys, *att_args, *consts)


FFN_CHUNK = 512


def _ffn_kernel(x_ref, sc_ref, csc_ref, sh_ref, csh_ref, gm_ref, cgm_ref, g2_ref, g3_ref, w1_ref, w2_ref,
                o_ref, *, n_lat_rows):
    x = x_ref[...]
    is_ctx = _ctx_rows(x.shape[0], n_lat_rows)
    scale = jnp.where(is_ctx, csc_ref[...], sc_ref[...])
    shift = jnp.where(is_ctx, csh_ref[...], sh_ref[...])
    h = _prenorm(x, g2_ref[...], scale, shift).astype(BF16)
    hidden = w2_ref.shape[0]
    out = jnp.zeros(x.shape, F32)
    for lo in range(0, hidden, FFN_CHUNK):
        hi = min(lo + FFN_CHUNK, hidden)
        gt = jnp.dot(h, w1_ref[:, lo:hi], preferred_element_type=F32)
        up = jnp.dot(h, w1_ref[:, hidden + lo:hidden + hi], preferred_element_type=F32)
        out = out + jnp.dot((_silu(gt) * up).astype(BF16), w2_ref[lo:hi, :], preferred_element_type=F32)
    ms = jnp.mean(out * out, axis=-1, keepdims=True)
    gmod = jnp.where(is_ctx, cgm_ref[...], gm_ref[...])
    o_ref[...] = x + gmod * (out * lax.rsqrt(ms + NORM_EPS) * g3_ref[...])


def _ffn(xs, mod, gain2, gain3, w1, w2, n_lat_rows):
    batch, rows, d = xs.shape
    tm = _row_tile(rows, 544)
    tile = pl.BlockSpec((None, tm, d), lambda b, i: (b, i, 0))
    consts = [gain2, gain3, w1, w2]
    return pl.pallas_call(
        functools.partial(_ffn_kernel, n_lat_rows=n_lat_rows),
        grid=(batch, rows // tm),
        in_specs=[tile] + _mod_specs(4, batch) + _mod_specs(3, batch) + _mod_specs(5, batch)
        + [_vec_spec(c.shape) for c in consts],
        out_specs=tile,
        out_shape=jax.ShapeDtypeStruct((batch, rows, d), F32),
        compiler_params=_cparams(("parallel", "parallel")),
        name="swiglu",
    )(xs, *([mod] * 6), *consts)


def _pad_cols(w, width):
    return jnp.pad(w, ((0, 0), (0, width - w.shape[1])))


def _layer_weights(w_in, mu, w2, a2, g2, conv, a_log, dt_bias, gw2, gb):
    wts = {}
    off_b, off_c, off_d, off_g = A_IN, A_IN + B_IN, A_IN + B_IN + C_IN, A_IN + B_IN + C_IN + D_IN
    w = BRANCH_W
    wts["wa"] = _pad_cols(w_in[:, 0:A_IN], 4 * w).astype(BF16)
    wts["mu"] = _pad_cols(mu[None, :], 4 * w)
    lo = np.cumsum((0,) + A_LORA)
    place = lambda m, r0: jnp.zeros((w, w), F32).at[r0:r0 + m.shape[0], :].set(m)
    wts["w2p"] = jnp.stack([place(w2[0], lo[0]), place(w2[1], lo[1])])
    wts["a2p"] = jnp.stack([place(a2[0], lo[2]), place(a2[1], lo[3])])
    wts["g2p"] = place(g2, lo[4])

    wb = w_in[:, off_b:off_b + B_IN]
    wts["wb"] = jnp.concatenate([wb[:, 0:3 * w], wb[:, 3 * w + 16:], _pad_cols(wb[:, 3 * w:3 * w + 16], 128)],
                                axis=1).astype(BF16)
    nh = N_HEADS
    expand = np.zeros((4, 128, w), np.float32)
    for grp in range(4):
        for h in range(nh):
            expand[grp, grp * nh + h, h * HEAD:(h + 1) * HEAD] = 1.0
    wts["eb"] = jnp.asarray(expand[0:2])
    wts["ea"] = jnp.asarray(expand[2:4])
    vec = lambda t: jnp.zeros((1, 128), F32).at[0, 2 * nh:4 * nh].set(t.reshape(-1))
    wts["alog"] = vec(a_log)
    wts["dt"] = vec(dt_bias)
    wts["conv"] = conv

    wc = w_in[:, off_c:off_c + C_IN]
    pad_heads = lambda m: jnp.pad(m.reshape(m.shape[0], nh, C_DK),
                                  ((0, 0), (0, 0), (0, HEAD - C_DK))).reshape(m.shape[0], w)
    qc, kc, vc = wc[:, 0:C_QK], wc[:, C_QK:2 * C_QK], wc[:, 2 * C_QK:2 * C_QK + w]
    loc = wc[:, 2 * C_QK + w:2 * C_QK + w + 2 * C_GATE_R]
    gc = wc[:, 2 * C_QK + w + 2 * C_GATE_R:]
    wts["wc"] = jnp.concatenate([pad_heads(qc), pad_heads(kc), vc, gc, _pad_cols(loc, 128)], axis=1).astype(BF16)
    gwp = jnp.zeros((2, 128, w), F32)
    for d in range(2):
        gwp = gwp.at[d, d * C_GATE_R:(d + 1) * C_GATE_R, :].set(pad_heads(gw2[d]))
    wts["gwp"] = gwp
    wts["gbp"] = pad_heads(gb)
    wts["glane"] = jnp.asarray((np.arange(w) % HEAD < C_DK).astype(np.float32))[None, :]

    wd = w_in[:, off_d:off_d + D_IN]
    qd = wd[:, 0:w]
    dup = lambda m: jnp.concatenate([m[:, 0:HEAD], m[:, 0:HEAD], m[:, HEAD:], m[:, HEAD:]], axis=1)
    wts["wd"] = jnp.concatenate([qd, dup(wd[:, w:w + 2 * HEAD]), dup(wd[:, w + 2 * HEAD:])], axis=1).astype(BF16)
    wts["wg"] = w_in[:, off_g:].astype(BF16)
    return wts


def _rope_tables(n_lat_rows, n_ctx_rows):
    quarter = HEAD // 4
    inv = ROPE_BASE ** (-np.arange(quarter, dtype=np.float32) / quarter)
    pos = np.arange(n_lat_rows)
    rows = (pos // GRID_W).astype(np.float32)
    cols = (pos % GRID_W).astype(np.float32)
    inv = jnp.asarray(inv)
    ang_r = jnp.asarray(rows)[:, None] * inv[None, :]
    ang_c = jnp.asarray(cols)[:, None] * inv[None, :]
    cos = jnp.concatenate([jnp.cos(ang_r)] * 2 + [jnp.cos(ang_c)] * 2, axis=1)
    sin = jnp.concatenate([-jnp.sin(ang_r), jnp.sin(ang_r), -jnp.sin(ang_c), jnp.sin(ang_c)], axis=1)
    cos = jnp.concatenate([cos, jnp.ones((n_ctx_rows, HEAD), F32)], axis=0)
    sin = jnp.concatenate([sin, jnp.zeros((n_ctx_rows, HEAD), F32)], axis=0)
    return jnp.tile(cos, (1, N_HEADS)), jnp.tile(sin, (1, N_HEADS))


RWKV_SLOTS = {"r": ("sh", 0), "v": ("sh", 1), "a": ("sh", 2), "w": ("dw", 0), "k": ("dk", 0), "b": ("dk", 1)}
GDN_SLOTS = RWKV_SLOTS
GLA_SLOTS = {"r": ("sh", 0), "k": ("sh", 1), "v": ("sh", 2)}


def kernel(x, c, ctx, c_ctx, ada_w, ada_b, norm_g, w_in, gate_b, w_branch, w_out, rwkv_mu, rwkv_w0, rwkv_w2, rwkv_a0, rwkv_a2, rwkv_g2, rwkv_kk, rwkv_ka, rwkv_rk, rwkv_ln_g, rwkv_ln_b, gdn_conv, gdn_a_log, gdn_dt_bias, gdn_norm_g, gla_gw2, gla_gb, gla_norm_g, attn_sink, ffn_w1, ffn_w2):
    batch, n_lat, d = x.shape
    n_ctx = ctx.shape[1]
    depth = ada_w.shape[0]
    assert n_ctx % PREP_ROWS == 0 and n_lat % PREP_ROWS == 0 and d == D_MODEL

    mod_rows = 8 * ((batch + 1 + 7) // 8)
    c_rows = jnp.concatenate([c, c_ctx[None, :], jnp.zeros((mod_rows - batch - 1, d), F32)], axis=0)
    mod_all = _modulation(c_rows, ada_w, ada_b)
    cos, sin = _rope_tables(n_lat, n_ctx)

    xs = jnp.concatenate([x, ctx], axis=1)
    rows = n_lat + n_ctx
    row = lambda t: t.reshape(1, -1)
    for l in range(depth):
        out_rows = n_lat if l == depth - 1 else rows
        mod = mod_all[l].reshape(mod_rows, 1, 6 * d)
        ng = norm_g[l]
        wts = _layer_weights(w_in[l], rwkv_mu[l], rwkv_w2[l], rwkv_a2[l], rwkv_g2[l], gdn_conv[l],
                             gdn_a_log[l], gdn_dt_bias[l], gla_gw2[l], gla_gb[l])
        ones = jnp.asarray(_head_block_ones())
        around = jnp.asarray(0.5 * (_shift_matrix(-1) + _shift_matrix(1)), BF16)
        rwkv_consts = [around, wts["mu"], wts["w2p"], wts["a2p"], wts["g2p"], rwkv_w0[l], rwkv_a0[l],
                       row(rwkv_kk[l]), row(rwkv_ka[l]), row(rwkv_rk[l]), ones]
        gdn_consts = [wts["conv"], wts["alog"], wts["dt"], wts["eb"], wts["ea"], ones]
        p_gla, p_att, rwkv_ops, gdn_ops = _front(xs, mod, row(ng[0]), (wts["wc"], wts["wb"], wts["wd"], wts["wa"]),
                                                 cos, sin, rwkv_consts, gdn_consts, n_lat)

        sh, dks, dws, fin = rwkv_ops
        ya = _bidir_scan(sh, dks, dws, n_lat, fin, row(rwkv_ln_g[l]), row(rwkv_ln_b[l]), finish="groupnorm",
                         scalar_decay=False, lowrank=True, slots=RWKV_SLOTS)

        sh, dks, dws, fin = gdn_ops
        gnorm = row(jnp.tile(gdn_norm_g[l], N_HEADS))
        yb = _bidir_scan(sh, dks, dws, n_lat, fin, gnorm, gnorm, finish="rms",
                         scalar_decay=True, lowrank=True, slots=GDN_SLOTS)

        yc = _gla_scan(p_gla, n_lat, wts["gwp"], wts["gbp"], wts["glane"], row(jnp.tile(gla_norm_g[l], N_HEADS)))

        xs = _merge(xs, mod, row(ng[0]), row(ng[1]), (ya, yb, yc), p_att, attn_sink[l], wts["wg"], gate_b[l],
                    w_branch[l].astype(BF16), w_out[l].astype(BF16), n_lat, out_rows)
        xs = _ffn(xs, mod, row(ng[2]), row(ng[3]), ffn_w1[l].astype(BF16), ffn_w2[l].astype(BF16), n_lat)
    return xs
```

```python
import functools
import math

import numpy as np
import jax
import jax.numpy as jnp
from jax import lax
from jax.experimental import pallas as pl
from jax.experimental.pallas import tpu as pltpu

F32 = jnp.float32
BF16 = jnp.bfloat16

D_MODEL = 1024
N_BRANCH = 4
BRANCH_W = 256
HEAD = 64
N_HEADS = BRANCH_W // HEAD
NORM_EPS = 1e-6
A_GN_EPS = 64e-5
A_LORA = (32, 32, 32, 32, 64)
A_IN = 3 * BRANCH_W + sum(A_LORA)
B_IN = 4 * BRANCH_W + 4 * N_HEADS
C_DK = 32
C_QK = N_HEADS * C_DK
C_GATE_R = 16
C_GATE_NORM = 16.0
C_IN = 2 * C_QK + 2 * BRANCH_W + 2 * C_GATE_R
D_KV_HEADS = 2
D_IN = BRANCH_W + 2 * D_KV_HEADS * HEAD
B_CONV = 7
WINDOW = 128
ROPE_BASE = 10000.0
GRID_W = 64
FFN_HIDDEN = 2816

CHUNK = 64
PREP_ROWS = 256
HALO = 16
ATT_BLOCK = 128
VMEM_LIMIT = 48 * 1024 * 1024

NN = (((1,), (0,)), ((), ()))
NT = (((1,), (1,)), ((), ()))
TN = (((0,), (0,)), ((), ()))


def _mm(a, b, dims=NN, mode="bf16"):
    if mode == "f32":
        return lax.dot_general(a, b, dims, precision=lax.Precision.HIGHEST, preferred_element_type=F32)
    if mode == "x3":
        ah = a.astype(BF16)
        al = (a - ah.astype(F32)).astype(BF16)
        bh = b.astype(BF16)
        bl = (b - bh.astype(F32)).astype(BF16)
        dot = functools.partial(lax.dot_general, dimension_numbers=dims, preferred_element_type=F32)
        return dot(ah, bh) + (dot(ah, bl) + dot(al, bh))
    return lax.dot_general(a.astype(BF16), b.astype(BF16), dims, preferred_element_type=F32)


def _sigmoid(x):
    return 1.0 / (1.0 + jnp.exp(-x))


def _silu(x):
    return x * _sigmoid(x)


def _softplus(x):
    return jnp.maximum(x, 0.0) + jnp.log1p(jnp.exp(-jnp.abs(x)))


def _cparams(sem):
    return pltpu.CompilerParams(dimension_semantics=sem, vmem_limit_bytes=VMEM_LIMIT)


def _mod_kernel(c_ref, w_ref, b_ref, o_ref):
    c = c_ref[...]
    o_ref[...] = _mm(_silu(c), w_ref[...], mode="f32") + b_ref[...]


def _modulation(c_rows, ada_w, ada_b):
    depth, d, n = ada_w.shape
    rows = c_rows.shape[0]
    tn = 1024
    return pl.pallas_call(
        _mod_kernel,
        grid=(depth, n // tn),
        in_specs=[
            pl.BlockSpec((rows, d), lambda l, j: (0, 0)),
            pl.BlockSpec((None, d, tn), lambda l, j: (l, 0, j)),
            pl.BlockSpec((None, 1, tn), lambda l, j: (l, 0, j)),
        ],
        out_specs=pl.BlockSpec((None, rows, tn), lambda l, j: (l, 0, j)),
        out_shape=jax.ShapeDtypeStruct((depth, rows, n), F32),
        compiler_params=_cparams(("parallel", "parallel")),
        name="adaln_mod",
    )(c_rows, ada_w, ada_b.reshape(depth, 1, n))


def _row_tile(rows, target):
    return max(t for t in range(8, target + 1, 8) if rows % t == 0)


def _mod_specs(which, batch):
    lat = pl.BlockSpec((None, 1, D_MODEL), lambda b, i, *_: (b, 0, which))
    ctx = pl.BlockSpec((None, 1, D_MODEL), lambda b, i, *_: (batch, 0, which))
    return [lat, ctx]


def _ctx_rows(tm, n_lat_rows):
    row = pl.program_id(1) * tm + lax.broadcasted_iota(jnp.int32, (tm, 1), 0)
    return row >= n_lat_rows


def _prenorm(x, gain, scale, shift):
    ms = jnp.mean(x * x, axis=-1, keepdims=True)
    y = x * lax.rsqrt(ms + NORM_EPS) * gain
    return y * (1.0 + scale) + shift


def _rope(x, cos, sin):
    width = x.shape[-1]
    lane = lax.broadcasted_iota(jnp.int32, x.shape, 1)
    swapped = jnp.where(lane % 32 < 16, pltpu.roll(x, width - 16, axis=1), pltpu.roll(x, 16, axis=1))
    return x * cos + swapped * sin


def _head_block_ones():
    idx = np.arange(BRANCH_W)
    return (idx[:, None] // HEAD == idx[None, :] // HEAD).astype(np.float32)


N_LEVELS = 6


def _level_mask(ri, ci, s, reverse):
    b = 1 << s
    blk = (ri // (2 * b)) == (ci // (2 * b))
    hi_r, hi_c = (ri // b) % 2 == 1, (ci // b) % 2 == 1
    return blk & ((hi_c & ~hi_r) if reverse else (hi_r & ~hi_c))


SCAN_HEADS = 2
GROUP_W = SCAN_HEADS * HEAD
N_GROUPS = N_HEADS // SCAN_HEADS


def _scan_masks(reverse):
    n = SCAN_HEADS * CHUNK
    ri = np.arange(CHUNK)[:, None]
    ci = np.arange(n)[None, :] % CHUNK
    strict, incl = (ci > ri, ci >= ri) if reverse else (ci < ri, ci <= ri)
    compact = [strict, incl, ci == ri] + [_level_mask(ri, ci, s, reverse) for s in range(N_LEVELS)]
    r = np.arange(n)[:, None]
    c = np.arange(n)[None, :]
    block = [(r // CHUNK) == (c // CHUNK), r == c]
    return np.stack(compact).astype(np.float32), np.stack(block).astype(np.float32)


EXACT_TERMS = 3
NORM_TERMS = 2


def _tri(reverse):
    i = np.arange(CHUNK)
    m = (i[None, :] >= i[:, None]) if reverse else (i[None, :] <= i[:, None])
    return np.tile(m.astype(np.float32), (1, EXACT_TERMS))


def _split_terms(x, terms):
    out = []
    for _ in range(terms - 1):
        part = x.astype(BF16)
        out.append(part)
        x = x - part.astype(F32)
    out.append(x.astype(BF16))
    return out


def _mm_exact(a, b, dims=NN, split="a", terms=EXACT_TERMS):
    assert dims == NN
    if split == "a":
        lhs = jnp.concatenate(_split_terms(a, terms), axis=1)
        rhs = jnp.concatenate([b.astype(BF16)] * terms, axis=0)
    else:
        lhs = a.astype(BF16)
        rhs = jnp.concatenate(_split_terms(b, terms), axis=0)
    return lax.dot_general(lhs, rhs, dims, preferred_element_type=F32)


C_STRICT, C_INCL, C_EYE, C_LVL0 = 0, 1, 2, 3
B_SAME, B_EYE = 0, 1
SCAN_BATCH = 8
SCAN_ROWS = 256
assert CHUNK == HEAD and (1 << N_LEVELS) == CHUNK


def _scan_chunk(vals, st, tri, ones, cm_ref, bm_ref, same_bf, *, reverse, scalar_decay, lowrank, mm_mode):
    same = bm_ref[B_SAME]

    def get(name):
        return vals[name]

    def expand(x):
        return jnp.concatenate([x.astype(BF16)] * SCAN_HEADS, axis=0) * same_bf

    def keep(x, k):
        return jnp.where(cm_ref[k] > 0.5, x, 0.0)

    logw = get("w")
    r, k, v = get("r"), get("k"), get("v")
    cum = _mm_exact(tri, logw, split="b")
    cum_x = cum - logw
    last = 0 if reverse else CHUNK - 1
    total = cum[last:last + 1, :]
    to_end = jnp.exp(total - cum)

    if scalar_decay:
        diag = jnp.concatenate([cum] * SCAN_HEADS, axis=0) * bm_ref[B_EYE]
        cum_row = _mm_exact(ones, diag, split="b")
        d_ii = jnp.exp(jnp.where(cm_ref[C_INCL] > 0.5, cum - cum_row, -1e30))
        r_q, k_q = r, k
    else:
        ref_row = cum[CHUNK // 2:CHUNK // 2 + 1, :]
        p_inv = jnp.exp(ref_row - cum)
        r_q, k_q = r * jnp.exp(cum - ref_row), k * p_inv

    k_e, v_e = expand(k_q), expand(v)
    r_abs = r * jnp.exp(cum)
    if not lowrank:
        s_k = _mm(r_q, k_e, NT, mm_mode)
        a_rk = s_k * d_ii if scalar_decay else keep(s_k, C_INCL)
        y = _mm(r_abs, st, NT, mm_mode) + _mm(a_rk, v_e, NN, mm_mode)
        upd = _mm(v, k * to_end, TN, mm_mode)
        return y, st * jnp.exp(total) + upd * same

    a, b = get("a"), get("b")
    if scalar_decay:
        d_xi = jnp.exp(jnp.where(cm_ref[C_STRICT] > 0.5, cum_x - cum_row, -1e30))
        a_q, b_q = a, b
    else:
        a_q, b_q = a * jnp.exp(cum_x - ref_row), b * p_inv
    lhs = jnp.concatenate([a_q, r_q], axis=0)
    scores = _mm(lhs, jnp.concatenate([expand(b_q), k_e], axis=0), NT, mm_mode)
    s_b, s_k = scores[:, 0:GROUP_W], scores[:, GROUP_W:]
    if scalar_decay:
        a_ab, a_rb = s_b[0:CHUNK] * d_xi, s_b[CHUNK:] * d_ii
        a_ak, a_rk = s_k[0:CHUNK] * d_xi, s_k[CHUNK:] * d_ii
    else:
        a_ab, a_rb = keep(s_b[0:CHUNK], C_STRICT), keep(s_b[CHUNK:], C_INCL)
        a_ak, a_rk = keep(s_k[0:CHUNK], C_STRICT), keep(s_k[CHUNK:], C_INCL)

    inv = cm_ref[C_EYE] + a_ab * cm_ref[C_LVL0]
    for s in range(1, N_LEVELS):
        c_s = expand(a_ab * cm_ref[C_LVL0 + s])
        inv = inv + _mm(_mm(inv, c_s, NN, mm_mode), expand(inv), NN, mm_mode)

    a_abs = a * jnp.exp(cum_x)
    from_state = _mm(jnp.concatenate([a_abs, r_abs], axis=0), st, NT, mm_mode)
    from_v = _mm(jnp.concatenate([a_ak, a_rk], axis=0), v_e, NN, mm_mode)
    both = from_state + from_v
    z = _mm(inv, expand(both[0:CHUNK]), NN, mm_mode)
    y = both[CHUNK:] + _mm(a_rb, expand(z), NN, mm_mode)
    upd = _mm(jnp.concatenate([v, z], axis=0), jnp.concatenate([k * to_end, b * to_end], axis=0), TN, mm_mode)
    return y, st * jnp.exp(total) + upd * same


def _scan_kernel(*refs, reverse, scalar_decay, lowrank, slots, sources, finish, mm_mode, lora, r_scale):
    it = iter(refs)
    src_refs = {name: next(it) for name in sources}
    sh_ref = src_refs[sources[0]]
    cm_ref, bm_ref, tri_ref = next(it), next(it), next(it)
    if lora:
        lo_ref, gw_ref, gb_ref, lane_ref = next(it), next(it), next(it), next(it)
    if finish:
        ob_ref, fin_ref, p1_ref, p2_ref, avg_ref = next(it), next(it), next(it), next(it), next(it)
    o_ref, st_ref = next(it), next(it)

    @pl.when(pl.program_id(1) == 0)
    def _():
        st_ref[...] = jnp.zeros_like(st_ref)

    nb, block_rows = sh_ref.shape[0], sh_ref.shape[1]
    n_chunks = block_rows // CHUNK
    n = nb * N_GROUPS
    tri = jnp.broadcast_to(tri_ref[...], (n,) + tri_ref.shape)
    ones = jnp.ones((n, CHUNK, EXACT_TERMS * SCAN_HEADS * CHUNK), F32)
    chunk = functools.partial(_scan_chunk, cm_ref=cm_ref, bm_ref=bm_ref, same_bf=bm_ref[B_SAME].astype(BF16),
                              reverse=reverse, scalar_decay=scalar_decay, lowrank=lowrank, mm_mode=mm_mode)

    def step(c, carry):
        rows = pl.ds(pl.multiple_of((n_chunks - 1 - c if reverse else c) * CHUNK, CHUNK), CHUNK)

        def groups(name):
            src, idx = slots[name]
            ref = src_refs[src]
            parts = [ref[:, rows, idx * BRANCH_W + g * GROUP_W:idx * BRANCH_W + (g + 1) * GROUP_W].astype(F32)
                     for g in range(N_GROUPS)]
            return jnp.stack(parts, axis=1).reshape(n, CHUNK, GROUP_W)

        vals = {name: groups(name) for name in slots}
        if lora:
            lo = lo_ref[:, rows, :].reshape(nb * CHUNK, lo_ref.shape[-1])
            z = _mm(lo, gw_ref[...], mode=mm_mode) + gb_ref[...]
            logw = ((-_softplus(-z) / C_GATE_NORM) * lane_ref[...]).reshape(nb, CHUNK, BRANCH_W)
            parts = [logw[:, :, g * GROUP_W:(g + 1) * GROUP_W] for g in range(N_GROUPS)]
            vals["w"] = jnp.stack(parts, axis=1).reshape(n, CHUNK, GROUP_W)
        if r_scale != 1.0:
            vals["r"] = vals["r"] * r_scale
        y, st_new = jax.vmap(chunk)(vals, st_ref[...], tri, ones)
        st_ref[...] = st_new
        y = y.reshape(nb, N_GROUPS, CHUNK, GROUP_W)
        y = jnp.concatenate([y[:, g] for g in range(N_GROUPS)], axis=-1)
        if not finish:
            o_ref[:, rows, :] = y
            return carry

        y = (y + ob_ref[:, rows, :]).reshape(nb * CHUNK, BRANCH_W)
        avg = avg_ref[...]
        if finish == "groupnorm":
            gate = fin_ref[:, rows, 0:BRANCH_W].reshape(y.shape)
            bonus = fin_ref[:, rows, BRANCH_W:2 * BRANCH_W].reshape(y.shape)
            cen = y - _mm_exact(y, avg)
            var = _mm_exact(cen * cen, avg)
            yn = cen * lax.rsqrt(var + A_GN_EPS) * p1_ref[...] + p2_ref[...]
            out = (yn + bonus) * gate
        else:
            ms = _mm_exact(y * y, avg)
            gate = fin_ref[:, rows, :].astype(F32).reshape(y.shape)
            out = y * lax.rsqrt(ms + NORM_EPS) * p1_ref[...] * _silu(gate)
        o_ref[:, rows, :] = out.reshape(nb, CHUNK, BRANCH_W)
        return carry

    lax.fori_loop(0, n_chunks, step, 0)


def _view(x):
    return x if isinstance(x, tuple) else (x, x.shape[-1], 0)


def _scan(srcs, n_lat_rows, *, reverse, scalar_decay, lowrank, slots, finish=None, fin_args=None,
          mm_mode="bf16", lora=None, r_scale=1.0):
    sources = tuple(srcs)
    views = [_view(srcs[s]) for s in sources]
    batch, rows, _ = views[0][0].shape
    nc, nlat = rows // SCAN_ROWS, n_lat_rows // SCAN_ROWS
    nctx = nc - nlat
    nb = math.gcd(batch, SCAN_BATCH)

    if reverse:
        def chunk(n):
            return nc - 1 - n
    else:
        def chunk(n):
            return jnp.where(n < nctx, nlat + n, n - nctx)

    def row_spec(width, col=0):
        return pl.BlockSpec((nb, SCAN_ROWS, width), lambda b, n: (b, chunk(n), col))

    def const_spec(shape):
        zeros = (0,) * len(shape)
        return pl.BlockSpec(shape, lambda b, n: zeros)

    cmask, bmask = (jnp.asarray(m) for m in _scan_masks(reverse))
    tri = jnp.asarray(_tri(reverse))
    in_specs = [row_spec(w, c) for _, w, c in views]
    in_specs += [const_spec(cmask.shape), const_spec(bmask.shape), const_spec(tri.shape)]
    args = [a for a, _, _ in views] + [cmask, bmask, tri]
    if lora:
        (lo, lo_w, lo_c), gw, gb, lane = _view(lora[0]), lora[1], lora[2], lora[3]
        in_specs += [row_spec(lo_w, lo_c), const_spec(gw.shape), const_spec(gb.shape), const_spec(lane.shape)]
        args += [lo, gw, gb, lane]
    if finish:
        ob, fin, p1, p2 = fin_args
        fin, fin_w, fin_c = _view(fin)
        avg = jnp.asarray(_head_block_ones() / HEAD)
        in_specs += [row_spec(BRANCH_W), row_spec(fin_w, fin_c), const_spec(p1.shape), const_spec(p2.shape),
                     const_spec(avg.shape)]
        args += [ob, fin, p1, p2, avg]
    kern = functools.partial(_scan_kernel, reverse=reverse, scalar_decay=scalar_decay, lowrank=lowrank,
                             slots=slots, sources=sources, finish=finish, mm_mode=mm_mode, lora=bool(lora),
                             r_scale=r_scale)
    return pl.pallas_call(
        kern,
        grid=(batch // nb, nc),
        in_specs=in_specs,
        out_specs=row_spec(BRANCH_W),
        out_shape=jax.ShapeDtypeStruct((batch, rows, BRANCH_W), F32),
        scratch_shapes=[pltpu.VMEM((nb * N_GROUPS, GROUP_W, GROUP_W), F32)],
        compiler_params=_cparams(("parallel", "arbitrary")),
        name="dplr_scan_" + ("bwd" if reverse else "fwd"),
    )(*args)


def _gla_scan(p, n_lat_rows, gwp, gbp, lane, norm_g):
    w = BRANCH_W
    srcs = {"sh": (p, 3 * w, 0)}
    lo = (p, gwp.shape[1], 4 * w // gwp.shape[1])
    kw = dict(scalar_decay=False, lowrank=False, slots=GLA_SLOTS, r_scale=C_DK ** -0.5)
    ob = _scan(srcs, n_lat_rows, reverse=True, lora=(lo, gwp[1], gbp[1:2], lane), **kw)
    return _scan(srcs, n_lat_rows, reverse=False, lora=(lo, gwp[0], gbp[0:1], lane), finish="rms",
                 fin_args=(ob, (p, w, 3), norm_g, norm_g), **kw)


def _bidir_scan(sh, dks, dws, n_lat_rows, fin, p1, p2, *, finish, **kw):
    def srcs(d):
        out = {"sh": sh, "dw": dws[d]}
        if dks is not None:
            out["dk"] = dks[d]
        return out

    ob = _scan(srcs(1), n_lat_rows, reverse=True, **kw)
    return _scan(srcs(0), n_lat_rows, reverse=False, finish=finish, fin_args=(ob, fin, p1, p2), **kw)


def _seq_edges(n_lat_blocks):
    i = pl.program_id(1)
    first = (i == 0) | (i == n_lat_blocks)
    lastb = (i == n_lat_blocks - 1) | (i == pl.num_programs(1) - 1)
    return first, lastb


def _vec_spec(shape):
    zeros = (0,) * len(shape)
    return pl.BlockSpec(shape, lambda b, i: zeros)


def _shift_matrix(offset):
    m = np.zeros((PREP_ROWS, PREP_ROWS + 2 * HALO), np.float32)
    t = np.arange(PREP_ROWS)
    src = t + offset
    col = np.where(src < 0, PREP_ROWS + HALO + src, np.where(src >= PREP_ROWS, HALO + src, src))
    m[t, col] = 1.0
    return m


def _rwkv_prep(x, prev, nxt, first, lastb, shift_ref, mu_ref, w2_ref, a2_ref, g2_ref, w0_ref, a0_ref, kk_ref,
               ka_ref, rk_ref, ones_ref, sh_ref, dk0_ref, dk1_ref, dw0_ref, dw1_ref, fin_ref, mode="bf16"):
    zero = jnp.zeros((), x.dtype)
    xe = jnp.concatenate([x, jnp.where(first, zero, prev), jnp.where(lastb, zero, nxt)], axis=0)
    around = jnp.dot(shift_ref[...], xe, preferred_element_type=F32)
    x = x.astype(F32)
    xm = x + (around - x) * mu_ref[...]
    w = BRANCH_W
    r, k, v, lo = xm[:, 0:w], xm[:, w:2 * w], xm[:, 2 * w:3 * w], xm[:, 3 * w:4 * w]
    ones = ones_ref[...]
    th, sg = jnp.tanh(lo), _sigmoid(lo)
    gate = _mm(sg, g2_ref[...], mode=mode)
    kx = k * kk_ref[...]
    kk = kx * lax.rsqrt(_mm_exact(kx * kx, ones, terms=NORM_TERMS) + 1e-6)
    sh_ref[:, 0:w] = r.astype(BF16)
    sh_ref[:, w:2 * w] = v.astype(BF16)
    sh_ref[:, 2 * w:3 * w] = (-kk).astype(BF16)
    bonus = jnp.zeros_like(v)
    for d, (dk_ref, dw_ref) in enumerate(((dk0_ref, dw0_ref), (dk1_ref, dw1_ref))):
        w_raw = w0_ref[d:d + 1, :] + _mm(th, w2_ref[d], mode=mode)
        dw_ref[...] = -math.exp(-0.5) * _sigmoid(w_raw)
        a = _sigmoid(a0_ref[d:d + 1, :] + _mm(lo, a2_ref[d], mode=mode))
        kd = k * (1.0 + (a - 1.0) * ka_ref[...])
        dk_ref[:, 0:w] = kd.astype(BF16)
        dk_ref[:, w:2 * w] = (kk * a).astype(BF16)
        bonus = bonus + _mm_exact(r * kd * rk_ref[...], ones) * v
    fin_ref[:, 0:w] = gate
    fin_ref[:, w:2 * w] = bonus


def _gdn_prep(p, prev, nxt, first, lastb, conv_ref, alog_ref, dt_ref, eb_ref, ea_ref, ones_ref,
              sh_ref, dk0_ref, dk1_ref, dw0_ref, dw1_ref, fin_ref):
    w = BRANCH_W
    x = p[:, 0:3 * w].astype(F32)
    top = jnp.where(first, 0.0, prev[:, 0:3 * w].astype(F32))
    bot = jnp.where(lastb, 0.0, nxt[:, 0:3 * w].astype(F32))
    xe = jnp.concatenate([top, x, bot], axis=0)
    ext = PREP_ROWS + 2 * HALO
    acc = jnp.zeros_like(x)
    for s in range(B_CONV):
        shift = (B_CONV // 2 - s) % ext
        rolled = xe if shift == 0 else pltpu.roll(xe, shift, axis=0)
        acc = acc + rolled[HALO:HALO + PREP_ROWS] * conv_ref[s:s + 1, :]
    qkv = _silu(acc)
    ones = ones_ref[...]

    def l2n(t):
        return t * lax.rsqrt(_mm_exact(t * t, ones, terms=NORM_TERMS) + 1e-6)

    q = l2n(qkv[:, 0:w]) * (HEAD ** -0.5)
    k = l2n(qkv[:, w:2 * w])
    v = qkv[:, 2 * w:3 * w]
    sh_ref[:, 0:w] = q.astype(BF16)
    sh_ref[:, w:2 * w] = v.astype(BF16)
    sh_ref[:, 2 * w:3 * w] = k.astype(BF16)
    sr = p[:, 4 * w:4 * w + 128].astype(F32)
    beta_all = _sigmoid(sr)
    g_all = -jnp.exp(alog_ref[...]) * _softplus(sr + dt_ref[...])
    for d, (dk_ref, dw_ref) in enumerate(((dk0_ref, dw0_ref), (dk1_ref, dw1_ref))):
        beta = _mm_exact(beta_all, eb_ref[d])
        g = _mm_exact(g_all, ea_ref[d])
        kb = k * beta
        dw_ref[...] = g
        dk_ref[:, 0:w] = kb.astype(BF16)
        dk_ref[:, w:2 * w] = (-jnp.exp(g) * kb).astype(BF16)
    fin_ref[...] = p[:, 3 * w:4 * w].astype(F32)


SEG_GLA, SEG_GDN, SEG_ATT, SEG_RWKV = range(4)
N_RWKV_CONSTS, N_GDN_CONSTS = 11, 6
RWKV_OUTS = [(3, BF16), (2, BF16), (2, BF16), (1, F32), (1, F32), (2, F32)]
GDN_OUTS = [(3, BF16), (2, BF16), (2, BF16), (1, F32), (1, F32), (1, F32)]


def _stream_parts(xs):
    return list(xs) if isinstance(xs, (tuple, list)) else [xs]


def _stream_specs(xs, block_rows, halo_rows=0):
    specs, args, starts = [], [], []
    base = 0
    for part in _stream_parts(xs):
        d = part.shape[-1]
        nblk = part.shape[1] // block_rows
        own = lambda b, i, base=base, nblk=nblk: (b, jnp.clip(i - base, 0, nblk - 1), 0)
        specs.append(pl.BlockSpec((None, block_rows, d), own))
        args.append(part)
        if halo_rows:
            per, last = block_rows // halo_rows, part.shape[1] // halo_rows - 1
            prev = lambda b, i, base=base, per=per, last=last: (b, jnp.clip((i - base) * per - 1, 0, last), 0)
            nxt = lambda b, i, base=base, per=per, last=last: (b, jnp.clip((i - base + 1) * per, 0, last), 0)
            specs += [pl.BlockSpec((None, halo_rows, d), prev), pl.BlockSpec((None, halo_rows, d), nxt)]
            args += [part, part]
        starts.append(base)
        base += nblk
    return specs, args, starts


def _stream_block(refs, starts):
    val = refs[0][...]
    for ref, start in zip(refs[1:], starts[1:]):
        val = jnp.where(pl.program_id(1) >= start, ref[...], val)
    return val


def _front_kernel(*refs, n_lat_blocks, starts):
    it = iter(refs)
    x_refs = [next(it) for _ in range(3 * len(starts))]
    x_own, x_prev, x_next = (_stream_block(x_refs[k::3], starts) for k in range(3))
    sc_ref, csc_ref, sh_ref, csh_ref, g_ref = (next(it) for _ in range(5))
    w_refs = [next(it) for _ in range(4)]
    cos_ref, sin_ref = next(it), next(it)
    rwkv_consts = [next(it) for _ in range(N_RWKV_CONSTS)]
    gdn_consts = [next(it) for _ in range(N_GDN_CONSTS)]
    pc_ref, pd_ref = next(it), next(it)
    rwkv_outs = [next(it) for _ in RWKV_OUTS]
    gdn_outs = [next(it) for _ in GDN_OUTS]

    first, lastb = _seq_edges(n_lat_blocks)
    is_ctx = pl.program_id(1) >= n_lat_blocks
    scale = jnp.where(is_ctx, csc_ref[...], sc_ref[...])
    shift = jnp.where(is_ctx, csh_ref[...], sh_ref[...])
    h = _prenorm(x_own, g_ref[...], scale, shift).astype(BF16)
    halo = jnp.concatenate([x_prev, x_next], axis=0)
    h_halo = _prenorm(halo, g_ref[...], scale, shift).astype(BF16)

    def proj(rows, which):
        return jnp.dot(rows, w_refs[which][...], preferred_element_type=F32)

    pc_ref[...] = proj(h, SEG_GLA).astype(BF16)
    att = proj(h, SEG_ATT)
    w = BRANCH_W
    for c0 in range(0, att.shape[1], w):
        part = att[:, c0:c0 + w]
        if c0 < 2 * w:
            part = _rope(part, cos_ref[...], sin_ref[...])
        pd_ref[:, c0:c0 + w] = part.astype(BF16)

    h_ext = jnp.concatenate([h, h_halo], axis=0)
    n = h.shape[0]
    pb = proj(h_ext, SEG_GDN).astype(BF16)
    _gdn_prep(pb[0:n], pb[n:n + HALO], pb[n + HALO:], first, lastb, *gdn_consts, *gdn_outs)
    pa = proj(h_ext, SEG_RWKV).astype(BF16)
    _rwkv_prep(pa[0:n], pa[n:n + HALO], pa[n + HALO:], first, lastb, *rwkv_consts, *rwkv_outs)


def _front(xs, mod, gain, weights, cos, sin, rwkv_consts, gdn_consts, n_lat_rows):
    parts = _stream_parts(xs)
    batch, rows = parts[0].shape[0], sum(t.shape[1] for t in parts)
    x_specs, x_args, starts = _stream_specs(xs, PREP_ROWS, HALO)
    assert len(rwkv_consts) == N_RWKV_CONSTS and len(gdn_consts) == N_GDN_CONSTS and len(weights) == 4
    tab = pl.BlockSpec((PREP_ROWS, BRANCH_W), lambda b, i: (i, 0))
    consts = list(rwkv_consts) + list(gdn_consts)
    outs = [(weights[s].shape[1], BF16) for s in (SEG_GLA, SEG_ATT)]
    outs += [(n * BRANCH_W, dt) for n, dt in RWKV_OUTS + GDN_OUTS]
    out_specs = [pl.BlockSpec((None, PREP_ROWS, lanes), lambda b, i: (b, i, 0)) for lanes, _ in outs]
    out_shape = [jax.ShapeDtypeStruct((batch, rows, lanes), dt) for lanes, dt in outs]
    res = pl.pallas_call(
        functools.partial(_front_kernel, n_lat_blocks=n_lat_rows // PREP_ROWS, starts=tuple(starts)),
        grid=(batch, rows // PREP_ROWS),
        in_specs=x_specs + _mod_specs(1, batch) + _mod_specs(0, batch)
        + [_vec_spec(gain.shape)] + [_vec_spec(w.shape) for w in weights] + [tab, tab]
        + [_vec_spec(c.shape) for c in consts],
        out_specs=out_specs,
        out_shape=out_shape,
        compiler_params=_cparams(("parallel", "parallel")),
        name="front_proj_prep",
    )(*x_args, mod, mod, mod, mod, gain, *weights, cos, sin, *consts)
    pc, pd = res[0], res[1]
    a_sh, a_dk0, a_dk1, a_dw0, a_dw1, a_fin = res[2:8]
    b_sh, b_dk0, b_dk1, b_dw0, b_dw1, b_fin = res[8:14]
    return pc, pd, (a_sh, (a_dk0, a_dk1), (a_dw0, a_dw1), a_fin), (b_sh, (b_dk0, b_dk1), (b_dw0, b_dw1), b_fin)


NEG_BIG = -1e30


def _window_bias():
    iq = np.arange(2 * ATT_BLOCK)[:, None] % ATT_BLOCK
    ik = np.arange(ATT_BLOCK)[None, :]
    ok = np.stack([ik >= iq, np.ones_like(ik >= iq), ik <= iq])
    assert WINDOW == ATT_BLOCK
    return np.where(ok, 0.0, NEG_BIG).astype(np.float32)


ATT_STEP = 2


N_ATT_REFS = 5 + 2 * (ATT_STEP + 2)


def _attn_rows(refs, n_lat_rows):
    sink_ref, q_ref, kc_ref, vc_ref = refs[0:4]
    nband = ATT_STEP + 2
    k_all, v_all = refs[4:4 + nband], refs[4 + nband:4 + 2 * nband]
    bias_ref = refs[4 + 2 * nband]
    blk = ATT_BLOCK
    lane = lax.broadcasted_iota(jnp.int32, (blk, 2 * HEAD), 1)
    row2 = lax.broadcasted_iota(jnp.int32, (2 * blk, 1), 0)
    out_rows = []
    for u in range(ATT_STEP):
        out_cols = []
        t = pl.program_id(1) * ATT_STEP + u
        rows = slice(u * blk, (u + 1) * blk)
        q = q_ref[rows, :] * (HEAD ** -0.5)
        k_band, v_band = k_all[u:u + 3], v_all[u:u + 3]
        q_lat = t * blk < n_lat_rows
        in_seq = [q_lat & (t >= 1), q_lat, q_lat & ((t + 1) * blk < n_lat_rows)]
        bias = [jnp.where(in_seq[j], bias_ref[j], NEG_BIG) for j in range(3)]
        for g in range(D_KV_HEADS):
            cols = slice(g * 2 * HEAD, (g + 1) * 2 * HEAD)
            qg = q[:, cols]
            zero = jnp.zeros((), qg.dtype)
            qs = jnp.concatenate([jnp.where(lane < HEAD, qg, zero), jnp.where(lane >= HEAD, qg, zero)], axis=0)
            s_ctx = _mm(qs, kc_ref[:, cols], NT)
            s_band = [_mm(qs, k_band[j][:, cols], NT) + bias[j] for j in range(3)]
            sink = jnp.where(row2 < blk, sink_ref[2 * g], sink_ref[2 * g + 1])
            m = jnp.maximum(jnp.max(s_ctx, axis=-1, keepdims=True), sink)
            for s in s_band:
                m = jnp.maximum(m, jnp.max(s, axis=-1, keepdims=True))
            p_ctx = jnp.exp(s_ctx - m)
            den = jnp.sum(p_ctx, axis=-1, keepdims=True) + jnp.exp(sink - m)
            acc = _mm(p_ctx, vc_ref[:, cols], NN)
            for j in range(3):
                pj = jnp.exp(s_band[j] - m)
                den = den + jnp.sum(pj, axis=-1, keepdims=True)
                acc = acc + _mm(pj, v_band[j][:, cols], NN)
            og = acc / den
            out_cols.append(jnp.where(lane < HEAD, og[0:blk], og[blk:2 * blk]))
        out_rows.append(jnp.concatenate(out_cols, axis=1))
    return jnp.concatenate(out_rows, axis=0)


ATT_ROWS = ATT_STEP * ATT_BLOCK


def _attention_inputs(p, sink, n_lat_rows):
    _, rows, _ = p.shape
    blk = ATT_BLOCK
    nb = rows // blk
    w = BRANCH_W
    n_ctx_rows = rows - n_lat_rows
    assert n_lat_rows % n_ctx_rows == 0 and n_ctx_rows % ATT_ROWS == 0

    def band(col, off):
        def index(b, t):
            return (b, jnp.clip(t * ATT_STEP + off, 0, nb - 1), col)
        return pl.BlockSpec((None, blk, w), index)

    offsets = range(-1, ATT_STEP + 1)
    ctx = lambda col: pl.BlockSpec((None, n_ctx_rows, w), lambda b, t: (b, n_lat_rows // n_ctx_rows, col))
    bias = jnp.asarray(_window_bias())
    specs = [pl.BlockSpec(memory_space=pltpu.SMEM), pl.BlockSpec((None, ATT_ROWS, w), lambda b, t: (b, t, 0)),
             ctx(1), ctx(2)]
    specs += [band(1, o) for o in offsets] + [band(2, o) for o in offsets] + [_vec_spec(bias.shape)]
    args = [sink] + [p] * (3 + 2 * len(offsets)) + [bias]
    assert len(specs) == N_ATT_REFS
    return specs, args


def _merge_kernel(*refs, n_lat_rows, starts):
    x = _stream_block(refs[0:len(starts)], starts)
    refs = refs[len(starts):]
    sc_ref, csc_ref, sh_ref, csh_ref, gm_ref, cgm_ref, g0_ref, g1_ref, ya_ref, yb_ref, yc_ref = refs[0:11]
    att_refs = refs[11:11 + N_ATT_REFS]
    wg_ref, gb_ref, wb_ref, wo_ref, o_ref = refs[11 + N_ATT_REFS:]
    is_ctx = _ctx_rows(x.shape[0], n_lat_rows)
    scale = jnp.where(is_ctx, csc_ref[...], sc_ref[...])
    shift = jnp.where(is_ctx, csh_ref[...], sh_ref[...])
    gmod = jnp.where(is_ctx, cgm_ref[...], gm_ref[...])
    h = _prenorm(x, g0_ref[...], scale, shift).astype(BF16)
    branches = [ya_ref[...], yb_ref[...], yc_ref[...], _attn_rows(att_refs, n_lat_rows)]
    acc = jnp.zeros(x.shape, F32)
    for i, y in enumerate(branches):
        pre = jnp.dot(h, wg_ref[:, i * D_MODEL:(i + 1) * D_MODEL], preferred_element_type=F32)
        gate = _sigmoid(pre + gb_ref[i:i + 1, :])
        acc = acc + gate * jnp.dot(y.astype(BF16), wb_ref[i], preferred_element_type=F32)
    out = jnp.dot(acc.astype(BF16), wo_ref[...], preferred_element_type=F32)
    ms = jnp.mean(out * out, axis=-1, keepdims=True)
    o_ref[...] = x + gmod * (out * lax.rsqrt(ms + NORM_EPS) * g1_ref[...])


def _merge(xs, mod, gain0, gain1, ys, p_att, sink, wg, gate_b, wb, wo, n_lat_rows, out_rows):
    parts = _stream_parts(xs)
    batch, d = parts[0].shape[0], parts[0].shape[-1]
    rows = out_rows
    tm = ATT_ROWS
    assert rows % tm == 0
    x_specs, x_args, starts = _stream_specs(xs, tm)
    tile = lambda width: pl.BlockSpec((None, tm, width), lambda b, i: (b, i, 0))
    consts = [wg, gate_b, wb, wo]
    att_specs, att_args = _attention_inputs(p_att, sink, n_lat_rows)
    return pl.pallas_call(
        functools.partial(_merge_kernel, n_lat_rows=n_lat_rows, starts=tuple(starts)),
        grid=(batch, rows // tm),
        in_specs=x_specs + _mod_specs(1, batch) + _mod_specs(0, batch) + _mod_specs(2, batch)
        + [_vec_spec(gain0.shape), _vec_spec(gain1.shape)]
        + [tile(BRANCH_W)] * 3 + att_specs + [_vec_spec(c.shape) for c in consts],
        out_specs=tile(d),
        out_shape=jax.ShapeDtypeStruct((batch, rows, d), F32),
        compiler_params=_cparams(("parallel", "parallel")),
        name="attn_merge_out",
    )(*x_args, *([mod] * 6), gain0, gain1, *ys, *att_args, *consts)


FFN_CHUNK = 512


def _ffn_kernel(x_ref, sc_ref, csc_ref, sh_ref, csh_ref, gm_ref, cgm_ref, g2_ref, g3_ref, w1_ref, w2_ref,
                o_ref, *, n_lat_rows):
    x = x_ref[...]
    is_ctx = _ctx_rows(x.shape[0], n_lat_rows)
    scale = jnp.where(is_ctx, csc_ref[...], sc_ref[...])
    shift = jnp.where(is_ctx, csh_ref[...], sh_ref[...])
    h = _prenorm(x, g2_ref[...], scale, shift).astype(BF16)
    hidden = w2_ref.shape[0]
    out = jnp.zeros(x.shape, F32)
    for lo in range(0, hidden, FFN_CHUNK):
        hi = min(lo + FFN_CHUNK, hidden)
        gt = jnp.dot(h, w1_ref[:, lo:hi], preferred_element_type=F32)
        up = jnp.dot(h, w1_ref[:, hidden + lo:hidden + hi], preferred_element_type=F32)
        out = out + jnp.dot((_silu(gt) * up).astype(BF16), w2_ref[lo:hi, :], preferred_element_type=F32)
    ms = jnp.mean(out * out, axis=-1, keepdims=True)
    gmod = jnp.where(is_ctx, cgm_ref[...], gm_ref[...])
    o_ref[...] = x + gmod * (out * lax.rsqrt(ms + NORM_EPS) * g3_ref[...])


def _ffn(xs, mod, gain2, gain3, w1, w2, n_lat_rows):
    batch, rows, d = xs.shape
    tm = _row_tile(rows, 544)
    tile = pl.BlockSpec((None, tm, d), lambda b, i: (b, i, 0))
    consts = [gain2, gain3, w1, w2]
    return pl.pallas_call(
        functools.partial(_ffn_kernel, n_lat_rows=n_lat_rows),
        grid=(batch, rows // tm),
        in_specs=[tile] + _mod_specs(4, batch) + _mod_specs(3, batch) + _mod_specs(5, batch)
        + [_vec_spec(c.shape) for c in consts],
        out_specs=tile,
        out_shape=jax.ShapeDtypeStruct((batch, rows, d), F32),
        compiler_params=_cparams(("parallel", "parallel")),
        name="swiglu",
    )(xs, *([mod] * 6), *consts)


def _pad_cols(w, width):
    return jnp.pad(w, ((0, 0), (0, width - w.shape[1])))


def _layer_weights(w_in, mu, w2, a2, g2, conv, a_log, dt_bias, gw2, gb):
    wts = {}
    off_b, off_c, off_d, off_g = A_IN, A_IN + B_IN, A_IN + B_IN + C_IN, A_IN + B_IN + C_IN + D_IN
    w = BRANCH_W
    wts["wa"] = _pad_cols(w_in[:, 0:A_IN], 4 * w).astype(BF16)
    wts["mu"] = _pad_cols(mu[None, :], 4 * w)
    lo = np.cumsum((0,) + A_LORA)
    place = lambda m, r0: jnp.zeros((w, w), F32).at[r0:r0 + m.shape[0], :].set(m)
    wts["w2p"] = jnp.stack([place(w2[0], lo[0]), place(w2[1], lo[1])])
    wts["a2p"] = jnp.stack([place(a2[0], lo[2]), place(a2[1], lo[3])])
    wts["g2p"] = place(g2, lo[4])

    wb = w_in[:, off_b:off_b + B_IN]
    wts["wb"] = jnp.concatenate([wb[:, 0:3 * w], wb[:, 3 * w + 16:], _pad_cols(wb[:, 3 * w:3 * w + 16], 128)],
                                axis=1).astype(BF16)
    nh = N_HEADS
    expand = np.zeros((4, 128, w), np.float32)
    for grp in range(4):
        for h in range(nh):
            expand[grp, grp * nh + h, h * HEAD:(h + 1) * HEAD] = 1.0
    wts["eb"] = jnp.asarray(expand[0:2])
    wts["ea"] = jnp.asarray(expand[2:4])
    vec = lambda t: jnp.zeros((1, 128), F32).at[0, 2 * nh:4 * nh].set(t.reshape(-1))
    wts["alog"] = vec(a_log)
    wts["dt"] = vec(dt_bias)
    wts["conv"] = conv

    wc = w_in[:, off_c:off_c + C_IN]
    pad_heads = lambda m: jnp.pad(m.reshape(m.shape[0], nh, C_DK),
                                  ((0, 0), (0, 0), (0, HEAD - C_DK))).reshape(m.shape[0], w)
    qc, kc, vc = wc[:, 0:C_QK], wc[:, C_QK:2 * C_QK], wc[:, 2 * C_QK:2 * C_QK + w]
    loc = wc[:, 2 * C_QK + w:2 * C_QK + w + 2 * C_GATE_R]
    gc = wc[:, 2 * C_QK + w + 2 * C_GATE_R:]
    wts["wc"] = jnp.concatenate([pad_heads(qc), pad_heads(kc), vc, gc, _pad_cols(loc, 128)], axis=1).astype(BF16)
    gwp = jnp.zeros((2, 128, w), F32)
    for d in range(2):
        gwp = gwp.at[d, d * C_GATE_R:(d + 1) * C_GATE_R, :].set(pad_heads(gw2[d]))
    wts["gwp"] = gwp
    wts["gbp"] = pad_heads(gb)
    wts["glane"] = jnp.asarray((np.arange(w) % HEAD < C_DK).astype(np.float32))[None, :]

    wd = w_in[:, off_d:off_d + D_IN]
    qd = wd[:, 0:w]
    dup = lambda m: jnp.concatenate([m[:, 0:HEAD], m[:, 0:HEAD], m[:, HEAD:], m[:, HEAD:]], axis=1)
    wts["wd"] = jnp.concatenate([qd, dup(wd[:, w:w + 2 * HEAD]), dup(wd[:, w + 2 * HEAD:])], axis=1).astype(BF16)
    wts["wg"] = w_in[:, off_g:].astype(BF16)
    return wts


def _rope_tables(n_lat_rows, n_ctx_rows):
    quarter = HEAD // 4
    inv = ROPE_BASE ** (-np.arange(quarter, dtype=np.float32) / quarter)
    pos = np.arange(n_lat_rows)
    rows = (pos // GRID_W).astype(np.float32)
    cols = (pos % GRID_W).astype(np.float32)
    inv = jnp.asarray(inv)
    ang_r = jnp.asarray(rows)[:, None] * inv[None, :]
    ang_c = jnp.asarray(cols)[:, None] * inv[None, :]
    cos = jnp.concatenate([jnp.cos(ang_r)] * 2 + [jnp.cos(ang_c)] * 2, axis=1)
    sin = jnp.concatenate([-jnp.sin(ang_r), jnp.sin(ang_r), -jnp.sin(ang_c), jnp.sin(ang_c)], axis=1)
    cos = jnp.concatenate([cos, jnp.ones((n_ctx_rows, HEAD), F32)], axis=0)
    sin = jnp.concatenate([sin, jnp.zeros((n_ctx_rows, HEAD), F32)], axis=0)
    return jnp.tile(cos, (1, N_HEADS)), jnp.tile(sin, (1, N_HEADS))


RWKV_SLOTS = {"r": ("sh", 0), "v": ("sh", 1), "a": ("sh", 2), "w": ("dw", 0), "k": ("dk", 0), "b": ("dk", 1)}
GDN_SLOTS = RWKV_SLOTS
GLA_SLOTS = {"r": ("sh", 0), "k": ("sh", 1), "v": ("sh", 2)}


def kernel(x, c, ctx, c_ctx, ada_w, ada_b, norm_g, w_in, gate_b, w_branch, w_out, rwkv_mu, rwkv_w0, rwkv_w2, rwkv_a0, rwkv_a2, rwkv_g2, rwkv_kk, rwkv_ka, rwkv_rk, rwkv_ln_g, rwkv_ln_b, gdn_conv, gdn_a_log, gdn_dt_bias, gdn_norm_g, gla_gw2, gla_gb, gla_norm_g, attn_sink, ffn_w1, ffn_w2):
    batch, n_lat, d = x.shape
    n_ctx = ctx.shape[1]
    depth = ada_w.shape[0]
    assert n_ctx % PREP_ROWS == 0 and n_lat % PREP_ROWS == 0 and d == D_MODEL

    mod_rows = 8 * ((batch + 1 + 7) // 8)
    c_rows = jnp.concatenate([c, c_ctx[None, :], jnp.zeros((mod_rows - batch - 1, d), F32)], axis=0)
    mod_all = _modulation(c_rows, ada_w, ada_b)
    cos, sin = _rope_tables(n_lat, n_ctx)

    xs = (x, ctx)
    rows = n_lat + n_ctx
    row = lambda t: t.reshape(1, -1)
    for l in range(depth):
        out_rows = n_lat if l == depth - 1 else rows
        mod = mod_all[l].reshape(mod_rows, 1, 6 * d)
        ng = norm_g[l]
        wts = _layer_weights(w_in[l], rwkv_mu[l], rwkv_w2[l], rwkv_a2[l], rwkv_g2[l], gdn_conv[l],
                             gdn_a_log[l], gdn_dt_bias[l], gla_gw2[l], gla_gb[l])
        ones = jnp.asarray(_head_block_ones())
        around = jnp.asarray(0.5 * (_shift_matrix(-1) + _shift_matrix(1)), BF16)
        rwkv_consts = [around, wts["mu"], wts["w2p"], wts["a2p"], wts["g2p"], rwkv_w0[l], rwkv_a0[l],
                       row(rwkv_kk[l]), row(rwkv_ka[l]), row(rwkv_rk[l]), ones]
        gdn_consts = [wts["conv"], wts["alog"], wts["dt"], wts["eb"], wts["ea"], ones]
        p_gla, p_att, rwkv_ops, gdn_ops = _front(xs, mod, row(ng[0]), (wts["wc"], wts["wb"], wts["wd"], wts["wa"]),
                                                 cos, sin, rwkv_consts, gdn_consts, n_lat)

        sh, dks, dws, fin = rwkv_ops
        ya = _bidir_scan(sh, dks, dws, n_lat, fin, row(rwkv_ln_g[l]), row(rwkv_ln_b[l]), finish="groupnorm",
                         scalar_decay=False, lowrank=True, slots=RWKV_SLOTS)

        sh, dks, dws, fin = gdn_ops
        gnorm = row(jnp.tile(gdn_norm_g[l], N_HEADS))
        yb = _bidir_scan(sh, dks, dws, n_lat, fin, gnorm, gnorm, finish="rms",
                         scalar_decay=True, lowrank=True, slots=GDN_SLOTS)

        yc = _gla_scan(p_gla, n_lat, wts["gwp"], wts["gbp"], wts["glane"], row(jnp.tile(gla_norm_g[l], N_HEADS)))

        xs = _merge(xs, mod, row(ng[0]), row(ng[1]), (ya, yb, yc), p_att, attn_sink[l], wts["wg"], gate_b[l],
                    w_branch[l].astype(BF16), w_out[l].astype(BF16), n_lat, out_rows)
        xs = _ffn(xs, mod, row(ng[2]), row(ng[3]), ffn_w1[l].astype(BF16), ffn_w2[l].astype(BF16), n_lat)
    return xs
```

```python
import functools
import math

import numpy as np
import jax
import jax.numpy as jnp
from jax import lax
from jax.experimental import pallas as pl
from jax.experimental.pallas import tpu as pltpu

F32 = jnp.float32
BF16 = jnp.bfloat16

D_MODEL = 1024
N_BRANCH = 4
BRANCH_W = 256
HEAD = 64
N_HEADS = BRANCH_W // HEAD
NORM_EPS = 1e-6
A_GN_EPS = 64e-5
A_LORA = (32, 32, 32, 32, 64)
A_IN = 3 * BRANCH_W + sum(A_LORA)
B_IN = 4 * BRANCH_W + 4 * N_HEADS
C_DK = 32
C_QK = N_HEADS * C_DK
C_GATE_R = 16
C_GATE_NORM = 16.0
C_IN = 2 * C_QK + 2 * BRANCH_W + 2 * C_GATE_R
D_KV_HEADS = 2
D_IN = BRANCH_W + 2 * D_KV_HEADS * HEAD
B_CONV = 7
WINDOW = 128
ROPE_BASE = 10000.0
GRID_W = 64
FFN_HIDDEN = 2816

LANES = 128
CHUNK = 64
PREP_ROWS = 256
HALO = 16
ATT_BLOCK = 128
VMEM_LIMIT = 48 * 1024 * 1024
NEG_BIG = -1e30

NN = (((1,), (0,)), ((), ()))
NT = (((1,), (1,)), ((), ()))
TN = (((0,), (0,)), ((), ()))


def _mm(a, b, dims=NN, mode="bf16"):
    if mode == "f32":
        return lax.dot_general(a, b, dims, precision=lax.Precision.HIGHEST, preferred_element_type=F32)
    return lax.dot_general(a.astype(BF16), b.astype(BF16), dims, preferred_element_type=F32)


def _sigmoid(x):
    return 1.0 / (1.0 + jnp.exp(-x))


def _silu(x):
    return x * _sigmoid(x)


def _softplus(x):
    return jnp.maximum(x, 0.0) + jnp.log1p(jnp.exp(-jnp.abs(x)))


def _cparams(sem):
    return pltpu.CompilerParams(dimension_semantics=sem, vmem_limit_bytes=VMEM_LIMIT)


def _mod_kernel(c_ref, w_ref, b_ref, o_ref):
    c = c_ref[...]
    o_ref[...] = _mm(_silu(c), w_ref[...], mode="f32") + b_ref[...]


def _modulation(c_rows, ada_w, ada_b):
    depth, d, n = ada_w.shape
    rows = c_rows.shape[0]
    tn = 1024
    return pl.pallas_call(
        _mod_kernel,
        grid=(depth, n // tn),
        in_specs=[
            pl.BlockSpec((rows, d), lambda l, j: (0, 0)),
            pl.BlockSpec((None, d, tn), lambda l, j: (l, 0, j)),
            pl.BlockSpec((None, 1, tn), lambda l, j: (l, 0, j)),
        ],
        out_specs=pl.BlockSpec((None, rows, tn), lambda l, j: (l, 0, j)),
        out_shape=jax.ShapeDtypeStruct((depth, rows, n), F32),
        compiler_params=_cparams(("parallel", "parallel")),
        name="adaln_mod",
    )(c_rows, ada_w, ada_b.reshape(depth, 1, n))


def _row_tile(rows, target):
    return max(t for t in range(8, target + 1, 8) if rows % t == 0)


def _mod_specs(which, batch):
    lat = pl.BlockSpec((None, 1, D_MODEL), lambda b, i, *_: (b, 0, which))
    ctx = pl.BlockSpec((None, 1, D_MODEL), lambda b, i, *_: (batch, 0, which))
    return [lat, ctx]


def _ctx_rows(tm, n_lat_rows):
    row = pl.program_id(1) * tm + lax.broadcasted_iota(jnp.int32, (tm, 1), 0)
    return row >= n_lat_rows


def _prenorm(x, gain, scale, shift):
    ms = jnp.mean(x * x, axis=-1, keepdims=True)
    y = x * lax.rsqrt(ms + NORM_EPS) * gain
    return y * (1.0 + scale) + shift


def _rope(x, cos, sin):
    width = x.shape[-1]
    lane = lax.broadcasted_iota(jnp.int32, x.shape, 1)
    half, quarter = HEAD // 2, HEAD // 4
    swapped = jnp.where(lane % half < quarter, pltpu.roll(x, width - quarter, axis=1), pltpu.roll(x, quarter, axis=1))
    return x * cos + swapped * sin


def _head_block_ones():
    idx = np.arange(BRANCH_W)
    return (idx[:, None] // HEAD == idx[None, :] // HEAD).astype(np.float32)


N_LEVELS = 6


def _level_mask(ri, ci, s, reverse):
    b = 1 << s
    blk = (ri // (2 * b)) == (ci // (2 * b))
    hi_r, hi_c = (ri // b) % 2 == 1, (ci // b) % 2 == 1
    return blk & ((hi_c & ~hi_r) if reverse else (hi_r & ~hi_c))


SCAN_HEADS = 2
GROUP_W = SCAN_HEADS * HEAD
N_GROUPS = N_HEADS // SCAN_HEADS


def _scan_masks(reverse):
    n = SCAN_HEADS * CHUNK
    ri = np.arange(CHUNK)[:, None]
    ci = np.arange(n)[None, :] % CHUNK
    strict, incl = (ci > ri, ci >= ri) if reverse else (ci < ri, ci <= ri)
    compact = [strict, incl, ci == ri] + [_level_mask(ri, ci, s, reverse) for s in range(N_LEVELS)]
    r = np.arange(n)[:, None]
    c = np.arange(n)[None, :]
    block = [(r // CHUNK) == (c // CHUNK), r == c]
    return np.stack(compact).astype(np.float32), np.stack(block).astype(np.float32)


EXACT_TERMS = 3
NORM_TERMS = 2


def _tri(reverse):
    i = np.arange(CHUNK)
    m = (i[None, :] >= i[:, None]) if reverse else (i[None, :] <= i[:, None])
    return np.tile(m.astype(np.float32), (1, EXACT_TERMS))


def _split_terms(x, terms):
    out = []
    for _ in range(terms - 1):
        part = x.astype(BF16)
        out.append(part)
        x = x - part.astype(F32)
    out.append(x.astype(BF16))
    return out


def _mm_exact(a, b, dims=NN, split="a", terms=EXACT_TERMS):
    assert dims == NN
    if split == "a":
        lhs = jnp.concatenate(_split_terms(a, terms), axis=1)
        rhs = jnp.concatenate([b.astype(BF16)] * terms, axis=0)
    else:
        lhs = a.astype(BF16)
        rhs = jnp.concatenate(_split_terms(b, terms), axis=0)
    return lax.dot_general(lhs, rhs, dims, preferred_element_type=F32)


C_STRICT, C_INCL, C_EYE, C_LVL0 = 0, 1, 2, 3
B_SAME, B_EYE = 0, 1
SCAN_BATCH = 8
SCAN_ROWS = 256
assert CHUNK == HEAD and (1 << N_LEVELS) == CHUNK


def _scan_chunk(vals, st, tri, ones, cm_ref, bm_ref, same_bf, *, reverse, scalar_decay, lowrank, mm_mode):
    same = bm_ref[B_SAME]

    def get(name):
        return vals[name]

    def expand(x):
        return jnp.concatenate([x.astype(BF16)] * SCAN_HEADS, axis=0) * same_bf

    def keep(x, k):
        return jnp.where(cm_ref[k] > 0.5, x, 0.0)

    logw = get("w")
    r, k, v = get("r"), get("k"), get("v")
    cum = _mm_exact(tri, logw, split="b")
    cum_x = cum - logw
    last = 0 if reverse else CHUNK - 1
    total = cum[last:last + 1, :]

    if scalar_decay:
        diag = jnp.concatenate([cum] * SCAN_HEADS, axis=0) * bm_ref[B_EYE]
        cum_row = _mm_exact(ones, diag, split="b")
        d_ii = jnp.exp(jnp.where(cm_ref[C_INCL] > 0.5, cum - cum_row, NEG_BIG))
        r_q, k_q = r, k
        to_end = jnp.exp(total - cum)
        r_abs = r * jnp.exp(cum)
    else:
        ref_row = cum[CHUNK // 2:CHUNK // 2 + 1, :]
        e_ref = jnp.exp(ref_row)
        p_inv = jnp.exp(ref_row - cum)
        r_q, k_q = r * jnp.exp(cum - ref_row), k * p_inv
        to_end = p_inv * jnp.exp(total - ref_row)
        r_abs = r_q * e_ref

    k_e, v_e = expand(k_q), expand(v)
    if not lowrank:
        s_k = _mm(r_q, k_e, NT, mm_mode)
        a_rk = s_k * d_ii if scalar_decay else keep(s_k, C_INCL)
        y = _mm(r_abs, st, NT, mm_mode) + _mm(a_rk, v_e, NN, mm_mode)
        upd = _mm(v, k * to_end, TN, mm_mode)
        return y, st * jnp.exp(total) + upd * same

    a, b = get("a"), get("b")
    if scalar_decay:
        d_xi = jnp.exp(jnp.where(cm_ref[C_STRICT] > 0.5, cum_x - cum_row, NEG_BIG))
        a_q, b_q = a, b
    else:
        a_q, b_q = a * jnp.exp(cum_x - ref_row), b * p_inv
    lhs = jnp.concatenate([a_q, r_q], axis=0)
    scores = _mm(lhs, jnp.concatenate([expand(b_q), k_e], axis=0), NT, mm_mode)
    s_b, s_k = scores[:, 0:GROUP_W], scores[:, GROUP_W:]
    if scalar_decay:
        a_ab, a_rb = s_b[0:CHUNK] * d_xi, s_b[CHUNK:] * d_ii
        a_ak, a_rk = s_k[0:CHUNK] * d_xi, s_k[CHUNK:] * d_ii
    else:
        a_ab, a_rb = keep(s_b[0:CHUNK], C_STRICT), keep(s_b[CHUNK:], C_INCL)
        a_ak, a_rk = keep(s_k[0:CHUNK], C_STRICT), keep(s_k[CHUNK:], C_INCL)

    inv = cm_ref[C_EYE] + a_ab * cm_ref[C_LVL0]
    for s in range(1, N_LEVELS):
        c_s = expand(a_ab * cm_ref[C_LVL0 + s])
        inv = inv + _mm(_mm(inv, c_s, NN, mm_mode), expand(inv), NN, mm_mode)

    a_abs = a * jnp.exp(cum_x) if scalar_decay else a_q * e_ref
    from_state = _mm(jnp.concatenate([a_abs, r_abs], axis=0), st, NT, mm_mode)
    from_v = _mm(jnp.concatenate([a_ak, a_rk], axis=0), v_e, NN, mm_mode)
    both = from_state + from_v
    z = _mm(inv, expand(both[0:CHUNK]), NN, mm_mode)
    y = both[CHUNK:] + _mm(a_rb, expand(z), NN, mm_mode)
    upd = _mm(jnp.concatenate([v, z], axis=0), jnp.concatenate([k * to_end, b * to_end], axis=0), TN, mm_mode)
    return y, st * jnp.exp(total) + upd * same


def _scan_kernel(*refs, reverse, scalar_decay, lowrank, slots, sources, finish, mm_mode, lora, r_scale):
    it = iter(refs)
    src_refs = {name: next(it) for name in sources}
    sh_ref = src_refs[sources[0]]
    cm_ref, bm_ref, tri_ref = next(it), next(it), next(it)
    if lora:
        lo_ref, gw_ref, gb_ref, lane_ref = next(it), next(it), next(it), next(it)
    if finish:
        ob_ref, fin_ref, p1_ref, p2_ref, avg_ref = next(it), next(it), next(it), next(it), next(it)
    o_ref, st_ref = next(it), next(it)

    @pl.when(pl.program_id(1) == 0)
    def _():
        st_ref[...] = jnp.zeros_like(st_ref)

    nb, block_rows = sh_ref.shape[0], sh_ref.shape[1]
    n_chunks = block_rows // CHUNK
    n = nb * N_GROUPS
    tri = jnp.broadcast_to(tri_ref[...], (n,) + tri_ref.shape)
    ones = jnp.ones((n, CHUNK, EXACT_TERMS * SCAN_HEADS * CHUNK), F32)
    chunk = functools.partial(_scan_chunk, cm_ref=cm_ref, bm_ref=bm_ref, same_bf=bm_ref[B_SAME].astype(BF16),
                              reverse=reverse, scalar_decay=scalar_decay, lowrank=lowrank, mm_mode=mm_mode)

    def step(c, carry):
        rows = pl.ds(pl.multiple_of((n_chunks - 1 - c if reverse else c) * CHUNK, CHUNK), CHUNK)

        def groups(name):
            src, idx = slots[name]
            ref = src_refs[src]
            parts = [ref[:, rows, idx * BRANCH_W + g * GROUP_W:idx * BRANCH_W + (g + 1) * GROUP_W].astype(F32)
                     for g in range(N_GROUPS)]
            return jnp.stack(parts, axis=1).reshape(n, CHUNK, GROUP_W)

        vals = {name: groups(name) for name in slots}
        if lora:
            lo = lo_ref[:, rows, :].reshape(nb * CHUNK, lo_ref.shape[-1])
            z = _mm(lo, gw_ref[...], mode=mm_mode) + gb_ref[...]
            logw = ((-_softplus(-z) / C_GATE_NORM) * lane_ref[...]).reshape(nb, CHUNK, BRANCH_W)
            parts = [logw[:, :, g * GROUP_W:(g + 1) * GROUP_W] for g in range(N_GROUPS)]
            vals["w"] = jnp.stack(parts, axis=1).reshape(n, CHUNK, GROUP_W)
        if r_scale != 1.0:
            vals["r"] = vals["r"] * r_scale
        y, st_new = jax.vmap(chunk)(vals, st_ref[...], tri, ones)
        st_ref[...] = st_new
        y = y.reshape(nb, N_GROUPS, CHUNK, GROUP_W)
        y = jnp.concatenate([y[:, g] for g in range(N_GROUPS)], axis=-1)
        if not finish:
            o_ref[:, rows, :] = y
            return carry

        y = (y + ob_ref[:, rows, :]).reshape(nb * CHUNK, BRANCH_W)
        avg = avg_ref[...]
        if finish == "groupnorm":
            gate = fin_ref[:, rows, 0:BRANCH_W].reshape(y.shape)
            bonus = fin_ref[:, rows, BRANCH_W:2 * BRANCH_W].reshape(y.shape)
            cen = y - _mm_exact(y, avg)
            var = _mm_exact(cen * cen, avg)
            yn = cen * lax.rsqrt(var + A_GN_EPS) * p1_ref[...] + p2_ref[...]
            out = (yn + bonus) * gate
        else:
            ms = _mm_exact(y * y, avg)
            gate = fin_ref[:, rows, :].astype(F32).reshape(y.shape)
            out = y * lax.rsqrt(ms + NORM_EPS) * p1_ref[...] * _silu(gate)
        o_ref[:, rows, :] = out.reshape(nb, CHUNK, BRANCH_W)
        return carry

    lax.fori_loop(0, n_chunks, step, 0)


def _view(x):
    return x if isinstance(x, tuple) else (x, x.shape[-1], 0)


def _scan(srcs, n_lat_rows, *, reverse, scalar_decay, lowrank, slots, finish=None, fin_args=None,
          mm_mode="bf16", lora=None, r_scale=1.0):
    sources = tuple(srcs)
    views = [_view(srcs[s]) for s in sources]
    batch, rows, _ = views[0][0].shape
    nc, nlat = rows // SCAN_ROWS, n_lat_rows // SCAN_ROWS
    nctx = nc - nlat
    nb = math.gcd(batch, SCAN_BATCH)

    if reverse:
        def chunk(n):
            return nc - 1 - n
    else:
        def chunk(n):
            return jnp.where(n < nctx, nlat + n, n - nctx)

    def row_spec(width, col=0):
        return pl.BlockSpec((nb, SCAN_ROWS, width), lambda b, n: (b, chunk(n), col))

    def const_spec(shape):
        zeros = (0,) * len(shape)
        return pl.BlockSpec(shape, lambda b, n: zeros)

    cmask, bmask = (jnp.asarray(m) for m in _scan_masks(reverse))
    tri = jnp.asarray(_tri(reverse))
    in_specs = [row_spec(w, c) for _, w, c in views]
    in_specs += [const_spec(cmask.shape), const_spec(bmask.shape), const_spec(tri.shape)]
    args = [a for a, _, _ in views] + [cmask, bmask, tri]
    if lora:
        (lo, lo_w, lo_c), gw, gb, lane = _view(lora[0]), lora[1], lora[2], lora[3]
        in_specs += [row_spec(lo_w, lo_c), const_spec(gw.shape), const_spec(gb.shape), const_spec(lane.shape)]
        args += [lo, gw, gb, lane]
    if finish:
        ob, fin, p1, p2 = fin_args
        fin, fin_w, fin_c = _view(fin)
        avg = jnp.asarray(_head_block_ones() / HEAD)
        in_specs += [row_spec(BRANCH_W), row_spec(fin_w, fin_c), const_spec(p1.shape), const_spec(p2.shape),
                     const_spec(avg.shape)]
        args += [ob, fin, p1, p2, avg]
    kern = functools.partial(_scan_kernel, reverse=reverse, scalar_decay=scalar_decay, lowrank=lowrank,
                             slots=slots, sources=sources, finish=finish, mm_mode=mm_mode, lora=bool(lora),
                             r_scale=r_scale)
    return pl.pallas_call(
        kern,
        grid=(batch // nb, nc),
        in_specs=in_specs,
        out_specs=row_spec(BRANCH_W),
        out_shape=jax.ShapeDtypeStruct((batch, rows, BRANCH_W), F32),
        scratch_shapes=[pltpu.VMEM((nb * N_GROUPS, GROUP_W, GROUP_W), F32)],
        compiler_params=_cparams(("parallel", "arbitrary")),
        name="dplr_scan_" + ("bwd" if reverse else "fwd"),
    )(*args)


def _gla_scan(p, n_lat_rows, gwp, gbp, lane, norm_g):
    w = BRANCH_W
    srcs = {"sh": (p, 3 * w, 0)}
    lo = (p, gwp.shape[1], 4 * w // gwp.shape[1])
    kw = dict(scalar_decay=False, lowrank=False, slots=GLA_SLOTS, r_scale=C_DK ** -0.5)
    ob = _scan(srcs, n_lat_rows, reverse=True, lora=(lo, gwp[1], gbp[1:2], lane), **kw)
    return _scan(srcs, n_lat_rows, reverse=False, lora=(lo, gwp[0], gbp[0:1], lane), finish="rms",
                 fin_args=(ob, (p, w, 3), norm_g, norm_g), **kw)


def _bidir_scan(sh, dks, dws, n_lat_rows, fin, p1, p2, *, finish, **kw):
    def srcs(d):
        out = {"sh": sh, "dw": dws[d]}
        if dks is not None:
            out["dk"] = dks[d]
        return out

    ob = _scan(srcs(1), n_lat_rows, reverse=True, **kw)
    return _scan(srcs(0), n_lat_rows, reverse=False, finish=finish, fin_args=(ob, fin, p1, p2), **kw)


def _seq_edges(n_lat_blocks):
    i = pl.program_id(1)
    first = (i == 0) | (i == n_lat_blocks)
    lastb = (i == n_lat_blocks - 1) | (i == pl.num_programs(1) - 1)
    return first, lastb


def _vec_spec(shape):
    zeros = (0,) * len(shape)
    return pl.BlockSpec(shape, lambda b, i: zeros)


def _shift_matrix(offset):
    m = np.zeros((PREP_ROWS, PREP_ROWS + 2 * HALO), np.float32)
    t = np.arange(PREP_ROWS)
    src = t + offset
    col = np.where(src < 0, PREP_ROWS + HALO + src, np.where(src >= PREP_ROWS, HALO + src, src))
    m[t, col] = 1.0
    return m


def _rwkv_prep(x, prev, nxt, first, lastb, shift_ref, mu_ref, w2_ref, a2_ref, g2_ref, w0_ref, a0_ref, kk_ref,
               ka_ref, rk_ref, ones_ref, sh_ref, dk0_ref, dk1_ref, dw0_ref, dw1_ref, fin_ref, mode="bf16"):
    zero = jnp.zeros((), x.dtype)
    xe = jnp.concatenate([x, jnp.where(first, zero, prev), jnp.where(lastb, zero, nxt)], axis=0)
    around = jnp.dot(shift_ref[...], xe, preferred_element_type=F32)
    x = x.astype(F32)
    xm = x + (around - x) * mu_ref[...]
    w = BRANCH_W
    r, k, v, lo = xm[:, 0:w], xm[:, w:2 * w], xm[:, 2 * w:3 * w], xm[:, 3 * w:4 * w]
    ones = ones_ref[...]
    th, sg = jnp.tanh(lo), _sigmoid(lo)
    gate = _mm(sg, g2_ref[...], mode=mode)
    kx = k * kk_ref[...]
    kk = kx * lax.rsqrt(_mm_exact(kx * kx, ones, terms=NORM_TERMS) + 1e-6)
    sh_ref[:, 0:w] = r.astype(BF16)
    sh_ref[:, w:2 * w] = v.astype(BF16)
    sh_ref[:, 2 * w:3 * w] = (-kk).astype(BF16)
    bonus = jnp.zeros_like(v)
    for d, (dk_ref, dw_ref) in enumerate(((dk0_ref, dw0_ref), (dk1_ref, dw1_ref))):
        w_raw = w0_ref[d:d + 1, :] + _mm(th, w2_ref[d], mode=mode)
        dw_ref[...] = -math.exp(-0.5) * _sigmoid(w_raw)
        a = _sigmoid(a0_ref[d:d + 1, :] + _mm(lo, a2_ref[d], mode=mode))
        kd = k * (1.0 + (a - 1.0) * ka_ref[...])
        dk_ref[:, 0:w] = kd.astype(BF16)
        dk_ref[:, w:2 * w] = (kk * a).astype(BF16)
        bonus = bonus + _mm_exact(r * kd * rk_ref[...], ones) * v
    fin_ref[:, 0:w] = gate
    fin_ref[:, w:2 * w] = bonus


def _gdn_prep(p, prev, nxt, first, lastb, conv_ref, alog_ref, dt_ref, eb_ref, ea_ref, ones_ref,
              sh_ref, dk0_ref, dk1_ref, dw0_ref, dw1_ref, fin_ref):
    w = BRANCH_W
    x = p[:, 0:3 * w].astype(F32)
    top = jnp.where(first, 0.0, prev[:, 0:3 * w].astype(F32))
    bot = jnp.where(lastb, 0.0, nxt[:, 0:3 * w].astype(F32))
    xe = jnp.concatenate([top, x, bot], axis=0)
    ext = PREP_ROWS + 2 * HALO
    acc = jnp.zeros_like(x)
    for s in range(B_CONV):
        shift = (B_CONV // 2 - s) % ext
        rolled = xe if shift == 0 else pltpu.roll(xe, shift, axis=0)
        acc = acc + rolled[HALO:HALO + PREP_ROWS] * conv_ref[s:s + 1, :]
    qkv = _silu(acc)
    ones = ones_ref[...]

    def l2n(t):
        return t * lax.rsqrt(_mm_exact(t * t, ones, terms=NORM_TERMS) + 1e-6)

    q = l2n(qkv[:, 0:w]) * (HEAD ** -0.5)
    k = l2n(qkv[:, w:2 * w])
    v = qkv[:, 2 * w:3 * w]
    sh_ref[:, 0:w] = q.astype(BF16)
    sh_ref[:, w:2 * w] = v.astype(BF16)
    sh_ref[:, 2 * w:3 * w] = k.astype(BF16)
    sr = p[:, 4 * w:4 * w + LANES].astype(F32)
    beta_all = _sigmoid(sr)
    g_all = -jnp.exp(alog_ref[...]) * _softplus(sr + dt_ref[...])
    for d, (dk_ref, dw_ref) in enumerate(((dk0_ref, dw0_ref), (dk1_ref, dw1_ref))):
        beta = _mm_exact(beta_all, eb_ref[d])
        g = _mm_exact(g_all, ea_ref[d])
        kb = k * beta
        dw_ref[...] = g
        dk_ref[:, 0:w] = kb.astype(BF16)
        dk_ref[:, w:2 * w] = (-jnp.exp(g) * kb).astype(BF16)
    fin_ref[...] = p[:, 3 * w:4 * w].astype(F32)


SEG_GLA, SEG_GDN, SEG_ATT, SEG_RWKV = range(4)
N_RWKV_CONSTS, N_GDN_CONSTS = 11, 6
RWKV_OUTS = [(3, BF16), (2, BF16), (2, BF16), (1, F32), (1, F32), (2, F32)]
GDN_OUTS = [(3, BF16), (2, BF16), (2, BF16), (1, F32), (1, F32), (1, F32)]


def _stream_parts(xs):
    return list(xs) if isinstance(xs, (tuple, list)) else [xs]


def _stream_specs(xs, block_rows, halo_rows=0):
    specs, args, starts = [], [], []
    base = 0
    for part in _stream_parts(xs):
        d = part.shape[-1]
        nblk = part.shape[1] // block_rows
        own = lambda b, i, base=base, nblk=nblk: (b, jnp.clip(i - base, 0, nblk - 1), 0)
        specs.append(pl.BlockSpec((None, block_rows, d), own))
        args.append(part)
        if halo_rows:
            per, last = block_rows // halo_rows, part.shape[1] // halo_rows - 1
            prev = lambda b, i, base=base, per=per, last=last: (b, jnp.clip((i - base) * per - 1, 0, last), 0)
            nxt = lambda b, i, base=base, per=per, last=last: (b, jnp.clip((i - base + 1) * per, 0, last), 0)
            specs += [pl.BlockSpec((None, halo_rows, d), prev), pl.BlockSpec((None, halo_rows, d), nxt)]
            args += [part, part]
        starts.append(base)
        base += nblk
    return specs, args, starts


def _stream_block(refs, starts):
    val = refs[0][...]
    for ref, start in zip(refs[1:], starts[1:]):
        val = jnp.where(pl.program_id(1) >= start, ref[...], val)
    return val


def _front_kernel(*refs, n_lat_blocks, starts):
    it = iter(refs)
    x_refs = [next(it) for _ in range(3 * len(starts))]
    x_own, x_prev, x_next = (_stream_block(x_refs[k::3], starts) for k in range(3))
    sc_ref, csc_ref, sh_ref, csh_ref, g_ref = (next(it) for _ in range(5))
    w_refs = [next(it) for _ in range(4)]
    cos_ref, sin_ref = next(it), next(it)
    rwkv_consts = [next(it) for _ in range(N_RWKV_CONSTS)]
    gdn_consts = [next(it) for _ in range(N_GDN_CONSTS)]
    pc_ref, pd_ref = next(it), next(it)
    rwkv_outs = [next(it) for _ in RWKV_OUTS]
    gdn_outs = [next(it) for _ in GDN_OUTS]

    first, lastb = _seq_edges(n_lat_blocks)
    is_ctx = pl.program_id(1) >= n_lat_blocks
    scale = jnp.where(is_ctx, csc_ref[...], sc_ref[...])
    shift = jnp.where(is_ctx, csh_ref[...], sh_ref[...])
    h = _prenorm(x_own, g_ref[...], scale, shift).astype(BF16)
    halo = jnp.concatenate([x_prev, x_next], axis=0)
    h_halo = _prenorm(halo, g_ref[...], scale, shift).astype(BF16)

    def proj(rows, which):
        return jnp.dot(rows, w_refs[which][...], preferred_element_type=F32)

    pc_ref[...] = proj(h, SEG_GLA).astype(BF16)
    att = proj(h, SEG_ATT)
    w = BRANCH_W
    for c0 in range(0, att.shape[1], w):
        part = att[:, c0:c0 + w]
        if c0 < 2 * w:
            part = _rope(part, cos_ref[...], sin_ref[...])
        pd_ref[:, c0:c0 + w] = part.astype(BF16)

    h_ext = jnp.concatenate([h, h_halo], axis=0)
    n = h.shape[0]
    pb = proj(h_ext, SEG_GDN).astype(BF16)
    _gdn_prep(pb[0:n], pb[n:n + HALO], pb[n + HALO:], first, lastb, *gdn_consts, *gdn_outs)
    pa = proj(h_ext, SEG_RWKV).astype(BF16)
    _rwkv_prep(pa[0:n], pa[n:n + HALO], pa[n + HALO:], first, lastb, *rwkv_consts, *rwkv_outs)


def _front(xs, mod, gain, weights, cos, sin, rwkv_consts, gdn_consts, n_lat_rows):
    parts = _stream_parts(xs)
    batch, rows = parts[0].shape[0], sum(t.shape[1] for t in parts)
    x_specs, x_args, starts = _stream_specs(xs, PREP_ROWS, HALO)
    assert len(rwkv_consts) == N_RWKV_CONSTS and len(gdn_consts) == N_GDN_CONSTS and len(weights) == 4
    tab = pl.BlockSpec((PREP_ROWS, BRANCH_W), lambda b, i: (i, 0))
    consts = list(rwkv_consts) + list(gdn_consts)
    outs = [(weights[s].shape[1], BF16) for s in (SEG_GLA, SEG_ATT)]
    outs += [(n * BRANCH_W, dt) for n, dt in RWKV_OUTS + GDN_OUTS]
    out_specs = [pl.BlockSpec((None, PREP_ROWS, lanes), lambda b, i: (b, i, 0)) for lanes, _ in outs]
    out_shape = [jax.ShapeDtypeStruct((batch, rows, lanes), dt) for lanes, dt in outs]
    res = pl.pallas_call(
        functools.partial(_front_kernel, n_lat_blocks=n_lat_rows // PREP_ROWS, starts=tuple(starts)),
        grid=(batch, rows // PREP_ROWS),
        in_specs=x_specs + _mod_specs(1, batch) + _mod_specs(0, batch)
        + [_vec_spec(gain.shape)] + [_vec_spec(w.shape) for w in weights] + [tab, tab]
        + [_vec_spec(c.shape) for c in consts],
        out_specs=out_specs,
        out_shape=out_shape,
        compiler_params=_cparams(("parallel", "parallel")),
        name="front_proj_prep",
    )(*x_args, mod, mod, mod, mod, gain, *weights, cos, sin, *consts)
    pc, pd = res[0], res[1]
    a_sh, a_dk0, a_dk1, a_dw0, a_dw1, a_fin = res[2:8]
    b_sh, b_dk0, b_dk1, b_dw0, b_dw1, b_fin = res[8:14]
    return pc, pd, (a_sh, (a_dk0, a_dk1), (a_dw0, a_dw1), a_fin), (b_sh, (b_dk0, b_dk1), (b_dw0, b_dw1), b_fin)


def _window_bias():
    iq = np.arange(2 * ATT_BLOCK)[:, None] % ATT_BLOCK
    ik = np.arange(ATT_BLOCK)[None, :]
    ok = np.stack([ik >= iq, np.ones_like(ik >= iq), ik <= iq])
    assert WINDOW == ATT_BLOCK
    return np.where(ok, 0.0, NEG_BIG).astype(np.float32)


ATT_STEP = 2


N_ATT_REFS = 5 + 2 * (ATT_STEP + 2)


def _attn_rows(refs, n_lat_rows):
    sink_ref, q_ref, kc_ref, vc_ref = refs[0:4]
    nband = ATT_STEP + 2
    k_all, v_all = refs[4:4 + nband], refs[4 + nband:4 + 2 * nband]
    bias_ref = refs[4 + 2 * nband]
    blk = ATT_BLOCK
    lane = lax.broadcasted_iota(jnp.int32, (blk, 2 * HEAD), 1)
    row2 = lax.broadcasted_iota(jnp.int32, (2 * blk, 1), 0)
    out_rows = []
    for u in range(ATT_STEP):
        out_cols = []
        t = pl.program_id(1) * ATT_STEP + u
        rows = slice(u * blk, (u + 1) * blk)
        q = q_ref[rows, :] * (HEAD ** -0.5)
        k_band, v_band = k_all[u:u + 3], v_all[u:u + 3]
        q_lat = t * blk < n_lat_rows
        in_seq = [q_lat & (t >= 1), q_lat, q_lat & ((t + 1) * blk < n_lat_rows)]
        bias = [jnp.where(in_seq[j], bias_ref[j], NEG_BIG) for j in range(3)]
        for g in range(D_KV_HEADS):
            cols = slice(g * 2 * HEAD, (g + 1) * 2 * HEAD)
            qg = q[:, cols]
            zero = jnp.zeros((), qg.dtype)
            qs = jnp.concatenate([jnp.where(lane < HEAD, qg, zero), jnp.where(lane >= HEAD, qg, zero)], axis=0)
            s_ctx = _mm(qs, kc_ref[:, cols], NT)
            s_band = [_mm(qs, k_band[j][:, cols], NT) + bias[j] for j in range(3)]
            sink = jnp.where(row2 < blk, sink_ref[2 * g], sink_ref[2 * g + 1])
            m = jnp.maximum(jnp.max(s_ctx, axis=-1, keepdims=True), sink)
            for s in s_band:
                m = jnp.maximum(m, jnp.max(s, axis=-1, keepdims=True))
            p_ctx = jnp.exp(s_ctx - m)
            den = jnp.sum(p_ctx, axis=-1, keepdims=True) + jnp.exp(sink - m)
            acc = _mm(p_ctx, vc_ref[:, cols], NN)
            for j in range(3):
                pj = jnp.exp(s_band[j] - m)
                den = den + jnp.sum(pj, axis=-1, keepdims=True)
                acc = acc + _mm(pj, v_band[j][:, cols], NN)
            og = acc / den
            out_cols.append(jnp.where(lane < HEAD, og[0:blk], og[blk:2 * blk]))
        out_rows.append(jnp.concatenate(out_cols, axis=1))
    return jnp.concatenate(out_rows, axis=0)


ATT_ROWS = ATT_STEP * ATT_BLOCK


def _attention_inputs(p, sink, n_lat_rows):
    _, rows, _ = p.shape
    blk = ATT_BLOCK
    nb = rows // blk
    w = BRANCH_W
    n_ctx_rows = rows - n_lat_rows
    assert n_lat_rows % n_ctx_rows == 0 and n_ctx_rows % ATT_ROWS == 0

    def band(col, off):
        def index(b, t):
            return (b, jnp.clip(t * ATT_STEP + off, 0, nb - 1), col)
        return pl.BlockSpec((None, blk, w), index)

    offsets = range(-1, ATT_STEP + 1)
    ctx = lambda col: pl.BlockSpec((None, n_ctx_rows, w), lambda b, t: (b, n_lat_rows // n_ctx_rows, col))
    bias = jnp.asarray(_window_bias())
    specs = [pl.BlockSpec(memory_space=pltpu.SMEM), pl.BlockSpec((None, ATT_ROWS, w), lambda b, t: (b, t, 0)),
             ctx(1), ctx(2)]
    specs += [band(1, o) for o in offsets] + [band(2, o) for o in offsets] + [_vec_spec(bias.shape)]
    args = [sink] + [p] * (3 + 2 * len(offsets)) + [bias]
    assert len(specs) == N_ATT_REFS
    return specs, args


def _merge_kernel(*refs, n_lat_rows, starts):
    x = _stream_block(refs[0:len(starts)], starts)
    refs = refs[len(starts):]
    sc_ref, csc_ref, sh_ref, csh_ref, gm_ref, cgm_ref, g0_ref, g1_ref, ya_ref, yb_ref, yc_ref = refs[0:11]
    att_refs = refs[11:11 + N_ATT_REFS]
    wg_ref, gb_ref, wb_ref, wo_ref, o_ref = refs[11 + N_ATT_REFS:]
    is_ctx = _ctx_rows(x.shape[0], n_lat_rows)
    scale = jnp.where(is_ctx, csc_ref[...], sc_ref[...])
    shift = jnp.where(is_ctx, csh_ref[...], sh_ref[...])
    gmod = jnp.where(is_ctx, cgm_ref[...], gm_ref[...])
    h = _prenorm(x, g0_ref[...], scale, shift).astype(BF16)
    branches = [ya_ref[...], yb_ref[...], yc_ref[...], _attn_rows(att_refs, n_lat_rows)]
    acc = jnp.zeros(x.shape, F32)
    for i, y in enumerate(branches):
        pre = jnp.dot(h, wg_ref[:, i * D_MODEL:(i + 1) * D_MODEL], preferred_element_type=F32)
        gate = _sigmoid(pre + gb_ref[i:i + 1, :])
        acc = acc + gate * jnp.dot(y.astype(BF16), wb_ref[i], preferred_element_type=F32)
    out = jnp.dot(acc.astype(BF16), wo_ref[...], preferred_element_type=F32)
    ms = jnp.mean(out * out, axis=-1, keepdims=True)
    o_ref[...] = x + gmod * (out * lax.rsqrt(ms + NORM_EPS) * g1_ref[...])


def _merge(xs, mod, gain0, gain1, ys, p_att, sink, wg, gate_b, wb, wo, n_lat_rows, out_rows):
    parts = _stream_parts(xs)
    batch, d = parts[0].shape[0], parts[0].shape[-1]
    rows = out_rows
    tm = ATT_ROWS
    assert rows % tm == 0
    x_specs, x_args, starts = _stream_specs(xs, tm)
    tile = lambda width: pl.BlockSpec((None, tm, width), lambda b, i: (b, i, 0))
    consts = [wg, gate_b, wb, wo]
    att_specs, att_args = _attention_inputs(p_att, sink, n_lat_rows)
    return pl.pallas_call(
        functools.partial(_merge_kernel, n_lat_rows=n_lat_rows, starts=tuple(starts)),
        grid=(batch, rows // tm),
        in_specs=x_specs + _mod_specs(1, batch) + _mod_specs(0, batch) + _mod_specs(2, batch)
        + [_vec_spec(gain0.shape), _vec_spec(gain1.shape)]
        + [tile(BRANCH_W)] * 3 + att_specs + [_vec_spec(c.shape) for c in consts],
        out_specs=tile(d),
        out_shape=jax.ShapeDtypeStruct((batch, rows, d), F32),
        compiler_params=_cparams(("parallel", "parallel")),
        name="attn_merge_out",
    )(*x_args, *([mod] * 6), gain0, gain1, *ys, *att_args, *consts)


FFN_CHUNK = 512


def _ffn_kernel(x_ref, sc_ref, csc_ref, sh_ref, csh_ref, gm_ref, cgm_ref, g2_ref, g3_ref, w1_ref, w2_ref,
                o_ref, *, n_lat_rows):
    x = x_ref[...]
    is_ctx = _ctx_rows(x.shape[0], n_lat_rows)
    scale = jnp.where(is_ctx, csc_ref[...], sc_ref[...])
    shift = jnp.where(is_ctx, csh_ref[...], sh_ref[...])
    h = _prenorm(x, g2_ref[...], scale, shift).astype(BF16)
    hidden = w2_ref.shape[0]
    out = jnp.zeros(x.shape, F32)
    for lo in range(0, hidden, FFN_CHUNK):
        hi = min(lo + FFN_CHUNK, hidden)
        gt = jnp.dot(h, w1_ref[:, lo:hi], preferred_element_type=F32)
        up = jnp.dot(h, w1_ref[:, hidden + lo:hidden + hi], preferred_element_type=F32)
        out = out + jnp.dot((_silu(gt) * up).astype(BF16), w2_ref[lo:hi, :], preferred_element_type=F32)
    ms = jnp.mean(out * out, axis=-1, keepdims=True)
    gmod = jnp.where(is_ctx, cgm_ref[...], gm_ref[...])
    o_ref[...] = x + gmod * (out * lax.rsqrt(ms + NORM_EPS) * g3_ref[...])


def _ffn(xs, mod, gain2, gain3, w1, w2, n_lat_rows):
    batch, rows, d = xs.shape
    tm = _row_tile(rows, 544)
    tile = pl.BlockSpec((None, tm, d), lambda b, i: (b, i, 0))
    consts = [gain2, gain3, w1, w2]
    return pl.pallas_call(
        functools.partial(_ffn_kernel, n_lat_rows=n_lat_rows),
        grid=(batch, rows // tm),
        in_specs=[tile] + _mod_specs(4, batch) + _mod_specs(3, batch) + _mod_specs(5, batch)
        + [_vec_spec(c.shape) for c in consts],
        out_specs=tile,
        out_shape=jax.ShapeDtypeStruct((batch, rows, d), F32),
        compiler_params=_cparams(("parallel", "parallel")),
        name="swiglu",
    )(xs, *([mod] * 6), *consts)


def _pad_cols(w, width):
    return jnp.pad(w, ((0, 0), (0, width - w.shape[1])))


def _layer_weights(w_in, mu, w2, a2, g2, conv, a_log, dt_bias, gw2, gb):
    wts = {}
    off_b, off_c, off_d, off_g = A_IN, A_IN + B_IN, A_IN + B_IN + C_IN, A_IN + B_IN + C_IN + D_IN
    w = BRANCH_W
    wts["wa"] = _pad_cols(w_in[:, 0:A_IN], 4 * w).astype(BF16)
    wts["mu"] = _pad_cols(mu[None, :], 4 * w)
    lo = np.cumsum((0,) + A_LORA)
    place = lambda m, r0: jnp.zeros((w, w), F32).at[r0:r0 + m.shape[0], :].set(m)
    wts["w2p"] = jnp.stack([place(w2[0], lo[0]), place(w2[1], lo[1])])
    wts["a2p"] = jnp.stack([place(a2[0], lo[2]), place(a2[1], lo[3])])
    wts["g2p"] = place(g2, lo[4])

    wb = w_in[:, off_b:off_b + B_IN]
    nh = N_HEADS
    n_scalar = 4 * nh
    wts["wb"] = jnp.concatenate([wb[:, 0:3 * w], wb[:, 3 * w + n_scalar:],
                                 _pad_cols(wb[:, 3 * w:3 * w + n_scalar], LANES)], axis=1).astype(BF16)
    expand = np.zeros((4, LANES, w), np.float32)
    for grp in range(4):
        for h in range(nh):
            expand[grp, grp * nh + h, h * HEAD:(h + 1) * HEAD] = 1.0
    wts["eb"] = jnp.asarray(expand[0:2])
    wts["ea"] = jnp.asarray(expand[2:4])
    vec = lambda t: jnp.zeros((1, LANES), F32).at[0, 2 * nh:4 * nh].set(t.reshape(-1))
    wts["alog"] = vec(a_log)
    wts["dt"] = vec(dt_bias)
    wts["conv"] = conv

    wc = w_in[:, off_c:off_c + C_IN]
    pad_heads = lambda m: jnp.pad(m.reshape(m.shape[0], nh, C_DK),
                                  ((0, 0), (0, 0), (0, HEAD - C_DK))).reshape(m.shape[0], w)
    qc, kc, vc = wc[:, 0:C_QK], wc[:, C_QK:2 * C_QK], wc[:, 2 * C_QK:2 * C_QK + w]
    loc = wc[:, 2 * C_QK + w:2 * C_QK + w + 2 * C_GATE_R]
    gc = wc[:, 2 * C_QK + w + 2 * C_GATE_R:]
    wts["wc"] = jnp.concatenate([pad_heads(qc), pad_heads(kc), vc, gc, _pad_cols(loc, LANES)], axis=1).astype(BF16)
    gwp = jnp.zeros((2, LANES, w), F32)
    for d in range(2):
        gwp = gwp.at[d, d * C_GATE_R:(d + 1) * C_GATE_R, :].set(pad_heads(gw2[d]))
    wts["gwp"] = gwp
    wts["gbp"] = pad_heads(gb)
    wts["glane"] = jnp.asarray((np.arange(w) % HEAD < C_DK).astype(np.float32))[None, :]

    wd = w_in[:, off_d:off_d + D_IN]
    qd = wd[:, 0:w]
    dup = lambda m: jnp.concatenate([m[:, 0:HEAD], m[:, 0:HEAD], m[:, HEAD:], m[:, HEAD:]], axis=1)
    wts["wd"] = jnp.concatenate([qd, dup(wd[:, w:w + 2 * HEAD]), dup(wd[:, w + 2 * HEAD:])], axis=1).astype(BF16)
    wts["wg"] = w_in[:, off_g:].astype(BF16)
    return wts


def _rope_tables(n_lat_rows, n_ctx_rows):
    quarter = HEAD // 4
    inv = ROPE_BASE ** (-np.arange(quarter, dtype=np.float32) / quarter)
    pos = np.arange(n_lat_rows)
    rows = (pos // GRID_W).astype(np.float32)
    cols = (pos % GRID_W).astype(np.float32)
    inv = jnp.asarray(inv)
    ang_r = jnp.asarray(rows)[:, None] * inv[None, :]
    ang_c = jnp.asarray(cols)[:, None] * inv[None, :]
    cos = jnp.concatenate([jnp.cos(ang_r)] * 2 + [jnp.cos(ang_c)] * 2, axis=1)
    sin = jnp.concatenate([-jnp.sin(ang_r), jnp.sin(ang_r), -jnp.sin(ang_c), jnp.sin(ang_c)], axis=1)
    cos = jnp.concatenate([cos, jnp.ones((n_ctx_rows, HEAD), F32)], axis=0)
    sin = jnp.concatenate([sin, jnp.zeros((n_ctx_rows, HEAD), F32)], axis=0)
    return jnp.tile(cos, (1, N_HEADS)), jnp.tile(sin, (1, N_HEADS))


RWKV_SLOTS = {"r": ("sh", 0), "v": ("sh", 1), "a": ("sh", 2), "w": ("dw", 0), "k": ("dk", 0), "b": ("dk", 1)}
GDN_SLOTS = RWKV_SLOTS
GLA_SLOTS = {"r": ("sh", 0), "k": ("sh", 1), "v": ("sh", 2)}


def kernel(x, c, ctx, c_ctx, ada_w, ada_b, norm_g, w_in, gate_b, w_branch, w_out, rwkv_mu, rwkv_w0, rwkv_w2, rwkv_a0, rwkv_a2, rwkv_g2, rwkv_kk, rwkv_ka, rwkv_rk, rwkv_ln_g, rwkv_ln_b, gdn_conv, gdn_a_log, gdn_dt_bias, gdn_norm_g, gla_gw2, gla_gb, gla_norm_g, attn_sink, ffn_w1, ffn_w2):
    batch, n_lat, d = x.shape
    n_ctx = ctx.shape[1]
    depth = ada_w.shape[0]
    assert n_ctx % PREP_ROWS == 0 and n_lat % PREP_ROWS == 0 and d == D_MODEL

    mod_rows = 8 * ((batch + 1 + 7) // 8)
    c_rows = jnp.concatenate([c, c_ctx[None, :], jnp.zeros((mod_rows - batch - 1, d), F32)], axis=0)
    mod_all = _modulation(c_rows, ada_w, ada_b)
    cos, sin = _rope_tables(n_lat, n_ctx)

    xs = (x, ctx)
    rows = n_lat + n_ctx
    row = lambda t: t.reshape(1, -1)
    for l in range(depth):
        out_rows = n_lat if l == depth - 1 else rows
        mod = mod_all[l].reshape(mod_rows, 1, 6 * d)
        ng = norm_g[l]
        wts = _layer_weights(w_in[l], rwkv_mu[l], rwkv_w2[l], rwkv_a2[l], rwkv_g2[l], gdn_conv[l],
                             gdn_a_log[l], gdn_dt_bias[l], gla_gw2[l], gla_gb[l])
        ones = jnp.asarray(_head_block_ones())
        around = jnp.asarray(0.5 * (_shift_matrix(-1) + _shift_matrix(1)), BF16)
        rwkv_consts = [around, wts["mu"], wts["w2p"], wts["a2p"], wts["g2p"], rwkv_w0[l], rwkv_a0[l],
                       row(rwkv_kk[l]), row(rwkv_ka[l]), row(rwkv_rk[l]), ones]
        gdn_consts = [wts["conv"], wts["alog"], wts["dt"], wts["eb"], wts["ea"], ones]
        p_gla, p_att, rwkv_ops, gdn_ops = _front(xs, mod, row(ng[0]), (wts["wc"], wts["wb"], wts["wd"], wts["wa"]),
                                                 cos, sin, rwkv_consts, gdn_consts, n_lat)

        sh, dks, dws, fin = rwkv_ops
        ya = _bidir_scan(sh, dks, dws, n_lat, fin, row(rwkv_ln_g[l]), row(rwkv_ln_b[l]), finish="groupnorm",
                         scalar_decay=False, lowrank=True, slots=RWKV_SLOTS)

        sh, dks, dws, fin = gdn_ops
        gnorm = row(jnp.tile(gdn_norm_g[l], N_HEADS))
        yb = _bidir_scan(sh, dks, dws, n_lat, fin, gnorm, gnorm, finish="rms",
                         scalar_decay=True, lowrank=True, slots=GDN_SLOTS)

        yc = _gla_scan(p_gla, n_lat, wts["gwp"], wts["gbp"], wts["glane"], row(jnp.tile(gla_norm_g[l], N_HEADS)))

        xs = _merge(xs, mod, row(ng[0]), row(ng[1]), (ya, yb, yc), p_att, attn_sink[l], wts["wg"], gate_b[l],
                    w_branch[l].astype(BF16), w_out[l].astype(BF16), n_lat, out_rows)
        xs = _ffn(xs, mod, row(ng[2]), row(ng[3]), ffn_w1[l].astype(BF16), ffn_w2[l].astype(BF16), n_lat)
    return xs
```

```python
import functools
import math

import numpy as np
import jax
import jax.numpy as jnp
from jax import lax
from jax.experimental import pallas as pl
from jax.experimental.pallas import tpu as pltpu

F32 = jnp.float32
BF16 = jnp.bfloat16

D_MODEL = 1024
N_BRANCH = 4
BRANCH_W = 256
HEAD = 64
N_HEADS = BRANCH_W // HEAD
NORM_EPS = 1e-6
A_GN_EPS = 64e-5
A_LORA = (32, 32, 32, 32, 64)
A_IN = 3 * BRANCH_W + sum(A_LORA)
B_IN = 4 * BRANCH_W + 4 * N_HEADS
C_DK = 32
C_QK = N_HEADS * C_DK
C_GATE_R = 16
C_GATE_NORM = 16.0
C_IN = 2 * C_QK + 2 * BRANCH_W + 2 * C_GATE_R
D_KV_HEADS = 2
D_IN = BRANCH_W + 2 * D_KV_HEADS * HEAD
B_CONV = 7
WINDOW = 128
ROPE_BASE = 10000.0
GRID_W = 64
FFN_HIDDEN = 2816

LANES = 128
CHUNK = 64
PREP_ROWS = 256
HALO = 16
ATT_BLOCK = 128
VMEM_LIMIT = 48 * 1024 * 1024
NEG_BIG = -1e30

NN = (((1,), (0,)), ((), ()))
NT = (((1,), (1,)), ((), ()))
TN = (((0,), (0,)), ((), ()))


def _mm(a, b, dims=NN, mode="bf16"):
    if mode == "f32":
        return lax.dot_general(a, b, dims, precision=lax.Precision.HIGHEST, preferred_element_type=F32)
    return lax.dot_general(a.astype(BF16), b.astype(BF16), dims, preferred_element_type=F32)


def _sigmoid(x):
    return 1.0 / (1.0 + jnp.exp(-x))


def _silu(x):
    return x * _sigmoid(x)


def _softplus(x):
    return jnp.maximum(x, 0.0) + jnp.log1p(jnp.exp(-jnp.abs(x)))


def _cparams(sem):
    return pltpu.CompilerParams(dimension_semantics=sem, vmem_limit_bytes=VMEM_LIMIT)


def _mod_kernel(c_ref, w_ref, b_ref, o_ref):
    c = c_ref[...]
    o_ref[...] = _mm(_silu(c), w_ref[...], mode="f32") + b_ref[...]


def _modulation(c_rows, ada_w, ada_b):
    depth, d, n = ada_w.shape
    rows = c_rows.shape[0]
    tn = 1024
    return pl.pallas_call(
        _mod_kernel,
        grid=(depth, n // tn),
        in_specs=[
            pl.BlockSpec((rows, d), lambda l, j: (0, 0)),
            pl.BlockSpec((None, d, tn), lambda l, j: (l, 0, j)),
            pl.BlockSpec((None, 1, tn), lambda l, j: (l, 0, j)),
        ],
        out_specs=pl.BlockSpec((None, rows, tn), lambda l, j: (l, 0, j)),
        out_shape=jax.ShapeDtypeStruct((depth, rows, n), F32),
        compiler_params=_cparams(("parallel", "parallel")),
        name="adaln_mod",
    )(c_rows, ada_w, ada_b.reshape(depth, 1, n))


def _row_tile(rows, target):
    return max(t for t in range(8, target + 1, 8) if rows % t == 0)


def _mod_specs(which, batch):
    lat = pl.BlockSpec((None, 1, D_MODEL), lambda b, i, *_: (b, 0, which))
    ctx = pl.BlockSpec((None, 1, D_MODEL), lambda b, i, *_: (batch, 0, which))
    return [lat, ctx]


def _ctx_rows(tm, n_lat_rows):
    row = pl.program_id(1) * tm + lax.broadcasted_iota(jnp.int32, (tm, 1), 0)
    return row >= n_lat_rows


def _prenorm(x, gain, scale, shift):
    ms = jnp.mean(x * x, axis=-1, keepdims=True)
    return (x * lax.rsqrt(ms + NORM_EPS)) * (gain * (1.0 + scale)) + shift


def _rope(x, cos, sin):
    width = x.shape[-1]
    lane = lax.broadcasted_iota(jnp.int32, x.shape, 1)
    half, quarter = HEAD // 2, HEAD // 4
    swapped = jnp.where(lane % half < quarter, pltpu.roll(x, width - quarter, axis=1), pltpu.roll(x, quarter, axis=1))
    return x * cos + swapped * sin


def _head_block_ones():
    idx = np.arange(BRANCH_W)
    return (idx[:, None] // HEAD == idx[None, :] // HEAD).astype(np.float32)


N_LEVELS = 6


def _level_mask(ri, ci, s, reverse):
    b = 1 << s
    blk = (ri // (2 * b)) == (ci // (2 * b))
    hi_r, hi_c = (ri // b) % 2 == 1, (ci // b) % 2 == 1
    return blk & ((hi_c & ~hi_r) if reverse else (hi_r & ~hi_c))


SCAN_HEADS = 2
GROUP_W = SCAN_HEADS * HEAD
N_GROUPS = N_HEADS // SCAN_HEADS


def _scan_masks(reverse):
    n = SCAN_HEADS * CHUNK
    ri = np.arange(CHUNK)[:, None]
    ci = np.arange(n)[None, :] % CHUNK
    strict, incl = (ci > ri, ci >= ri) if reverse else (ci < ri, ci <= ri)
    compact = [strict, incl, ci == ri] + [_level_mask(ri, ci, s, reverse) for s in range(N_LEVELS)]
    r = np.arange(n)[:, None]
    c = np.arange(n)[None, :]
    block = [(r // CHUNK) == (c // CHUNK), r == c]
    return np.stack(compact).astype(np.float32), np.stack(block).astype(np.float32)


EXACT_TERMS = 3
NORM_TERMS = 2


def _tri(reverse):
    i = np.arange(CHUNK)
    m = (i[None, :] >= i[:, None]) if reverse else (i[None, :] <= i[:, None])
    return np.tile(m.astype(np.float32), (1, EXACT_TERMS))


def _split_terms(x, terms):
    out = []
    for _ in range(terms - 1):
        part = x.astype(BF16)
        out.append(part)
        x = x - part.astype(F32)
    out.append(x.astype(BF16))
    return out


def _mm_exact(a, b, dims=NN, split="a", terms=EXACT_TERMS):
    assert dims == NN
    if split == "a":
        lhs = jnp.concatenate(_split_terms(a, terms), axis=1)
        rhs = jnp.concatenate([b.astype(BF16)] * terms, axis=0)
    else:
        lhs = a.astype(BF16)
        rhs = jnp.concatenate(_split_terms(b, terms), axis=0)
    return lax.dot_general(lhs, rhs, dims, preferred_element_type=F32)


C_STRICT, C_INCL, C_EYE, C_LVL0 = 0, 1, 2, 3
B_SAME, B_EYE = 0, 1
SCAN_BATCH = 8
SCAN_ROWS = 256
assert CHUNK == HEAD and (1 << N_LEVELS) == CHUNK


def _scan_chunk(vals, st, tri, ones, cm_ref, bm_ref, same_bf, *, reverse, scalar_decay, lowrank, mm_mode):
    same = bm_ref[B_SAME]

    def get(name):
        return vals[name]

    def expand(x):
        return jnp.concatenate([x.astype(BF16)] * SCAN_HEADS, axis=0) * same_bf

    def keep(x, k):
        return jnp.where(cm_ref[k] > 0.5, x, 0.0)

    logw = get("w")
    r, k, v = get("r"), get("k"), get("v")
    cum = _mm_exact(tri, logw, split="b")
    cum_x = cum - logw
    last = 0 if reverse else CHUNK - 1
    total = cum[last:last + 1, :]

    if scalar_decay:
        diag = jnp.concatenate([cum] * SCAN_HEADS, axis=0) * bm_ref[B_EYE]
        cum_row = _mm_exact(ones, diag, split="b")
        d_ii = jnp.exp(jnp.where(cm_ref[C_INCL] > 0.5, cum - cum_row, NEG_BIG))
        r_q, k_q = r, k
        to_end = jnp.exp(total - cum)
        r_abs = r * jnp.exp(cum)
    else:
        ref_row = cum[CHUNK // 2:CHUNK // 2 + 1, :]
        e_ref = jnp.exp(ref_row)
        p_inv = jnp.exp(ref_row - cum)
        r_q, k_q = r * jnp.exp(cum - ref_row), k * p_inv
        to_end = p_inv * jnp.exp(total - ref_row)
        r_abs = r_q * e_ref

    k_e, v_e = expand(k_q), expand(v)
    if not lowrank:
        s_k = _mm(r_q, k_e, NT, mm_mode)
        a_rk = s_k * d_ii if scalar_decay else keep(s_k, C_INCL)
        y = _mm(r_abs, st, NT, mm_mode) + _mm(a_rk, v_e, NN, mm_mode)
        upd = _mm(v, k * to_end, TN, mm_mode)
        return y, st * jnp.exp(total) + upd * same

    a, b = get("a"), get("b")
    if scalar_decay:
        d_xi = jnp.exp(jnp.where(cm_ref[C_STRICT] > 0.5, cum_x - cum_row, NEG_BIG))
        a_q, b_q = a, b
    else:
        a_q, b_q = a * jnp.exp(cum_x - ref_row), b * p_inv
    lhs = jnp.concatenate([a_q, r_q], axis=0)
    scores = _mm(lhs, jnp.concatenate([expand(b_q), k_e], axis=0), NT, mm_mode)
    s_b, s_k = scores[:, 0:GROUP_W], scores[:, GROUP_W:]
    if scalar_decay:
        a_ab, a_rb = s_b[0:CHUNK] * d_xi, s_b[CHUNK:] * d_ii
        a_ak, a_rk = s_k[0:CHUNK] * d_xi, s_k[CHUNK:] * d_ii
    else:
        a_ab, a_rb = keep(s_b[0:CHUNK], C_STRICT), keep(s_b[CHUNK:], C_INCL)
        a_ak, a_rk = keep(s_k[0:CHUNK], C_STRICT), keep(s_k[CHUNK:], C_INCL)

    inv = cm_ref[C_EYE] + a_ab * cm_ref[C_LVL0]
    for s in range(1, N_LEVELS):
        c_s = expand(a_ab * cm_ref[C_LVL0 + s])
        inv = inv + _mm(_mm(inv, c_s, NN, mm_mode), expand(inv), NN, mm_mode)

    a_abs = a * jnp.exp(cum_x) if scalar_decay else a_q * e_ref
    from_state = _mm(jnp.concatenate([a_abs, r_abs], axis=0), st, NT, mm_mode)
    from_v = _mm(jnp.concatenate([a_ak, a_rk], axis=0), v_e, NN, mm_mode)
    both = from_state + from_v
    z = _mm(inv, expand(both[0:CHUNK]), NN, mm_mode)
    y = both[CHUNK:] + _mm(a_rb, expand(z), NN, mm_mode)
    upd = _mm(jnp.concatenate([v, z], axis=0), jnp.concatenate([k * to_end, b * to_end], axis=0), TN, mm_mode)
    return y, st * jnp.exp(total) + upd * same


def _scan_kernel(*refs, reverse, scalar_decay, lowrank, slots, sources, finish, mm_mode, lora, r_scale):
    it = iter(refs)
    src_refs = {name: next(it) for name in sources}
    sh_ref = src_refs[sources[0]]
    cm_ref, bm_ref, tri_ref = next(it), next(it), next(it)
    if lora:
        lo_ref, gw_ref, gb_ref, lane_ref = next(it), next(it), next(it), next(it)
    if finish:
        ob_ref, fin_ref, p1_ref, p2_ref, avg_ref = next(it), next(it), next(it), next(it), next(it)
    o_ref, st_ref = next(it), next(it)

    @pl.when(pl.program_id(1) == 0)
    def _():
        st_ref[...] = jnp.zeros_like(st_ref)

    nb, block_rows = sh_ref.shape[0], sh_ref.shape[1]
    n_chunks = block_rows // CHUNK
    n = nb * N_GROUPS
    tri = jnp.broadcast_to(tri_ref[...], (n,) + tri_ref.shape)
    ones = jnp.ones((n, CHUNK, EXACT_TERMS * SCAN_HEADS * CHUNK), F32)
    chunk = functools.partial(_scan_chunk, cm_ref=cm_ref, bm_ref=bm_ref, same_bf=bm_ref[B_SAME].astype(BF16),
                              reverse=reverse, scalar_decay=scalar_decay, lowrank=lowrank, mm_mode=mm_mode)

    def step(c, carry):
        rows = pl.ds(pl.multiple_of((n_chunks - 1 - c if reverse else c) * CHUNK, CHUNK), CHUNK)

        def groups(name):
            src, idx = slots[name]
            ref = src_refs[src]
            parts = [ref[:, rows, idx * BRANCH_W + g * GROUP_W:idx * BRANCH_W + (g + 1) * GROUP_W].astype(F32)
                     for g in range(N_GROUPS)]
            return jnp.stack(parts, axis=1).reshape(n, CHUNK, GROUP_W)

        vals = {name: groups(name) for name in slots}
        if lora:
            lo = lo_ref[:, rows, :].reshape(nb * CHUNK, lo_ref.shape[-1])
            z = _mm(lo, gw_ref[...], mode=mm_mode) + gb_ref[...]
            logw = ((-_softplus(-z) / C_GATE_NORM) * lane_ref[...]).reshape(nb, CHUNK, BRANCH_W)
            parts = [logw[:, :, g * GROUP_W:(g + 1) * GROUP_W] for g in range(N_GROUPS)]
            vals["w"] = jnp.stack(parts, axis=1).reshape(n, CHUNK, GROUP_W)
        if r_scale != 1.0:
            vals["r"] = vals["r"] * r_scale
        y, st_new = jax.vmap(chunk)(vals, st_ref[...], tri, ones)
        st_ref[...] = st_new
        y = y.reshape(nb, N_GROUPS, CHUNK, GROUP_W)
        y = jnp.concatenate([y[:, g] for g in range(N_GROUPS)], axis=-1)
        if not finish:
            o_ref[:, rows, :] = y
            return carry

        y = (y + ob_ref[:, rows, :]).reshape(nb * CHUNK, BRANCH_W)
        avg = avg_ref[...]
        if finish == "groupnorm":
            gate = fin_ref[:, rows, 0:BRANCH_W].reshape(y.shape)
            bonus = fin_ref[:, rows, BRANCH_W:2 * BRANCH_W].reshape(y.shape)
            cen = y - _mm_exact(y, avg)
            var = _mm_exact(cen * cen, avg)
            yn = cen * lax.rsqrt(var + A_GN_EPS) * p1_ref[...] + p2_ref[...]
            out = (yn + bonus) * gate
        else:
            ms = _mm_exact(y * y, avg)
            gate = fin_ref[:, rows, :].astype(F32).reshape(y.shape)
            out = y * lax.rsqrt(ms + NORM_EPS) * p1_ref[...] * _silu(gate)
        o_ref[:, rows, :] = out.reshape(nb, CHUNK, BRANCH_W).astype(o_ref.dtype)
        return carry

    lax.fori_loop(0, n_chunks, step, 0)


def _view(x):
    return x if isinstance(x, tuple) else (x, x.shape[-1], 0)


def _scan(srcs, n_lat_rows, *, reverse, scalar_decay, lowrank, slots, finish=None, fin_args=None,
          mm_mode="bf16", lora=None, r_scale=1.0):
    sources = tuple(srcs)
    views = [_view(srcs[s]) for s in sources]
    batch, rows, _ = views[0][0].shape
    nc, nlat = rows // SCAN_ROWS, n_lat_rows // SCAN_ROWS
    nctx = nc - nlat
    nb = math.gcd(batch, SCAN_BATCH)

    if reverse:
        def chunk(n):
            return nc - 1 - n
    else:
        def chunk(n):
            return jnp.where(n < nctx, nlat + n, n - nctx)

    def row_spec(width, col=0):
        return pl.BlockSpec((nb, SCAN_ROWS, width), lambda b, n: (b, chunk(n), col))

    def const_spec(shape):
        zeros = (0,) * len(shape)
        return pl.BlockSpec(shape, lambda b, n: zeros)

    cmask, bmask = (jnp.asarray(m) for m in _scan_masks(reverse))
    tri = jnp.asarray(_tri(reverse))
    in_specs = [row_spec(w, c) for _, w, c in views]
    in_specs += [const_spec(cmask.shape), const_spec(bmask.shape), const_spec(tri.shape)]
    args = [a for a, _, _ in views] + [cmask, bmask, tri]
    if lora:
        (lo, lo_w, lo_c), gw, gb, lane = _view(lora[0]), lora[1], lora[2], lora[3]
        in_specs += [row_spec(lo_w, lo_c), const_spec(gw.shape), const_spec(gb.shape), const_spec(lane.shape)]
        args += [lo, gw, gb, lane]
    if finish:
        ob, fin, p1, p2 = fin_args
        fin, fin_w, fin_c = _view(fin)
        avg = jnp.asarray(_head_block_ones() / HEAD)
        in_specs += [row_spec(BRANCH_W), row_spec(fin_w, fin_c), const_spec(p1.shape), const_spec(p2.shape),
                     const_spec(avg.shape)]
        args += [ob, fin, p1, p2, avg]
    kern = functools.partial(_scan_kernel, reverse=reverse, scalar_decay=scalar_decay, lowrank=lowrank,
                             slots=slots, sources=sources, finish=finish, mm_mode=mm_mode, lora=bool(lora),
                             r_scale=r_scale)
    return pl.pallas_call(
        kern,
        grid=(batch // nb, nc),
        in_specs=in_specs,
        out_specs=row_spec(BRANCH_W),
        out_shape=jax.ShapeDtypeStruct((batch, rows, BRANCH_W), BF16 if finish else F32),
        scratch_shapes=[pltpu.VMEM((nb * N_GROUPS, GROUP_W, GROUP_W), F32)],
        compiler_params=_cparams(("parallel", "arbitrary")),
        name="dplr_scan_" + ("bwd" if reverse else "fwd"),
    )(*args)


def _gla_scan(p, n_lat_rows, gwp, gbp, lane, norm_g):
    w = BRANCH_W
    srcs = {"sh": (p, 3 * w, 0)}
    lo = (p, gwp.shape[1], 4 * w // gwp.shape[1])
    kw = dict(scalar_decay=False, lowrank=False, slots=GLA_SLOTS, r_scale=C_DK ** -0.5)
    ob = _scan(srcs, n_lat_rows, reverse=True, lora=(lo, gwp[1], gbp[1:2], lane), **kw)
    return _scan(srcs, n_lat_rows, reverse=False, lora=(lo, gwp[0], gbp[0:1], lane), finish="rms",
                 fin_args=(ob, (p, w, 3), norm_g, norm_g), **kw)


def _bidir_scan(sh, dks, dws, n_lat_rows, fin, p1, p2, *, finish, **kw):
    def srcs(d):
        out = {"sh": sh, "dw": dws[d]}
        if dks is not None:
            out["dk"] = dks[d]
        return out

    ob = _scan(srcs(1), n_lat_rows, reverse=True, **kw)
    return _scan(srcs(0), n_lat_rows, reverse=False, finish=finish, fin_args=(ob, fin, p1, p2), **kw)


def _seq_edges(n_lat_blocks):
    i = pl.program_id(1)
    first = (i == 0) | (i == n_lat_blocks)
    lastb = (i == n_lat_blocks - 1) | (i == pl.num_programs(1) - 1)
    return first, lastb


def _vec_spec(shape):
    zeros = (0,) * len(shape)
    return pl.BlockSpec(shape, lambda b, i: zeros)


def _shift_matrix(offset):
    m = np.zeros((PREP_ROWS, PREP_ROWS + 2 * HALO), np.float32)
    t = np.arange(PREP_ROWS)
    src = t + offset
    col = np.where(src < 0, PREP_ROWS + HALO + src, np.where(src >= PREP_ROWS, HALO + src, src))
    m[t, col] = 1.0
    return m


def _rwkv_prep(x, prev, nxt, first, lastb, shift_ref, mu_ref, w2_ref, a2_ref, g2_ref, w0_ref, a0_ref, kk_ref,
               ka_ref, rk_ref, ones_ref, sh_ref, dk0_ref, dk1_ref, dw0_ref, dw1_ref, fin_ref, mode="bf16"):
    zero = jnp.zeros((), x.dtype)
    xe = jnp.concatenate([x, jnp.where(first, zero, prev), jnp.where(lastb, zero, nxt)], axis=0)
    around = jnp.dot(shift_ref[...], xe, preferred_element_type=F32)
    x = x.astype(F32)
    xm = x + (around - x) * mu_ref[...]
    w = BRANCH_W
    r, k, v, lo = xm[:, 0:w], xm[:, w:2 * w], xm[:, 2 * w:3 * w], xm[:, 3 * w:4 * w]
    ones = ones_ref[...]
    lo_a, lo_g = lo[:, 0:LANES], lo[:, LANES:2 * LANES]
    th = jnp.tanh(lo_a)
    gate = _mm(_sigmoid(lo_g), g2_ref[LANES:2 * LANES, :], mode=mode)
    kx = k * kk_ref[...]
    kk = kx * lax.rsqrt(_mm_exact(kx * kx, ones, terms=NORM_TERMS) + 1e-6)
    sh_ref[:, 0:w] = r.astype(BF16)
    sh_ref[:, w:2 * w] = v.astype(BF16)
    sh_ref[:, 2 * w:3 * w] = (-kk).astype(BF16)
    bonus = jnp.zeros_like(v)
    for d, (dk_ref, dw_ref) in enumerate(((dk0_ref, dw0_ref), (dk1_ref, dw1_ref))):
        w_raw = w0_ref[d:d + 1, :] + _mm(th, w2_ref[d, 0:LANES, :], mode=mode)
        dw_ref[...] = -math.exp(-0.5) * _sigmoid(w_raw)
        a = _sigmoid(a0_ref[d:d + 1, :] + _mm(lo_a, a2_ref[d, 0:LANES, :], mode=mode))
        kd = k * (1.0 + (a - 1.0) * ka_ref[...])
        dk_ref[:, 0:w] = kd.astype(BF16)
        dk_ref[:, w:2 * w] = (kk * a).astype(BF16)
        bonus = bonus + _mm_exact(r * kd * rk_ref[...], ones, terms=NORM_TERMS) * v
    fin_ref[:, 0:w] = gate
    fin_ref[:, w:2 * w] = bonus


def _gdn_prep(p, prev, nxt, first, lastb, conv_ref, alog_ref, dt_ref, eb_ref, ea_ref, ones_ref,
              sh_ref, dk0_ref, dk1_ref, dw0_ref, dw1_ref, fin_ref):
    w = BRANCH_W
    x = p[:, 0:3 * w].astype(F32)
    top = jnp.where(first, 0.0, prev[:, 0:3 * w].astype(F32))
    bot = jnp.where(lastb, 0.0, nxt[:, 0:3 * w].astype(F32))
    xe = jnp.concatenate([top, x, bot], axis=0)
    ext = PREP_ROWS + 2 * HALO
    acc = jnp.zeros_like(x)
    for s in range(B_CONV):
        shift = (B_CONV // 2 - s) % ext
        rolled = xe if shift == 0 else pltpu.roll(xe, shift, axis=0)
        acc = acc + rolled[HALO:HALO + PREP_ROWS] * conv_ref[s:s + 1, :]
    qkv = _silu(acc)
    ones = ones_ref[...]

    def l2n(t):
        return t * lax.rsqrt(_mm_exact(t * t, ones, terms=NORM_TERMS) + 1e-6)

    q = l2n(qkv[:, 0:w]) * (HEAD ** -0.5)
    k = l2n(qkv[:, w:2 * w])
    v = qkv[:, 2 * w:3 * w]
    sh_ref[:, 0:w] = q.astype(BF16)
    sh_ref[:, w:2 * w] = v.astype(BF16)
    sh_ref[:, 2 * w:3 * w] = k.astype(BF16)
    sr = p[:, 4 * w:4 * w + LANES].astype(F32)
    beta_all = _sigmoid(sr)
    g_all = -jnp.exp(alog_ref[...]) * _softplus(sr + dt_ref[...])
    for d, (dk_ref, dw_ref) in enumerate(((dk0_ref, dw0_ref), (dk1_ref, dw1_ref))):
        beta = _mm_exact(beta_all, eb_ref[d])
        g = _mm_exact(g_all, ea_ref[d])
        kb = k * beta
        dw_ref[...] = g
        dk_ref[:, 0:w] = kb.astype(BF16)
        dk_ref[:, w:2 * w] = (-jnp.exp(g) * kb).astype(BF16)
    fin_ref[...] = p[:, 3 * w:4 * w].astype(F32)


SEG_GLA, SEG_GDN, SEG_ATT, SEG_RWKV = range(4)
N_RWKV_CONSTS, N_GDN_CONSTS = 11, 6
RWKV_OUTS = [(3, BF16), (2, BF16), (2, BF16), (1, F32), (1, F32), (2, F32)]
GDN_OUTS = [(3, BF16), (2, BF16), (2, BF16), (1, F32), (1, F32), (1, F32)]


def _stream_parts(xs):
    return list(xs) if isinstance(xs, (tuple, list)) else [xs]


def _stream_specs(xs, block_rows, halo_rows=0):
    specs, args, starts = [], [], []
    base = 0
    for part in _stream_parts(xs):
        d = part.shape[-1]
        nblk = part.shape[1] // block_rows
        own = lambda b, i, base=base, nblk=nblk: (b, jnp.clip(i - base, 0, nblk - 1), 0)
        specs.append(pl.BlockSpec((None, block_rows, d), own))
        args.append(part)
        if halo_rows:
            per, last = block_rows // halo_rows, part.shape[1] // halo_rows - 1
            prev = lambda b, i, base=base, per=per, last=last: (b, jnp.clip((i - base) * per - 1, 0, last), 0)
            nxt = lambda b, i, base=base, per=per, last=last: (b, jnp.clip((i - base + 1) * per, 0, last), 0)
            specs += [pl.BlockSpec((None, halo_rows, d), prev), pl.BlockSpec((None, halo_rows, d), nxt)]
            args += [part, part]
        starts.append(base)
        base += nblk
    return specs, args, starts


def _stream_block(refs, starts):
    val = refs[0][...]
    for ref, start in zip(refs[1:], starts[1:]):
        val = jnp.where(pl.program_id(1) >= start, ref[...], val)
    return val


def _front_kernel(*refs, n_lat_blocks, starts):
    it = iter(refs)
    x_refs = [next(it) for _ in range(3 * len(starts))]
    x_own, x_prev, x_next = (_stream_block(x_refs[k::3], starts) for k in range(3))
    sc_ref, csc_ref, sh_ref, csh_ref, g_ref = (next(it) for _ in range(5))
    w_refs = [next(it) for _ in range(4)]
    cos_ref, sin_ref = next(it), next(it)
    rwkv_consts = [next(it) for _ in range(N_RWKV_CONSTS)]
    gdn_consts = [next(it) for _ in range(N_GDN_CONSTS)]
    pc_ref, pd_ref = next(it), next(it)
    rwkv_outs = [next(it) for _ in RWKV_OUTS]
    gdn_outs = [next(it) for _ in GDN_OUTS]

    first, lastb = _seq_edges(n_lat_blocks)
    is_ctx = pl.program_id(1) >= n_lat_blocks
    scale = jnp.where(is_ctx, csc_ref[...], sc_ref[...])
    shift = jnp.where(is_ctx, csh_ref[...], sh_ref[...])
    h = _prenorm(x_own, g_ref[...], scale, shift).astype(BF16)
    halo = jnp.concatenate([x_prev, x_next], axis=0)
    h_halo = _prenorm(halo, g_ref[...], scale, shift).astype(BF16)

    def proj(rows, which):
        return jnp.dot(rows, w_refs[which][...], preferred_element_type=F32)

    pc_ref[...] = proj(h, SEG_GLA).astype(BF16)
    att = proj(h, SEG_ATT)
    w = BRANCH_W
    for c0 in range(0, att.shape[1], w):
        part = att[:, c0:c0 + w]
        if c0 < 2 * w:
            part = _rope(part, cos_ref[...], sin_ref[...])
        pd_ref[:, c0:c0 + w] = part.astype(BF16)

    h_ext = jnp.concatenate([h, h_halo], axis=0)
    n = h.shape[0]
    pb = proj(h_ext, SEG_GDN).astype(BF16)
    _gdn_prep(pb[0:n], pb[n:n + HALO], pb[n + HALO:], first, lastb, *gdn_consts, *gdn_outs)
    pa = proj(h_ext, SEG_RWKV).astype(BF16)
    _rwkv_prep(pa[0:n], pa[n:n + HALO], pa[n + HALO:], first, lastb, *rwkv_consts, *rwkv_outs)


def _front(xs, mod, gain, weights, cos, sin, rwkv_consts, gdn_consts, n_lat_rows):
    parts = _stream_parts(xs)
    batch, rows = parts[0].shape[0], sum(t.shape[1] for t in parts)
    x_specs, x_args, starts = _stream_specs(xs, PREP_ROWS, HALO)
    assert len(rwkv_consts) == N_RWKV_CONSTS and len(gdn_consts) == N_GDN_CONSTS and len(weights) == 4
    tab = pl.BlockSpec((PREP_ROWS, BRANCH_W), lambda b, i: (i, 0))
    consts = list(rwkv_consts) + list(gdn_consts)
    outs = [(weights[s].shape[1], BF16) for s in (SEG_GLA, SEG_ATT)]
    outs += [(n * BRANCH_W, dt) for n, dt in RWKV_OUTS + GDN_OUTS]
    out_specs = [pl.BlockSpec((None, PREP_ROWS, lanes), lambda b, i: (b, i, 0)) for lanes, _ in outs]
    out_shape = [jax.ShapeDtypeStruct((batch, rows, lanes), dt) for lanes, dt in outs]
    res = pl.pallas_call(
        functools.partial(_front_kernel, n_lat_blocks=n_lat_rows // PREP_ROWS, starts=tuple(starts)),
        grid=(batch, rows // PREP_ROWS),
        in_specs=x_specs + _mod_specs(1, batch) + _mod_specs(0, batch)
        + [_vec_spec(gain.shape)] + [_vec_spec(w.shape) for w in weights] + [tab, tab]
        + [_vec_spec(c.shape) for c in consts],
        out_specs=out_specs,
        out_shape=out_shape,
        compiler_params=_cparams(("parallel", "parallel")),
        name="front_proj_prep",
    )(*x_args, mod, mod, mod, mod, gain, *weights, cos, sin, *consts)
    pc, pd = res[0], res[1]
    a_sh, a_dk0, a_dk1, a_dw0, a_dw1, a_fin = res[2:8]
    b_sh, b_dk0, b_dk1, b_dw0, b_dw1, b_fin = res[8:14]
    return pc, pd, (a_sh, (a_dk0, a_dk1), (a_dw0, a_dw1), a_fin), (b_sh, (b_dk0, b_dk1), (b_dw0, b_dw1), b_fin)


def _window_bias():
    iq = np.arange(2 * ATT_BLOCK)[:, None] % ATT_BLOCK
    ik = np.arange(ATT_BLOCK)[None, :]
    ok = np.stack([ik >= iq, np.ones_like(ik >= iq), ik <= iq])
    assert WINDOW == ATT_BLOCK
    return np.where(ok, 0.0, NEG_BIG).astype(np.float32)


ATT_STEP = 2


N_ATT_REFS = 5 + 2 * (ATT_STEP + 2)


def _attn_rows(refs, n_lat_rows):
    sink_ref, q_ref, kc_ref, vc_ref = refs[0:4]
    nband = ATT_STEP + 2
    k_all, v_all = refs[4:4 + nband], refs[4 + nband:4 + 2 * nband]
    bias_ref = refs[4 + 2 * nband]
    blk = ATT_BLOCK
    lane = lax.broadcasted_iota(jnp.int32, (blk, 2 * HEAD), 1)
    row2 = lax.broadcasted_iota(jnp.int32, (2 * blk, 1), 0)
    out_rows = []
    for u in range(ATT_STEP):
        out_cols = []
        t = pl.program_id(1) * ATT_STEP + u
        rows = slice(u * blk, (u + 1) * blk)
        q = q_ref[rows, :] * (HEAD ** -0.5)
        k_band, v_band = k_all[u:u + 3], v_all[u:u + 3]
        q_lat = t * blk < n_lat_rows
        in_seq = [q_lat & (t >= 1), q_lat, q_lat & ((t + 1) * blk < n_lat_rows)]
        bias = [jnp.where(in_seq[j], bias_ref[j], NEG_BIG) for j in range(3)]
        for g in range(D_KV_HEADS):
            cols = slice(g * 2 * HEAD, (g + 1) * 2 * HEAD)
            qg = q[:, cols]
            zero = jnp.zeros((), qg.dtype)
            qs = jnp.concatenate([jnp.where(lane < HEAD, qg, zero), jnp.where(lane >= HEAD, qg, zero)], axis=0)
            s_ctx = _mm(qs, kc_ref[:, cols], NT)
            s_band = [_mm(qs, k_band[j][:, cols], NT) + bias[j] for j in range(3)]
            sink = jnp.where(row2 < blk, sink_ref[2 * g], sink_ref[2 * g + 1])
            m = jnp.maximum(jnp.max(s_ctx, axis=-1, keepdims=True), sink)
            for s in s_band:
                m = jnp.maximum(m, jnp.max(s, axis=-1, keepdims=True))
            p_ctx = jnp.exp(s_ctx - m)
            den = jnp.sum(p_ctx, axis=-1, keepdims=True) + jnp.exp(sink - m)
            acc = _mm(p_ctx, vc_ref[:, cols], NN)
            for j in range(3):
                pj = jnp.exp(s_band[j] - m)
                den = den + jnp.sum(pj, axis=-1, keepdims=True)
                acc = acc + _mm(pj, v_band[j][:, cols], NN)
            og = acc / den
            out_cols.append(jnp.where(lane < HEAD, og[0:blk], og[blk:2 * blk]))
        out_rows.append(jnp.concatenate(out_cols, axis=1))
    return jnp.concatenate(out_rows, axis=0)


ATT_ROWS = ATT_STEP * ATT_BLOCK


def _attention_inputs(p, sink, n_lat_rows):
    _, rows, _ = p.shape
    blk = ATT_BLOCK
    nb = rows // blk
    w = BRANCH_W
    n_ctx_rows = rows - n_lat_rows
    assert n_lat_rows % n_ctx_rows == 0 and n_ctx_rows % ATT_ROWS == 0

    def band(col, off):
        def index(b, t):
            return (b, jnp.clip(t * ATT_STEP + off, 0, nb - 1), col)
        return pl.BlockSpec((None, blk, w), index)

    offsets = range(-1, ATT_STEP + 1)
    ctx = lambda col: pl.BlockSpec((None, n_ctx_rows, w), lambda b, t: (b, n_lat_rows // n_ctx_rows, col))
    bias = jnp.asarray(_window_bias())
    specs = [pl.BlockSpec(memory_space=pltpu.SMEM), pl.BlockSpec((None, ATT_ROWS, w), lambda b, t: (b, t, 0)),
             ctx(1), ctx(2)]
    specs += [band(1, o) for o in offsets] + [band(2, o) for o in offsets] + [_vec_spec(bias.shape)]
    args = [sink] + [p] * (3 + 2 * len(offsets)) + [bias]
    assert len(specs) == N_ATT_REFS
    return specs, args


def _merge_kernel(*refs, n_lat_rows, starts):
    x = _stream_block(refs[0:len(starts)], starts)
    refs = refs[len(starts):]
    sc_ref, csc_ref, sh_ref, csh_ref, gm_ref, cgm_ref, g0_ref, g1_ref, ya_ref, yb_ref, yc_ref = refs[0:11]
    att_refs = refs[11:11 + N_ATT_REFS]
    wg_ref, gb_ref, wb_ref, wo_ref, o_ref = refs[11 + N_ATT_REFS:]
    is_ctx = _ctx_rows(x.shape[0], n_lat_rows)
    scale = jnp.where(is_ctx, csc_ref[...], sc_ref[...])
    shift = jnp.where(is_ctx, csh_ref[...], sh_ref[...])
    gmod = jnp.where(is_ctx, cgm_ref[...], gm_ref[...])
    h = _prenorm(x, g0_ref[...], scale, shift).astype(BF16)
    branches = [ya_ref[...], yb_ref[...], yc_ref[...], _attn_rows(att_refs, n_lat_rows)]
    acc = jnp.zeros(x.shape, F32)
    for i, y in enumerate(branches):
        pre = jnp.dot(h, wg_ref[:, i * D_MODEL:(i + 1) * D_MODEL], preferred_element_type=F32)
        gate = _sigmoid(pre + gb_ref[i:i + 1, :])
        acc = acc + gate * jnp.dot(y.astype(BF16), wb_ref[i], preferred_element_type=F32)
    out = jnp.dot(acc.astype(BF16), wo_ref[...], preferred_element_type=F32)
    ms = jnp.mean(out * out, axis=-1, keepdims=True)
    o_ref[...] = x + gmod * (out * lax.rsqrt(ms + NORM_EPS) * g1_ref[...])


def _merge(xs, mod, gain0, gain1, ys, p_att, sink, wg, gate_b, wb, wo, n_lat_rows, out_rows):
    parts = _stream_parts(xs)
    batch, d = parts[0].shape[0], parts[0].shape[-1]
    rows = out_rows
    tm = ATT_ROWS
    assert rows % tm == 0
    x_specs, x_args, starts = _stream_specs(xs, tm)
    tile = lambda width: pl.BlockSpec((None, tm, width), lambda b, i: (b, i, 0))
    consts = [wg, gate_b, wb, wo]
    att_specs, att_args = _attention_inputs(p_att, sink, n_lat_rows)
    return pl.pallas_call(
        functools.partial(_merge_kernel, n_lat_rows=n_lat_rows, starts=tuple(starts)),
        grid=(batch, rows // tm),
        in_specs=x_specs + _mod_specs(1, batch) + _mod_specs(0, batch) + _mod_specs(2, batch)
        + [_vec_spec(gain0.shape), _vec_spec(gain1.shape)]
        + [tile(BRANCH_W)] * 3 + att_specs + [_vec_spec(c.shape) for c in consts],
        out_specs=tile(d),
        out_shape=jax.ShapeDtypeStruct((batch, rows, d), F32),
        compiler_params=_cparams(("parallel", "parallel")),
        name="attn_merge_out",
    )(*x_args, *([mod] * 6), gain0, gain1, *ys, *att_args, *consts)


FFN_CHUNK = 512


def _ffn_kernel(x_ref, sc_ref, csc_ref, sh_ref, csh_ref, gm_ref, cgm_ref, g2_ref, g3_ref, w1_ref, w2_ref,
                o_ref, *, n_lat_rows):
    x = x_ref[...]
    is_ctx = _ctx_rows(x.shape[0], n_lat_rows)
    scale = jnp.where(is_ctx, csc_ref[...], sc_ref[...])
    shift = jnp.where(is_ctx, csh_ref[...], sh_ref[...])
    h = _prenorm(x, g2_ref[...], scale, shift).astype(BF16)
    hidden = w2_ref.shape[0]
    out = jnp.zeros(x.shape, F32)
    for lo in range(0, hidden, FFN_CHUNK):
        hi = min(lo + FFN_CHUNK, hidden)
        gt = jnp.dot(h, w1_ref[:, lo:hi], preferred_element_type=F32)
        up = jnp.dot(h, w1_ref[:, hidden + lo:hidden + hi], preferred_element_type=F32)
        out = out + jnp.dot((_silu(gt) * up).astype(BF16), w2_ref[lo:hi, :], preferred_element_type=F32)
    ms = jnp.mean(out * out, axis=-1, keepdims=True)
    gmod = jnp.where(is_ctx, cgm_ref[...], gm_ref[...])
    o_ref[...] = x + gmod * (out * lax.rsqrt(ms + NORM_EPS) * g3_ref[...])


def _ffn(xs, mod, gain2, gain3, w1, w2, n_lat_rows):
    batch, rows, d = xs.shape
    tm = _row_tile(rows, 544)
    tile = pl.BlockSpec((None, tm, d), lambda b, i: (b, i, 0))
    consts = [gain2, gain3, w1, w2]
    return pl.pallas_call(
        functools.partial(_ffn_kernel, n_lat_rows=n_lat_rows),
        grid=(batch, rows // tm),
        in_specs=[tile] + _mod_specs(4, batch) + _mod_specs(3, batch) + _mod_specs(5, batch)
        + [_vec_spec(c.shape) for c in consts],
        out_specs=tile,
        out_shape=jax.ShapeDtypeStruct((batch, rows, d), F32),
        compiler_params=_cparams(("parallel", "parallel")),
        name="swiglu",
    )(xs, *([mod] * 6), *consts)


def _pad_cols(w, width):
    return jnp.pad(w, ((0, 0), (0, width - w.shape[1])))


def _layer_weights(w_in, mu, w2, a2, g2, conv, a_log, dt_bias, gw2, gb):
    wts = {}
    off_b, off_c, off_d, off_g = A_IN, A_IN + B_IN, A_IN + B_IN + C_IN, A_IN + B_IN + C_IN + D_IN
    w = BRANCH_W
    wts["wa"] = _pad_cols(w_in[:, 0:A_IN], 4 * w).astype(BF16)
    wts["mu"] = _pad_cols(mu[None, :], 4 * w)
    lo = np.cumsum((0,) + A_LORA)
    place = lambda m, r0: jnp.zeros((w, w), F32).at[r0:r0 + m.shape[0], :].set(m)
    wts["w2p"] = jnp.stack([place(w2[0], lo[0]), place(w2[1], lo[1])])
    wts["a2p"] = jnp.stack([place(a2[0], lo[2]), place(a2[1], lo[3])])
    wts["g2p"] = place(g2, lo[4])

    wb = w_in[:, off_b:off_b + B_IN]
    nh = N_HEADS
    n_scalar = 4 * nh
    wts["wb"] = jnp.concatenate([wb[:, 0:3 * w], wb[:, 3 * w + n_scalar:],
                                 _pad_cols(wb[:, 3 * w:3 * w + n_scalar], LANES)], axis=1).astype(BF16)
    expand = np.zeros((4, LANES, w), np.float32)
    for grp in range(4):
        for h in range(nh):
            expand[grp, grp * nh + h, h * HEAD:(h + 1) * HEAD] = 1.0
    wts["eb"] = jnp.asarray(expand[0:2])
    wts["ea"] = jnp.asarray(expand[2:4])
    vec = lambda t: jnp.zeros((1, LANES), F32).at[0, 2 * nh:4 * nh].set(t.reshape(-1))
    wts["alog"] = vec(a_log)
    wts["dt"] = vec(dt_bias)
    wts["conv"] = conv

    wc = w_in[:, off_c:off_c + C_IN]
    pad_heads = lambda m: jnp.pad(m.reshape(m.shape[0], nh, C_DK),
                                  ((0, 0), (0, 0), (0, HEAD - C_DK))).reshape(m.shape[0], w)
    qc, kc, vc = wc[:, 0:C_QK], wc[:, C_QK:2 * C_QK], wc[:, 2 * C_QK:2 * C_QK + w]
    loc = wc[:, 2 * C_QK + w:2 * C_QK + w + 2 * C_GATE_R]
    gc = wc[:, 2 * C_QK + w + 2 * C_GATE_R:]
    wts["wc"] = jnp.concatenate([pad_heads(qc), pad_heads(kc), vc, gc, _pad_cols(loc, LANES)], axis=1).astype(BF16)
    gwp = jnp.zeros((2, LANES, w), F32)
    for d in range(2):
        gwp = gwp.at[d, d * C_GATE_R:(d + 1) * C_GATE_R, :].set(pad_heads(gw2[d]))
    wts["gwp"] = gwp
    wts["gbp"] = pad_heads(gb)
    wts["glane"] = jnp.asarray((np.arange(w) % HEAD < C_DK).astype(np.float32))[None, :]

    wd = w_in[:, off_d:off_d + D_IN]
    qd = wd[:, 0:w]
    dup = lambda m: jnp.concatenate([m[:, 0:HEAD], m[:, 0:HEAD], m[:, HEAD:], m[:, HEAD:]], axis=1)
    wts["wd"] = jnp.concatenate([qd, dup(wd[:, w:w + 2 * HEAD]), dup(wd[:, w + 2 * HEAD:])], axis=1).astype(BF16)
    wts["wg"] = w_in[:, off_g:].astype(BF16)
    return wts


def _rope_tables(n_lat_rows, n_ctx_rows):
    quarter = HEAD // 4
    inv = ROPE_BASE ** (-np.arange(quarter, dtype=np.float32) / quarter)
    pos = np.arange(n_lat_rows)
    rows = (pos // GRID_W).astype(np.float32)
    cols = (pos % GRID_W).astype(np.float32)
    inv = jnp.asarray(inv)
    ang_r = jnp.asarray(rows)[:, None] * inv[None, :]
    ang_c = jnp.asarray(cols)[:, None] * inv[None, :]
    cos = jnp.concatenate([jnp.cos(ang_r)] * 2 + [jnp.cos(ang_c)] * 2, axis=1)
    sin = jnp.concatenate([-jnp.sin(ang_r), jnp.sin(ang_r), -jnp.sin(ang_c), jnp.sin(ang_c)], axis=1)
    cos = jnp.concatenate([cos, jnp.ones((n_ctx_rows, HEAD), F32)], axis=0)
    sin = jnp.concatenate([sin, jnp.zeros((n_ctx_rows, HEAD), F32)], axis=0)
    return jnp.tile(cos, (1, N_HEADS)), jnp.tile(sin, (1, N_HEADS))


RWKV_SLOTS = {"r": ("sh", 0), "v": ("sh", 1), "a": ("sh", 2), "w": ("dw", 0), "k": ("dk", 0), "b": ("dk", 1)}
GDN_SLOTS = RWKV_SLOTS
GLA_SLOTS = {"r": ("sh", 0), "k": ("sh", 1), "v": ("sh", 2)}


def kernel(x, c, ctx, c_ctx, ada_w, ada_b, norm_g, w_in, gate_b, w_branch, w_out, rwkv_mu, rwkv_w0, rwkv_w2, rwkv_a0, rwkv_a2, rwkv_g2, rwkv_kk, rwkv_ka, rwkv_rk, rwkv_ln_g, rwkv_ln_b, gdn_conv, gdn_a_log, gdn_dt_bias, gdn_norm_g, gla_gw2, gla_gb, gla_norm_g, attn_sink, ffn_w1, ffn_w2):
    batch, n_lat, d = x.shape
    n_ctx = ctx.shape[1]
    depth = ada_w.shape[0]
    assert n_ctx % PREP_ROWS == 0 and n_lat % PREP_ROWS == 0 and d == D_MODEL

    mod_rows = 8 * ((batch + 1 + 7) // 8)
    c_rows = jnp.concatenate([c, c_ctx[None, :], jnp.zeros((mod_rows - batch - 1, d), F32)], axis=0)
    mod_all = _modulation(c_rows, ada_w, ada_b)
    cos, sin = _rope_tables(n_lat, n_ctx)

    xs = (x, ctx)
    rows = n_lat + n_ctx
    row = lambda t: t.reshape(1, -1)
    for l in range(depth):
        out_rows = n_lat if l == depth - 1 else rows
        mod = mod_all[l].reshape(mod_rows, 1, 6 * d)
        ng = norm_g[l]
        wts = _layer_weights(w_in[l], rwkv_mu[l], rwkv_w2[l], rwkv_a2[l], rwkv_g2[l], gdn_conv[l],
                             gdn_a_log[l], gdn_dt_bias[l], gla_gw2[l], gla_gb[l])
        ones = jnp.asarray(_head_block_ones())
        around = jnp.asarray(0.5 * (_shift_matrix(-1) + _shift_matrix(1)), BF16)
        rwkv_consts = [around, wts["mu"], wts["w2p"], wts["a2p"], wts["g2p"], rwkv_w0[l], rwkv_a0[l],
                       row(rwkv_kk[l]), row(rwkv_ka[l]), row(rwkv_rk[l]), ones]
        gdn_consts = [wts["conv"], wts["alog"], wts["dt"], wts["eb"], wts["ea"], ones]
        p_gla, p_att, rwkv_ops, gdn_ops = _front(xs, mod, row(ng[0]), (wts["wc"], wts["wb"], wts["wd"], wts["wa"]),
                                                 cos, sin, rwkv_consts, gdn_consts, n_lat)

        sh, dks, dws, fin = rwkv_ops
        ya = _bidir_scan(sh, dks, dws, n_lat, fin, row(rwkv_ln_g[l]), row(rwkv_ln_b[l]), finish="groupnorm",
                         scalar_decay=False, lowrank=True, slots=RWKV_SLOTS)

        sh, dks, dws, fin = gdn_ops
        gnorm = row(jnp.tile(gdn_norm_g[l], N_HEADS))
        yb = _bidir_scan(sh, dks, dws, n_lat, fin, gnorm, gnorm, finish="rms",
                         scalar_decay=True, lowrank=True, slots=GDN_SLOTS)

        yc = _gla_scan(p_gla, n_lat, wts["gwp"], wts["gbp"], wts["glane"], row(jnp.tile(gla_norm_g[l], N_HEADS)))

        xs = _merge(xs, mod, row(ng[0]), row(ng[1]), (ya, yb, yc), p_att, attn_sink[l], wts["wg"], gate_b[l],
                    w_branch[l].astype(BF16), w_out[l].astype(BF16), n_lat, out_rows)
        xs = _ffn(xs, mod, row(ng[2]), row(ng[3]), ffn_w1[l].astype(BF16), ffn_w2[l].astype(BF16), n_lat)
    return xs
```

```python
import functools
import math

import numpy as np
import jax
import jax.numpy as jnp
from jax import lax
from jax.experimental import pallas as pl
from jax.experimental.pallas import tpu as pltpu

F32 = jnp.float32
BF16 = jnp.bfloat16

D_MODEL = 1024
N_BRANCH = 4
BRANCH_W = 256
HEAD = 64
N_HEADS = BRANCH_W // HEAD
NORM_EPS = 1e-6
A_GN_EPS = 64e-5
A_LORA = (32, 32, 32, 32, 64)
A_IN = 3 * BRANCH_W + sum(A_LORA)
B_IN = 4 * BRANCH_W + 4 * N_HEADS
C_DK = 32
C_QK = N_HEADS * C_DK
C_GATE_R = 16
C_GATE_NORM = 16.0
C_IN = 2 * C_QK + 2 * BRANCH_W + 2 * C_GATE_R
D_KV_HEADS = 2
D_IN = BRANCH_W + 2 * D_KV_HEADS * HEAD
B_CONV = 7
WINDOW = 128
ROPE_BASE = 10000.0
GRID_W = 64
FFN_HIDDEN = 2816

LANES = 128
CHUNK = 64
PREP_ROWS = 256
HALO = 16
ATT_BLOCK = 128
VMEM_LIMIT = 48 * 1024 * 1024
NEG_BIG = -1e30

NN = (((1,), (0,)), ((), ()))
NT = (((1,), (1,)), ((), ()))
TN = (((0,), (0,)), ((), ()))


def _mm(a, b, dims=NN, mode="bf16"):
    if mode == "f32":
        return lax.dot_general(a, b, dims, precision=lax.Precision.HIGHEST, preferred_element_type=F32)
    return lax.dot_general(a.astype(BF16), b.astype(BF16), dims, preferred_element_type=F32)


def _sigmoid(x):
    return 1.0 / (1.0 + jnp.exp(-x))


def _silu(x):
    return x * _sigmoid(x)


def _softplus(x):
    return jnp.maximum(x, 0.0) + jnp.log1p(jnp.exp(-jnp.abs(x)))


def _cparams(sem):
    return pltpu.CompilerParams(dimension_semantics=sem, vmem_limit_bytes=VMEM_LIMIT)


def _mod_kernel(c_ref, w_ref, b_ref, o_ref):
    c = c_ref[...]
    o_ref[...] = _mm(_silu(c), w_ref[...], mode="f32") + b_ref[...]


def _modulation(c_rows, ada_w, ada_b):
    depth, d, n = ada_w.shape
    rows = c_rows.shape[0]
    tn = 1024
    return pl.pallas_call(
        _mod_kernel,
        grid=(depth, n // tn),
        in_specs=[
            pl.BlockSpec((rows, d), lambda l, j: (0, 0)),
            pl.BlockSpec((None, d, tn), lambda l, j: (l, 0, j)),
            pl.BlockSpec((None, 1, tn), lambda l, j: (l, 0, j)),
        ],
        out_specs=pl.BlockSpec((None, rows, tn), lambda l, j: (l, 0, j)),
        out_shape=jax.ShapeDtypeStruct((depth, rows, n), F32),
        compiler_params=_cparams(("parallel", "parallel")),
        name="adaln_mod",
    )(c_rows, ada_w, ada_b.reshape(depth, 1, n))


def _row_tile(rows, target):
    return max(t for t in range(8, target + 1, 8) if rows % t == 0)


def _mod_specs(which, batch):
    lat = pl.BlockSpec((None, 1, D_MODEL), lambda b, i, *_: (b, 0, which))
    ctx = pl.BlockSpec((None, 1, D_MODEL), lambda b, i, *_: (batch, 0, which))
    return [lat, ctx]


def _ctx_rows(tm, n_lat_rows):
    row = pl.program_id(1) * tm + lax.broadcasted_iota(jnp.int32, (tm, 1), 0)
    return row >= n_lat_rows


def _prenorm(x, gain, scale, shift):
    ms = jnp.mean(x * x, axis=-1, keepdims=True)
    return (x * lax.rsqrt(ms + NORM_EPS)) * (gain * (1.0 + scale)) + shift


def _rope(x, cos, sin):
    width = x.shape[-1]
    lane = lax.broadcasted_iota(jnp.int32, x.shape, 1)
    half, quarter = HEAD // 2, HEAD // 4
    swapped = jnp.where(lane % half < quarter, pltpu.roll(x, width - quarter, axis=1), pltpu.roll(x, quarter, axis=1))
    return x * cos + swapped * sin


def _head_block_ones():
    idx = np.arange(BRANCH_W)
    return (idx[:, None] // HEAD == idx[None, :] // HEAD).astype(np.float32)


N_LEVELS = 6


def _level_mask(ri, ci, s, reverse):
    b = 1 << s
    blk = (ri // (2 * b)) == (ci // (2 * b))
    hi_r, hi_c = (ri // b) % 2 == 1, (ci // b) % 2 == 1
    return blk & ((hi_c & ~hi_r) if reverse else (hi_r & ~hi_c))


SCAN_HEADS = 2
GROUP_W = SCAN_HEADS * HEAD
N_GROUPS = N_HEADS // SCAN_HEADS


def _scan_masks(reverse):
    n = SCAN_HEADS * CHUNK
    ri = np.arange(CHUNK)[:, None]
    ci = np.arange(n)[None, :] % CHUNK
    strict, incl = (ci > ri, ci >= ri) if reverse else (ci < ri, ci <= ri)
    compact = [strict, incl, ci == ri] + [_level_mask(ri, ci, s, reverse) for s in range(N_LEVELS)]
    r = np.arange(n)[:, None]
    c = np.arange(n)[None, :]
    block = [(r // CHUNK) == (c // CHUNK), r == c]
    return np.stack(compact).astype(np.float32), np.stack(block).astype(np.float32)


EXACT_TERMS = 3
NORM_TERMS = 2


def _tri(reverse):
    i = np.arange(CHUNK)
    m = (i[None, :] >= i[:, None]) if reverse else (i[None, :] <= i[:, None])
    return np.tile(m.astype(np.float32), (1, EXACT_TERMS))


def _split_terms(x, terms):
    out = []
    for _ in range(terms - 1):
        part = x.astype(BF16)
        out.append(part)
        x = x - part.astype(F32)
    out.append(x.astype(BF16))
    return out


def _mm_exact(a, b, dims=NN, split="a", terms=EXACT_TERMS):
    assert dims == NN
    if split == "a":
        lhs = jnp.concatenate(_split_terms(a, terms), axis=1)
        rhs = jnp.concatenate([b.astype(BF16)] * terms, axis=0)
    else:
        lhs = a.astype(BF16)
        rhs = jnp.concatenate(_split_terms(b, terms), axis=0)
    return lax.dot_general(lhs, rhs, dims, preferred_element_type=F32)


C_STRICT, C_INCL, C_EYE, C_LVL0 = 0, 1, 2, 3
B_SAME, B_EYE = 0, 1
SCAN_BATCH = 8
SCAN_ROWS = 256
assert CHUNK == HEAD and (1 << N_LEVELS) == CHUNK


def _scan_chunk(vals, st, tri, ones, cm_ref, bm_ref, same_bf, *, reverse, scalar_decay, lowrank, mm_mode):
    same = bm_ref[B_SAME]

    def get(name):
        return vals[name]

    def expand(x):
        return jnp.concatenate([x.astype(BF16)] * SCAN_HEADS, axis=0) * same_bf

    def keep(x, k):
        return jnp.where(cm_ref[k] > 0.5, x, 0.0)

    logw = get("w")
    r, k, v = get("r"), get("k"), get("v")
    cum = _mm_exact(tri, logw, split="b")
    cum_x = cum - logw
    last = 0 if reverse else CHUNK - 1
    total = cum[last:last + 1, :]

    if scalar_decay:
        diag = jnp.concatenate([cum] * SCAN_HEADS, axis=0) * bm_ref[B_EYE]
        cum_row = _mm_exact(ones, diag, split="b")
        d_ii = jnp.exp(jnp.where(cm_ref[C_INCL] > 0.5, cum - cum_row, NEG_BIG))
        r_q, k_q = r, k
        to_end = jnp.exp(total - cum)
        r_abs = r * jnp.exp(cum)
    else:
        ref_row = cum[CHUNK // 2:CHUNK // 2 + 1, :]
        e_ref = jnp.exp(ref_row)
        p_inv = jnp.exp(ref_row - cum)
        r_q, k_q = r * jnp.exp(cum - ref_row), k * p_inv
        to_end = p_inv * jnp.exp(total - ref_row)
        r_abs = r_q * e_ref

    k_e, v_e = expand(k_q), expand(v)
    if not lowrank:
        s_k = _mm(r_q, k_e, NT, mm_mode)
        a_rk = s_k * d_ii if scalar_decay else keep(s_k, C_INCL)
        y = _mm(r_abs, st, NT, mm_mode) + _mm(a_rk, v_e, NN, mm_mode)
        upd = _mm(v, k * to_end, TN, mm_mode)
        return y, st * jnp.exp(total) + upd * same

    a, b = get("a"), get("b")
    if scalar_decay:
        d_xi = jnp.exp(jnp.where(cm_ref[C_STRICT] > 0.5, cum_x - cum_row, NEG_BIG))
        a_q, b_q = a, b
    else:
        a_q, b_q = a * jnp.exp(cum_x - ref_row), b * p_inv
    lhs = jnp.concatenate([a_q, r_q], axis=0)
    scores = _mm(lhs, jnp.concatenate([expand(b_q), k_e], axis=0), NT, mm_mode)
    s_b, s_k = scores[:, 0:GROUP_W], scores[:, GROUP_W:]
    if scalar_decay:
        a_ab, a_rb = s_b[0:CHUNK] * d_xi, s_b[CHUNK:] * d_ii
        a_ak, a_rk = s_k[0:CHUNK] * d_xi, s_k[CHUNK:] * d_ii
    else:
        a_ab, a_rb = keep(s_b[0:CHUNK], C_STRICT), keep(s_b[CHUNK:], C_INCL)
        a_ak, a_rk = keep(s_k[0:CHUNK], C_STRICT), keep(s_k[CHUNK:], C_INCL)

    inv = cm_ref[C_EYE] + a_ab * cm_ref[C_LVL0]
    for s in range(1, N_LEVELS):
        c_s = expand(a_ab * cm_ref[C_LVL0 + s])
        inv = inv + _mm(_mm(inv, c_s, NN, mm_mode), expand(inv), NN, mm_mode)

    a_abs = a * jnp.exp(cum_x) if scalar_decay else a_q * e_ref
    from_state = _mm(jnp.concatenate([a_abs, r_abs], axis=0), st, NT, mm_mode)
    from_v = _mm(jnp.concatenate([a_ak, a_rk], axis=0), v_e, NN, mm_mode)
    both = from_state + from_v
    z = _mm(inv, expand(both[0:CHUNK]), NN, mm_mode)
    y = both[CHUNK:] + _mm(a_rb, expand(z), NN, mm_mode)
    upd = _mm(jnp.concatenate([v, z], axis=0), jnp.concatenate([k * to_end, b * to_end], axis=0), TN, mm_mode)
    return y, st * jnp.exp(total) + upd * same


def _scan_kernel(*refs, reverse, scalar_decay, lowrank, slots, sources, finish, mm_mode, lora, r_scale):
    it = iter(refs)
    src_refs = {name: next(it) for name in sources}
    sh_ref = src_refs[sources[0]]
    cm_ref, bm_ref, tri_ref = next(it), next(it), next(it)
    if lora:
        lo_ref, gw_ref, gb_ref, lane_ref = next(it), next(it), next(it), next(it)
    if finish:
        ob_ref, fin_ref, p1_ref, p2_ref, avg_ref = next(it), next(it), next(it), next(it), next(it)
    o_ref, st_ref = next(it), next(it)

    @pl.when(pl.program_id(1) == 0)
    def _():
        st_ref[...] = jnp.zeros_like(st_ref)

    nb, block_rows = sh_ref.shape[0], sh_ref.shape[1]
    n_chunks = block_rows // CHUNK
    n = nb * N_GROUPS
    tri = jnp.broadcast_to(tri_ref[...], (n,) + tri_ref.shape)
    ones = jnp.ones((n, CHUNK, EXACT_TERMS * SCAN_HEADS * CHUNK), F32)
    chunk = functools.partial(_scan_chunk, cm_ref=cm_ref, bm_ref=bm_ref, same_bf=bm_ref[B_SAME].astype(BF16),
                              reverse=reverse, scalar_decay=scalar_decay, lowrank=lowrank, mm_mode=mm_mode)

    def step(c, carry):
        rows = pl.ds(pl.multiple_of((n_chunks - 1 - c if reverse else c) * CHUNK, CHUNK), CHUNK)

        def groups(name):
            src, idx = slots[name]
            ref = src_refs[src]
            parts = [ref[:, rows, idx * BRANCH_W + g * GROUP_W:idx * BRANCH_W + (g + 1) * GROUP_W].astype(F32)
                     for g in range(N_GROUPS)]
            return jnp.stack(parts, axis=1).reshape(n, CHUNK, GROUP_W)

        vals = {name: groups(name) for name in slots}
        if lora:
            lo = lo_ref[:, rows, :].reshape(nb * CHUNK, lo_ref.shape[-1])
            z = _mm(lo, gw_ref[...], mode=mm_mode) + gb_ref[...]
            logw = ((-_softplus(-z) / C_GATE_NORM) * lane_ref[...]).reshape(nb, CHUNK, BRANCH_W)
            parts = [logw[:, :, g * GROUP_W:(g + 1) * GROUP_W] for g in range(N_GROUPS)]
            vals["w"] = jnp.stack(parts, axis=1).reshape(n, CHUNK, GROUP_W)
        if r_scale != 1.0:
            vals["r"] = vals["r"] * r_scale
        y, st_new = jax.vmap(chunk)(vals, st_ref[...], tri, ones)
        st_ref[...] = st_new
        y = y.reshape(nb, N_GROUPS, CHUNK, GROUP_W)
        y = jnp.concatenate([y[:, g] for g in range(N_GROUPS)], axis=-1)
        if not finish:
            o_ref[:, rows, :] = y
            return carry

        y = (y + ob_ref[:, rows, :]).reshape(nb * CHUNK, BRANCH_W)
        avg = avg_ref[...]
        if finish == "groupnorm":
            gate = fin_ref[:, rows, 0:BRANCH_W].reshape(y.shape)
            bonus = fin_ref[:, rows, BRANCH_W:2 * BRANCH_W].reshape(y.shape)
            cen = y - _mm_exact(y, avg)
            var = _mm_exact(cen * cen, avg)
            yn = cen * lax.rsqrt(var + A_GN_EPS) * p1_ref[...] + p2_ref[...]
            out = (yn + bonus) * gate
        else:
            ms = _mm_exact(y * y, avg)
            gate = fin_ref[:, rows, :].astype(F32).reshape(y.shape)
            out = y * lax.rsqrt(ms + NORM_EPS) * p1_ref[...] * _silu(gate)
        o_ref[:, rows, :] = out.reshape(nb, CHUNK, BRANCH_W).astype(o_ref.dtype)
        return carry

    lax.fori_loop(0, n_chunks, step, 0)


def _view(x):
    return x if isinstance(x, tuple) else (x, x.shape[-1], 0)


def _scan(srcs, n_lat_rows, *, reverse, scalar_decay, lowrank, slots, finish=None, fin_args=None,
          mm_mode="bf16", lora=None, r_scale=1.0):
    sources = tuple(srcs)
    views = [_view(srcs[s]) for s in sources]
    batch, rows, _ = views[0][0].shape
    nc, nlat = rows // SCAN_ROWS, n_lat_rows // SCAN_ROWS
    nctx = nc - nlat
    nb = math.gcd(batch, SCAN_BATCH)

    if reverse:
        def chunk(n):
            return nc - 1 - n
    else:
        def chunk(n):
            return jnp.where(n < nctx, nlat + n, n - nctx)

    def row_spec(width, col=0):
        return pl.BlockSpec((nb, SCAN_ROWS, width), lambda b, n: (b, chunk(n), col))

    def const_spec(shape):
        zeros = (0,) * len(shape)
        return pl.BlockSpec(shape, lambda b, n: zeros)

    cmask, bmask = (jnp.asarray(m) for m in _scan_masks(reverse))
    tri = jnp.asarray(_tri(reverse))
    in_specs = [row_spec(w, c) for _, w, c in views]
    in_specs += [const_spec(cmask.shape), const_spec(bmask.shape), const_spec(tri.shape)]
    args = [a for a, _, _ in views] + [cmask, bmask, tri]
    if lora:
        (lo, lo_w, lo_c), gw, gb, lane = _view(lora[0]), lora[1], lora[2], lora[3]
        in_specs += [row_spec(lo_w, lo_c), const_spec(gw.shape), const_spec(gb.shape), const_spec(lane.shape)]
        args += [lo, gw, gb, lane]
    if finish:
        ob, fin, p1, p2 = fin_args
        fin, fin_w, fin_c = _view(fin)
        avg = jnp.asarray(_head_block_ones() / HEAD)
        in_specs += [row_spec(BRANCH_W), row_spec(fin_w, fin_c), const_spec(p1.shape), const_spec(p2.shape),
                     const_spec(avg.shape)]
        args += [ob, fin, p1, p2, avg]
    kern = functools.partial(_scan_kernel, reverse=reverse, scalar_decay=scalar_decay, lowrank=lowrank,
                             slots=slots, sources=sources, finish=finish, mm_mode=mm_mode, lora=bool(lora),
                             r_scale=r_scale)
    return pl.pallas_call(
        kern,
        grid=(batch // nb, nc),
        in_specs=in_specs,
        out_specs=row_spec(BRANCH_W),
        out_shape=jax.ShapeDtypeStruct((batch, rows, BRANCH_W), BF16 if finish else F32),
        scratch_shapes=[pltpu.VMEM((nb * N_GROUPS, GROUP_W, GROUP_W), F32)],
        compiler_params=_cparams(("parallel", "arbitrary")),
        name="dplr_scan_" + ("bwd" if reverse else "fwd"),
    )(*args)


def _gla_scan(p, n_lat_rows, gwp, gbp, lane, norm_g):
    w = BRANCH_W
    srcs = {"sh": (p, 3 * w, 0)}
    lo = (p, gwp.shape[1], 4 * w // gwp.shape[1])
    kw = dict(scalar_decay=False, lowrank=False, slots=GLA_SLOTS, r_scale=C_DK ** -0.5)
    ob = _scan(srcs, n_lat_rows, reverse=True, lora=(lo, gwp[1], gbp[1:2], lane), **kw)
    return _scan(srcs, n_lat_rows, reverse=False, lora=(lo, gwp[0], gbp[0:1], lane), finish="rms",
                 fin_args=(ob, (p, w, 3), norm_g, norm_g), **kw)


def _bidir_scan(sh, dks, dws, n_lat_rows, fin, p1, p2, *, finish, **kw):
    def srcs(d):
        out = {"sh": sh, "dw": dws[d]}
        if dks is not None:
            out["dk"] = dks[d]
        return out

    ob = _scan(srcs(1), n_lat_rows, reverse=True, **kw)
    return _scan(srcs(0), n_lat_rows, reverse=False, finish=finish, fin_args=(ob, fin, p1, p2), **kw)


def _seq_edges(n_lat_blocks):
    i = pl.program_id(1)
    first = (i == 0) | (i == n_lat_blocks)
    lastb = (i == n_lat_blocks - 1) | (i == pl.num_programs(1) - 1)
    return first, lastb


def _vec_spec(shape):
    zeros = (0,) * len(shape)
    return pl.BlockSpec(shape, lambda b, i: zeros)


def _shift_matrix(offset):
    m = np.zeros((PREP_ROWS, PREP_ROWS + 2 * HALO), np.float32)
    t = np.arange(PREP_ROWS)
    src = t + offset
    col = np.where(src < 0, PREP_ROWS + HALO + src, np.where(src >= PREP_ROWS, HALO + src, src))
    m[t, col] = 1.0
    return m


def _rwkv_prep(x, prev, nxt, first, lastb, shift_ref, mu_ref, w2_ref, a2_ref, g2_ref, w0_ref, a0_ref, kk_ref,
               ka_ref, rk_ref, ones_ref, sh_ref, dk0_ref, dk1_ref, dw0_ref, dw1_ref, fin_ref, mode="bf16"):
    zero = jnp.zeros((), x.dtype)
    xe = jnp.concatenate([x, jnp.where(first, zero, prev), jnp.where(lastb, zero, nxt)], axis=0)
    around = jnp.dot(shift_ref[...], xe, preferred_element_type=F32)
    x = x.astype(F32)
    xm = x + (around - x) * mu_ref[...]
    w = BRANCH_W
    r, k, v, lo = xm[:, 0:w], xm[:, w:2 * w], xm[:, 2 * w:3 * w], xm[:, 3 * w:4 * w]
    ones = ones_ref[...]
    lo_a, lo_g = lo[:, 0:LANES], lo[:, LANES:2 * LANES]
    th = jnp.tanh(lo_a)
    gate = _mm(_sigmoid(lo_g), g2_ref[LANES:2 * LANES, :], mode=mode)
    kx = k * kk_ref[...]
    kk = kx * lax.rsqrt(_mm_exact(kx * kx, ones, terms=NORM_TERMS) + 1e-6)
    sh_ref[:, 0:w] = r.astype(BF16)
    sh_ref[:, w:2 * w] = v.astype(BF16)
    sh_ref[:, 2 * w:3 * w] = (-kk).astype(BF16)
    bonus = jnp.zeros_like(v)
    for d, (dk_ref, dw_ref) in enumerate(((dk0_ref, dw0_ref), (dk1_ref, dw1_ref))):
        w_raw = w0_ref[d:d + 1, :] + _mm(th, w2_ref[d, 0:LANES, :], mode=mode)
        dw_ref[...] = -math.exp(-0.5) * _sigmoid(w_raw)
        a = _sigmoid(a0_ref[d:d + 1, :] + _mm(lo_a, a2_ref[d, 0:LANES, :], mode=mode))
        kd = k * (1.0 + (a - 1.0) * ka_ref[...])
        dk_ref[:, 0:w] = kd.astype(BF16)
        dk_ref[:, w:2 * w] = (kk * a).astype(BF16)
        bonus = bonus + _mm_exact(r * kd * rk_ref[...], ones, terms=NORM_TERMS) * v
    fin_ref[:, 0:w] = gate
    fin_ref[:, w:2 * w] = bonus


def _gdn_prep(p, prev, nxt, first, lastb, conv_ref, alog_ref, dt_ref, eb_ref, ea_ref, ones_ref,
              sh_ref, dk0_ref, dk1_ref, dw0_ref, dw1_ref, fin_ref):
    w = BRANCH_W
    x = p[:, 0:3 * w].astype(F32)
    top = jnp.where(first, 0.0, prev[:, 0:3 * w].astype(F32))
    bot = jnp.where(lastb, 0.0, nxt[:, 0:3 * w].astype(F32))
    xe = jnp.concatenate([top, x, bot], axis=0)
    ext = PREP_ROWS + 2 * HALO
    acc = jnp.zeros_like(x)
    for s in range(B_CONV):
        shift = (B_CONV // 2 - s) % ext
        rolled = xe if shift == 0 else pltpu.roll(xe, shift, axis=0)
        acc = acc + rolled[HALO:HALO + PREP_ROWS] * conv_ref[s:s + 1, :]
    qkv = _silu(acc)
    ones = ones_ref[...]

    def l2n(t):
        return t * lax.rsqrt(_mm_exact(t * t, ones, terms=NORM_TERMS) + 1e-6)

    q = l2n(qkv[:, 0:w]) * (HEAD ** -0.5)
    k = l2n(qkv[:, w:2 * w])
    v = qkv[:, 2 * w:3 * w]
    sh_ref[:, 0:w] = q.astype(BF16)
    sh_ref[:, w:2 * w] = v.astype(BF16)
    sh_ref[:, 2 * w:3 * w] = k.astype(BF16)
    sr = p[:, 4 * w:4 * w + LANES].astype(F32)
    beta_all = _sigmoid(sr)
    g_all = -jnp.exp(alog_ref[...]) * _softplus(sr + dt_ref[...])
    for d, (dk_ref, dw_ref) in enumerate(((dk0_ref, dw0_ref), (dk1_ref, dw1_ref))):
        beta = _mm_exact(beta_all, eb_ref[d])
        g = _mm_exact(g_all, ea_ref[d])
        kb = k * beta
        dw_ref[...] = g
        dk_ref[:, 0:w] = kb.astype(BF16)
        dk_ref[:, w:2 * w] = (-jnp.exp(g) * kb).astype(BF16)
    fin_ref[...] = p[:, 3 * w:4 * w].astype(F32)


SEG_GLA, SEG_GDN, SEG_ATT, SEG_RWKV = range(4)
N_RWKV_CONSTS, N_GDN_CONSTS = 11, 6
RWKV_OUTS = [(3, BF16), (2, BF16), (2, BF16), (1, F32), (1, F32), (2, F32)]
GDN_OUTS = [(3, BF16), (2, BF16), (2, BF16), (1, F32), (1, F32), (1, F32)]


def _stream_parts(xs):
    return list(xs) if isinstance(xs, (tuple, list)) else [xs]


def _stream_specs(xs, block_rows, halo_rows=0):
    specs, args, starts = [], [], []
    base = 0
    for part in _stream_parts(xs):
        d = part.shape[-1]
        nblk = part.shape[1] // block_rows
        own = lambda b, i, base=base, nblk=nblk: (b, jnp.clip(i - base, 0, nblk - 1), 0)
        specs.append(pl.BlockSpec((None, block_rows, d), own))
        args.append(part)
        if halo_rows:
            per, last = block_rows // halo_rows, part.shape[1] // halo_rows - 1
            prev = lambda b, i, base=base, per=per, last=last: (b, jnp.clip((i - base) * per - 1, 0, last), 0)
            nxt = lambda b, i, base=base, per=per, last=last: (b, jnp.clip((i - base + 1) * per, 0, last), 0)
            specs += [pl.BlockSpec((None, halo_rows, d), prev), pl.BlockSpec((None, halo_rows, d), nxt)]
            args += [part, part]
        starts.append(base)
        base += nblk
    return specs, args, starts


def _stream_block(refs, starts):
    val = refs[0][...]
    for ref, start in zip(refs[1:], starts[1:]):
        val = jnp.where(pl.program_id(1) >= start, ref[...], val)
    return val


def _front_kernel(*refs, n_lat_blocks, starts):
    it = iter(refs)
    x_refs = [next(it) for _ in range(3 * len(starts))]
    x_own, x_prev, x_next = (_stream_block(x_refs[k::3], starts) for k in range(3))
    sc_ref, csc_ref, sh_ref, csh_ref, g_ref = (next(it) for _ in range(5))
    w_refs = [next(it) for _ in range(4)]
    cos_ref, sin_ref = next(it), next(it)
    rwkv_consts = [next(it) for _ in range(N_RWKV_CONSTS)]
    gdn_consts = [next(it) for _ in range(N_GDN_CONSTS)]
    pc_ref, pd_ref = next(it), next(it)
    rwkv_outs = [next(it) for _ in RWKV_OUTS]
    gdn_outs = [next(it) for _ in GDN_OUTS]

    first, lastb = _seq_edges(n_lat_blocks)
    is_ctx = pl.program_id(1) >= n_lat_blocks
    scale = jnp.where(is_ctx, csc_ref[...], sc_ref[...])
    shift = jnp.where(is_ctx, csh_ref[...], sh_ref[...])
    h = _prenorm(x_own, g_ref[...], scale, shift).astype(BF16)
    halo = jnp.concatenate([x_prev, x_next], axis=0)
    h_halo = _prenorm(halo, g_ref[...], scale, shift).astype(BF16)

    def proj(rows, which):
        return jnp.dot(rows, w_refs[which][...], preferred_element_type=F32)

    pc_ref[...] = proj(h, SEG_GLA).astype(BF16)
    att = proj(h, SEG_ATT)
    w = BRANCH_W
    for c0 in range(0, att.shape[1], w):
        part = att[:, c0:c0 + w]
        if c0 < 2 * w:
            part = _rope(part, cos_ref[...], sin_ref[...])
        pd_ref[:, c0:c0 + w] = part.astype(BF16)

    h_ext = jnp.concatenate([h, h_halo], axis=0)
    n = h.shape[0]
    pb = proj(h_ext, SEG_GDN).astype(BF16)
    _gdn_prep(pb[0:n], pb[n:n + HALO], pb[n + HALO:], first, lastb, *gdn_consts, *gdn_outs)
    pa = proj(h_ext, SEG_RWKV).astype(BF16)
    _rwkv_prep(pa[0:n], pa[n:n + HALO], pa[n + HALO:], first, lastb, *rwkv_consts, *rwkv_outs)


def _front(xs, mod, gain, weights, cos, sin, rwkv_consts, gdn_consts, n_lat_rows):
    parts = _stream_parts(xs)
    batch, rows = parts[0].shape[0], sum(t.shape[1] for t in parts)
    x_specs, x_args, starts = _stream_specs(xs, PREP_ROWS, HALO)
    assert len(rwkv_consts) == N_RWKV_CONSTS and len(gdn_consts) == N_GDN_CONSTS and len(weights) == 4
    tab = pl.BlockSpec((PREP_ROWS, BRANCH_W), lambda b, i: (i, 0))
    consts = list(rwkv_consts) + list(gdn_consts)
    outs = [(weights[s].shape[1], BF16) for s in (SEG_GLA, SEG_ATT)]
    outs += [(n * BRANCH_W, dt) for n, dt in RWKV_OUTS + GDN_OUTS]
    out_specs = [pl.BlockSpec((None, PREP_ROWS, lanes), lambda b, i: (b, i, 0)) for lanes, _ in outs]
    out_shape = [jax.ShapeDtypeStruct((batch, rows, lanes), dt) for lanes, dt in outs]
    res = pl.pallas_call(
        functools.partial(_front_kernel, n_lat_blocks=n_lat_rows // PREP_ROWS, starts=tuple(starts)),
        grid=(batch, rows // PREP_ROWS),
        in_specs=x_specs + _mod_specs(1, batch) + _mod_specs(0, batch)
        + [_vec_spec(gain.shape)] + [_vec_spec(w.shape) for w in weights] + [tab, tab]
        + [_vec_spec(c.shape) for c in consts],
        out_specs=out_specs,
        out_shape=out_shape,
        compiler_params=_cparams(("parallel", "parallel")),
        name="front_proj_prep",
    )(*x_args, mod, mod, mod, mod, gain, *weights, cos, sin, *consts)
    pc, pd = res[0], res[1]
    a_sh, a_dk0, a_dk1, a_dw0, a_dw1, a_fin = res[2:8]
    b_sh, b_dk0, b_dk1, b_dw0, b_dw1, b_fin = res[8:14]
    return pc, pd, (a_sh, (a_dk0, a_dk1), (a_dw0, a_dw1), a_fin), (b_sh, (b_dk0, b_dk1), (b_dw0, b_dw1), b_fin)


def _window_bias():
    iq = np.arange(2 * ATT_BLOCK)[:, None] % ATT_BLOCK
    ik = np.arange(ATT_BLOCK)[None, :]
    ok = np.stack([ik >= iq, np.ones_like(ik >= iq), ik <= iq])
    assert WINDOW == ATT_BLOCK
    return np.where(ok, 0.0, NEG_BIG).astype(np.float32)


ATT_STEP = 2


N_ATT_REFS = 5 + 2 * (ATT_STEP + 2)


def _attn_rows(refs, n_lat_rows):
    sink_ref, q_ref, kc_ref, vc_ref = refs[0:4]
    nband = ATT_STEP + 2
    k_all, v_all = refs[4:4 + nband], refs[4 + nband:4 + 2 * nband]
    bias_ref = refs[4 + 2 * nband]
    blk = ATT_BLOCK
    lane = lax.broadcasted_iota(jnp.int32, (blk, 2 * HEAD), 1)
    row2 = lax.broadcasted_iota(jnp.int32, (2 * blk, 1), 0)
    out_rows = []
    for u in range(ATT_STEP):
        out_cols = []
        t = pl.program_id(1) * ATT_STEP + u
        rows = slice(u * blk, (u + 1) * blk)
        q = q_ref[rows, :] * (HEAD ** -0.5)
        k_band, v_band = k_all[u:u + 3], v_all[u:u + 3]
        q_lat = t * blk < n_lat_rows
        in_seq = [q_lat & (t >= 1), q_lat, q_lat & ((t + 1) * blk < n_lat_rows)]
        bias = [jnp.where(in_seq[j], bias_ref[j], NEG_BIG) for j in range(3)]
        for g in range(D_KV_HEADS):
            cols = slice(g * 2 * HEAD, (g + 1) * 2 * HEAD)
            qg = q[:, cols]
            zero = jnp.zeros((), qg.dtype)
            qs = jnp.concatenate([jnp.where(lane < HEAD, qg, zero), jnp.where(lane >= HEAD, qg, zero)], axis=0)
            s_ctx = _mm(qs, kc_ref[:, cols], NT)
            s_band = [_mm(qs, k_band[j][:, cols], NT) + bias[j] for j in range(3)]
            sink = jnp.where(row2 < blk, sink_ref[2 * g], sink_ref[2 * g + 1])
            m = jnp.maximum(jnp.max(s_ctx, axis=-1, keepdims=True), sink)
            for s in s_band:
                m = jnp.maximum(m, jnp.max(s, axis=-1, keepdims=True))
            p_ctx = jnp.exp(s_ctx - m)
            den = jnp.sum(p_ctx, axis=-1, keepdims=True) + jnp.exp(sink - m)
            acc = _mm(p_ctx, vc_ref[:, cols], NN)
            for j in range(3):
                pj = jnp.exp(s_band[j] - m)
                den = den + jnp.sum(pj, axis=-1, keepdims=True)
                acc = acc + _mm(pj, v_band[j][:, cols], NN)
            og = acc / den
            out_cols.append(jnp.where(lane < HEAD, og[0:blk], og[blk:2 * blk]))
        out_rows.append(jnp.concatenate(out_cols, axis=1))
    return jnp.concatenate(out_rows, axis=0)


ATT_ROWS = ATT_STEP * ATT_BLOCK


def _attention_inputs(p, sink, n_lat_rows):
    _, rows, _ = p.shape
    blk = ATT_BLOCK
    nb = rows // blk
    w = BRANCH_W
    n_ctx_rows = rows - n_lat_rows
    assert n_lat_rows % n_ctx_rows == 0 and n_ctx_rows % ATT_ROWS == 0

    def band(col, off):
        def index(b, t):
            return (b, jnp.clip(t * ATT_STEP + off, 0, nb - 1), col)
        return pl.BlockSpec((None, blk, w), index)

    offsets = range(-1, ATT_STEP + 1)
    ctx = lambda col: pl.BlockSpec((None, n_ctx_rows, w), lambda b, t: (b, n_lat_rows // n_ctx_rows, col))
    bias = jnp.asarray(_window_bias())
    specs = [pl.BlockSpec(memory_space=pltpu.SMEM), pl.BlockSpec((None, ATT_ROWS, w), lambda b, t: (b, t, 0)),
             ctx(1), ctx(2)]
    specs += [band(1, o) for o in offsets] + [band(2, o) for o in offsets] + [_vec_spec(bias.shape)]
    args = [sink] + [p] * (3 + 2 * len(offsets)) + [bias]
    assert len(specs) == N_ATT_REFS
    return specs, args


def _merge_kernel(*refs, n_lat_rows, starts):
    x = _stream_block(refs[0:len(starts)], starts)
    refs = refs[len(starts):]
    sc_ref, csc_ref, sh_ref, csh_ref, gm_ref, cgm_ref, g0_ref, g1_ref, ya_ref, yb_ref, yc_ref = refs[0:11]
    att_refs = refs[11:11 + N_ATT_REFS]
    wg_ref, gb_ref, wb_ref, wo_ref = refs[11 + N_ATT_REFS:15 + N_ATT_REFS]
    ffn_refs, o_ref = refs[15 + N_ATT_REFS:-1], refs[-1]
    is_ctx = _ctx_rows(x.shape[0], n_lat_rows)
    scale = jnp.where(is_ctx, csc_ref[...], sc_ref[...])
    shift = jnp.where(is_ctx, csh_ref[...], sh_ref[...])
    gmod = jnp.where(is_ctx, cgm_ref[...], gm_ref[...])
    h = _prenorm(x, g0_ref[...], scale, shift).astype(BF16)
    branches = [ya_ref[...], yb_ref[...], yc_ref[...], _attn_rows(att_refs, n_lat_rows)]
    acc = jnp.zeros(x.shape, F32)
    for i, y in enumerate(branches):
        pre = jnp.dot(h, wg_ref[:, i * D_MODEL:(i + 1) * D_MODEL], preferred_element_type=F32)
        gate = _sigmoid(pre + gb_ref[i:i + 1, :])
        acc = acc + gate * jnp.dot(y.astype(BF16), wb_ref[i], preferred_element_type=F32)
    out = jnp.dot(acc.astype(BF16), wo_ref[...], preferred_element_type=F32)
    ms = jnp.mean(out * out, axis=-1, keepdims=True)
    x = x + gmod * (out * lax.rsqrt(ms + NORM_EPS) * g1_ref[...])
    o_ref[...] = _swiglu_rows(x, is_ctx, *ffn_refs)


def _swiglu_rows(x, is_ctx, sc_ref, csc_ref, sh_ref, csh_ref, gm_ref, cgm_ref, g2_ref, g3_ref, w1_ref, w2_ref):
    scale = jnp.where(is_ctx, csc_ref[...], sc_ref[...])
    shift = jnp.where(is_ctx, csh_ref[...], sh_ref[...])
    h = _prenorm(x, g2_ref[...], scale, shift).astype(BF16)
    hidden = w2_ref.shape[0]
    out = jnp.zeros(x.shape, F32)
    for lo in range(0, hidden, FFN_CHUNK):
        hi = min(lo + FFN_CHUNK, hidden)
        gt = jnp.dot(h, w1_ref[:, lo:hi], preferred_element_type=F32)
        up = jnp.dot(h, w1_ref[:, hidden + lo:hidden + hi], preferred_element_type=F32)
        out = out + jnp.dot((_silu(gt) * up).astype(BF16), w2_ref[lo:hi, :], preferred_element_type=F32)
    ms = jnp.mean(out * out, axis=-1, keepdims=True)
    gmod = jnp.where(is_ctx, cgm_ref[...], gm_ref[...])
    return x + gmod * (out * lax.rsqrt(ms + NORM_EPS) * g3_ref[...])


def _merge(xs, mod, gain0, gain1, ys, p_att, sink, wg, gate_b, wb, wo, ffn_consts, n_lat_rows, out_rows):
    parts = _stream_parts(xs)
    batch, d = parts[0].shape[0], parts[0].shape[-1]
    rows = out_rows
    tm = ATT_ROWS
    assert rows % tm == 0
    x_specs, x_args, starts = _stream_specs(xs, tm)
    tile = lambda width: pl.BlockSpec((None, tm, width), lambda b, i: (b, i, 0))
    consts = [wg, gate_b, wb, wo]
    att_specs, att_args = _attention_inputs(p_att, sink, n_lat_rows)
    return pl.pallas_call(
        functools.partial(_merge_kernel, n_lat_rows=n_lat_rows, starts=tuple(starts)),
        grid=(batch, rows // tm),
        in_specs=x_specs + _mod_specs(1, batch) + _mod_specs(0, batch) + _mod_specs(2, batch)
        + [_vec_spec(gain0.shape), _vec_spec(gain1.shape)]
        + [tile(BRANCH_W)] * 3 + att_specs + [_vec_spec(c.shape) for c in consts]
        + _mod_specs(4, batch) + _mod_specs(3, batch) + _mod_specs(5, batch)
        + [_vec_spec(c.shape) for c in ffn_consts],
        out_specs=tile(d),
        out_shape=jax.ShapeDtypeStruct((batch, rows, d), F32),
        compiler_params=_cparams(("parallel", "parallel")),
        name="attn_merge_swiglu",
    )(*x_args, *([mod] * 6), gain0, gain1, *ys, *att_args, *consts, *([mod] * 6), *ffn_consts)


FFN_CHUNK = 512


def _pad_cols(w, width):
    return jnp.pad(w, ((0, 0), (0, width - w.shape[1])))


def _layer_weights(w_in, mu, w2, a2, g2, conv, a_log, dt_bias, gw2, gb):
    wts = {}
    off_b, off_c, off_d, off_g = A_IN, A_IN + B_IN, A_IN + B_IN + C_IN, A_IN + B_IN + C_IN + D_IN
    w = BRANCH_W
    wts["wa"] = _pad_cols(w_in[:, 0:A_IN], 4 * w).astype(BF16)
    wts["mu"] = _pad_cols(mu[None, :], 4 * w)
    lo = np.cumsum((0,) + A_LORA)
    place = lambda m, r0: jnp.zeros((w, w), F32).at[r0:r0 + m.shape[0], :].set(m)
    wts["w2p"] = jnp.stack([place(w2[0], lo[0]), place(w2[1], lo[1])])
    wts["a2p"] = jnp.stack([place(a2[0], lo[2]), place(a2[1], lo[3])])
    wts["g2p"] = place(g2, lo[4])

    wb = w_in[:, off_b:off_b + B_IN]
    nh = N_HEADS
    n_scalar = 4 * nh
    wts["wb"] = jnp.concatenate([wb[:, 0:3 * w], wb[:, 3 * w + n_scalar:],
                                 _pad_cols(wb[:, 3 * w:3 * w + n_scalar], LANES)], axis=1).astype(BF16)
    expand = np.zeros((4, LANES, w), np.float32)
    for grp in range(4):
        for h in range(nh):
            expand[grp, grp * nh + h, h * HEAD:(h + 1) * HEAD] = 1.0
    wts["eb"] = jnp.asarray(expand[0:2])
    wts["ea"] = jnp.asarray(expand[2:4])
    vec = lambda t: jnp.zeros((1, LANES), F32).at[0, 2 * nh:4 * nh].set(t.reshape(-1))
    wts["alog"] = vec(a_log)
    wts["dt"] = vec(dt_bias)
    wts["conv"] = conv

    wc = w_in[:, off_c:off_c + C_IN]
    pad_heads = lambda m: jnp.pad(m.reshape(m.shape[0], nh, C_DK),
                                  ((0, 0), (0, 0), (0, HEAD - C_DK))).reshape(m.shape[0], w)
    qc, kc, vc = wc[:, 0:C_QK], wc[:, C_QK:2 * C_QK], wc[:, 2 * C_QK:2 * C_QK + w]
    loc = wc[:, 2 * C_QK + w:2 * C_QK + w + 2 * C_GATE_R]
    gc = wc[:, 2 * C_QK + w + 2 * C_GATE_R:]
    wts["wc"] = jnp.concatenate([pad_heads(qc), pad_heads(kc), vc, gc, _pad_cols(loc, LANES)], axis=1).astype(BF16)
    gwp = jnp.zeros((2, LANES, w), F32)
    for d in range(2):
        gwp = gwp.at[d, d * C_GATE_R:(d + 1) * C_GATE_R, :].set(pad_heads(gw2[d]))
    wts["gwp"] = gwp
    wts["gbp"] = pad_heads(gb)
    wts["glane"] = jnp.asarray((np.arange(w) % HEAD < C_DK).astype(np.float32))[None, :]

    wd = w_in[:, off_d:off_d + D_IN]
    qd = wd[:, 0:w]
    dup = lambda m: jnp.concatenate([m[:, 0:HEAD], m[:, 0:HEAD], m[:, HEAD:], m[:, HEAD:]], axis=1)
    wts["wd"] = jnp.concatenate([qd, dup(wd[:, w:w + 2 * HEAD]), dup(wd[:, w + 2 * HEAD:])], axis=1).astype(BF16)
    wts["wg"] = w_in[:, off_g:].astype(BF16)
    return wts


def _rope_tables(n_lat_rows, n_ctx_rows):
    quarter = HEAD // 4
    inv = ROPE_BASE ** (-np.arange(quarter, dtype=np.float32) / quarter)
    pos = np.arange(n_lat_rows)
    rows = (pos // GRID_W).astype(np.float32)
    cols = (pos % GRID_W).astype(np.float32)
    inv = jnp.asarray(inv)
    ang_r = jnp.asarray(rows)[:, None] * inv[None, :]
    ang_c = jnp.asarray(cols)[:, None] * inv[None, :]
    cos = jnp.concatenate([jnp.cos(ang_r)] * 2 + [jnp.cos(ang_c)] * 2, axis=1)
    sin = jnp.concatenate([-jnp.sin(ang_r), jnp.sin(ang_r), -jnp.sin(ang_c), jnp.sin(ang_c)], axis=1)
    cos = jnp.concatenate([cos, jnp.ones((n_ctx_rows, HEAD), F32)], axis=0)
    sin = jnp.concatenate([sin, jnp.zeros((n_ctx_rows, HEAD), F32)], axis=0)
    return jnp.tile(cos, (1, N_HEADS)), jnp.tile(sin, (1, N_HEADS))


RWKV_SLOTS = {"r": ("sh", 0), "v": ("sh", 1), "a": ("sh", 2), "w": ("dw", 0), "k": ("dk", 0), "b": ("dk", 1)}
GDN_SLOTS = RWKV_SLOTS
GLA_SLOTS = {"r": ("sh", 0), "k": ("sh", 1), "v": ("sh", 2)}


def kernel(x, c, ctx, c_ctx, ada_w, ada_b, norm_g, w_in, gate_b, w_branch, w_out, rwkv_mu, rwkv_w0, rwkv_w2, rwkv_a0, rwkv_a2, rwkv_g2, rwkv_kk, rwkv_ka, rwkv_rk, rwkv_ln_g, rwkv_ln_b, gdn_conv, gdn_a_log, gdn_dt_bias, gdn_norm_g, gla_gw2, gla_gb, gla_norm_g, attn_sink, ffn_w1, ffn_w2):
    batch, n_lat, d = x.shape
    n_ctx = ctx.shape[1]
    depth = ada_w.shape[0]
    assert n_ctx % PREP_ROWS == 0 and n_lat % PREP_ROWS == 0 and d == D_MODEL

    mod_rows = 8 * ((batch + 1 + 7) // 8)
    c_rows = jnp.concatenate([c, c_ctx[None, :], jnp.zeros((mod_rows - batch - 1, d), F32)], axis=0)
    mod_all = _modulation(c_rows, ada_w, ada_b)
    cos, sin = _rope_tables(n_lat, n_ctx)

    xs = (x, ctx)
    rows = n_lat + n_ctx
    row = lambda t: t.reshape(1, -1)
    for l in range(depth):
        out_rows = n_lat if l == depth - 1 else rows
        mod = mod_all[l].reshape(mod_rows, 1, 6 * d)
        ng = norm_g[l]
        wts = _layer_weights(w_in[l], rwkv_mu[l], rwkv_w2[l], rwkv_a2[l], rwkv_g2[l], gdn_conv[l],
                             gdn_a_log[l], gdn_dt_bias[l], gla_gw2[l], gla_gb[l])
        ones = jnp.asarray(_head_block_ones())
        around = jnp.asarray(0.5 * (_shift_matrix(-1) + _shift_matrix(1)), BF16)
        rwkv_consts = [around, wts["mu"], wts["w2p"], wts["a2p"], wts["g2p"], rwkv_w0[l], rwkv_a0[l],
                       row(rwkv_kk[l]), row(rwkv_ka[l]), row(rwkv_rk[l]), ones]
        gdn_consts = [wts["conv"], wts["alog"], wts["dt"], wts["eb"], wts["ea"], ones]
        p_gla, p_att, rwkv_ops, gdn_ops = _front(xs, mod, row(ng[0]), (wts["wc"], wts["wb"], wts["wd"], wts["wa"]),
                                                 cos, sin, rwkv_consts, gdn_consts, n_lat)

        sh, dks, dws, fin = rwkv_ops
        ya = _bidir_scan(sh, dks, dws, n_lat, fin, row(rwkv_ln_g[l]), row(rwkv_ln_b[l]), finish="groupnorm",
                         scalar_decay=False, lowrank=True, slots=RWKV_SLOTS)

        sh, dks, dws, fin = gdn_ops
        gnorm = row(jnp.tile(gdn_norm_g[l], N_HEADS))
        yb = _bidir_scan(sh, dks, dws, n_lat, fin, gnorm, gnorm, finish="rms",
                         scalar_decay=True, lowrank=True, slots=GDN_SLOTS)

        yc = _gla_scan(p_gla, n_lat, wts["gwp"], wts["gbp"], wts["glane"], row(jnp.tile(gla_norm_g[l], N_HEADS)))

        ffn_consts = [row(ng[2]), row(ng[3]), ffn_w1[l].astype(BF16), ffn_w2[l].astype(BF16)]
        xs = _merge(xs, mod, row(ng[0]), row(ng[1]), (ya, yb, yc), p_att, attn_sink[l], wts["wg"], gate_b[l],
                    w_branch[l].astype(BF16), w_out[l].astype(BF16), ffn_consts, n_lat, out_rows)
    return xs
```

```python
import functools
import math

import numpy as np
import jax
import jax.numpy as jnp
from jax import lax
from jax.experimental import pallas as pl
from jax.experimental.pallas import tpu as pltpu

F32 = jnp.float32
BF16 = jnp.bfloat16

D_MODEL = 1024
N_BRANCH = 4
BRANCH_W = 256
HEAD = 64
N_HEADS = BRANCH_W // HEAD
NORM_EPS = 1e-6
A_GN_EPS = 64e-5
A_LORA = (32, 32, 32, 32, 64)
A_IN = 3 * BRANCH_W + sum(A_LORA)
B_IN = 4 * BRANCH_W + 4 * N_HEADS
C_DK = 32
C_QK = N_HEADS * C_DK
C_GATE_R = 16
C_GATE_NORM = 16.0
C_IN = 2 * C_QK + 2 * BRANCH_W + 2 * C_GATE_R
D_KV_HEADS = 2
D_IN = BRANCH_W + 2 * D_KV_HEADS * HEAD
B_CONV = 7
WINDOW = 128
ROPE_BASE = 10000.0
GRID_W = 64
FFN_HIDDEN = 2816

LANES = 128
CHUNK = 64
PREP_ROWS = 256
HALO = 16
ATT_BLOCK = 128
VMEM_LIMIT = 48 * 1024 * 1024
NEG_BIG = -1e30

NN = (((1,), (0,)), ((), ()))
NT = (((1,), (1,)), ((), ()))
TN = (((0,), (0,)), ((), ()))


def _mm(a, b, dims=NN, mode="bf16"):
    if mode == "f32":
        return lax.dot_general(a, b, dims, precision=lax.Precision.HIGHEST, preferred_element_type=F32)
    return lax.dot_general(a.astype(BF16), b.astype(BF16), dims, preferred_element_type=F32)


def _sigmoid(x):
    return 1.0 / (1.0 + jnp.exp(-x))


def _silu(x):
    return x * _sigmoid(x)


def _softplus(x):
    return jnp.maximum(x, 0.0) + jnp.log1p(jnp.exp(-jnp.abs(x)))


def _cparams(sem):
    return pltpu.CompilerParams(dimension_semantics=sem, vmem_limit_bytes=VMEM_LIMIT)


def _mod_kernel(c_ref, w_ref, b_ref, o_ref):
    c = c_ref[...]
    o_ref[...] = _mm(_silu(c), w_ref[...], mode="f32") + b_ref[...]


def _modulation(c_rows, ada_w, ada_b):
    depth, d, n = ada_w.shape
    rows = c_rows.shape[0]
    tn = 1024
    return pl.pallas_call(
        _mod_kernel,
        grid=(depth, n // tn),
        in_specs=[
            pl.BlockSpec((rows, d), lambda l, j: (0, 0)),
            pl.BlockSpec((None, d, tn), lambda l, j: (l, 0, j)),
            pl.BlockSpec((None, 1, tn), lambda l, j: (l, 0, j)),
        ],
        out_specs=pl.BlockSpec((None, rows, tn), lambda l, j: (l, 0, j)),
        out_shape=jax.ShapeDtypeStruct((depth, rows, n), F32),
        compiler_params=_cparams(("parallel", "parallel")),
        name="adaln_mod",
    )(c_rows, ada_w, ada_b.reshape(depth, 1, n))


def _row_tile(rows, target):
    return max(t for t in range(8, target + 1, 8) if rows % t == 0)


def _mod_specs(which, batch):
    lat = pl.BlockSpec((None, 1, D_MODEL), lambda b, i, *_: (b, 0, which))
    ctx = pl.BlockSpec((None, 1, D_MODEL), lambda b, i, *_: (batch, 0, which))
    return [lat, ctx]


def _ctx_rows(tm, n_lat_rows):
    row = pl.program_id(1) * tm + lax.broadcasted_iota(jnp.int32, (tm, 1), 0)
    return row >= n_lat_rows


def _prenorm(x, gain, scale, shift):
    ms = jnp.mean(x * x, axis=-1, keepdims=True)
    return (x * lax.rsqrt(ms + NORM_EPS)) * (gain * (1.0 + scale)) + shift


def _rope(x, cos, sin):
    width = x.shape[-1]
    lane = lax.broadcasted_iota(jnp.int32, x.shape, 1)
    half, quarter = HEAD // 2, HEAD // 4
    swapped = jnp.where(lane % half < quarter, pltpu.roll(x, width - quarter, axis=1), pltpu.roll(x, quarter, axis=1))
    return x * cos + swapped * sin


def _head_block_ones():
    idx = np.arange(BRANCH_W)
    return (idx[:, None] // HEAD == idx[None, :] // HEAD).astype(np.float32)


N_LEVELS = 6


def _level_mask(ri, ci, s, reverse):
    b = 1 << s
    blk = (ri // (2 * b)) == (ci // (2 * b))
    hi_r, hi_c = (ri // b) % 2 == 1, (ci // b) % 2 == 1
    return blk & ((hi_c & ~hi_r) if reverse else (hi_r & ~hi_c))


SCAN_HEADS = 2
GROUP_W = SCAN_HEADS * HEAD
N_GROUPS = N_HEADS // SCAN_HEADS


def _scan_masks(reverse):
    n = SCAN_HEADS * CHUNK
    ri = np.arange(CHUNK)[:, None]
    ci = np.arange(n)[None, :] % CHUNK
    strict, incl = (ci > ri, ci >= ri) if reverse else (ci < ri, ci <= ri)
    compact = [strict, incl, ci == ri] + [_level_mask(ri, ci, s, reverse) for s in range(N_LEVELS)]
    r = np.arange(n)[:, None]
    c = np.arange(n)[None, :]
    block = [(r // CHUNK) == (c // CHUNK), r == c]
    return np.stack(compact).astype(np.float32), np.stack(block).astype(np.float32)


EXACT_TERMS = 3
NORM_TERMS = 2


def _tri(reverse):
    i = np.arange(CHUNK)
    m = (i[None, :] >= i[:, None]) if reverse else (i[None, :] <= i[:, None])
    return np.tile(m.astype(np.float32), (1, EXACT_TERMS))


def _split_terms(x, terms):
    out = []
    for _ in range(terms - 1):
        part = x.astype(BF16)
        out.append(part)
        x = x - part.astype(F32)
    out.append(x.astype(BF16))
    return out


def _mm_exact(a, b, dims=NN, split="a", terms=EXACT_TERMS):
    assert dims == NN
    if split == "a":
        lhs = jnp.concatenate(_split_terms(a, terms), axis=1)
        rhs = jnp.concatenate([b.astype(BF16)] * terms, axis=0)
    else:
        lhs = a.astype(BF16)
        rhs = jnp.concatenate(_split_terms(b, terms), axis=0)
    return lax.dot_general(lhs, rhs, dims, preferred_element_type=F32)


C_STRICT, C_INCL, C_EYE, C_LVL0 = 0, 1, 2, 3
B_SAME, B_EYE = 0, 1
SCAN_BATCH = 8
SCAN_ROWS = 256
assert CHUNK == HEAD and (1 << N_LEVELS) == CHUNK


def _scan_chunk(vals, st, tri, ones, cm_ref, bm_ref, same_bf, *, reverse, scalar_decay, lowrank, mm_mode):
    same = bm_ref[B_SAME]

    def get(name):
        return vals[name]

    def expand(x):
        return jnp.concatenate([x.astype(BF16)] * SCAN_HEADS, axis=0) * same_bf

    def keep(x, k):
        return jnp.where(cm_ref[k] > 0.5, x, 0.0)

    logw = get("w")
    r, k, v = get("r"), get("k"), get("v")
    cum = _mm_exact(tri, logw, split="b")
    cum_x = cum - logw
    last = 0 if reverse else CHUNK - 1
    total = cum[last:last + 1, :]

    if scalar_decay:
        diag = jnp.concatenate([cum] * SCAN_HEADS, axis=0) * bm_ref[B_EYE]
        cum_row = _mm_exact(ones, diag, split="b")
        d_ii = jnp.exp(jnp.where(cm_ref[C_INCL] > 0.5, cum - cum_row, NEG_BIG))
        r_q, k_q = r, k
        to_end = jnp.exp(total - cum)
        r_abs = r * jnp.exp(cum)
    else:
        ref_row = cum[CHUNK // 2:CHUNK // 2 + 1, :]
        e_ref = jnp.exp(ref_row)
        p_inv = jnp.exp(ref_row - cum)
        r_q, k_q = r * jnp.exp(cum - ref_row), k * p_inv
        to_end = p_inv * jnp.exp(total - ref_row)
        r_abs = r_q * e_ref

    k_e, v_e = expand(k_q), expand(v)
    if not lowrank:
        s_k = _mm(r_q, k_e, NT, mm_mode)
        a_rk = s_k * d_ii if scalar_decay else keep(s_k, C_INCL)
        y = _mm(r_abs, st, NT, mm_mode) + _mm(a_rk, v_e, NN, mm_mode)
        upd = _mm(v, k * to_end, TN, mm_mode)
        return y, st * jnp.exp(total) + upd * same

    a, b = get("a"), get("b")
    if scalar_decay:
        d_xi = jnp.exp(jnp.where(cm_ref[C_STRICT] > 0.5, cum_x - cum_row, NEG_BIG))
        a_q, b_q = a, b
    else:
        a_q, b_q = a * jnp.exp(cum_x - ref_row), b * p_inv
    lhs = jnp.concatenate([a_q, r_q], axis=0)
    scores = _mm(lhs, jnp.concatenate([expand(b_q), k_e], axis=0), NT, mm_mode)
    s_b, s_k = scores[:, 0:GROUP_W], scores[:, GROUP_W:]
    if scalar_decay:
        a_ab, a_rb = s_b[0:CHUNK] * d_xi, s_b[CHUNK:] * d_ii
        a_ak, a_rk = s_k[0:CHUNK] * d_xi, s_k[CHUNK:] * d_ii
    else:
        a_ab, a_rb = keep(s_b[0:CHUNK], C_STRICT), keep(s_b[CHUNK:], C_INCL)
        a_ak, a_rk = keep(s_k[0:CHUNK], C_STRICT), keep(s_k[CHUNK:], C_INCL)

    inv = cm_ref[C_EYE] + a_ab * cm_ref[C_LVL0]
    for s in range(1, N_LEVELS):
        c_s = expand(a_ab * cm_ref[C_LVL0 + s])
        inv = inv + _mm(_mm(inv, c_s, NN, mm_mode), expand(inv), NN, mm_mode)

    a_abs = a * jnp.exp(cum_x) if scalar_decay else a_q * e_ref
    from_state = _mm(jnp.concatenate([a_abs, r_abs], axis=0), st, NT, mm_mode)
    from_v = _mm(jnp.concatenate([a_ak, a_rk], axis=0), v_e, NN, mm_mode)
    both = from_state + from_v
    z = _mm(inv, expand(both[0:CHUNK]), NN, mm_mode)
    y = both[CHUNK:] + _mm(a_rb, expand(z), NN, mm_mode)
    upd = _mm(jnp.concatenate([v, z], axis=0), jnp.concatenate([k * to_end, b * to_end], axis=0), TN, mm_mode)
    return y, st * jnp.exp(total) + upd * same


def _scan_kernel(*refs, reverse, scalar_decay, lowrank, slots, sources, finish, mm_mode, lora, r_scale):
    it = iter(refs)
    src_refs = {name: next(it) for name in sources}
    sh_ref = src_refs[sources[0]]
    cm_ref, bm_ref, tri_ref = next(it), next(it), next(it)
    if lora:
        lo_ref, gw_ref, gb_ref, lane_ref = next(it), next(it), next(it), next(it)
    if finish:
        ob_ref, fin_ref, p1_ref, p2_ref, avg_ref = next(it), next(it), next(it), next(it), next(it)
    o_ref, st_ref = next(it), next(it)
    if finish:
        y_ref = next(it)

    @pl.when(pl.program_id(1) == 0)
    def _():
        st_ref[...] = jnp.zeros_like(st_ref)

    nb, block_rows = sh_ref.shape[0], sh_ref.shape[1]
    n_chunks = block_rows // CHUNK
    n = nb * N_GROUPS
    tri = jnp.broadcast_to(tri_ref[...], (n,) + tri_ref.shape)
    ones = jnp.ones((n, CHUNK, EXACT_TERMS * SCAN_HEADS * CHUNK), F32)
    chunk = functools.partial(_scan_chunk, cm_ref=cm_ref, bm_ref=bm_ref, same_bf=bm_ref[B_SAME].astype(BF16),
                              reverse=reverse, scalar_decay=scalar_decay, lowrank=lowrank, mm_mode=mm_mode)

    def step(c, carry):
        rows = pl.ds(pl.multiple_of((n_chunks - 1 - c if reverse else c) * CHUNK, CHUNK), CHUNK)

        def groups(name):
            src, idx = slots[name]
            ref = src_refs[src]
            parts = [ref[:, rows, idx * BRANCH_W + g * GROUP_W:idx * BRANCH_W + (g + 1) * GROUP_W].astype(F32)
                     for g in range(N_GROUPS)]
            return jnp.stack(parts, axis=1).reshape(n, CHUNK, GROUP_W)

        vals = {name: groups(name) for name in slots}
        if lora:
            lo = lo_ref[:, rows, :].reshape(nb * CHUNK, lo_ref.shape[-1])
            z = _mm(lo, gw_ref[...], mode=mm_mode) + gb_ref[...]
            logw = ((-_softplus(-z) / C_GATE_NORM) * lane_ref[...]).reshape(nb, CHUNK, BRANCH_W)
            parts = [logw[:, :, g * GROUP_W:(g + 1) * GROUP_W] for g in range(N_GROUPS)]
            vals["w"] = jnp.stack(parts, axis=1).reshape(n, CHUNK, GROUP_W)
        if r_scale != 1.0:
            vals["r"] = vals["r"] * r_scale
        y, st_new = jax.vmap(chunk)(vals, st_ref[...], tri, ones)
        st_ref[...] = st_new
        y = y.reshape(nb, N_GROUPS, CHUNK, GROUP_W)
        y = jnp.concatenate([y[:, g] for g in range(N_GROUPS)], axis=-1)
        if not finish:
            o_ref[:, rows, :] = y
        else:
            y_ref[:, rows, :] = y
        return carry

    lax.fori_loop(0, n_chunks, step, 0)
    if not finish:
        return

    y = (y_ref[...] + ob_ref[...]).reshape(nb * block_rows, BRANCH_W)
    avg = avg_ref[...]
    if finish == "groupnorm":
        gate = fin_ref[:, :, 0:BRANCH_W].reshape(y.shape)
        bonus = fin_ref[:, :, BRANCH_W:2 * BRANCH_W].reshape(y.shape)
        cen = y - _mm_exact(y, avg)
        var = _mm_exact(cen * cen, avg)
        yn = cen * lax.rsqrt(var + A_GN_EPS) * p1_ref[...] + p2_ref[...]
        out = (yn + bonus) * gate
    else:
        ms = _mm_exact(y * y, avg)
        gate = fin_ref[...].astype(F32).reshape(y.shape)
        out = y * lax.rsqrt(ms + NORM_EPS) * p1_ref[...] * _silu(gate)
    o_ref[...] = out.reshape(nb, block_rows, BRANCH_W).astype(o_ref.dtype)


def _view(x):
    return x if isinstance(x, tuple) else (x, x.shape[-1], 0)


def _scan(srcs, n_lat_rows, *, reverse, scalar_decay, lowrank, slots, finish=None, fin_args=None,
          mm_mode="bf16", lora=None, r_scale=1.0):
    sources = tuple(srcs)
    views = [_view(srcs[s]) for s in sources]
    batch, rows, _ = views[0][0].shape
    nc, nlat = rows // SCAN_ROWS, n_lat_rows // SCAN_ROWS
    nctx = nc - nlat
    nb = math.gcd(batch, SCAN_BATCH)

    if reverse:
        def chunk(n):
            return nc - 1 - n
    else:
        def chunk(n):
            return jnp.where(n < nctx, nlat + n, n - nctx)

    def row_spec(width, col=0):
        return pl.BlockSpec((nb, SCAN_ROWS, width), lambda b, n: (b, chunk(n), col))

    def const_spec(shape):
        zeros = (0,) * len(shape)
        return pl.BlockSpec(shape, lambda b, n: zeros)

    cmask, bmask = (jnp.asarray(m) for m in _scan_masks(reverse))
    tri = jnp.asarray(_tri(reverse))
    in_specs = [row_spec(w, c) for _, w, c in views]
    in_specs += [const_spec(cmask.shape), const_spec(bmask.shape), const_spec(tri.shape)]
    args = [a for a, _, _ in views] + [cmask, bmask, tri]
    if lora:
        (lo, lo_w, lo_c), gw, gb, lane = _view(lora[0]), lora[1], lora[2], lora[3]
        in_specs += [row_spec(lo_w, lo_c), const_spec(gw.shape), const_spec(gb.shape), const_spec(lane.shape)]
        args += [lo, gw, gb, lane]
    if finish:
        ob, fin, p1, p2 = fin_args
        fin, fin_w, fin_c = _view(fin)
        avg = jnp.asarray(_head_block_ones() / HEAD)
        in_specs += [row_spec(BRANCH_W), row_spec(fin_w, fin_c), const_spec(p1.shape), const_spec(p2.shape),
                     const_spec(avg.shape)]
        args += [ob, fin, p1, p2, avg]
    kern = functools.partial(_scan_kernel, reverse=reverse, scalar_decay=scalar_decay, lowrank=lowrank,
                             slots=slots, sources=sources, finish=finish, mm_mode=mm_mode, lora=bool(lora),
                             r_scale=r_scale)
    return pl.pallas_call(
        kern,
        grid=(batch // nb, nc),
        in_specs=in_specs,
        out_specs=row_spec(BRANCH_W),
        out_shape=jax.ShapeDtypeStruct((batch, rows, BRANCH_W), BF16 if finish else F32),
        scratch_shapes=[pltpu.VMEM((nb * N_GROUPS, GROUP_W, GROUP_W), F32)]
        + ([pltpu.VMEM((nb, SCAN_ROWS, BRANCH_W), F32)] if finish else []),
        compiler_params=_cparams(("parallel", "arbitrary")),
        name="dplr_scan_" + ("bwd" if reverse else "fwd"),
    )(*args)


def _gla_scan(p, n_lat_rows, gwp, gbp, lane, norm_g):
    w = BRANCH_W
    srcs = {"sh": (p, 3 * w, 0)}
    lo = (p, gwp.shape[1], 4 * w // gwp.shape[1])
    kw = dict(scalar_decay=False, lowrank=False, slots=GLA_SLOTS, r_scale=C_DK ** -0.5)
    ob = _scan(srcs, n_lat_rows, reverse=True, lora=(lo, gwp[1], gbp[1:2], lane), **kw)
    return _scan(srcs, n_lat_rows, reverse=False, lora=(lo, gwp[0], gbp[0:1], lane), finish="rms",
                 fin_args=(ob, (p, w, 3), norm_g, norm_g), **kw)


def _bidir_scan(sh, dks, dws, n_lat_rows, fin, p1, p2, *, finish, **kw):
    def srcs(d):
        out = {"sh": sh, "dw": dws[d]}
        if dks is not None:
            out["dk"] = dks[d]
        return out

    ob = _scan(srcs(1), n_lat_rows, reverse=True, **kw)
    return _scan(srcs(0), n_lat_rows, reverse=False, finish=finish, fin_args=(ob, fin, p1, p2), **kw)


def _seq_edges(n_lat_blocks):
    i = pl.program_id(1)
    first = (i == 0) | (i == n_lat_blocks)
    lastb = (i == n_lat_blocks - 1) | (i == pl.num_programs(1) - 1)
    return first, lastb


def _vec_spec(shape):
    zeros = (0,) * len(shape)
    return pl.BlockSpec(shape, lambda b, i: zeros)


def _shift_matrix(offset):
    m = np.zeros((PREP_ROWS, PREP_ROWS + 2 * HALO), np.float32)
    t = np.arange(PREP_ROWS)
    src = t + offset
    col = np.where(src < 0, PREP_ROWS + HALO + src, np.where(src >= PREP_ROWS, HALO + src, src))
    m[t, col] = 1.0
    return m


def _rwkv_prep(x, prev, nxt, first, lastb, shift_ref, mu_ref, w2_ref, a2_ref, g2_ref, w0_ref, a0_ref, kk_ref,
               ka_ref, rk_ref, ones_ref, sh_ref, dk0_ref, dk1_ref, dw0_ref, dw1_ref, fin_ref, mode="bf16"):
    zero = jnp.zeros((), x.dtype)
    xe = jnp.concatenate([x, jnp.where(first, zero, prev), jnp.where(lastb, zero, nxt)], axis=0)
    around = jnp.dot(shift_ref[...], xe, preferred_element_type=F32)
    x = x.astype(F32)
    xm = x + (around - x) * mu_ref[...]
    w = BRANCH_W
    r, k, v, lo = xm[:, 0:w], xm[:, w:2 * w], xm[:, 2 * w:3 * w], xm[:, 3 * w:4 * w]
    ones = ones_ref[...]
    lo_a, lo_g = lo[:, 0:LANES], lo[:, LANES:2 * LANES]
    th = jnp.tanh(lo_a)
    gate = _mm(_sigmoid(lo_g), g2_ref[LANES:2 * LANES, :], mode=mode)
    kx = k * kk_ref[...]
    kk = kx * lax.rsqrt(_mm_exact(kx * kx, ones, terms=NORM_TERMS) + 1e-6)
    sh_ref[:, 0:w] = r.astype(BF16)
    sh_ref[:, w:2 * w] = v.astype(BF16)
    sh_ref[:, 2 * w:3 * w] = (-kk).astype(BF16)
    bonus = jnp.zeros_like(v)
    for d, (dk_ref, dw_ref) in enumerate(((dk0_ref, dw0_ref), (dk1_ref, dw1_ref))):
        w_raw = w0_ref[d:d + 1, :] + _mm(th, w2_ref[d, 0:LANES, :], mode=mode)
        dw_ref[...] = -math.exp(-0.5) * _sigmoid(w_raw)
        a = _sigmoid(a0_ref[d:d + 1, :] + _mm(lo_a, a2_ref[d, 0:LANES, :], mode=mode))
        kd = k * (1.0 + (a - 1.0) * ka_ref[...])
        dk_ref[:, 0:w] = kd.astype(BF16)
        dk_ref[:, w:2 * w] = (kk * a).astype(BF16)
        bonus = bonus + _mm_exact(r * kd * rk_ref[...], ones, terms=NORM_TERMS) * v
    fin_ref[:, 0:w] = gate
    fin_ref[:, w:2 * w] = bonus


def _gdn_prep(p, prev, nxt, first, lastb, conv_ref, alog_ref, dt_ref, eb_ref, ea_ref, ones_ref,
              sh_ref, dk0_ref, dk1_ref, dw0_ref, dw1_ref, fin_ref):
    w = BRANCH_W
    x = p[:, 0:3 * w].astype(F32)
    top = jnp.where(first, 0.0, prev[:, 0:3 * w].astype(F32))
    bot = jnp.where(lastb, 0.0, nxt[:, 0:3 * w].astype(F32))
    xe = jnp.concatenate([top, x, bot], axis=0)
    ext = PREP_ROWS + 2 * HALO
    acc = jnp.zeros_like(x)
    for s in range(B_CONV):
        shift = (B_CONV // 2 - s) % ext
        rolled = xe if shift == 0 else pltpu.roll(xe, shift, axis=0)
        acc = acc + rolled[HALO:HALO + PREP_ROWS] * conv_ref[s:s + 1, :]
    qkv = _silu(acc)
    ones = ones_ref[...]

    def l2n(t):
        return t * lax.rsqrt(_mm_exact(t * t, ones, terms=NORM_TERMS) + 1e-6)

    q = l2n(qkv[:, 0:w]) * (HEAD ** -0.5)
    k = l2n(qkv[:, w:2 * w])
    v = qkv[:, 2 * w:3 * w]
    sh_ref[:, 0:w] = q.astype(BF16)
    sh_ref[:, w:2 * w] = v.astype(BF16)
    sh_ref[:, 2 * w:3 * w] = k.astype(BF16)
    sr = p[:, 4 * w:4 * w + LANES].astype(F32)
    beta_all = _sigmoid(sr)
    g_all = -jnp.exp(alog_ref[...]) * _softplus(sr + dt_ref[...])
    for d, (dk_ref, dw_ref) in enumerate(((dk0_ref, dw0_ref), (dk1_ref, dw1_ref))):
        beta = _mm_exact(beta_all, eb_ref[d])
        g = _mm_exact(g_all, ea_ref[d])
        kb = k * beta
        dw_ref[...] = g
        dk_ref[:, 0:w] = kb.astype(BF16)
        dk_ref[:, w:2 * w] = (-jnp.exp(g) * kb).astype(BF16)
    fin_ref[...] = p[:, 3 * w:4 * w].astype(F32)


SEG_GLA, SEG_GDN, SEG_ATT, SEG_RWKV = range(4)
N_RWKV_CONSTS, N_GDN_CONSTS = 11, 6
RWKV_OUTS = [(3, BF16), (2, BF16), (2, BF16), (1, F32), (1, F32), (2, F32)]
GDN_OUTS = [(3, BF16), (2, BF16), (2, BF16), (1, F32), (1, F32), (1, F32)]


def _stream_parts(xs):
    return list(xs) if isinstance(xs, (tuple, list)) else [xs]


def _stream_specs(xs, block_rows, halo_rows=0):
    specs, args, starts = [], [], []
    base = 0
    for part in _stream_parts(xs):
        d = part.shape[-1]
        nblk = part.shape[1] // block_rows
        own = lambda b, i, base=base, nblk=nblk: (b, jnp.clip(i - base, 0, nblk - 1), 0)
        specs.append(pl.BlockSpec((None, block_rows, d), own))
        args.append(part)
        if halo_rows:
            per, last = block_rows // halo_rows, part.shape[1] // halo_rows - 1
            prev = lambda b, i, base=base, per=per, last=last: (b, jnp.clip((i - base) * per - 1, 0, last), 0)
            nxt = lambda b, i, base=base, per=per, last=last: (b, jnp.clip((i - base + 1) * per, 0, last), 0)
            specs += [pl.BlockSpec((None, halo_rows, d), prev), pl.BlockSpec((None, halo_rows, d), nxt)]
            args += [part, part]
        starts.append(base)
        base += nblk
    return specs, args, starts


def _stream_block(refs, starts):
    val = refs[0][...]
    for ref, start in zip(refs[1:], starts[1:]):
        val = jnp.where(pl.program_id(1) >= start, ref[...], val)
    return val


def _front_kernel(*refs, n_lat_blocks, starts):
    it = iter(refs)
    x_refs = [next(it) for _ in range(3 * len(starts))]
    x_own, x_prev, x_next = (_stream_block(x_refs[k::3], starts) for k in range(3))
    sc_ref, csc_ref, sh_ref, csh_ref, g_ref = (next(it) for _ in range(5))
    w_refs = [next(it) for _ in range(4)]
    cos_ref, sin_ref = next(it), next(it)
    rwkv_consts = [next(it) for _ in range(N_RWKV_CONSTS)]
    gdn_consts = [next(it) for _ in range(N_GDN_CONSTS)]
    pc_ref, pd_ref = next(it), next(it)
    rwkv_outs = [next(it) for _ in RWKV_OUTS]
    gdn_outs = [next(it) for _ in GDN_OUTS]

    first, lastb = _seq_edges(n_lat_blocks)
    is_ctx = pl.program_id(1) >= n_lat_blocks
    scale = jnp.where(is_ctx, csc_ref[...], sc_ref[...])
    shift = jnp.where(is_ctx, csh_ref[...], sh_ref[...])
    h = _prenorm(x_own, g_ref[...], scale, shift).astype(BF16)
    halo = jnp.concatenate([x_prev, x_next], axis=0)
    h_halo = _prenorm(halo, g_ref[...], scale, shift).astype(BF16)

    def proj(rows, which):
        return jnp.dot(rows, w_refs[which][...], preferred_element_type=F32)

    pc_ref[...] = proj(h, SEG_GLA).astype(BF16)
    att = proj(h, SEG_ATT)
    w = BRANCH_W
    for c0 in range(0, att.shape[1], w):
        part = att[:, c0:c0 + w]
        if c0 < 2 * w:
            part = _rope(part, cos_ref[...], sin_ref[...])
        pd_ref[:, c0:c0 + w] = part.astype(BF16)

    h_ext = jnp.concatenate([h, h_halo], axis=0)
    n = h.shape[0]
    pb = proj(h_ext, SEG_GDN).astype(BF16)
    _gdn_prep(pb[0:n], pb[n:n + HALO], pb[n + HALO:], first, lastb, *gdn_consts, *gdn_outs)
    pa = proj(h_ext, SEG_RWKV).astype(BF16)
    _rwkv_prep(pa[0:n], pa[n:n + HALO], pa[n + HALO:], first, lastb, *rwkv_consts, *rwkv_outs)


def _front(xs, mod, gain, weights, cos, sin, rwkv_consts, gdn_consts, n_lat_rows):
    parts = _stream_parts(xs)
    batch, rows = parts[0].shape[0], sum(t.shape[1] for t in parts)
    x_specs, x_args, starts = _stream_specs(xs, PREP_ROWS, HALO)
    assert len(rwkv_consts) == N_RWKV_CONSTS and len(gdn_consts) == N_GDN_CONSTS and len(weights) == 4
    tab = pl.BlockSpec((PREP_ROWS, BRANCH_W), lambda b, i: (i, 0))
    consts = list(rwkv_consts) + list(gdn_consts)
    outs = [(weights[s].shape[1], BF16) for s in (SEG_GLA, SEG_ATT)]
    outs += [(n * BRANCH_W, dt) for n, dt in RWKV_OUTS + GDN_OUTS]
    out_specs = [pl.BlockSpec((None, PREP_ROWS, lanes), lambda b, i: (b, i, 0)) for lanes, _ in outs]
    out_shape = [jax.ShapeDtypeStruct((batch, rows, lanes), dt) for lanes, dt in outs]
    res = pl.pallas_call(
        functools.partial(_front_kernel, n_lat_blocks=n_lat_rows // PREP_ROWS, starts=tuple(starts)),
        grid=(batch, rows // PREP_ROWS),
        in_specs=x_specs + _mod_specs(1, batch) + _mod_specs(0, batch)
        + [_vec_spec(gain.shape)] + [_vec_spec(w.shape) for w in weights] + [tab, tab]
        + [_vec_spec(c.shape) for c in consts],
        out_specs=out_specs,
        out_shape=out_shape,
        compiler_params=_cparams(("parallel", "parallel")),
        name="front_proj_prep",
    )(*x_args, mod, mod, mod, mod, gain, *weights, cos, sin, *consts)
    pc, pd = res[0], res[1]
    a_sh, a_dk0, a_dk1, a_dw0, a_dw1, a_fin = res[2:8]
    b_sh, b_dk0, b_dk1, b_dw0, b_dw1, b_fin = res[8:14]
    return pc, pd, (a_sh, (a_dk0, a_dk1), (a_dw0, a_dw1), a_fin), (b_sh, (b_dk0, b_dk1), (b_dw0, b_dw1), b_fin)


def _window_bias():
    iq = np.arange(2 * ATT_BLOCK)[:, None] % ATT_BLOCK
    ik = np.arange(ATT_BLOCK)[None, :]
    ok = np.stack([ik >= iq, np.ones_like(ik >= iq), ik <= iq])
    assert WINDOW == ATT_BLOCK
    return np.where(ok, 0.0, NEG_BIG).astype(np.float32)


ATT_STEP = 2


N_ATT_REFS = 5 + 2 * (ATT_STEP + 2)


def _attn_rows(refs, n_lat_rows):
    sink_ref, q_ref, kc_ref, vc_ref = refs[0:4]
    nband = ATT_STEP + 2
    k_all, v_all = refs[4:4 + nband], refs[4 + nband:4 + 2 * nband]
    bias_ref = refs[4 + 2 * nband]
    blk = ATT_BLOCK
    lane = lax.broadcasted_iota(jnp.int32, (blk, 2 * HEAD), 1)
    row2 = lax.broadcasted_iota(jnp.int32, (2 * blk, 1), 0)
    out_rows = []
    for u in range(ATT_STEP):
        out_cols = []
        t = pl.program_id(1) * ATT_STEP + u
        rows = slice(u * blk, (u + 1) * blk)
        q = q_ref[rows, :] * (HEAD ** -0.5)
        k_band, v_band = k_all[u:u + 3], v_all[u:u + 3]
        q_lat = t * blk < n_lat_rows
        in_seq = [q_lat & (t >= 1), q_lat, q_lat & ((t + 1) * blk < n_lat_rows)]
        bias = [jnp.where(in_seq[j], bias_ref[j], NEG_BIG) for j in range(3)]
        for g in range(D_KV_HEADS):
            cols = slice(g * 2 * HEAD, (g + 1) * 2 * HEAD)
            qg = q[:, cols]
            zero = jnp.zeros((), qg.dtype)
            qs = jnp.concatenate([jnp.where(lane < HEAD, qg, zero), jnp.where(lane >= HEAD, qg, zero)], axis=0)
            s_ctx = _mm(qs, kc_ref[:, cols], NT)
            s_band = [_mm(qs, k_band[j][:, cols], NT) + bias[j] for j in range(3)]
            sink = jnp.where(row2 < blk, sink_ref[2 * g], sink_ref[2 * g + 1])
            m = jnp.maximum(jnp.max(s_ctx, axis=-1, keepdims=True), sink)
            for s in s_band:
                m = jnp.maximum(m, jnp.max(s, axis=-1, keepdims=True))
            p_ctx = jnp.exp(s_ctx - m)
            den = jnp.sum(p_ctx, axis=-1, keepdims=True) + jnp.exp(sink - m)
            acc = _mm(p_ctx, vc_ref[:, cols], NN)
            for j in range(3):
                pj = jnp.exp(s_band[j] - m)
                den = den + jnp.sum(pj, axis=-1, keepdims=True)
                acc = acc + _mm(pj, v_band[j][:, cols], NN)
            og = acc / den
            out_cols.append(jnp.where(lane < HEAD, og[0:blk], og[blk:2 * blk]))
        out_rows.append(jnp.concatenate(out_cols, axis=1))
    return jnp.concatenate(out_rows, axis=0)


ATT_ROWS = ATT_STEP * ATT_BLOCK


def _attention_inputs(p, sink, n_lat_rows):
    _, rows, _ = p.shape
    blk = ATT_BLOCK
    nb = rows // blk
    w = BRANCH_W
    n_ctx_rows = rows - n_lat_rows
    assert n_lat_rows % n_ctx_rows == 0 and n_ctx_rows % ATT_ROWS == 0

    def band(col, off):
        def index(b, t):
            return (b, jnp.clip(t * ATT_STEP + off, 0, nb - 1), col)
        return pl.BlockSpec((None, blk, w), index)

    offsets = range(-1, ATT_STEP + 1)
    ctx = lambda col: pl.BlockSpec((None, n_ctx_rows, w), lambda b, t: (b, n_lat_rows // n_ctx_rows, col))
    bias = jnp.asarray(_window_bias())
    specs = [pl.BlockSpec(memory_space=pltpu.SMEM), pl.BlockSpec((None, ATT_ROWS, w), lambda b, t: (b, t, 0)),
             ctx(1), ctx(2)]
    specs += [band(1, o) for o in offsets] + [band(2, o) for o in offsets] + [_vec_spec(bias.shape)]
    args = [sink] + [p] * (3 + 2 * len(offsets)) + [bias]
    assert len(specs) == N_ATT_REFS
    return specs, args


def _merge_kernel(*refs, n_lat_rows, starts):
    x = _stream_block(refs[0:len(starts)], starts)
    refs = refs[len(starts):]
    sc_ref, csc_ref, sh_ref, csh_ref, gm_ref, cgm_ref, g0_ref, g1_ref, ya_ref, yb_ref, yc_ref = refs[0:11]
    att_refs = refs[11:11 + N_ATT_REFS]
    wg_ref, gb_ref, wb_ref, wo_ref, o_ref = refs[11 + N_ATT_REFS:]
    is_ctx = _ctx_rows(x.shape[0], n_lat_rows)
    scale = jnp.where(is_ctx, csc_ref[...], sc_ref[...])
    shift = jnp.where(is_ctx, csh_ref[...], sh_ref[...])
    gmod = jnp.where(is_ctx, cgm_ref[...], gm_ref[...])
    h = _prenorm(x, g0_ref[...], scale, shift).astype(BF16)
    branches = [ya_ref[...], yb_ref[...], yc_ref[...], _attn_rows(att_refs, n_lat_rows)]
    acc = jnp.zeros(x.shape, F32)
    for i, y in enumerate(branches):
        pre = jnp.dot(h, wg_ref[:, i * D_MODEL:(i + 1) * D_MODEL], preferred_element_type=F32)
        gate = _sigmoid(pre + gb_ref[i:i + 1, :])
        acc = acc + gate * jnp.dot(y.astype(BF16), wb_ref[i], preferred_element_type=F32)
    out = jnp.dot(acc.astype(BF16), wo_ref[...], preferred_element_type=F32)
    ms = jnp.mean(out * out, axis=-1, keepdims=True)
    o_ref[...] = x + gmod * (out * lax.rsqrt(ms + NORM_EPS) * g1_ref[...])


def _merge(xs, mod, gain0, gain1, ys, p_att, sink, wg, gate_b, wb, wo, n_lat_rows, out_rows):
    parts = _stream_parts(xs)
    batch, d = parts[0].shape[0], parts[0].shape[-1]
    rows = out_rows
    tm = ATT_ROWS
    assert rows % tm == 0
    x_specs, x_args, starts = _stream_specs(xs, tm)
    tile = lambda width: pl.BlockSpec((None, tm, width), lambda b, i: (b, i, 0))
    consts = [wg, gate_b, wb, wo]
    att_specs, att_args = _attention_inputs(p_att, sink, n_lat_rows)
    return pl.pallas_call(
        functools.partial(_merge_kernel, n_lat_rows=n_lat_rows, starts=tuple(starts)),
        grid=(batch, rows // tm),
        in_specs=x_specs + _mod_specs(1, batch) + _mod_specs(0, batch) + _mod_specs(2, batch)
        + [_vec_spec(gain0.shape), _vec_spec(gain1.shape)]
        + [tile(BRANCH_W)] * 3 + att_specs + [_vec_spec(c.shape) for c in consts],
        out_specs=tile(d),
        out_shape=jax.ShapeDtypeStruct((batch, rows, d), F32),
        compiler_params=_cparams(("parallel", "parallel")),
        name="attn_merge_out",
    )(*x_args, *([mod] * 6), gain0, gain1, *ys, *att_args, *consts)


FFN_CHUNK = 512


def _ffn_kernel(x_ref, sc_ref, csc_ref, sh_ref, csh_ref, gm_ref, cgm_ref, g2_ref, g3_ref, w1_ref, w2_ref,
                o_ref, *, n_lat_rows):
    x = x_ref[...]
    is_ctx = _ctx_rows(x.shape[0], n_lat_rows)
    scale = jnp.where(is_ctx, csc_ref[...], sc_ref[...])
    shift = jnp.where(is_ctx, csh_ref[...], sh_ref[...])
    h = _prenorm(x, g2_ref[...], scale, shift).astype(BF16)
    hidden = w2_ref.shape[0]
    out = jnp.zeros(x.shape, F32)
    for lo in range(0, hidden, FFN_CHUNK):
        hi = min(lo + FFN_CHUNK, hidden)
        gt = jnp.dot(h, w1_ref[:, lo:hi], preferred_element_type=F32)
        up = jnp.dot(h, w1_ref[:, hidden + lo:hidden + hi], preferred_element_type=F32)
        out = out + jnp.dot((_silu(gt) * up).astype(BF16), w2_ref[lo:hi, :], preferred_element_type=F32)
    ms = jnp.mean(out * out, axis=-1, keepdims=True)
    gmod = jnp.where(is_ctx, cgm_ref[...], gm_ref[...])
    o_ref[...] = x + gmod * (out * lax.rsqrt(ms + NORM_EPS) * g3_ref[...])


def _ffn(xs, mod, gain2, gain3, w1, w2, n_lat_rows):
    batch, rows, d = xs.shape
    tm = _row_tile(rows, 544)
    tile = pl.BlockSpec((None, tm, d), lambda b, i: (b, i, 0))
    consts = [gain2, gain3, w1, w2]
    return pl.pallas_call(
        functools.partial(_ffn_kernel, n_lat_rows=n_lat_rows),
        grid=(batch, rows // tm),
        in_specs=[tile] + _mod_specs(4, batch) + _mod_specs(3, batch) + _mod_specs(5, batch)
        + [_vec_spec(c.shape) for c in consts],
        out_specs=tile,
        out_shape=jax.ShapeDtypeStruct((batch, rows, d), F32),
        compiler_params=_cparams(("parallel", "parallel")),
        name="swiglu",
    )(xs, *([mod] * 6), *consts)


def _pad_cols(w, width):
    return jnp.pad(w, ((0, 0), (0, width - w.shape[1])))


def _layer_weights(w_in, mu, w2, a2, g2, conv, a_log, dt_bias, gw2, gb):
    wts = {}
    off_b, off_c, off_d, off_g = A_IN, A_IN + B_IN, A_IN + B_IN + C_IN, A_IN + B_IN + C_IN + D_IN
    w = BRANCH_W
    wts["wa"] = _pad_cols(w_in[:, 0:A_IN], 4 * w).astype(BF16)
    wts["mu"] = _pad_cols(mu[None, :], 4 * w)
    lo = np.cumsum((0,) + A_LORA)
    place = lambda m, r0: jnp.zeros((w, w), F32).at[r0:r0 + m.shape[0], :].set(m)
    wts["w2p"] = jnp.stack([place(w2[0], lo[0]), place(w2[1], lo[1])])
    wts["a2p"] = jnp.stack([place(a2[0], lo[2]), place(a2[1], lo[3])])
    wts["g2p"] = place(g2, lo[4])

    wb = w_in[:, off_b:off_b + B_IN]
    nh = N_HEADS
    n_scalar = 4 * nh
    wts["wb"] = jnp.concatenate([wb[:, 0:3 * w], wb[:, 3 * w + n_scalar:],
                                 _pad_cols(wb[:, 3 * w:3 * w + n_scalar], LANES)], axis=1).astype(BF16)
    expand = np.zeros((4, LANES, w), np.float32)
    for grp in range(4):
        for h in range(nh):
            expand[grp, grp * nh + h, h * HEAD:(h + 1) * HEAD] = 1.0
    wts["eb"] = jnp.asarray(expand[0:2])
    wts["ea"] = jnp.asarray(expand[2:4])
    vec = lambda t: jnp.zeros((1, LANES), F32).at[0, 2 * nh:4 * nh].set(t.reshape(-1))
    wts["alog"] = vec(a_log)
    wts["dt"] = vec(dt_bias)
    wts["conv"] = conv

    wc = w_in[:, off_c:off_c + C_IN]
    pad_heads = lambda m: jnp.pad(m.reshape(m.shape[0], nh, C_DK),
                                  ((0, 0), (0, 0), (0, HEAD - C_DK))).reshape(m.shape[0], w)
    qc, kc, vc = wc[:, 0:C_QK], wc[:, C_QK:2 * C_QK], wc[:, 2 * C_QK:2 * C_QK + w]
    loc = wc[:, 2 * C_QK + w:2 * C_QK + w + 2 * C_GATE_R]
    gc = wc[:, 2 * C_QK + w + 2 * C_GATE_R:]
    wts["wc"] = jnp.concatenate([pad_heads(qc), pad_heads(kc), vc, gc, _pad_cols(loc, LANES)], axis=1).astype(BF16)
    gwp = jnp.zeros((2, LANES, w), F32)
    for d in range(2):
        gwp = gwp.at[d, d * C_GATE_R:(d + 1) * C_GATE_R, :].set(pad_heads(gw2[d]))
    wts["gwp"] = gwp
    wts["gbp"] = pad_heads(gb)
    wts["glane"] = jnp.asarray((np.arange(w) % HEAD < C_DK).astype(np.float32))[None, :]

    wd = w_in[:, off_d:off_d + D_IN]
    qd = wd[:, 0:w]
    dup = lambda m: jnp.concatenate([m[:, 0:HEAD], m[:, 0:HEAD], m[:, HEAD:], m[:, HEAD:]], axis=1)
    wts["wd"] = jnp.concatenate([qd, dup(wd[:, w:w + 2 * HEAD]), dup(wd[:, w + 2 * HEAD:])], axis=1).astype(BF16)
    wts["wg"] = w_in[:, off_g:].astype(BF16)
    return wts


def _rope_tables(n_lat_rows, n_ctx_rows):
    quarter = HEAD // 4
    inv = ROPE_BASE ** (-np.arange(quarter, dtype=np.float32) / quarter)
    pos = np.arange(n_lat_rows)
    rows = (pos // GRID_W).astype(np.float32)
    cols = (pos % GRID_W).astype(np.float32)
    inv = jnp.asarray(inv)
    ang_r = jnp.asarray(rows)[:, None] * inv[None, :]
    ang_c = jnp.asarray(cols)[:, None] * inv[None, :]
    cos = jnp.concatenate([jnp.cos(ang_r)] * 2 + [jnp.cos(ang_c)] * 2, axis=1)
    sin = jnp.concatenate([-jnp.sin(ang_r), jnp.sin(ang_r), -jnp.sin(ang_c), jnp.sin(ang_c)], axis=1)
    cos = jnp.concatenate([cos, jnp.ones((n_ctx_rows, HEAD), F32)], axis=0)
    sin = jnp.concatenate([sin, jnp.zeros((n_ctx_rows, HEAD), F32)], axis=0)
    return jnp.tile(cos, (1, N_HEADS)), jnp.tile(sin, (1, N_HEADS))


RWKV_SLOTS = {"r": ("sh", 0), "v": ("sh", 1), "a": ("sh", 2), "w": ("dw", 0), "k": ("dk", 0), "b": ("dk", 1)}
GDN_SLOTS = RWKV_SLOTS
GLA_SLOTS = {"r": ("sh", 0), "k": ("sh", 1), "v": ("sh", 2)}


def kernel(x, c, ctx, c_ctx, ada_w, ada_b, norm_g, w_in, gate_b, w_branch, w_out, rwkv_mu, rwkv_w0, rwkv_w2, rwkv_a0, rwkv_a2, rwkv_g2, rwkv_kk, rwkv_ka, rwkv_rk, rwkv_ln_g, rwkv_ln_b, gdn_conv, gdn_a_log, gdn_dt_bias, gdn_norm_g, gla_gw2, gla_gb, gla_norm_g, attn_sink, ffn_w1, ffn_w2):
    batch, n_lat, d = x.shape
    n_ctx = ctx.shape[1]
    depth = ada_w.shape[0]
    assert n_ctx % PREP_ROWS == 0 and n_lat % PREP_ROWS == 0 and d == D_MODEL

    mod_rows = 8 * ((batch + 1 + 7) // 8)
    c_rows = jnp.concatenate([c, c_ctx[None, :], jnp.zeros((mod_rows - batch - 1, d), F32)], axis=0)
    mod_all = _modulation(c_rows, ada_w, ada_b)
    cos, sin = _rope_tables(n_lat, n_ctx)

    xs = (x, ctx)
    rows = n_lat + n_ctx
    row = lambda t: t.reshape(1, -1)
    for l in range(depth):
        out_rows = n_lat if l == depth - 1 else rows
        mod = mod_all[l].reshape(mod_rows, 1, 6 * d)
        ng = norm_g[l]
        wts = _layer_weights(w_in[l], rwkv_mu[l], rwkv_w2[l], rwkv_a2[l], rwkv_g2[l], gdn_conv[l],
                             gdn_a_log[l], gdn_dt_bias[l], gla_gw2[l], gla_gb[l])
        ones = jnp.asarray(_head_block_ones())
        around = jnp.asarray(0.5 * (_shift_matrix(-1) + _shift_matrix(1)), BF16)
        rwkv_consts = [around, wts["mu"], wts["w2p"], wts["a2p"], wts["g2p"], rwkv_w0[l], rwkv_a0[l],
                       row(rwkv_kk[l]), row(rwkv_ka[l]), row(rwkv_rk[l]), ones]
        gdn_consts = [wts["conv"], wts["alog"], wts["dt"], wts["eb"], wts["ea"], ones]
        p_gla, p_att, rwkv_ops, gdn_ops = _front(xs, mod, row(ng[0]), (wts["wc"], wts["wb"], wts["wd"], wts["wa"]),
                                                 cos, sin, rwkv_consts, gdn_consts, n_lat)

        sh, dks, dws, fin = rwkv_ops
        ya = _bidir_scan(sh, dks, dws, n_lat, fin, row(rwkv_ln_g[l]), row(rwkv_ln_b[l]), finish="groupnorm",
                         scalar_decay=False, lowrank=True, slots=RWKV_SLOTS)

        sh, dks, dws, fin = gdn_ops
        gnorm = row(jnp.tile(gdn_norm_g[l], N_HEADS))
        yb = _bidir_scan(sh, dks, dws, n_lat, fin, gnorm, gnorm, finish="rms",
                         scalar_decay=True, lowrank=True, slots=GDN_SLOTS)

        yc = _gla_scan(p_gla, n_lat, wts["gwp"], wts["gbp"], wts["glane"], row(jnp.tile(gla_norm_g[l], N_HEADS)))

        xs = _merge(xs, mod, row(ng[0]), row(ng[1]), (ya, yb, yc), p_att, attn_sink[l], wts["wg"], gate_b[l],
                    w_branch[l].astype(BF16), w_out[l].astype(BF16), n_lat, out_rows)
        xs = _ffn(xs, mod, row(ng[2]), row(ng[3]), ffn_w1[l].astype(BF16), ffn_w2[l].astype(BF16), n_lat)
    return xs
```

```python
import functools
import math

import numpy as np
import jax
import jax.numpy as jnp
from jax import lax
from jax.experimental import pallas as pl
from jax.experimental.pallas import tpu as pltpu

F32 = jnp.float32
BF16 = jnp.bfloat16

D_MODEL = 1024
N_BRANCH = 4
BRANCH_W = 256
HEAD = 64
N_HEADS = BRANCH_W // HEAD
NORM_EPS = 1e-6
A_GN_EPS = 64e-5
A_LORA = (32, 32, 32, 32, 64)
A_IN = 3 * BRANCH_W + sum(A_LORA)
B_IN = 4 * BRANCH_W + 4 * N_HEADS
C_DK = 32
C_QK = N_HEADS * C_DK
C_GATE_R = 16
C_GATE_NORM = 16.0
C_IN = 2 * C_QK + 2 * BRANCH_W + 2 * C_GATE_R
D_KV_HEADS = 2
D_IN = BRANCH_W + 2 * D_KV_HEADS * HEAD
B_CONV = 7
WINDOW = 128
ROPE_BASE = 10000.0
GRID_W = 64
FFN_HIDDEN = 2816

LANES = 128
CHUNK = 64
PREP_ROWS = 256
HALO = 16
ATT_BLOCK = 128
VMEM_LIMIT = 48 * 1024 * 1024
NEG_BIG = -1e30

NN = (((1,), (0,)), ((), ()))
NT = (((1,), (1,)), ((), ()))
TN = (((0,), (0,)), ((), ()))


def _mm(a, b, dims=NN, mode="bf16"):
    if mode == "f32":
        return lax.dot_general(a, b, dims, precision=lax.Precision.HIGHEST, preferred_element_type=F32)
    return lax.dot_general(a.astype(BF16), b.astype(BF16), dims, preferred_element_type=F32)


def _sigmoid(x):
    return 1.0 / (1.0 + jnp.exp(-x))


def _silu(x):
    return x * _sigmoid(x)


def _softplus(x):
    return jnp.maximum(x, 0.0) + jnp.log1p(jnp.exp(-jnp.abs(x)))


def _cparams(sem):
    return pltpu.CompilerParams(dimension_semantics=sem, vmem_limit_bytes=VMEM_LIMIT)


def _mod_kernel(c_ref, w_ref, b_ref, o_ref):
    c = c_ref[...]
    o_ref[...] = _mm(_silu(c), w_ref[...], mode="f32") + b_ref[...]


def _modulation(c_rows, ada_w, ada_b):
    depth, d, n = ada_w.shape
    rows = c_rows.shape[0]
    tn = 1024
    return pl.pallas_call(
        _mod_kernel,
        grid=(depth, n // tn),
        in_specs=[
            pl.BlockSpec((rows, d), lambda l, j: (0, 0)),
            pl.BlockSpec((None, d, tn), lambda l, j: (l, 0, j)),
            pl.BlockSpec((None, 1, tn), lambda l, j: (l, 0, j)),
        ],
        out_specs=pl.BlockSpec((None, rows, tn), lambda l, j: (l, 0, j)),
        out_shape=jax.ShapeDtypeStruct((depth, rows, n), F32),
        compiler_params=_cparams(("parallel", "parallel")),
        name="adaln_mod",
    )(c_rows, ada_w, ada_b.reshape(depth, 1, n))


def _row_tile(rows, target):
    return max(t for t in range(8, target + 1, 8) if rows % t == 0)


def _mod_specs(which, batch):
    lat = pl.BlockSpec((None, 1, D_MODEL), lambda b, i, *_: (b, 0, which))
    ctx = pl.BlockSpec((None, 1, D_MODEL), lambda b, i, *_: (batch, 0, which))
    return [lat, ctx]


def _ctx_rows(tm, n_lat_rows):
    row = pl.program_id(1) * tm + lax.broadcasted_iota(jnp.int32, (tm, 1), 0)
    return row >= n_lat_rows


def _prenorm(x, gain, scale, shift):
    ms = jnp.mean(x * x, axis=-1, keepdims=True)
    return (x * lax.rsqrt(ms + NORM_EPS)) * (gain * (1.0 + scale)) + shift


def _rope(x, cos, sin):
    width = x.shape[-1]
    lane = lax.broadcasted_iota(jnp.int32, x.shape, 1)
    half, quarter = HEAD // 2, HEAD // 4
    swapped = jnp.where(lane % half < quarter, pltpu.roll(x, width - quarter, axis=1), pltpu.roll(x, quarter, axis=1))
    return x * cos + swapped * sin


def _head_block_ones():
    idx = np.arange(BRANCH_W)
    return (idx[:, None] // HEAD == idx[None, :] // HEAD).astype(np.float32)


N_LEVELS = 6


def _level_mask(ri, ci, s, reverse):
    b = 1 << s
    blk = (ri // (2 * b)) == (ci // (2 * b))
    hi_r, hi_c = (ri // b) % 2 == 1, (ci // b) % 2 == 1
    return blk & ((hi_c & ~hi_r) if reverse else (hi_r & ~hi_c))


SCAN_HEADS = 2
GROUP_W = SCAN_HEADS * HEAD
N_GROUPS = N_HEADS // SCAN_HEADS


def _scan_masks(reverse):
    n = SCAN_HEADS * CHUNK
    ri = np.arange(CHUNK)[:, None]
    ci = np.arange(n)[None, :] % CHUNK
    strict, incl = (ci > ri, ci >= ri) if reverse else (ci < ri, ci <= ri)
    compact = [strict, incl, ci == ri] + [_level_mask(ri, ci, s, reverse) for s in range(N_LEVELS)]
    r = np.arange(n)[:, None]
    c = np.arange(n)[None, :]
    block = [(r // CHUNK) == (c // CHUNK), r == c]
    return np.stack(compact).astype(np.float32), np.stack(block).astype(np.float32)


EXACT_TERMS = 3
NORM_TERMS = 2


def _tri(reverse):
    i = np.arange(CHUNK)
    m = (i[None, :] >= i[:, None]) if reverse else (i[None, :] <= i[:, None])
    return np.tile(m.astype(np.float32), (1, EXACT_TERMS))


def _split_terms(x, terms):
    out = []
    for _ in range(terms - 1):
        part = x.astype(BF16)
        out.append(part)
        x = x - part.astype(F32)
    out.append(x.astype(BF16))
    return out


def _mm_exact(a, b, dims=NN, split="a", terms=EXACT_TERMS):
    assert dims == NN
    if split == "a":
        lhs = jnp.concatenate(_split_terms(a, terms), axis=1)
        rhs = jnp.concatenate([b.astype(BF16)] * terms, axis=0)
    else:
        lhs = a.astype(BF16)
        rhs = jnp.concatenate(_split_terms(b, terms), axis=0)
    return lax.dot_general(lhs, rhs, dims, preferred_element_type=F32)


C_STRICT, C_INCL, C_EYE, C_LVL0 = 0, 1, 2, 3
B_SAME, B_EYE = 0, 1
SCAN_BATCH = 8
SCAN_ROWS = 256
assert CHUNK == HEAD and (1 << N_LEVELS) == CHUNK


def _scan_chunk(vals, st, tri, ones, cm_ref, bm_ref, same_bf, *, reverse, scalar_decay, lowrank, mm_mode):
    same = bm_ref[B_SAME]

    def get(name):
        return vals[name]

    def expand(x):
        return jnp.concatenate([x.astype(BF16)] * SCAN_HEADS, axis=0) * same_bf

    def keep(x, k):
        return jnp.where(cm_ref[k] > 0.5, x, 0.0)

    logw = get("w")
    r, k, v = get("r"), get("k"), get("v")
    cum = _mm_exact(tri, logw, split="b")
    cum_x = cum - logw
    last = 0 if reverse else CHUNK - 1
    total = cum[last:last + 1, :]

    if scalar_decay:
        diag = jnp.concatenate([cum] * SCAN_HEADS, axis=0) * bm_ref[B_EYE]
        cum_row = _mm_exact(ones, diag, split="b")
        d_ii = jnp.exp(jnp.where(cm_ref[C_INCL] > 0.5, cum - cum_row, NEG_BIG))
        r_q, k_q = r, k
        to_end = jnp.exp(total - cum)
        r_abs = r * jnp.exp(cum)
    else:
        ref_row = cum[CHUNK // 2:CHUNK // 2 + 1, :]
        e_ref = jnp.exp(ref_row)
        p_inv = jnp.exp(ref_row - cum)
        r_q, k_q = r * jnp.exp(cum - ref_row), k * p_inv
        to_end = p_inv * jnp.exp(total - ref_row)
        r_abs = r_q * e_ref

    k_e, v_e = expand(k_q), expand(v)
    if not lowrank:
        s_k = _mm(r_q, k_e, NT, mm_mode)
        a_rk = s_k * d_ii if scalar_decay else keep(s_k, C_INCL)
        y = _mm(r_abs, st, NT, mm_mode) + _mm(a_rk, v_e, NN, mm_mode)
        upd = _mm(v, k * to_end, TN, mm_mode)
        return y, st * jnp.exp(total) + upd * same

    a, b = get("a"), get("b")
    if scalar_decay:
        d_xi = jnp.exp(jnp.where(cm_ref[C_STRICT] > 0.5, cum_x - cum_row, NEG_BIG))
        a_q, b_q = a, b
    else:
        a_q, b_q = a * jnp.exp(cum_x - ref_row), b * p_inv
    lhs = jnp.concatenate([a_q, r_q], axis=0)
    scores = _mm(lhs, jnp.concatenate([expand(b_q), k_e], axis=0), NT, mm_mode)
    s_b, s_k = scores[:, 0:GROUP_W], scores[:, GROUP_W:]
    if scalar_decay:
        a_ab, a_rb = s_b[0:CHUNK] * d_xi, s_b[CHUNK:] * d_ii
        a_ak, a_rk = s_k[0:CHUNK] * d_xi, s_k[CHUNK:] * d_ii
    else:
        a_ab, a_rb = keep(s_b[0:CHUNK], C_STRICT), keep(s_b[CHUNK:], C_INCL)
        a_ak, a_rk = keep(s_k[0:CHUNK], C_STRICT), keep(s_k[CHUNK:], C_INCL)

    inv = cm_ref[C_EYE] + a_ab * cm_ref[C_LVL0]
    for s in range(1, N_LEVELS):
        c_s = expand(a_ab * cm_ref[C_LVL0 + s])
        inv = inv + _mm(_mm(inv, c_s, NN, mm_mode), expand(inv), NN, mm_mode)

    a_abs = a * jnp.exp(cum_x) if scalar_decay else a_q * e_ref
    from_state = _mm(jnp.concatenate([a_abs, r_abs], axis=0), st, NT, mm_mode)
    from_v = _mm(jnp.concatenate([a_ak, a_rk], axis=0), v_e, NN, mm_mode)
    both = from_state + from_v
    z = _mm(inv, expand(both[0:CHUNK]), NN, mm_mode)
    y = both[CHUNK:] + _mm(a_rb, expand(z), NN, mm_mode)
    upd = _mm(jnp.concatenate([v, z], axis=0), jnp.concatenate([k * to_end, b * to_end], axis=0), TN, mm_mode)
    return y, st * jnp.exp(total) + upd * same


def _scan_kernel(*refs, reverse, scalar_decay, lowrank, slots, sources, finish, mm_mode, lora, r_scale):
    it = iter(refs)
    src_refs = {name: next(it) for name in sources}
    sh_ref = src_refs[sources[0]]
    cm_ref, bm_ref, tri_ref = next(it), next(it), next(it)
    if lora:
        lo_ref, gw_ref, gb_ref, lane_ref = next(it), next(it), next(it), next(it)
    if finish:
        ob_ref, fin_ref, p1_ref, p2_ref, avg_ref = next(it), next(it), next(it), next(it), next(it)
    o_ref, st_ref = next(it), next(it)

    @pl.when(pl.program_id(1) == 0)
    def _():
        st_ref[...] = jnp.zeros_like(st_ref)

    nb, block_rows = sh_ref.shape[0], sh_ref.shape[1]
    n_chunks = block_rows // CHUNK
    n = nb * N_GROUPS
    tri = jnp.broadcast_to(tri_ref[...], (n,) + tri_ref.shape)
    ones = jnp.ones((n, CHUNK, EXACT_TERMS * SCAN_HEADS * CHUNK), F32)
    chunk = functools.partial(_scan_chunk, cm_ref=cm_ref, bm_ref=bm_ref, same_bf=bm_ref[B_SAME].astype(BF16),
                              reverse=reverse, scalar_decay=scalar_decay, lowrank=lowrank, mm_mode=mm_mode)

    def step(c, carry):
        rows = pl.ds(pl.multiple_of((n_chunks - 1 - c if reverse else c) * CHUNK, CHUNK), CHUNK)

        def groups(name):
            src, idx = slots[name]
            ref = src_refs[src]
            parts = [ref[:, rows, idx * BRANCH_W + g * GROUP_W:idx * BRANCH_W + (g + 1) * GROUP_W].astype(F32)
                     for g in range(N_GROUPS)]
            return jnp.stack(parts, axis=1).reshape(n, CHUNK, GROUP_W)

        vals = {name: groups(name) for name in slots}
        if lora:
            lo = lo_ref[:, rows, :].reshape(nb * CHUNK, lo_ref.shape[-1])
            z = _mm(lo, gw_ref[...], mode=mm_mode) + gb_ref[...]
            logw = ((-_softplus(-z) / C_GATE_NORM) * lane_ref[...]).reshape(nb, CHUNK, BRANCH_W)
            parts = [logw[:, :, g * GROUP_W:(g + 1) * GROUP_W] for g in range(N_GROUPS)]
            vals["w"] = jnp.stack(parts, axis=1).reshape(n, CHUNK, GROUP_W)
        if r_scale != 1.0:
            vals["r"] = vals["r"] * r_scale
        y, st_new = jax.vmap(chunk)(vals, st_ref[...], tri, ones)
        st_ref[...] = st_new
        y = y.reshape(nb, N_GROUPS, CHUNK, GROUP_W)
        y = jnp.concatenate([y[:, g] for g in range(N_GROUPS)], axis=-1)
        if not finish:
            o_ref[:, rows, :] = y
            return carry

        y = (y + ob_ref[:, rows, :]).reshape(nb * CHUNK, BRANCH_W)
        avg = avg_ref[...]
        if finish == "groupnorm":
            gate = fin_ref[:, rows, 0:BRANCH_W].reshape(y.shape)
            bonus = fin_ref[:, rows, BRANCH_W:2 * BRANCH_W].reshape(y.shape)
            cen = y - _mm_exact(y, avg)
            var = _mm_exact(cen * cen, avg)
            yn = cen * lax.rsqrt(var + A_GN_EPS) * p1_ref[...] + p2_ref[...]
            out = (yn + bonus) * gate
        else:
            ms = _mm_exact(y * y, avg)
            gate = fin_ref[:, rows, :].astype(F32).reshape(y.shape)
            out = y * lax.rsqrt(ms + NORM_EPS) * p1_ref[...] * _silu(gate)
        o_ref[:, rows, :] = out.reshape(nb, CHUNK, BRANCH_W).astype(o_ref.dtype)
        return carry

    lax.fori_loop(0, n_chunks, step, 0)


def _view(x):
    return x if isinstance(x, tuple) else (x, x.shape[-1], 0)


def _scan(srcs, n_lat_rows, *, reverse, scalar_decay, lowrank, slots, finish=None, fin_args=None,
          mm_mode="bf16", lora=None, r_scale=1.0):
    sources = tuple(srcs)
    views = [_view(srcs[s]) for s in sources]
    batch, rows, _ = views[0][0].shape
    nc, nlat = rows // SCAN_ROWS, n_lat_rows // SCAN_ROWS
    nctx = nc - nlat
    nb = math.gcd(batch, SCAN_BATCH)

    if reverse:
        def chunk(n):
            return nc - 1 - n
    else:
        def chunk(n):
            return jnp.where(n < nctx, nlat + n, n - nctx)

    def row_spec(width, col=0):
        return pl.BlockSpec((nb, SCAN_ROWS, width), lambda b, n: (b, chunk(n), col))

    def const_spec(shape):
        zeros = (0,) * len(shape)
        return pl.BlockSpec(shape, lambda b, n: zeros)

    cmask, bmask = (jnp.asarray(m) for m in _scan_masks(reverse))
    tri = jnp.asarray(_tri(reverse))
    in_specs = [row_spec(w, c) for _, w, c in views]
    in_specs += [const_spec(cmask.shape), const_spec(bmask.shape), const_spec(tri.shape)]
    args = [a for a, _, _ in views] + [cmask, bmask, tri]
    if lora:
        (lo, lo_w, lo_c), gw, gb, lane = _view(lora[0]), lora[1], lora[2], lora[3]
        in_specs += [row_spec(lo_w, lo_c), const_spec(gw.shape), const_spec(gb.shape), const_spec(lane.shape)]
        args += [lo, gw, gb, lane]
    if finish:
        ob, fin, p1, p2 = fin_args
        fin, fin_w, fin_c = _view(fin)
        avg = jnp.asarray(_head_block_ones() / HEAD)
        in_specs += [row_spec(BRANCH_W), row_spec(fin_w, fin_c), const_spec(p1.shape), const_spec(p2.shape),
                     const_spec(avg.shape)]
        args += [ob, fin, p1, p2, avg]
    kern = functools.partial(_scan_kernel, reverse=reverse, scalar_decay=scalar_decay, lowrank=lowrank,
                             slots=slots, sources=sources, finish=finish, mm_mode=mm_mode, lora=bool(lora),
                             r_scale=r_scale)
    return pl.pallas_call(
        kern,
        grid=(batch // nb, nc),
        in_specs=in_specs,
        out_specs=row_spec(BRANCH_W),
        out_shape=jax.ShapeDtypeStruct((batch, rows, BRANCH_W), BF16 if finish else F32),
        scratch_shapes=[pltpu.VMEM((nb * N_GROUPS, GROUP_W, GROUP_W), F32)],
        compiler_params=_cparams(("parallel", "arbitrary")),
        name="dplr_scan_" + ("bwd" if reverse else "fwd"),
    )(*args)


def _gla_scan(p, n_lat_rows, gwp, gbp, lane, norm_g):
    w = BRANCH_W
    srcs = {"sh": (p, 3 * w, 0)}
    lo = (p, gwp.shape[1], 4 * w // gwp.shape[1])
    kw = dict(scalar_decay=False, lowrank=False, slots=GLA_SLOTS, r_scale=C_DK ** -0.5)
    ob = _scan(srcs, n_lat_rows, reverse=True, lora=(lo, gwp[1], gbp[1:2], lane), **kw)
    return _scan(srcs, n_lat_rows, reverse=False, lora=(lo, gwp[0], gbp[0:1], lane), finish="rms",
                 fin_args=(ob, (p, w, 3), norm_g, norm_g), **kw)


def _bidir_scan(sh, dks, dws, n_lat_rows, fin, p1, p2, *, finish, **kw):
    def srcs(d):
        out = {"sh": sh, "dw": dws[d]}
        if dks is not None:
            out["dk"] = dks[d]
        return out

    ob = _scan(srcs(1), n_lat_rows, reverse=True, **kw)
    return _scan(srcs(0), n_lat_rows, reverse=False, finish=finish, fin_args=(ob, fin, p1, p2), **kw)


def _seq_edges(n_lat_blocks):
    i = pl.program_id(1)
    first = (i == 0) | (i == n_lat_blocks)
    lastb = (i == n_lat_blocks - 1) | (i == pl.num_programs(1) - 1)
    return first, lastb


def _vec_spec(shape):
    zeros = (0,) * len(shape)
    return pl.BlockSpec(shape, lambda b, i: zeros)


def _shift_matrix(offset):
    m = np.zeros((PREP_ROWS, PREP_ROWS + 2 * HALO), np.float32)
    t = np.arange(PREP_ROWS)
    src = t + offset
    col = np.where(src < 0, PREP_ROWS + HALO + src, np.where(src >= PREP_ROWS, HALO + src, src))
    m[t, col] = 1.0
    return m


def _rwkv_prep(x, prev, nxt, first, lastb, shift_ref, mu_ref, w2_ref, a2_ref, g2_ref, w0_ref, a0_ref, kk_ref,
               ka_ref, rk_ref, ones_ref, sh_ref, dk0_ref, dk1_ref, dw0_ref, dw1_ref, fin_ref, mode="bf16"):
    zero = jnp.zeros((), x.dtype)
    xe = jnp.concatenate([x, jnp.where(first, zero, prev), jnp.where(lastb, zero, nxt)], axis=0)
    around = jnp.dot(shift_ref[...], xe, preferred_element_type=F32)
    x = x.astype(F32)
    xm = x + (around - x) * mu_ref[...]
    w = BRANCH_W
    r, k, v, lo = xm[:, 0:w], xm[:, w:2 * w], xm[:, 2 * w:3 * w], xm[:, 3 * w:4 * w]
    ones = ones_ref[...]
    lo_a, lo_g = lo[:, 0:LANES], lo[:, LANES:2 * LANES]
    th = jnp.tanh(lo_a)
    gate = _mm(_sigmoid(lo_g), g2_ref[LANES:2 * LANES, :], mode=mode)
    kx = k * kk_ref[...]
    kk = kx * lax.rsqrt(_mm_exact(kx * kx, ones, terms=NORM_TERMS) + 1e-6)
    sh_ref[:, 0:w] = r.astype(BF16)
    sh_ref[:, w:2 * w] = v.astype(BF16)
    sh_ref[:, 2 * w:3 * w] = (-kk).astype(BF16)
    bonus = jnp.zeros_like(v)
    for d, (dk_ref, dw_ref) in enumerate(((dk0_ref, dw0_ref), (dk1_ref, dw1_ref))):
        w_raw = w0_ref[d:d + 1, :] + _mm(th, w2_ref[d, 0:LANES, :], mode=mode)
        dw_ref[...] = -math.exp(-0.5) * _sigmoid(w_raw)
        a = _sigmoid(a0_ref[d:d + 1, :] + _mm(lo_a, a2_ref[d, 0:LANES, :], mode=mode))
        kd = k * (1.0 + (a - 1.0) * ka_ref[...])
        dk_ref[:, 0:w] = kd.astype(BF16)
        dk_ref[:, w:2 * w] = (kk * a).astype(BF16)
        bonus = bonus + _mm_exact(r * kd * rk_ref[...], ones, terms=NORM_TERMS) * v
    fin_ref[:, 0:w] = gate
    fin_ref[:, w:2 * w] = bonus


def _gdn_prep(p, prev, nxt, first, lastb, conv_ref, alog_ref, dt_ref, eb_ref, ea_ref, ones_ref,
              sh_ref, dk0_ref, dk1_ref, dw0_ref, dw1_ref, fin_ref):
    w = BRANCH_W
    x = p[:, 0:3 * w].astype(F32)
    top = jnp.where(first, 0.0, prev[:, 0:3 * w].astype(F32))
    bot = jnp.where(lastb, 0.0, nxt[:, 0:3 * w].astype(F32))
    xe = jnp.concatenate([top, x, bot], axis=0)
    ext = PREP_ROWS + 2 * HALO
    acc = jnp.zeros_like(x)
    for s in range(B_CONV):
        shift = (B_CONV // 2 - s) % ext
        rolled = xe if shift == 0 else pltpu.roll(xe, shift, axis=0)
        acc = acc + rolled[HALO:HALO + PREP_ROWS] * conv_ref[s:s + 1, :]
    qkv = _silu(acc)
    ones = ones_ref[...]

    def l2n(t):
        return t * lax.rsqrt(_mm_exact(t * t, ones, terms=NORM_TERMS) + 1e-6)

    q = l2n(qkv[:, 0:w]) * (HEAD ** -0.5)
    k = l2n(qkv[:, w:2 * w])
    v = qkv[:, 2 * w:3 * w]
    sh_ref[:, 0:w] = q.astype(BF16)
    sh_ref[:, w:2 * w] = v.astype(BF16)
    sh_ref[:, 2 * w:3 * w] = k.astype(BF16)
    sr = p[:, 4 * w:4 * w + LANES].astype(F32)
    beta_all = _sigmoid(sr)
    g_all = -jnp.exp(alog_ref[...]) * _softplus(sr + dt_ref[...])
    for d, (dk_ref, dw_ref) in enumerate(((dk0_ref, dw0_ref), (dk1_ref, dw1_ref))):
        beta = _mm_exact(beta_all, eb_ref[d])
        g = _mm_exact(g_all, ea_ref[d])
        kb = k * beta
        dw_ref[...] = g
        dk_ref[:, 0:w] = kb.astype(BF16)
        dk_ref[:, w:2 * w] = (-jnp.exp(g) * kb).astype(BF16)
    fin_ref[...] = p[:, 3 * w:4 * w].astype(F32)


SEG_GLA, SEG_GDN, SEG_ATT, SEG_RWKV = range(4)
N_RWKV_CONSTS, N_GDN_CONSTS = 11, 6
RWKV_OUTS = [(3, BF16), (2, BF16), (2, BF16), (1, F32), (1, F32), (2, F32)]
GDN_OUTS = [(3, BF16), (2, BF16), (2, BF16), (1, F32), (1, F32), (1, F32)]


def _stream_parts(xs):
    return list(xs) if isinstance(xs, (tuple, list)) else [xs]


def _stream_specs(xs, block_rows, halo_rows=0):
    specs, args, starts = [], [], []
    base = 0
    for part in _stream_parts(xs):
        d = part.shape[-1]
        nblk = part.shape[1] // block_rows
        own = lambda b, i, base=base, nblk=nblk: (b, jnp.clip(i - base, 0, nblk - 1), 0)
        specs.append(pl.BlockSpec((None, block_rows, d), own))
        args.append(part)
        if halo_rows:
            per, last = block_rows // halo_rows, part.shape[1] // halo_rows - 1
            prev = lambda b, i, base=base, per=per, last=last: (b, jnp.clip((i - base) * per - 1, 0, last), 0)
            nxt = lambda b, i, base=base, per=per, last=last: (b, jnp.clip((i - base + 1) * per, 0, last), 0)
            specs += [pl.BlockSpec((None, halo_rows, d), prev), pl.BlockSpec((None, halo_rows, d), nxt)]
            args += [part, part]
        starts.append(base)
        base += nblk
    return specs, args, starts


def _stream_block(refs, starts):
    val = refs[0][...]
    for ref, start in zip(refs[1:], starts[1:]):
        val = jnp.where(pl.program_id(1) >= start, ref[...], val)
    return val


def _front_kernel(*refs, n_lat_blocks, starts):
    it = iter(refs)
    x_refs = [next(it) for _ in range(3 * len(starts))]
    x_own, x_prev, x_next = (_stream_block(x_refs[k::3], starts) for k in range(3))
    sc_ref, csc_ref, sh_ref, csh_ref, g_ref = (next(it) for _ in range(5))
    w_refs = [next(it) for _ in range(4)]
    cos_ref, sin_ref = next(it), next(it)
    rwkv_consts = [next(it) for _ in range(N_RWKV_CONSTS)]
    gdn_consts = [next(it) for _ in range(N_GDN_CONSTS)]
    pc_ref, pd_ref = next(it), next(it)
    rwkv_outs = [next(it) for _ in RWKV_OUTS]
    gdn_outs = [next(it) for _ in GDN_OUTS]

    first, lastb = _seq_edges(n_lat_blocks)
    is_ctx = pl.program_id(1) >= n_lat_blocks
    scale = jnp.where(is_ctx, csc_ref[...], sc_ref[...])
    shift = jnp.where(is_ctx, csh_ref[...], sh_ref[...])
    h = _prenorm(x_own, g_ref[...], scale, shift).astype(BF16)
    halo = jnp.concatenate([x_prev, x_next], axis=0)
    h_halo = _prenorm(halo, g_ref[...], scale, shift).astype(BF16)

    def proj(rows, which):
        return jnp.dot(rows, w_refs[which][...], preferred_element_type=F32)

    h_ext = jnp.concatenate([h, h_halo], axis=0)
    n = h.shape[0]
    pb = proj(h_ext, SEG_GDN).astype(BF16)
    pa = proj(h_ext, SEG_RWKV).astype(BF16)
    _gdn_prep(pb[0:n], pb[n:n + HALO], pb[n + HALO:], first, lastb, *gdn_consts, *gdn_outs)
    pc_ref[...] = proj(h, SEG_GLA).astype(BF16)
    att = proj(h, SEG_ATT)
    _rwkv_prep(pa[0:n], pa[n:n + HALO], pa[n + HALO:], first, lastb, *rwkv_consts, *rwkv_outs)
    w = BRANCH_W
    for c0 in range(0, att.shape[1], w):
        part = att[:, c0:c0 + w]
        if c0 < 2 * w:
            part = _rope(part, cos_ref[...], sin_ref[...])
        pd_ref[:, c0:c0 + w] = part.astype(BF16)


def _front(xs, mod, gain, weights, cos, sin, rwkv_consts, gdn_consts, n_lat_rows):
    parts = _stream_parts(xs)
    batch, rows = parts[0].shape[0], sum(t.shape[1] for t in parts)
    x_specs, x_args, starts = _stream_specs(xs, PREP_ROWS, HALO)
    assert len(rwkv_consts) == N_RWKV_CONSTS and len(gdn_consts) == N_GDN_CONSTS and len(weights) == 4
    tab = pl.BlockSpec((PREP_ROWS, BRANCH_W), lambda b, i: (i, 0))
    consts = list(rwkv_consts) + list(gdn_consts)
    outs = [(weights[s].shape[1], BF16) for s in (SEG_GLA, SEG_ATT)]
    outs += [(n * BRANCH_W, dt) for n, dt in RWKV_OUTS + GDN_OUTS]
    out_specs = [pl.BlockSpec((None, PREP_ROWS, lanes), lambda b, i: (b, i, 0)) for lanes, _ in outs]
    out_shape = [jax.ShapeDtypeStruct((batch, rows, lanes), dt) for lanes, dt in outs]
    res = pl.pallas_call(
        functools.partial(_front_kernel, n_lat_blocks=n_lat_rows // PREP_ROWS, starts=tuple(starts)),
        grid=(batch, rows // PREP_ROWS),
        in_specs=x_specs + _mod_specs(1, batch) + _mod_specs(0, batch)
        + [_vec_spec(gain.shape)] + [_vec_spec(w.shape) for w in weights] + [tab, tab]
        + [_vec_spec(c.shape) for c in consts],
        out_specs=out_specs,
        out_shape=out_shape,
        compiler_params=_cparams(("parallel", "parallel")),
        name="front_proj_prep",
    )(*x_args, mod, mod, mod, mod, gain, *weights, cos, sin, *consts)
    pc, pd = res[0], res[1]
    a_sh, a_dk0, a_dk1, a_dw0, a_dw1, a_fin = res[2:8]
    b_sh, b_dk0, b_dk1, b_dw0, b_dw1, b_fin = res[8:14]
    return pc, pd, (a_sh, (a_dk0, a_dk1), (a_dw0, a_dw1), a_fin), (b_sh, (b_dk0, b_dk1), (b_dw0, b_dw1), b_fin)


def _window_bias():
    iq = np.arange(2 * ATT_BLOCK)[:, None] % ATT_BLOCK
    ik = np.arange(ATT_BLOCK)[None, :]
    ok = np.stack([ik >= iq, np.ones_like(ik >= iq), ik <= iq])
    assert WINDOW == ATT_BLOCK
    return np.where(ok, 0.0, NEG_BIG).astype(np.float32)


ATT_STEP = 2


N_ATT_REFS = 5 + 2 * (ATT_STEP + 2)


def _attn_rows(refs, n_lat_rows):
    sink_ref, q_ref, kc_ref, vc_ref = refs[0:4]
    nband = ATT_STEP + 2
    k_all, v_all = refs[4:4 + nband], refs[4 + nband:4 + 2 * nband]
    bias_ref = refs[4 + 2 * nband]
    blk = ATT_BLOCK
    lane = lax.broadcasted_iota(jnp.int32, (blk, 2 * HEAD), 1)
    row2 = lax.broadcasted_iota(jnp.int32, (2 * blk, 1), 0)
    out_rows = []
    for u in range(ATT_STEP):
        out_cols = []
        t = pl.program_id(1) * ATT_STEP + u
        rows = slice(u * blk, (u + 1) * blk)
        q = q_ref[rows, :] * (HEAD ** -0.5)
        k_band, v_band = k_all[u:u + 3], v_all[u:u + 3]
        q_lat = t * blk < n_lat_rows
        in_seq = [q_lat & (t >= 1), q_lat, q_lat & ((t + 1) * blk < n_lat_rows)]
        bias = [jnp.where(in_seq[j], bias_ref[j], NEG_BIG) for j in range(3)]
        for g in range(D_KV_HEADS):
            cols = slice(g * 2 * HEAD, (g + 1) * 2 * HEAD)
            qg = q[:, cols]
            zero = jnp.zeros((), qg.dtype)
            qs = jnp.concatenate([jnp.where(lane < HEAD, qg, zero), jnp.where(lane >= HEAD, qg, zero)], axis=0)
            s_ctx = _mm(qs, kc_ref[:, cols], NT)
            s_band = [_mm(qs, k_band[j][:, cols], NT) + bias[j] for j in range(3)]
            sink = jnp.where(row2 < blk, sink_ref[2 * g], sink_ref[2 * g + 1])
            m = jnp.maximum(jnp.max(s_ctx, axis=-1, keepdims=True), sink)
            for s in s_band:
                m = jnp.maximum(m, jnp.max(s, axis=-1, keepdims=True))
            p_ctx = jnp.exp(s_ctx - m)
            den = jnp.sum(p_ctx, axis=-1, keepdims=True) + jnp.exp(sink - m)
            acc = _mm(p_ctx, vc_ref[:, cols], NN)
            for j in range(3):
                pj = jnp.exp(s_band[j] - m)
                den = den + jnp.sum(pj, axis=-1, keepdims=True)
                acc = acc + _mm(pj, v_band[j][:, cols], NN)
            og = acc / den
            out_cols.append(jnp.where(lane < HEAD, og[0:blk], og[blk:2 * blk]))
        out_rows.append(jnp.concatenate(out_cols, axis=1))
    return jnp.concatenate(out_rows, axis=0)


ATT_ROWS = ATT_STEP * ATT_BLOCK


def _attention_inputs(p, sink, n_lat_rows):
    _, rows, _ = p.shape
    blk = ATT_BLOCK
    nb = rows // blk
    w = BRANCH_W
    n_ctx_rows = rows - n_lat_rows
    assert n_lat_rows % n_ctx_rows == 0 and n_ctx_rows % ATT_ROWS == 0

    def band(col, off):
        def index(b, t):
            return (b, jnp.clip(t * ATT_STEP + off, 0, nb - 1), col)
        return pl.BlockSpec((None, blk, w), index)

    offsets = range(-1, ATT_STEP + 1)
    ctx = lambda col: pl.BlockSpec((None, n_ctx_rows, w), lambda b, t: (b, n_lat_rows // n_ctx_rows, col))
    bias = jnp.asarray(_window_bias())
    specs = [pl.BlockSpec(memory_space=pltpu.SMEM), pl.BlockSpec((None, ATT_ROWS, w), lambda b, t: (b, t, 0)),
             ctx(1), ctx(2)]
    specs += [band(1, o) for o in offsets] + [band(2, o) for o in offsets] + [_vec_spec(bias.shape)]
    args = [sink] + [p] * (3 + 2 * len(offsets)) + [bias]
    assert len(specs) == N_ATT_REFS
    return specs, args


def _merge_kernel(*refs, n_lat_rows, starts):
    x = _stream_block(refs[0:len(starts)], starts)
    refs = refs[len(starts):]
    sc_ref, csc_ref, sh_ref, csh_ref, gm_ref, cgm_ref, g0_ref, g1_ref, ya_ref, yb_ref, yc_ref = refs[0:11]
    att_refs = refs[11:11 + N_ATT_REFS]
    wg_ref, gb_ref, wb_ref, wo_ref, o_ref = refs[11 + N_ATT_REFS:]
    is_ctx = _ctx_rows(x.shape[0], n_lat_rows)
    scale = jnp.where(is_ctx, csc_ref[...], sc_ref[...])
    shift = jnp.where(is_ctx, csh_ref[...], sh_ref[...])
    gmod = jnp.where(is_ctx, cgm_ref[...], gm_ref[...])
    h = _prenorm(x, g0_ref[...], scale, shift).astype(BF16)
    branches = [ya_ref[...], yb_ref[...], yc_ref[...], _attn_rows(att_refs, n_lat_rows)]
    acc = jnp.zeros(x.shape, F32)
    for i, y in enumerate(branches):
        pre = jnp.dot(h, wg_ref[:, i * D_MODEL:(i + 1) * D_MODEL], preferred_element_type=F32)
        gate = _sigmoid(pre + gb_ref[i:i + 1, :])
        acc = acc + gate * jnp.dot(y.astype(BF16), wb_ref[i], preferred_element_type=F32)
    out = jnp.dot(acc.astype(BF16), wo_ref[...], preferred_element_type=F32)
    ms = jnp.mean(out * out, axis=-1, keepdims=True)
    o_ref[...] = x + gmod * (out * lax.rsqrt(ms + NORM_EPS) * g1_ref[...])


def _merge(xs, mod, gain0, gain1, ys, p_att, sink, wg, gate_b, wb, wo, n_lat_rows, out_rows):
    parts = _stream_parts(xs)
    batch, d = parts[0].shape[0], parts[0].shape[-1]
    rows = out_rows
    tm = ATT_ROWS
    assert rows % tm == 0
    x_specs, x_args, starts = _stream_specs(xs, tm)
    tile = lambda width: pl.BlockSpec((None, tm, width), lambda b, i: (b, i, 0))
    consts = [wg, gate_b, wb, wo]
    att_specs, att_args = _attention_inputs(p_att, sink, n_lat_rows)
    return pl.pallas_call(
        functools.partial(_merge_kernel, n_lat_rows=n_lat_rows, starts=tuple(starts)),
        grid=(batch, rows // tm),
        in_specs=x_specs + _mod_specs(1, batch) + _mod_specs(0, batch) + _mod_specs(2, batch)
        + [_vec_spec(gain0.shape), _vec_spec(gain1.shape)]
        + [tile(BRANCH_W)] * 3 + att_specs + [_vec_spec(c.shape) for c in consts],
        out_specs=tile(d),
        out_shape=jax.ShapeDtypeStruct((batch, rows, d), F32),
        compiler_params=_cparams(("parallel", "parallel")),
        name="attn_merge_out",
    )(*x_args, *([mod] * 6), gain0, gain1, *ys, *att_args, *consts)


FFN_CHUNK = 512


def _ffn_kernel(x_ref, sc_ref, csc_ref, sh_ref, csh_ref, gm_ref, cgm_ref, g2_ref, g3_ref, w1_ref, w2_ref,
                o_ref, *, n_lat_rows):
    x = x_ref[...]
    is_ctx = _ctx_rows(x.shape[0], n_lat_rows)
    scale = jnp.where(is_ctx, csc_ref[...], sc_ref[...])
    shift = jnp.where(is_ctx, csh_ref[...], sh_ref[...])
    h = _prenorm(x, g2_ref[...], scale, shift).astype(BF16)
    hidden = w2_ref.shape[0]
    out = jnp.zeros(x.shape, F32)
    for lo in range(0, hidden, FFN_CHUNK):
        hi = min(lo + FFN_CHUNK, hidden)
        gt = jnp.dot(h, w1_ref[:, lo:hi], preferred_element_type=F32)
        up = jnp.dot(h, w1_ref[:, hidden + lo:hidden + hi], preferred_element_type=F32)
        out = out + jnp.dot((_silu(gt) * up).astype(BF16), w2_ref[lo:hi, :], preferred_element_type=F32)
    ms = jnp.mean(out * out, axis=-1, keepdims=True)
    gmod = jnp.where(is_ctx, cgm_ref[...], gm_ref[...])
    o_ref[...] = x + gmod * (out * lax.rsqrt(ms + NORM_EPS) * g3_ref[...])


def _ffn(xs, mod, gain2, gain3, w1, w2, n_lat_rows):
    batch, rows, d = xs.shape
    tm = _row_tile(rows, 544)
    tile = pl.BlockSpec((None, tm, d), lambda b, i: (b, i, 0))
    consts = [gain2, gain3, w1, w2]
    return pl.pallas_call(
        functools.partial(_ffn_kernel, n_lat_rows=n_lat_rows),
        grid=(batch, rows // tm),
        in_specs=[tile] + _mod_specs(4, batch) + _mod_specs(3, batch) + _mod_specs(5, batch)
        + [_vec_spec(c.shape) for c in consts],
        out_specs=tile,
        out_shape=jax.ShapeDtypeStruct((batch, rows, d), F32),
        compiler_params=_cparams(("parallel", "parallel")),
        name="swiglu",
    )(xs, *([mod] * 6), *consts)


def _pad_cols(w, width):
    return jnp.pad(w, ((0, 0), (0, width - w.shape[1])))


def _layer_weights(w_in, mu, w2, a2, g2, conv, a_log, dt_bias, gw2, gb):
    wts = {}
    off_b, off_c, off_d, off_g = A_IN, A_IN + B_IN, A_IN + B_IN + C_IN, A_IN + B_IN + C_IN + D_IN
    w = BRANCH_W
    wts["wa"] = _pad_cols(w_in[:, 0:A_IN], 4 * w).astype(BF16)
    wts["mu"] = _pad_cols(mu[None, :], 4 * w)
    lo = np.cumsum((0,) + A_LORA)
    place = lambda m, r0: jnp.zeros((w, w), F32).at[r0:r0 + m.shape[0], :].set(m)
    wts["w2p"] = jnp.stack([place(w2[0], lo[0]), place(w2[1], lo[1])])
    wts["a2p"] = jnp.stack([place(a2[0], lo[2]), place(a2[1], lo[3])])
    wts["g2p"] = place(g2, lo[4])

    wb = w_in[:, off_b:off_b + B_IN]
    nh = N_HEADS
    n_scalar = 4 * nh
    wts["wb"] = jnp.concatenate([wb[:, 0:3 * w], wb[:, 3 * w + n_scalar:],
                                 _pad_cols(wb[:, 3 * w:3 * w + n_scalar], LANES)], axis=1).astype(BF16)
    expand = np.zeros((4, LANES, w), np.float32)
    for grp in range(4):
        for h in range(nh):
            expand[grp, grp * nh + h, h * HEAD:(h + 1) * HEAD] = 1.0
    wts["eb"] = jnp.asarray(expand[0:2])
    wts["ea"] = jnp.asarray(expand[2:4])
    vec = lambda t: jnp.zeros((1, LANES), F32).at[0, 2 * nh:4 * nh].set(t.reshape(-1))
    wts["alog"] = vec(a_log)
    wts["dt"] = vec(dt_bias)
    wts["conv"] = conv

    wc = w_in[:, off_c:off_c + C_IN]
    pad_heads = lambda m: jnp.pad(m.reshape(m.shape[0], nh, C_DK),
                                  ((0, 0), (0, 0), (0, HEAD - C_DK))).reshape(m.shape[0], w)
    qc, kc, vc = wc[:, 0:C_QK], wc[:, C_QK:2 * C_QK], wc[:, 2 * C_QK:2 * C_QK + w]
    loc = wc[:, 2 * C_QK + w:2 * C_QK + w + 2 * C_GATE_R]
    gc = wc[:, 2 * C_QK + w + 2 * C_GATE_R:]
    wts["wc"] = jnp.concatenate([pad_heads(qc), pad_heads(kc), vc, gc, _pad_cols(loc, LANES)], axis=1).astype(BF16)
    gwp = jnp.zeros((2, LANES, w), F32)
    for d in range(2):
        gwp = gwp.at[d, d * C_GATE_R:(d + 1) * C_GATE_R, :].set(pad_heads(gw2[d]))
    wts["gwp"] = gwp
    wts["gbp"] = pad_heads(gb)
    wts["glane"] = jnp.asarray((np.arange(w) % HEAD < C_DK).astype(np.float32))[None, :]

    wd = w_in[:, off_d:off_d + D_IN]
    qd = wd[:, 0:w]
    dup = lambda m: jnp.concatenate([m[:, 0:HEAD], m[:, 0:HEAD], m[:, HEAD:], m[:, HEAD:]], axis=1)
    wts["wd"] = jnp.concatenate([qd, dup(wd[:, w:w + 2 * HEAD]), dup(wd[:, w + 2 * HEAD:])], axis=1).astype(BF16)
    wts["wg"] = w_in[:, off_g:].astype(BF16)
    return wts


def _rope_tables(n_lat_rows, n_ctx_rows):
    quarter = HEAD // 4
    inv = ROPE_BASE ** (-np.arange(quarter, dtype=np.float32) / quarter)
    pos = np.arange(n_lat_rows)
    rows = (pos // GRID_W).astype(np.float32)
    cols = (pos % GRID_W).astype(np.float32)
    inv = jnp.asarray(inv)
    ang_r = jnp.asarray(rows)[:, None] * inv[None, :]
    ang_c = jnp.asarray(cols)[:, None] * inv[None, :]
    cos = jnp.concatenate([jnp.cos(ang_r)] * 2 + [jnp.cos(ang_c)] * 2, axis=1)
    sin = jnp.concatenate([-jnp.sin(ang_r), jnp.sin(ang_r), -jnp.sin(ang_c), jnp.sin(ang_c)], axis=1)
    cos = jnp.concatenate([cos, jnp.ones((n_ctx_rows, HEAD), F32)], axis=0)
    sin = jnp.concatenate([sin, jnp.zeros((n_ctx_rows, HEAD), F32)], axis=0)
    return jnp.tile(cos, (1, N_HEADS)), jnp.tile(sin, (1, N_HEADS))


RWKV_SLOTS = {"r": ("sh", 0), "v": ("sh", 1), "a": ("sh", 2), "w": ("dw", 0), "k": ("dk", 0), "b": ("dk", 1)}
GDN_SLOTS = RWKV_SLOTS
GLA_SLOTS = {"r": ("sh", 0), "k": ("sh", 1), "v": ("sh", 2)}


def kernel(x, c, ctx, c_ctx, ada_w, ada_b, norm_g, w_in, gate_b, w_branch, w_out, rwkv_mu, rwkv_w0, rwkv_w2, rwkv_a0, rwkv_a2, rwkv_g2, rwkv_kk, rwkv_ka, rwkv_rk, rwkv_ln_g, rwkv_ln_b, gdn_conv, gdn_a_log, gdn_dt_bias, gdn_norm_g, gla_gw2, gla_gb, gla_norm_g, attn_sink, ffn_w1, ffn_w2):
    batch, n_lat, d = x.shape
    n_ctx = ctx.shape[1]
    depth = ada_w.shape[0]
    assert n_ctx % PREP_ROWS == 0 and n_lat % PREP_ROWS == 0 and d == D_MODEL

    mod_rows = 8 * ((batch + 1 + 7) // 8)
    c_rows = jnp.concatenate([c, c_ctx[None, :], jnp.zeros((mod_rows - batch - 1, d), F32)], axis=0)
    mod_all = _modulation(c_rows, ada_w, ada_b)
    cos, sin = _rope_tables(n_lat, n_ctx)

    xs = (x, ctx)
    rows = n_lat + n_ctx
    row = lambda t: t.reshape(1, -1)
    for l in range(depth):
        out_rows = n_lat if l == depth - 1 else rows
        mod = mod_all[l].reshape(mod_rows, 1, 6 * d)
        ng = norm_g[l]
        wts = _layer_weights(w_in[l], rwkv_mu[l], rwkv_w2[l], rwkv_a2[l], rwkv_g2[l], gdn_conv[l],
                             gdn_a_log[l], gdn_dt_bias[l], gla_gw2[l], gla_gb[l])
        ones = jnp.asarray(_head_block_ones())
        around = jnp.asarray(0.5 * (_shift_matrix(-1) + _shift_matrix(1)), BF16)
        rwkv_consts = [around, wts["mu"], wts["w2p"], wts["a2p"], wts["g2p"], rwkv_w0[l], rwkv_a0[l],
                       row(rwkv_kk[l]), row(rwkv_ka[l]), row(rwkv_rk[l]), ones]
        gdn_consts = [wts["conv"], wts["alog"], wts["dt"], wts["eb"], wts["ea"], ones]
        p_gla, p_att, rwkv_ops, gdn_ops = _front(xs, mod, row(ng[0]), (wts["wc"], wts["wb"], wts["wd"], wts["wa"]),
                                                 cos, sin, rwkv_consts, gdn_consts, n_lat)

        sh, dks, dws, fin = rwkv_ops
        ya = _bidir_scan(sh, dks, dws, n_lat, fin, row(rwkv_ln_g[l]), row(rwkv_ln_b[l]), finish="groupnorm",
                         scalar_decay=False, lowrank=True, slots=RWKV_SLOTS)

        sh, dks, dws, fin = gdn_ops
        gnorm = row(jnp.tile(gdn_norm_g[l], N_HEADS))
        yb = _bidir_scan(sh, dks, dws, n_lat, fin, gnorm, gnorm, finish="rms",
                         scalar_decay=True, lowrank=True, slots=GDN_SLOTS)

        yc = _gla_scan(p_gla, n_lat, wts["gwp"], wts["gbp"], wts["glane"], row(jnp.tile(gla_norm_g[l], N_HEADS)))

        xs = _merge(xs, mod, row(ng[0]), row(ng[1]), (ya, yb, yc), p_att, attn_sink[l], wts["wg"], gate_b[l],
                    w_branch[l].astype(BF16), w_out[l].astype(BF16), n_lat, out_rows)
        xs = _ffn(xs, mod, row(ng[2]), row(ng[3]), ffn_w1[l].astype(BF16), ffn_w2[l].astype(BF16), n_lat)
    return xs
```
